```python
import math
import jax, jax.numpy as jnp
from jax import lax
import numpy as np

D_MODEL = 1024
BATCH = 4
SEQ = 4096
DEPTH = 1

MEM_LEN = 256
MEM_HEADS = 4
MEM_HEAD_DIM = D_MODEL // 16
MEM_WIDTH = MEM_HEADS * MEM_HEAD_DIM
DIFF_HEADS = 4
DIFF_HEAD_DIM = D_MODEL // 16
DIFF_WIDTH = DIFF_HEADS * 2 * DIFF_HEAD_DIM
FOURIER_GROUPS = 4
FOURIER_GROUP_DIM = D_MODEL // 16
FOURIER_WIDTH = FOURIER_GROUPS * FOURIER_GROUP_DIM
D_MIX = DIFF_WIDTH + FOURIER_WIDTH + MEM_WIDTH
IN_COLS = 3 * DIFF_WIDTH + FOURIER_WIDTH + MEM_WIDTH
N_BUCKETS = 32
MAX_DISTANCE = 128
Q_BLOCK = 128
N_GROUPS = 4
EXPERTS_PER_GROUP = 8
N_EXPERTS = N_GROUPS * EXPERTS_PER_GROUP
TOP_K = 2
D_EXPERT = D_MODEL // 4
LN_EPS = 1e-5
DEEPNORM_ALPHA = (2.0 * DEPTH) ** 0.25
DEEPNORM_BETA = (8.0 * DEPTH) ** -0.25

kernel_name = "hymba_diff_fnet_hmoe_encoder"

F32 = jnp.float32


def layer_norm(x, g, b):
    xf = x.astype(F32)
    mu = jnp.mean(xf, axis=-1, keepdims=True)
    xc = xf - mu
    var = jnp.mean(xc * xc, axis=-1, keepdims=True)
    return (xc * lax.rsqrt(var + LN_EPS) * g.astype(F32) + b.astype(F32)).astype(x.dtype)


def t5_bucket(rel):
    nb = N_BUCKETS // 2
    max_exact = nb // 2
    ret = (rel > 0).astype(jnp.int32) * nb
    n = jnp.abs(rel)
    nf = jnp.maximum(n, 1).astype(F32)
    large = max_exact + (jnp.log(nf / max_exact) / math.log(MAX_DISTANCE / max_exact)
                         * (nb - max_exact)).astype(jnp.int32)
    large = jnp.minimum(large, nb - 1)
    return ret + jnp.where(n < max_exact, n, large)


def diff_attention(q, k, v, lam, lam_init, subln_g, rel_bias):
    B, S = q.shape[0], q.shape[1]
    nblk = S // Q_BLOCK
    scale = DIFF_HEAD_DIM ** -0.5
    kf = k.astype(F32)
    vf = v.astype(F32)
    qb = (q.astype(F32) * scale).reshape(B, nblk, Q_BLOCK, DIFF_HEADS, 2, DIFF_HEAD_DIM)
    qb = qb.transpose(1, 0, 2, 3, 4, 5)
    starts = jnp.arange(nblk, dtype=jnp.int32) * Q_BLOCK
    kpos = jnp.arange(S, dtype=jnp.int32)
    table = rel_bias.astype(F32)

    def block(args):
        qblk, start = args
        qpos = start + jnp.arange(Q_BLOCK, dtype=jnp.int32)
        bucket = t5_bucket(kpos[None, :] - qpos[:, None])
        bias = jnp.take(table, bucket, axis=0).transpose(2, 0, 1)
        s = jnp.einsum('bqhcd,bkhcd->bchqk', qblk, kf) + bias[None, None]
        p = jax.nn.softmax(s, axis=-1)
        a = p[:, 0] - lam * p[:, 1]
        return jnp.einsum('bhqk,bkhe->bqhe', a, vf)

    o = lax.map(block, (qb, starts))
    o = o.transpose(1, 0, 2, 3, 4).reshape(B, S, DIFF_HEADS, 2 * DIFF_HEAD_DIM)
    o = o * lax.rsqrt(jnp.mean(o * o, axis=-1, keepdims=True) + LN_EPS) * subln_g.astype(F32)
    o = o * (1.0 - lam_init)
    return o.reshape(B, S, DIFF_WIDTH).astype(q.dtype)


def fourier_mix(u, w_f):
    B, S = u.shape[0], u.shape[1]
    ug = u.astype(F32).reshape(B, S, FOURIER_GROUPS, FOURIER_GROUP_DIM)
    spec = jnp.fft.fft2(ug, axes=(1, 3), norm="ortho").real
    out = jnp.einsum('bsgc,gcd->bsgd', spec, w_f.astype(F32))
    return out.reshape(B, S, FOURIER_WIDTH).astype(u.dtype)


def memory_attention(mq, mem, w_mem_kv):
    B, S = mq.shape[0], mq.shape[1]
    kv = mem @ w_mem_kv
    mk, mv = jnp.split(kv, 2, axis=-1)
    qh = mq.reshape(B, S, MEM_HEADS, MEM_HEAD_DIM).astype(F32) * (MEM_HEAD_DIM ** -0.5)
    kh = mk.reshape(B, -1, MEM_HEADS, MEM_HEAD_DIM).astype(F32)
    vh = mv.reshape(B, -1, MEM_HEADS, MEM_HEAD_DIM).astype(F32)
    p = jax.nn.softmax(jnp.einsum('bshd,bmhd->bhsm', qh, kh), axis=-1)
    o = jnp.einsum('bhsm,bmhd->bshd', p, vh)
    return o.reshape(B, S, MEM_WIDTH).astype(mq.dtype)


def hybrid_mixer(h, mem, w_in, w_mem_kv, w_fourier, lambda_q1, lambda_k1, lambda_q2, lambda_k2,
                 subln_g, w_out, rel_bias, lam_init):
    B, S = h.shape[0], h.shape[1]
    proj = h @ w_in
    q, k, v, f_in, mq = jnp.split(
        proj, [DIFF_WIDTH, 2 * DIFF_WIDTH, 3 * DIFF_WIDTH, 3 * DIFF_WIDTH + FOURIER_WIDTH], axis=-1)
    q = q.reshape(B, S, DIFF_HEADS, 2, DIFF_HEAD_DIM)
    k = k.reshape(B, S, DIFF_HEADS, 2, DIFF_HEAD_DIM)
    v = v.reshape(B, S, DIFF_HEADS, 2 * DIFF_HEAD_DIM)
    lam = (jnp.exp(jnp.sum(lambda_q1.astype(F32) * lambda_k1.astype(F32)))
           - jnp.exp(jnp.sum(lambda_q2.astype(F32) * lambda_k2.astype(F32))) + lam_init)
    o_diff = diff_attention(q, k, v, lam, lam_init, subln_g, rel_bias)
    o_four = fourier_mix(f_in, w_fourier)
    o_mem = memory_attention(mq, mem, w_mem_kv)
    o = jnp.concatenate([o_diff, o_four, o_mem], axis=-1)
    return o @ w_out


def hier_moe(h, w_group, b_group, w_router, b_router, w1, w3, w2):
    B, S, D = h.shape
    N = B * S
    t = h.reshape(N, D)
    tf = t.astype(F32)
    g_logits = tf @ w_group.astype(F32) + b_group.astype(F32)
    g_prob = jax.nn.softmax(g_logits, axis=-1)
    g_sel = jnp.argmax(g_logits, axis=-1)
    g_gate = jnp.take_along_axis(g_prob, g_sel[:, None], axis=-1)
    e_logits = (tf @ w_router.astype(F32) + b_router.astype(F32)).reshape(N, N_GROUPS, EXPERTS_PER_GROUP)
    e_in_group = jnp.take_along_axis(e_logits, g_sel[:, None, None], axis=1)[:, 0]
    top_val, top_idx = lax.top_k(e_in_group, TOP_K)
    top_w = jax.nn.softmax(top_val, axis=-1) * g_gate
    expert_id = g_sel[:, None] * EXPERTS_PER_GROUP + top_idx
    gates = jnp.sum(jax.nn.one_hot(expert_id, N_EXPERTS, dtype=F32) * top_w[..., None], axis=1)
    out = jnp.zeros((N, D), F32)
    for e in range(N_EXPERTS):
        hid = jax.nn.silu(t @ w1[e]) * (t @ w3[e])
        out = out + gates[:, e:e + 1] * (hid @ w2[e]).astype(F32)
    return out.reshape(B, S, D).astype(h.dtype)


def setup_inputs(seed: int = 0) -> dict:
    key = jax.random.key(seed)
    ks = jax.random.split(key, 32)

    def nrm(k, shape, s):
        return jax.random.normal(k, shape, jnp.float32) * s

    D = D_MODEL
    x = nrm(ks[0], (BATCH, SEQ, D), 1.0)
    mem = nrm(ks[1], (BATCH, MEM_LEN, D), 1.0)
    ln0_g = 1.0 + nrm(ks[2], (D,), 0.02)
    ln0_b = nrm(ks[3], (D,), 0.02)
    rel_bias = nrm(ks[4], (N_BUCKETS, DIFF_HEADS), 0.5)
    col_scale = jnp.concatenate([
        jnp.ones((2 * DIFF_WIDTH,), jnp.float32),
        jnp.full((DIFF_WIDTH,), DEEPNORM_BETA, jnp.float32),
        jnp.ones((FOURIER_WIDTH + MEM_WIDTH,), jnp.float32)])
    w_in = nrm(ks[5], (DEPTH, D, IN_COLS), D ** -0.5) * col_scale
    mem_scale = jnp.concatenate([jnp.ones((MEM_WIDTH,), jnp.float32),
                                 jnp.full((MEM_WIDTH,), DEEPNORM_BETA, jnp.float32)])
    w_mem_kv = nrm(ks[6], (DEPTH, D, 2 * MEM_WIDTH), D ** -0.5) * mem_scale
    w_fourier = nrm(ks[7], (DEPTH, FOURIER_GROUPS, FOURIER_GROUP_DIM, FOURIER_GROUP_DIM), FOURIER_GROUP_DIM ** -0.5)
    lambda_q1 = nrm(ks[8], (DEPTH, DIFF_HEAD_DIM), 0.1)
    lambda_k1 = nrm(ks[9], (DEPTH, DIFF_HEAD_DIM), 0.1)
    lambda_q2 = nrm(ks[10], (DEPTH, DIFF_HEAD_DIM), 0.1)
    lambda_k2 = nrm(ks[11], (DEPTH, DIFF_HEAD_DIM), 0.1)
    subln_g = 1.0 + nrm(ks[12], (DEPTH, 2 * DIFF_HEAD_DIM), 0.02)
    w_out = nrm(ks[13], (DEPTH, D_MIX, D), D_MIX ** -0.5 * DEEPNORM_BETA)
    ln1_g = 1.0 + nrm(ks[14], (DEPTH, D), 0.02)
    ln1_b = nrm(ks[15], (DEPTH, D), 0.02)
    w_group = nrm(ks[16], (DEPTH, D, N_GROUPS), D ** -0.5)
    b_group = nrm(ks[17], (DEPTH, N_GROUPS), 0.01)
    w_router = nrm(ks[18], (DEPTH, D, N_EXPERTS), D ** -0.5)
    b_router = nrm(ks[19], (DEPTH, N_EXPERTS), 0.01)
    w1 = nrm(ks[20], (DEPTH, N_EXPERTS, D, D_EXPERT), D ** -0.5)
    w3 = nrm(ks[21], (DEPTH, N_EXPERTS, D, D_EXPERT), D ** -0.5)
    w2 = nrm(ks[22], (DEPTH, N_EXPERTS, D_EXPERT, D), D_EXPERT ** -0.5 * DEEPNORM_BETA)
    ln2_g = 1.0 + nrm(ks[23], (DEPTH, D), 0.02)
    ln2_b = nrm(ks[24], (DEPTH, D), 0.02)
    return {"x": x, "mem": mem, "ln0_g": ln0_g, "ln0_b": ln0_b, "rel_bias": rel_bias,
            "w_in": w_in, "w_mem_kv": w_mem_kv, "w_fourier": w_fourier,
            "lambda_q1": lambda_q1, "lambda_k1": lambda_k1, "lambda_q2": lambda_q2, "lambda_k2": lambda_k2,
            "subln_g": subln_g, "w_out": w_out, "ln1_g": ln1_g, "ln1_b": ln1_b,
            "w_group": w_group, "b_group": b_group, "w_router": w_router, "b_router": b_router,
            "w1": w1, "w3": w3, "w2": w2, "ln2_g": ln2_g, "ln2_b": ln2_b}


def reference(x, mem, ln0_g, ln0_b, rel_bias, w_in, w_mem_kv, w_fourier,
              lambda_q1, lambda_k1, lambda_q2, lambda_k2, subln_g, w_out, ln1_g, ln1_b,
              w_group, b_group, w_router, b_router, w1, w3, w2, ln2_g, ln2_b):
    h = layer_norm(x, ln0_g, ln0_b)
    for l in range(DEPTH):
        lam_init = 0.8 - 0.6 * math.exp(-0.3 * l)
        a = hybrid_mixer(h, mem, w_in[l], w_mem_kv[l], w_fourier[l],
                         lambda_q1[l], lambda_k1[l], lambda_q2[l], lambda_k2[l],
                         subln_g[l], w_out[l], rel_bias, lam_init)
        h = layer_norm(DEEPNORM_ALPHA * h + a, ln1_g[l], ln1_b[l])
        f = hier_moe(h, w_group[l], b_group[l], w_router[l], b_router[l], w1[l], w3[l], w2[l])
        h = layer_norm(DEEPNORM_ALPHA * h + f, ln2_g[l], ln2_b[l])
    return h
```

```python
import functools
import math

import numpy as np
import jax
import jax.numpy as jnp
from jax import lax
from jax.experimental import pallas as pl
from jax.experimental.pallas import tpu as pltpu

F32 = jnp.float32
BF16 = jnp.bfloat16

D_MODEL = 1024
BATCH = 4
SEQ = 4096
N_TOK = BATCH * SEQ
MEM_LEN = 256
MEM_HEADS = 4
MEM_HEAD_DIM = 64
MEM_WIDTH = 256
DIFF_HEADS = 4
DIFF_HEAD_DIM = 64
DIFF_VDIM = 2 * DIFF_HEAD_DIM
DIFF_WIDTH = 512
FOURIER_GROUPS = 4
FOURIER_GROUP_DIM = 64
FOURIER_WIDTH = 256
N_BUCKETS = 32
MAX_DISTANCE = 128
N_GROUPS = 4
EXPERTS_PER_GROUP = 8
N_EXPERTS = 32
D_EXPERT = 256
LN_EPS = 1e-5
DEEPNORM_ALPHA = 2.0 ** 0.25
LAM_INIT = 0.8 - 0.6 * math.exp(-0.3 * 0)
LOG2E = 1.4426950408889634

ROW_TILE = 512
ATT_TILE = 256
MOE_TILE = 1024
DFT_ROW_TILE = 512
ROUTER_LANES = 128
GROUP_LANE0 = 32
NEG_BIG = -1e30

_NT = (((1,), (1,)), ((), ()))


def _vmem(nbytes):
    return pltpu.CompilerParams(vmem_limit_bytes=int(nbytes))


def _layer_norm(x, g, b):
    mu = jnp.mean(x, axis=-1, keepdims=True)
    xc = x - mu
    var = jnp.mean(xc * xc, axis=-1, keepdims=True)
    return xc * lax.rsqrt(var + LN_EPS) * g + b


def _t5_bucket(rel):
    nb = N_BUCKETS // 2
    max_exact = nb // 2
    ret = (rel > 0).astype(jnp.int32) * nb
    n = jnp.abs(rel)
    nf = jnp.maximum(n, 1).astype(F32)
    large = max_exact + (jnp.log(nf / max_exact) / math.log(MAX_DISTANCE / max_exact)
                         * (nb - max_exact)).astype(jnp.int32)
    large = jnp.minimum(large, nb - 1)
    return ret + jnp.where(n < max_exact, n, large)


def _ln0_inproj_kernel(x_ref, g_ref, b_ref, wqk_ref, wvt_ref, wfm_ref,
                       h_ref, q_ref, k_ref, vt_ref, f_ref, mq_ref):
    h = _layer_norm(x_ref[...], g_ref[...], b_ref[...])
    h_ref[...] = h
    hb = h.astype(BF16)
    qk = jnp.dot(hb, wqk_ref[...], preferred_element_type=F32)
    q_ref[...] = (qk[:, :DIFF_WIDTH] * (DIFF_HEAD_DIM ** -0.5 * LOG2E)).astype(BF16)
    k_ref[...] = qk[:, DIFF_WIDTH:].astype(BF16)
    vt = lax.dot_general(wvt_ref[...], hb, _NT, preferred_element_type=F32)
    for hh in range(DIFF_HEADS):
        for jj in range(ROW_TILE // ATT_TILE):
            vt_ref[0, hh, jj] = vt[hh * DIFF_VDIM:(hh + 1) * DIFF_VDIM,
                                   jj * ATT_TILE:(jj + 1) * ATT_TILE].astype(BF16)
    fm = jnp.dot(hb, wfm_ref[...], preferred_element_type=F32)
    f_ref[...] = fm[:, :FOURIER_WIDTH].astype(BF16)
    mq_ref[...] = (fm[:, FOURIER_WIDTH:] * MEM_HEAD_DIM ** -0.5).astype(BF16)


def _ln0_inproj(x2, ln_g, ln_b, w_qk, w_vt, w_fm):
    rt = SEQ // ROW_TILE
    nk = SEQ // ATT_TILE
    row = lambda r: (r, 0)
    const = lambda r: (0, 0)
    return pl.pallas_call(
        _ln0_inproj_kernel,
        grid=(N_TOK // ROW_TILE,),
        in_specs=[
            pl.BlockSpec((ROW_TILE, D_MODEL), row),
            pl.BlockSpec((1, D_MODEL), const),
            pl.BlockSpec((1, D_MODEL), const),
            pl.BlockSpec((D_MODEL, 2 * DIFF_WIDTH), const),
            pl.BlockSpec((DIFF_WIDTH, D_MODEL), const),
            pl.BlockSpec((D_MODEL, FOURIER_WIDTH + MEM_WIDTH), const),
        ],
        out_specs=[
            pl.BlockSpec((ROW_TILE, D_MODEL), row),
            pl.BlockSpec((ROW_TILE, DIFF_WIDTH), row),
            pl.BlockSpec((ROW_TILE, DIFF_WIDTH), row),
            pl.BlockSpec((1, DIFF_HEADS, ROW_TILE // ATT_TILE, DIFF_VDIM, ATT_TILE),
                         lambda r: (r // rt, 0, r % rt, 0, 0)),
            pl.BlockSpec((ROW_TILE, FOURIER_WIDTH), row),
            pl.BlockSpec((ROW_TILE, MEM_WIDTH), row),
        ],
        out_shape=[
            jax.ShapeDtypeStruct((N_TOK, D_MODEL), F32),
            jax.ShapeDtypeStruct((N_TOK, DIFF_WIDTH), BF16),
            jax.ShapeDtypeStruct((N_TOK, DIFF_WIDTH), BF16),
            jax.ShapeDtypeStruct((BATCH, DIFF_HEADS, nk, DIFF_VDIM, ATT_TILE), BF16),
            jax.ShapeDtypeStruct((N_TOK, FOURIER_WIDTH), BF16),
            jax.ShapeDtypeStruct((N_TOK, MEM_WIDTH), BF16),
        ],
        compiler_params=_vmem(48 << 20),
        name="ln0_inproj",
    )(x2, ln_g, ln_b, w_qk, w_vt, w_fm)


def _diff_attn_kernel(tbl_ref, q_ref, k_ref, vt_ref, bkt_ref, lam_ref, g_ref, o_ref,
                      bias_scr, acc_scr):
    T = ATT_TILE
    nk = SEQ // T
    h = pl.program_id(1)
    i = pl.program_id(2)

    @pl.when(i == 0)
    def _build_bias():
        for d in range(5):
            bk = bkt_ref[d]
            bias = jnp.zeros((T, T), F32)
            for n in range(N_BUCKETS):
                bias = jnp.where(bk == n, tbl_ref[n, h], bias)
            bias_scr[d] = bias * LOG2E

    q = q_ref[...]
    lane = lax.broadcasted_iota(jnp.int32, q.shape, 1)
    zero = jnp.zeros_like(q)
    q_comp = (jnp.where(lane < DIFF_HEAD_DIM, q, zero), jnp.where(lane >= DIFF_HEAD_DIM, q, zero))
    acc_scr[...] = jnp.zeros_like(acc_scr)

    def step(j, carry):
        d = jnp.clip(j - i, -2, 2) + 2
        bias = bias_scr[d]
        k_t = k_ref[pl.ds(pl.multiple_of(j * T, T), T), :]
        vt_t = vt_ref[0, 0, j]
        out = []
        for c in range(2):
            m, l = carry[2 * c], carry[2 * c + 1]
            s = lax.dot_general(k_t, q_comp[c], _NT, preferred_element_type=F32) + bias
            m_new = jnp.maximum(m, jnp.max(s, axis=0, keepdims=True))
            a = jnp.exp2(m - m_new)
            p = jnp.exp2(s - m_new)
            l = a * l + jnp.sum(p, axis=0, keepdims=True)
            pv = jnp.dot(vt_t, p.astype(BF16), preferred_element_type=F32)
            acc_scr[c] = a * acc_scr[c] + pv
            out += [m_new, l]
        return tuple(out)

    neg = jnp.full((1, T), NEG_BIG, F32)
    zer = jnp.zeros((1, T), F32)
    _, l0, _, l1 = lax.fori_loop(0, nk, step, (neg, zer, neg, zer))

    lam = (jnp.exp(jnp.sum(lam_ref[0:1, :] * lam_ref[1:2, :], axis=1, keepdims=True))
           - jnp.exp(jnp.sum(lam_ref[2:3, :] * lam_ref[3:4, :], axis=1, keepdims=True)) + LAM_INIT)
    o = acc_scr[0] * (1.0 / l0) - lam * (acc_scr[1] * (1.0 / l1))
    ms = jnp.mean(o * o, axis=0, keepdims=True)
    o = o * lax.rsqrt(ms + LN_EPS) * g_ref[...] * (1.0 - LAM_INIT)
    o_ref[...] = o.T.astype(BF16)


def _diff_attention(rel_bias, q, k, vt, buckets, lam_vecs, subln_col):
    T = ATT_TILE
    nq = SEQ // T
    nk = SEQ // T
    return pl.pallas_call(
        _diff_attn_kernel,
        grid=(BATCH, DIFF_HEADS, nq),
        in_specs=[
            pl.BlockSpec(memory_space=pltpu.SMEM),
            pl.BlockSpec((T, DIFF_VDIM), lambda b, h, i: (b * nq + i, h)),
            pl.BlockSpec((SEQ, DIFF_VDIM), lambda b, h, i: (b, h)),
            pl.BlockSpec((1, 1, nk, DIFF_VDIM, T), lambda b, h, i: (b, h, 0, 0, 0)),
            pl.BlockSpec((5, T, T), lambda b, h, i: (0, 0, 0)),
            pl.BlockSpec((4, DIFF_HEAD_DIM), lambda b, h, i: (0, 0)),
            pl.BlockSpec((DIFF_VDIM, 1), lambda b, h, i: (0, 0)),
        ],
        out_specs=pl.BlockSpec((T, DIFF_VDIM), lambda b, h, i: (b * nq + i, h)),
        out_shape=jax.ShapeDtypeStruct((N_TOK, DIFF_WIDTH), BF16),
        scratch_shapes=[pltpu.VMEM((5, T, T), F32), pltpu.VMEM((2, DIFF_VDIM, T), F32)],
        compiler_params=_vmem(40 << 20),
        name="diff_attn",
    )(rel_bias, q, k, vt, buckets, lam_vecs, subln_col)


def _fourier_chan_kernel(f_ref, bdc_ref, bds_ref, bdw_ref, z_ref):
    u = f_ref[...]
    w = bdw_ref[...]
    pc = jnp.dot(u, bdc_ref[...], preferred_element_type=F32).astype(BF16)
    ps = jnp.dot(u, bds_ref[...], preferred_element_type=F32).astype(BF16)
    z_ref[0, 0] = jnp.dot(pc, w, preferred_element_type=F32).astype(BF16)
    z_ref[0, 1] = jnp.dot(ps, w, preferred_element_type=F32).astype(BF16)


def _fourier_chan(f_in, bdc, bds, bdw):
    rt = SEQ // ROW_TILE
    const = lambda r: (0, 0)
    return pl.pallas_call(
        _fourier_chan_kernel,
        grid=(N_TOK // ROW_TILE,),
        in_specs=[
            pl.BlockSpec((ROW_TILE, FOURIER_WIDTH), lambda r: (r, 0)),
            pl.BlockSpec((FOURIER_WIDTH, FOURIER_WIDTH), const),
            pl.BlockSpec((FOURIER_WIDTH, FOURIER_WIDTH), const),
            pl.BlockSpec((FOURIER_WIDTH, FOURIER_WIDTH), const),
        ],
        out_specs=pl.BlockSpec((1, 2, ROW_TILE, FOURIER_WIDTH), lambda r: (r // rt, 0, r % rt, 0)),
        out_shape=jax.ShapeDtypeStruct((BATCH, 2, SEQ, FOURIER_WIDTH), BF16),
        name="fourier_chan",
    )(f_in, bdc, bds, bdw)


def _fourier_seq_kernel(dft_ref, z_ref, o_ref):
    o_ref[...] = jnp.dot(dft_ref[...], z_ref[0], preferred_element_type=F32).astype(BF16)


def _fourier_seq(dft, z):
    nr = SEQ // DFT_ROW_TILE
    return pl.pallas_call(
        _fourier_seq_kernel,
        grid=(nr, BATCH),
        in_specs=[
            pl.BlockSpec((DFT_ROW_TILE, 2 * SEQ), lambda r, b: (r, 0)),
            pl.BlockSpec((1, 2 * SEQ, FOURIER_WIDTH), lambda r, b: (b, 0, 0)),
        ],
        out_specs=pl.BlockSpec((DFT_ROW_TILE, FOURIER_WIDTH), lambda r, b: (b * nr + r, 0)),
        out_shape=jax.ShapeDtypeStruct((N_TOK, FOURIER_WIDTH), BF16),
        compiler_params=_vmem(48 << 20),
        name="fourier_seq",
    )(dft, z)


@functools.lru_cache(maxsize=None)
def _dft_constants():
    n = np.arange(SEQ, dtype=np.int64)
    ang = 2.0 * np.pi * ((n[:, None] * n[None, :]) % SEQ).astype(np.float64) / SEQ
    scale = 1.0 / math.sqrt(SEQ * FOURIER_GROUP_DIM)
    dft = np.concatenate([np.cos(ang), -np.sin(ang)], axis=1) * scale
    c = np.arange(FOURIER_GROUP_DIM, dtype=np.int64)
    angc = 2.0 * np.pi * ((c[:, None] * c[None, :]) % FOURIER_GROUP_DIM).astype(np.float64) / FOURIER_GROUP_DIM
    eye = np.eye(FOURIER_GROUPS)
    bdc = np.kron(eye, np.cos(angc))
    bds = np.kron(eye, np.sin(angc))
    return dft.astype(BF16), bdc.astype(BF16), bds.astype(BF16)


def _mem_kv_kernel(mem_ref, w_ref, kcat_ref, vcat_ref):
    kv = jnp.dot(mem_ref[...].astype(BF16), w_ref[...], preferred_element_type=F32)
    mk = kv[:, :MEM_WIDTH]
    mv = kv[:, MEM_WIDTH:]
    lane = lax.broadcasted_iota(jnp.int32, mk.shape, 1)
    for hh in range(MEM_HEADS):
        sel = (lane >= hh * MEM_HEAD_DIM) & (lane < (hh + 1) * MEM_HEAD_DIM)
        kcat_ref[0, hh * MEM_LEN:(hh + 1) * MEM_LEN, :] = jnp.where(sel, mk, 0.0).astype(BF16)
        vcat_ref[0, hh * MEM_LEN:(hh + 1) * MEM_LEN, :] = jnp.where(sel, mv, 0.0).astype(BF16)


def _mem_kv(mem2, w_kv):
    return pl.pallas_call(
        _mem_kv_kernel,
        grid=(BATCH,),
        in_specs=[
            pl.BlockSpec((MEM_LEN, D_MODEL), lambda b: (b, 0)),
            pl.BlockSpec((D_MODEL, 2 * MEM_WIDTH), lambda b: (0, 0)),
        ],
        out_specs=[
            pl.BlockSpec((1, MEM_HEADS * MEM_LEN, MEM_WIDTH), lambda b: (b, 0, 0)),
            pl.BlockSpec((1, MEM_HEADS * MEM_LEN, MEM_WIDTH), lambda b: (b, 0, 0)),
        ],
        out_shape=[
            jax.ShapeDtypeStruct((BATCH, MEM_HEADS * MEM_LEN, MEM_WIDTH), BF16),
            jax.ShapeDtypeStruct((BATCH, MEM_HEADS * MEM_LEN, MEM_WIDTH), BF16),
        ],
        name="mem_kv",
    )(mem2, w_kv)


def _mem_attn_kernel(mq_ref, kcat_ref, vcat_ref, o_ref):
    s = lax.dot_general(mq_ref[...], kcat_ref[0], _NT, preferred_element_type=F32)
    parts = []
    for hh in range(MEM_HEADS):
        sh = s[:, hh * MEM_LEN:(hh + 1) * MEM_LEN]
        p = jnp.exp(sh - jnp.max(sh, axis=1, keepdims=True))
        parts.append((p * (1.0 / jnp.sum(p, axis=1, keepdims=True))).astype(BF16))
    p_all = jnp.concatenate(parts, axis=1)
    o_ref[...] = jnp.dot(p_all, vcat_ref[0], preferred_element_type=F32).astype(BF16)


def _mem_attention(mq, kcat, vcat):
    rt = SEQ // ROW_TILE
    return pl.pallas_call(
        _mem_attn_kernel,
        grid=(N_TOK // ROW_TILE,),
        in_specs=[
            pl.BlockSpec((ROW_TILE, MEM_WIDTH), lambda r: (r, 0)),
            pl.BlockSpec((1, MEM_HEADS * MEM_LEN, MEM_WIDTH), lambda r: (r // rt, 0, 0)),
            pl.BlockSpec((1, MEM_HEADS * MEM_LEN, MEM_WIDTH), lambda r: (r // rt, 0, 0)),
        ],
        out_specs=pl.BlockSpec((ROW_TILE, MEM_WIDTH), lambda r: (r, 0)),
        out_shape=jax.ShapeDtypeStruct((N_TOK, MEM_WIDTH), BF16),
        name="mem_attn",
    )(mq, kcat, vcat)


def _route(logits):
    lane = lax.broadcasted_iota(jnp.int32, logits.shape, 1)
    big = jnp.int32(ROUTER_LANES)
    is_group = (lane >= GROUP_LANE0) & (lane < GROUP_LANE0 + N_GROUPS)
    gl = jnp.where(is_group, logits, -jnp.inf)
    gmax = jnp.max(gl, axis=1, keepdims=True)
    g_sel = jnp.min(jnp.where(gl == gmax, lane, big), axis=1, keepdims=True) - GROUP_LANE0
    g_gate = 1.0 / jnp.sum(jnp.where(is_group, jnp.exp(gl - gmax), 0.0), axis=1, keepdims=True)
    in_group = (lane >= g_sel * EXPERTS_PER_GROUP) & (lane < (g_sel + 1) * EXPERTS_PER_GROUP)
    el = jnp.where(in_group, logits, -jnp.inf)
    v1 = jnp.max(el, axis=1, keepdims=True)
    i1 = jnp.min(jnp.where(el == v1, lane, big), axis=1, keepdims=True)
    el2 = jnp.where(lane == i1, -jnp.inf, el)
    v2 = jnp.max(el2, axis=1, keepdims=True)
    i2 = jnp.min(jnp.where(el2 == v2, lane, big), axis=1, keepdims=True)
    e = jnp.exp(v2 - v1)
    w_first = g_gate / (1.0 + e)
    w_second = g_gate * e / (1.0 + e)
    return jnp.where(lane == i1, w_first, 0.0) + jnp.where(lane == i2, w_second, 0.0)


def _outproj_router_kernel(od_ref, of_ref, om_ref, h_ref, wo_ref, g_ref, b_ref,
                           wr_hi_ref, wr_lo_ref, br_ref, h1_ref, h1b_ref, gates_ref):
    o = jnp.concatenate([od_ref[...], of_ref[...], om_ref[...]], axis=1)
    a = jnp.dot(o, wo_ref[...], preferred_element_type=F32)
    h1 = _layer_norm(DEEPNORM_ALPHA * h_ref[...] + a, g_ref[...], b_ref[...])
    h1_ref[...] = h1
    hi = h1.astype(BF16)
    h1b_ref[...] = hi
    lo = (h1 - hi.astype(F32)).astype(BF16)
    w_hi = wr_hi_ref[...]
    logits = (jnp.dot(hi, w_hi, preferred_element_type=F32)
              + jnp.dot(hi, wr_lo_ref[...], preferred_element_type=F32)
              + jnp.dot(lo, w_hi, preferred_element_type=F32) + br_ref[...])
    gates_ref[...] = _route(logits)


def _outproj_router(o_diff, o_four, o_mem, h, w_out, ln_g, ln_b, wr_hi, wr_lo, b_r):
    row = lambda r: (r, 0)
    const = lambda r: (0, 0)
    return pl.pallas_call(
        _outproj_router_kernel,
        grid=(N_TOK // ROW_TILE,),
        in_specs=[
            pl.BlockSpec((ROW_TILE, DIFF_WIDTH), row),
            pl.BlockSpec((ROW_TILE, FOURIER_WIDTH), row),
            pl.BlockSpec((ROW_TILE, MEM_WIDTH), row),
            pl.BlockSpec((ROW_TILE, D_MODEL), row),
            pl.BlockSpec((D_MODEL, D_MODEL), const),
            pl.BlockSpec((1, D_MODEL), const),
            pl.BlockSpec((1, D_MODEL), const),
            pl.BlockSpec((D_MODEL, ROUTER_LANES), const),
            pl.BlockSpec((D_MODEL, ROUTER_LANES), const),
            pl.BlockSpec((1, ROUTER_LANES), const),
        ],
        out_specs=[
            pl.BlockSpec((ROW_TILE, D_MODEL), row),
            pl.BlockSpec((ROW_TILE, D_MODEL), row),
            pl.BlockSpec((ROW_TILE, ROUTER_LANES), row),
        ],
        out_shape=[
            jax.ShapeDtypeStruct((N_TOK, D_MODEL), F32),
            jax.ShapeDtypeStruct((N_TOK, D_MODEL), BF16),
            jax.ShapeDtypeStruct((N_TOK, ROUTER_LANES), F32),
        ],
        compiler_params=_vmem(40 << 20),
        name="outproj_router",
    )(o_diff, o_four, o_mem, h, w_out, ln_g, ln_b, wr_hi, wr_lo, b_r)


def _moe_kernel(x_ref, h1_ref, gates_ref, w1_ref, w3_ref, w2_ref, g_ref, b_ref, o_ref, acc_ref):
    e = pl.program_id(1)

    @pl.when(e == 0)
    def _():
        acc_ref[...] = jnp.zeros_like(acc_ref)

    x = x_ref[...]
    a = jnp.dot(x, w1_ref[0], preferred_element_type=F32)
    b = jnp.dot(x, w3_ref[0], preferred_element_type=F32)
    gates = gates_ref[...]
    lane = lax.broadcasted_iota(jnp.int32, gates.shape, 1)
    ge = jnp.sum(jnp.where(lane == e, gates, 0.0), axis=1, keepdims=True)
    hid = (a * jax.nn.sigmoid(a) * b * ge).astype(BF16)
    acc_ref[...] += jnp.dot(hid, w2_ref[0], preferred_element_type=F32)

    @pl.when(e == N_EXPERTS - 1)
    def _():
        o_ref[...] = _layer_norm(DEEPNORM_ALPHA * h1_ref[...] + acc_ref[...], g_ref[...], b_ref[...])


def _moe(h1b, h1, gates, w1, w3, w2, ln_g, ln_b):
    row = lambda i, e: (i, 0)
    const = lambda i, e: (0, 0)
    return pl.pallas_call(
        _moe_kernel,
        grid=(N_TOK // MOE_TILE, N_EXPERTS),
        in_specs=[
            pl.BlockSpec((MOE_TILE, D_MODEL), row),
            pl.BlockSpec((MOE_TILE, D_MODEL), row),
            pl.BlockSpec((MOE_TILE, ROUTER_LANES), row),
            pl.BlockSpec((1, D_MODEL, D_EXPERT), lambda i, e: (e, 0, 0)),
            pl.BlockSpec((1, D_MODEL, D_EXPERT), lambda i, e: (e, 0, 0)),
            pl.BlockSpec((1, D_EXPERT, D_MODEL), lambda i, e: (e, 0, 0)),
            pl.BlockSpec((1, D_MODEL), const),
            pl.BlockSpec((1, D_MODEL), const),
        ],
        out_specs=pl.BlockSpec((MOE_TILE, D_MODEL), row),
        out_shape=jax.ShapeDtypeStruct((N_TOK, D_MODEL), F32),
        scratch_shapes=[pltpu.VMEM((MOE_TILE, D_MODEL), F32)],
        compiler_params=_vmem(48 << 20),
        name="moe_experts",
    )(h1b, h1, gates, w1, w3, w2, ln_g, ln_b)


def kernel(x, mem, ln0_g, ln0_b, rel_bias, w_in, w_mem_kv, w_fourier, lambda_q1, lambda_k1, lambda_q2,
           lambda_k2, subln_g, w_out, ln1_g, ln1_b, w_group, b_group, w_router, b_router, w1, w3, w2,
           ln2_g, ln2_b):
    l = 0
    x2 = x.reshape(N_TOK, D_MODEL)
    wi = w_in[l]
    w_qk = wi[:, :2 * DIFF_WIDTH].astype(BF16)
    w_vt = wi[:, 2 * DIFF_WIDTH:3 * DIFF_WIDTH].T.astype(BF16)
    w_fm = wi[:, 3 * DIFF_WIDTH:].astype(BF16)
    h, q, k, vt, f_in, mq = _ln0_inproj(x2, ln0_g.reshape(1, -1), ln0_b.reshape(1, -1), w_qk, w_vt, w_fm)

    T = ATT_TILE
    kk = jnp.arange(T, dtype=jnp.int32)[:, None]
    qq = jnp.arange(T, dtype=jnp.int32)[None, :]
    buckets = jnp.stack([_t5_bucket(d * T + kk - qq) for d in range(-2, 3)])
    lam_vecs = jnp.stack([lambda_q1[l], lambda_k1[l], lambda_q2[l], lambda_k2[l]]).astype(F32)
    o_diff = _diff_attention(rel_bias.astype(F32), q, k, vt, buckets, lam_vecs,
                             subln_g[l].astype(F32).reshape(DIFF_VDIM, 1))

    dft, bdc, bds = (jnp.asarray(c) for c in _dft_constants())
    wf = w_fourier[l]
    bdw = jnp.zeros((FOURIER_WIDTH, FOURIER_WIDTH), F32)
    for g in range(FOURIER_GROUPS):
        sl = slice(g * FOURIER_GROUP_DIM, (g + 1) * FOURIER_GROUP_DIM)
        bdw = bdw.at[sl, sl].set(wf[g])
    z = _fourier_chan(f_in, bdc, bds, bdw.astype(BF16))
    o_four = _fourier_seq(dft, z.reshape(BATCH, 2 * SEQ, FOURIER_WIDTH))

    kcat, vcat = _mem_kv(mem.reshape(BATCH * MEM_LEN, D_MODEL), w_mem_kv[l].astype(BF16))
    o_mem = _mem_attention(mq, kcat, vcat)

    w_r = jnp.zeros((D_MODEL, ROUTER_LANES), F32)
    w_r = w_r.at[:, :N_EXPERTS].set(w_router[l].astype(F32))
    w_r = w_r.at[:, GROUP_LANE0:GROUP_LANE0 + N_GROUPS].set(w_group[l].astype(F32))
    wr_hi = w_r.astype(BF16)
    wr_lo = (w_r - wr_hi.astype(F32)).astype(BF16)
    b_r = jnp.zeros((1, ROUTER_LANES), F32)
    b_r = b_r.at[0, :N_EXPERTS].set(b_router[l].astype(F32))
    b_r = b_r.at[0, GROUP_LANE0:GROUP_LANE0 + N_GROUPS].set(b_group[l].astype(F32))
    h1, h1b, gates = _outproj_router(o_diff, o_four, o_mem, h, w_out[l].astype(BF16),
                                     ln1_g[l].reshape(1, -1), ln1_b[l].reshape(1, -1), wr_hi, wr_lo, b_r)

    out = _moe(h1b, h1, gates, w1[l].astype(BF16), w3[l].astype(BF16), w2[l].astype(BF16),
               ln2_g[l].reshape(1, -1), ln2_b[l].reshape(1, -1))
    return out.reshape(BATCH, SEQ, D_MODEL)
```

```python
import functools
import math

import numpy as np
import jax
import jax.numpy as jnp
from jax import lax
from jax.experimental import pallas as pl
from jax.experimental.pallas import tpu as pltpu

F32 = jnp.float32
BF16 = jnp.bfloat16

D_MODEL = 1024
BATCH = 4
SEQ = 4096
N_TOK = BATCH * SEQ
MEM_LEN = 256
MEM_HEADS = 4
MEM_HEAD_DIM = 64
MEM_WIDTH = 256
DIFF_HEADS = 4
DIFF_HEAD_DIM = 64
DIFF_VDIM = 2 * DIFF_HEAD_DIM
VT_ROWS = DIFF_VDIM + 16
DIFF_WIDTH = 512
FOURIER_GROUPS = 4
FOURIER_GROUP_DIM = 64
FOURIER_WIDTH = 256
N_BUCKETS = 32
MAX_DISTANCE = 128
N_GROUPS = 4
EXPERTS_PER_GROUP = 8
N_EXPERTS = 32
D_EXPERT = 256
LN_EPS = 1e-5
DEEPNORM_ALPHA = 2.0 ** 0.25
LAM_INIT = 0.8 - 0.6 * math.exp(-0.3 * 0)
LOG2E = 1.4426950408889634

ROW_TILE = 512
ATT_TILE = 256
MOE_TILE = 1024
DFT_ROW_TILE = 512
ROUTER_LANES = 128
GROUP_LANE0 = 32
NEG_BIG = -1e30

_NT = (((1,), (1,)), ((), ()))


def _vmem(nbytes):
    return pltpu.CompilerParams(vmem_limit_bytes=int(nbytes))


def _layer_norm(x, g, b):
    mu = jnp.mean(x, axis=-1, keepdims=True)
    xc = x - mu
    var = jnp.mean(xc * xc, axis=-1, keepdims=True)
    return xc * lax.rsqrt(var + LN_EPS) * g + b


def _t5_bucket(rel):
    nb = N_BUCKETS // 2
    max_exact = nb // 2
    ret = (rel > 0).astype(jnp.int32) * nb
    n = jnp.abs(rel)
    nf = jnp.maximum(n, 1).astype(F32)
    large = max_exact + (jnp.log(nf / max_exact) / math.log(MAX_DISTANCE / max_exact)
                         * (nb - max_exact)).astype(jnp.int32)
    large = jnp.minimum(large, nb - 1)
    return ret + jnp.where(n < max_exact, n, large)


def _ln0_inproj_kernel(x_ref, g_ref, b_ref, wqk_ref, wvt_ref, wfm_ref,
                       h_ref, q_ref, k_ref, vt_ref, f_ref, mq_ref):
    h = _layer_norm(x_ref[...], g_ref[...], b_ref[...])
    h_ref[...] = h
    hb = h.astype(BF16)
    qk = jnp.dot(hb, wqk_ref[...], preferred_element_type=F32)
    q_ref[...] = (qk[:, :DIFF_WIDTH] * (DIFF_HEAD_DIM ** -0.5 * LOG2E)).astype(BF16)
    k_ref[...] = qk[:, DIFF_WIDTH:].astype(BF16)
    vt = lax.dot_general(wvt_ref[...], hb, _NT, preferred_element_type=F32)
    ones = jnp.ones((VT_ROWS - DIFF_VDIM, ROW_TILE), BF16)
    for hh in range(DIFF_HEADS):
        vt_ref[0, hh, :DIFF_VDIM, :] = vt[hh * DIFF_VDIM:(hh + 1) * DIFF_VDIM, :].astype(BF16)
        vt_ref[0, hh, DIFF_VDIM:, :] = ones
    fm = jnp.dot(hb, wfm_ref[...], preferred_element_type=F32)
    f_ref[...] = fm[:, :FOURIER_WIDTH].astype(BF16)
    mq_ref[...] = (fm[:, FOURIER_WIDTH:] * MEM_HEAD_DIM ** -0.5).astype(BF16)


def _ln0_inproj(x2, ln_g, ln_b, w_qk, w_vt, w_fm):
    rt = SEQ // ROW_TILE
    row = lambda r: (r, 0)
    const = lambda r: (0, 0)
    return pl.pallas_call(
        _ln0_inproj_kernel,
        grid=(N_TOK // ROW_TILE,),
        in_specs=[
            pl.BlockSpec((ROW_TILE, D_MODEL), row),
            pl.BlockSpec((1, D_MODEL), const),
            pl.BlockSpec((1, D_MODEL), const),
            pl.BlockSpec((D_MODEL, 2 * DIFF_WIDTH), const),
            pl.BlockSpec((DIFF_WIDTH, D_MODEL), const),
            pl.BlockSpec((D_MODEL, FOURIER_WIDTH + MEM_WIDTH), const),
        ],
        out_specs=[
            pl.BlockSpec((ROW_TILE, D_MODEL), row),
            pl.BlockSpec((ROW_TILE, DIFF_WIDTH), row),
            pl.BlockSpec((ROW_TILE, DIFF_WIDTH), row),
            pl.BlockSpec((1, DIFF_HEADS, VT_ROWS, ROW_TILE), lambda r: (r // rt, 0, 0, r % rt)),
            pl.BlockSpec((ROW_TILE, FOURIER_WIDTH), row),
            pl.BlockSpec((ROW_TILE, MEM_WIDTH), row),
        ],
        out_shape=[
            jax.ShapeDtypeStruct((N_TOK, D_MODEL), F32),
            jax.ShapeDtypeStruct((N_TOK, DIFF_WIDTH), BF16),
            jax.ShapeDtypeStruct((N_TOK, DIFF_WIDTH), BF16),
            jax.ShapeDtypeStruct((BATCH, DIFF_HEADS, VT_ROWS, SEQ), BF16),
            jax.ShapeDtypeStruct((N_TOK, FOURIER_WIDTH), BF16),
            jax.ShapeDtypeStruct((N_TOK, MEM_WIDTH), BF16),
        ],
        compiler_params=_vmem(48 << 20),
        name="ln0_inproj",
    )(x2, ln_g, ln_b, w_qk, w_vt, w_fm)


def _diff_attn_kernel(tbl_ref, far_ref, q_ref, k_ref, vt_ref, bkt_ref, lam_ref, g_ref, o_ref,
                      bias_scr, s0_scr, s1_scr, p0_scr, p1_scr):
    T = ATT_TILE
    nk = SEQ // T
    h = pl.program_id(1)
    i = pl.program_id(2)

    @pl.when(i == 0)
    def _build_bias():
        for d in range(5):
            bk = bkt_ref[d]
            bias = jnp.zeros((T, T), F32)
            for n in range(N_BUCKETS):
                bias = jnp.where(bk == n, tbl_ref[n, h], bias)
            bias_scr[d] = bias * LOG2E

    q = q_ref[...]
    lane = lax.broadcasted_iota(jnp.int32, q.shape, 1)
    zero = jnp.zeros_like(q)
    q_comp = (jnp.where(lane < DIFF_HEAD_DIM, q, zero), jnp.where(lane >= DIFF_HEAD_DIM, q, zero))
    s_scr = (s0_scr, s1_scr)
    p_scr = (p0_scr, p1_scr)

    for c in range(2):
        s_scr[c][...] = lax.dot_general(k_ref[...], q_comp[c], _NT, preferred_element_type=F32)

    js = jnp.clip(i - 1, 0, nk - 3)
    for w in range(3):
        rows = pl.ds(pl.multiple_of((js + w) * T, T), T)
        bias = bias_scr[js + w - i + 2]
        for c in range(2):
            s_scr[c][rows, :] = s_scr[c][rows, :] + bias
    c_left = tbl_ref[far_ref[0], h] * LOG2E
    c_right = tbl_ref[far_ref[1], h] * LOG2E
    side = [jnp.where(j < js, c_left, jnp.where(j > js + 2, c_right, 0.0)) for j in range(nk)]

    m8 = [None, None]
    for j in range(nk):
        for c in range(2):
            t = jnp.max(s_scr[c][j * T:(j + 1) * T, :].reshape(T // 8, 8, T), axis=0) + side[j]
            m8[c] = t if m8[c] is None else jnp.maximum(m8[c], t)
    m = [jnp.max(m8[c], axis=0, keepdims=True) for c in range(2)]

    for j in range(nk):
        for c in range(2):
            x = s_scr[c][j * T:(j + 1) * T, :] + (side[j] - m[c])
            p_scr[c][j * T:(j + 1) * T, :] = jnp.exp2(x.astype(BF16))

    acc = [jnp.dot(vt_ref[0, 0], p_scr[c][...], preferred_element_type=F32) for c in range(2)]
    num = [a[:DIFF_VDIM] for a in acc]
    den = [a[DIFF_VDIM:DIFF_VDIM + 1] for a in acc]

    lam = (jnp.exp(jnp.sum(lam_ref[0:1, :] * lam_ref[1:2, :], axis=1, keepdims=True))
           - jnp.exp(jnp.sum(lam_ref[2:3, :] * lam_ref[3:4, :], axis=1, keepdims=True)) + LAM_INIT)
    o = num[0] * (1.0 / den[0]) - lam * (num[1] * (1.0 / den[1]))
    ms = jnp.mean(o * o, axis=0, keepdims=True)
    o = o * lax.rsqrt(ms + LN_EPS) * g_ref[...] * (1.0 - LAM_INIT)
    o_ref[...] = o.T.astype(BF16)


def _diff_attention(rel_bias, far_buckets, q, k, vt, buckets, lam_vecs, subln_col):
    T = ATT_TILE
    nq = SEQ // T
    return pl.pallas_call(
        _diff_attn_kernel,
        grid=(BATCH, DIFF_HEADS, nq),
        in_specs=[
            pl.BlockSpec(memory_space=pltpu.SMEM),
            pl.BlockSpec(memory_space=pltpu.SMEM),
            pl.BlockSpec((T, DIFF_VDIM), lambda b, h, i: (b * nq + i, h)),
            pl.BlockSpec((SEQ, DIFF_VDIM), lambda b, h, i: (b, h)),
            pl.BlockSpec((1, 1, VT_ROWS, SEQ), lambda b, h, i: (b, h, 0, 0)),
            pl.BlockSpec((5, T, T), lambda b, h, i: (0, 0, 0)),
            pl.BlockSpec((4, DIFF_HEAD_DIM), lambda b, h, i: (0, 0)),
            pl.BlockSpec((DIFF_VDIM, 1), lambda b, h, i: (0, 0)),
        ],
        out_specs=pl.BlockSpec((T, DIFF_VDIM), lambda b, h, i: (b * nq + i, h)),
        out_shape=jax.ShapeDtypeStruct((N_TOK, DIFF_WIDTH), BF16),
        scratch_shapes=[pltpu.VMEM((5, T, T), F32),
                        pltpu.VMEM((SEQ, T), F32), pltpu.VMEM((SEQ, T), F32),
                        pltpu.VMEM((SEQ, T), BF16), pltpu.VMEM((SEQ, T), BF16)],
        compiler_params=_vmem(40 << 20),
        name="diff_attn",
    )(rel_bias, far_buckets, q, k, vt, buckets, lam_vecs, subln_col)


def _fourier_chan_kernel(f_ref, bdc_ref, bds_ref, bdw_ref, z_ref):
    u = f_ref[...]
    w = bdw_ref[...]
    pc = jnp.dot(u, bdc_ref[...], preferred_element_type=F32).astype(BF16)
    ps = jnp.dot(u, bds_ref[...], preferred_element_type=F32).astype(BF16)
    z_ref[0, 0] = jnp.dot(pc, w, preferred_element_type=F32).astype(BF16)
    z_ref[0, 1] = jnp.dot(ps, w, preferred_element_type=F32).astype(BF16)


def _fourier_chan(f_in, bdc, bds, bdw):
    rt = SEQ // ROW_TILE
    const = lambda r: (0, 0)
    return pl.pallas_call(
        _fourier_chan_kernel,
        grid=(N_TOK // ROW_TILE,),
        in_specs=[
            pl.BlockSpec((ROW_TILE, FOURIER_WIDTH), lambda r: (r, 0)),
            pl.BlockSpec((FOURIER_WIDTH, FOURIER_WIDTH), const),
            pl.BlockSpec((FOURIER_WIDTH, FOURIER_WIDTH), const),
            pl.BlockSpec((FOURIER_WIDTH, FOURIER_WIDTH), const),
        ],
        out_specs=pl.BlockSpec((1, 2, ROW_TILE, FOURIER_WIDTH), lambda r: (r // rt, 0, r % rt, 0)),
        out_shape=jax.ShapeDtypeStruct((BATCH, 2, SEQ, FOURIER_WIDTH), BF16),
        name="fourier_chan",
    )(f_in, bdc, bds, bdw)


def _fourier_seq_kernel(dft_ref, z_ref, o_ref):
    o_ref[...] = jnp.dot(dft_ref[...], z_ref[0], preferred_element_type=F32).astype(BF16)


def _fourier_seq(dft, z):
    nr = SEQ // DFT_ROW_TILE
    return pl.pallas_call(
        _fourier_seq_kernel,
        grid=(nr, BATCH),
        in_specs=[
            pl.BlockSpec((DFT_ROW_TILE, 2 * SEQ), lambda r, b: (r, 0)),
            pl.BlockSpec((1, 2 * SEQ, FOURIER_WIDTH), lambda r, b: (b, 0, 0)),
        ],
        out_specs=pl.BlockSpec((DFT_ROW_TILE, FOURIER_WIDTH), lambda r, b: (b * nr + r, 0)),
        out_shape=jax.ShapeDtypeStruct((N_TOK, FOURIER_WIDTH), BF16),
        compiler_params=_vmem(48 << 20),
        name="fourier_seq",
    )(dft, z)


@functools.lru_cache(maxsize=None)
def _dft_constants():
    n = np.arange(SEQ, dtype=np.int64)
    ang = 2.0 * np.pi * ((n[:, None] * n[None, :]) % SEQ).astype(np.float64) / SEQ
    scale = 1.0 / math.sqrt(SEQ * FOURIER_GROUP_DIM)
    dft = np.concatenate([np.cos(ang), -np.sin(ang)], axis=1) * scale
    c = np.arange(FOURIER_GROUP_DIM, dtype=np.int64)
    angc = 2.0 * np.pi * ((c[:, None] * c[None, :]) % FOURIER_GROUP_DIM).astype(np.float64) / FOURIER_GROUP_DIM
    eye = np.eye(FOURIER_GROUPS)
    bdc = np.kron(eye, np.cos(angc))
    bds = np.kron(eye, np.sin(angc))
    return dft.astype(BF16), bdc.astype(BF16), bds.astype(BF16)


def _mem_kv_kernel(mem_ref, w_ref, kcat_ref, vcat_ref):
    kv = jnp.dot(mem_ref[...].astype(BF16), w_ref[...], preferred_element_type=F32)
    mk = kv[:, :MEM_WIDTH]
    mv = kv[:, MEM_WIDTH:]
    lane = lax.broadcasted_iota(jnp.int32, mk.shape, 1)
    for hh in range(MEM_HEADS):
        sel = (lane >= hh * MEM_HEAD_DIM) & (lane < (hh + 1) * MEM_HEAD_DIM)
        kcat_ref[0, hh * MEM_LEN:(hh + 1) * MEM_LEN, :] = jnp.where(sel, mk, 0.0).astype(BF16)
        vcat_ref[0, hh * MEM_LEN:(hh + 1) * MEM_LEN, :] = jnp.where(sel, mv, 0.0).astype(BF16)


def _mem_kv(mem2, w_kv):
    return pl.pallas_call(
        _mem_kv_kernel,
        grid=(BATCH,),
        in_specs=[
            pl.BlockSpec((MEM_LEN, D_MODEL), lambda b: (b, 0)),
            pl.BlockSpec((D_MODEL, 2 * MEM_WIDTH), lambda b: (0, 0)),
        ],
        out_specs=[
            pl.BlockSpec((1, MEM_HEADS * MEM_LEN, MEM_WIDTH), lambda b: (b, 0, 0)),
            pl.BlockSpec((1, MEM_HEADS * MEM_LEN, MEM_WIDTH), lambda b: (b, 0, 0)),
        ],
        out_shape=[
            jax.ShapeDtypeStruct((BATCH, MEM_HEADS * MEM_LEN, MEM_WIDTH), BF16),
            jax.ShapeDtypeStruct((BATCH, MEM_HEADS * MEM_LEN, MEM_WIDTH), BF16),
        ],
        name="mem_kv",
    )(mem2, w_kv)


def _mem_attn_kernel(mq_ref, kcat_ref, vcat_ref, o_ref):
    s = lax.dot_general(mq_ref[...], kcat_ref[0], _NT, preferred_element_type=F32)
    parts = []
    for hh in range(MEM_HEADS):
        sh = s[:, hh * MEM_LEN:(hh + 1) * MEM_LEN]
        p = jnp.exp(sh - jnp.max(sh, axis=1, keepdims=True))
        parts.append((p * (1.0 / jnp.sum(p, axis=1, keepdims=True))).astype(BF16))
    p_all = jnp.concatenate(parts, axis=1)
    o_ref[...] = jnp.dot(p_all, vcat_ref[0], preferred_element_type=F32).astype(BF16)


def _mem_attention(mq, kcat, vcat):
    rt = SEQ // ROW_TILE
    return pl.pallas_call(
        _mem_attn_kernel,
        grid=(N_TOK // ROW_TILE,),
        in_specs=[
            pl.BlockSpec((ROW_TILE, MEM_WIDTH), lambda r: (r, 0)),
            pl.BlockSpec((1, MEM_HEADS * MEM_LEN, MEM_WIDTH), lambda r: (r // rt, 0, 0)),
            pl.BlockSpec((1, MEM_HEADS * MEM_LEN, MEM_WIDTH), lambda r: (r // rt, 0, 0)),
        ],
        out_specs=pl.BlockSpec((ROW_TILE, MEM_WIDTH), lambda r: (r, 0)),
        out_shape=jax.ShapeDtypeStruct((N_TOK, MEM_WIDTH), BF16),
        name="mem_attn",
    )(mq, kcat, vcat)


def _route(logits):
    lane = lax.broadcasted_iota(jnp.int32, logits.shape, 1)
    big = jnp.int32(ROUTER_LANES)
    is_group = (lane >= GROUP_LANE0) & (lane < GROUP_LANE0 + N_GROUPS)
    gl = jnp.where(is_group, logits, -jnp.inf)
    gmax = jnp.max(gl, axis=1, keepdims=True)
    g_sel = jnp.min(jnp.where(gl == gmax, lane, big), axis=1, keepdims=True) - GROUP_LANE0
    g_gate = 1.0 / jnp.sum(jnp.where(is_group, jnp.exp(gl - gmax), 0.0), axis=1, keepdims=True)
    in_group = (lane >= g_sel * EXPERTS_PER_GROUP) & (lane < (g_sel + 1) * EXPERTS_PER_GROUP)
    el = jnp.where(in_group, logits, -jnp.inf)
    v1 = jnp.max(el, axis=1, keepdims=True)
    i1 = jnp.min(jnp.where(el == v1, lane, big), axis=1, keepdims=True)
    el2 = jnp.where(lane == i1, -jnp.inf, el)
    v2 = jnp.max(el2, axis=1, keepdims=True)
    i2 = jnp.min(jnp.where(el2 == v2, lane, big), axis=1, keepdims=True)
    e = jnp.exp(v2 - v1)
    w_first = g_gate / (1.0 + e)
    w_second = g_gate * e / (1.0 + e)
    return jnp.where(lane == i1, w_first, 0.0) + jnp.where(lane == i2, w_second, 0.0)


def _outproj_router_kernel(od_ref, of_ref, om_ref, h_ref, wo_ref, g_ref, b_ref,
                           wr_hi_ref, wr_lo_ref, br_ref, h1_ref, h1b_ref, gates_ref):
    o = jnp.concatenate([od_ref[...], of_ref[...], om_ref[...]], axis=1)
    a = jnp.dot(o, wo_ref[...], preferred_element_type=F32)
    h1 = _layer_norm(DEEPNORM_ALPHA * h_ref[...] + a, g_ref[...], b_ref[...])
    h1_ref[...] = h1
    hi = h1.astype(BF16)
    h1b_ref[...] = hi
    lo = (h1 - hi.astype(F32)).astype(BF16)
    w_hi = wr_hi_ref[...]
    logits = (jnp.dot(hi, w_hi, preferred_element_type=F32)
              + jnp.dot(hi, wr_lo_ref[...], preferred_element_type=F32)
              + jnp.dot(lo, w_hi, preferred_element_type=F32) + br_ref[...])
    gates_ref[...] = _route(logits)


def _outproj_router(o_diff, o_four, o_mem, h, w_out, ln_g, ln_b, wr_hi, wr_lo, b_r):
    row = lambda r: (r, 0)
    const = lambda r: (0, 0)
    return pl.pallas_call(
        _outproj_router_kernel,
        grid=(N_TOK // ROW_TILE,),
        in_specs=[
            pl.BlockSpec((ROW_TILE, DIFF_WIDTH), row),
            pl.BlockSpec((ROW_TILE, FOURIER_WIDTH), row),
            pl.BlockSpec((ROW_TILE, MEM_WIDTH), row),
            pl.BlockSpec((ROW_TILE, D_MODEL), row),
            pl.BlockSpec((D_MODEL, D_MODEL), const),
            pl.BlockSpec((1, D_MODEL), const),
            pl.BlockSpec((1, D_MODEL), const),
            pl.BlockSpec((D_MODEL, ROUTER_LANES), const),
            pl.BlockSpec((D_MODEL, ROUTER_LANES), const),
            pl.BlockSpec((1, ROUTER_LANES), const),
        ],
        out_specs=[
            pl.BlockSpec((ROW_TILE, D_MODEL), row),
            pl.BlockSpec((ROW_TILE, D_MODEL), row),
            pl.BlockSpec((ROW_TILE, ROUTER_LANES), row),
        ],
        out_shape=[
            jax.ShapeDtypeStruct((N_TOK, D_MODEL), F32),
            jax.ShapeDtypeStruct((N_TOK, D_MODEL), BF16),
            jax.ShapeDtypeStruct((N_TOK, ROUTER_LANES), F32),
        ],
        compiler_params=_vmem(40 << 20),
        name="outproj_router",
    )(o_diff, o_four, o_mem, h, w_out, ln_g, ln_b, wr_hi, wr_lo, b_r)


def _moe_kernel(x_ref, h1_ref, gates_ref, w1_ref, w3_ref, w2_ref, g_ref, b_ref, o_ref, acc_ref):
    e = pl.program_id(1)

    @pl.when(e == 0)
    def _():
        acc_ref[...] = jnp.zeros_like(acc_ref)

    x = x_ref[...]
    a = jnp.dot(x, w1_ref[0], preferred_element_type=F32)
    b = jnp.dot(x, w3_ref[0], preferred_element_type=F32)
    gates = gates_ref[...]
    lane = lax.broadcasted_iota(jnp.int32, gates.shape, 1)
    ge = jnp.sum(jnp.where(lane == e, gates, 0.0), axis=1, keepdims=True)
    hid = (a * jax.nn.sigmoid(a) * b * ge).astype(BF16)
    acc_ref[...] += jnp.dot(hid, w2_ref[0], preferred_element_type=F32)

    @pl.when(e == N_EXPERTS - 1)
    def _():
        o_ref[...] = _layer_norm(DEEPNORM_ALPHA * h1_ref[...] + acc_ref[...], g_ref[...], b_ref[...])


def _moe(h1b, h1, gates, w1, w3, w2, ln_g, ln_b):
    row = lambda i, e: (i, 0)
    const = lambda i, e: (0, 0)
    return pl.pallas_call(
        _moe_kernel,
        grid=(N_TOK // MOE_TILE, N_EXPERTS),
        in_specs=[
            pl.BlockSpec((MOE_TILE, D_MODEL), row),
            pl.BlockSpec((MOE_TILE, D_MODEL), row),
            pl.BlockSpec((MOE_TILE, ROUTER_LANES), row),
            pl.BlockSpec((1, D_MODEL, D_EXPERT), lambda i, e: (e, 0, 0)),
            pl.BlockSpec((1, D_MODEL, D_EXPERT), lambda i, e: (e, 0, 0)),
            pl.BlockSpec((1, D_EXPERT, D_MODEL), lambda i, e: (e, 0, 0)),
            pl.BlockSpec((1, D_MODEL), const),
            pl.BlockSpec((1, D_MODEL), const),
        ],
        out_specs=pl.BlockSpec((MOE_TILE, D_MODEL), row),
        out_shape=jax.ShapeDtypeStruct((N_TOK, D_MODEL), F32),
        scratch_shapes=[pltpu.VMEM((MOE_TILE, D_MODEL), F32)],
        compiler_params=_vmem(48 << 20),
        name="moe_experts",
    )(h1b, h1, gates, w1, w3, w2, ln_g, ln_b)


def kernel(x, mem, ln0_g, ln0_b, rel_bias, w_in, w_mem_kv, w_fourier, lambda_q1, lambda_k1, lambda_q2,
           lambda_k2, subln_g, w_out, ln1_g, ln1_b, w_group, b_group, w_router, b_router, w1, w3, w2,
           ln2_g, ln2_b):
    l = 0
    x2 = x.reshape(N_TOK, D_MODEL)
    wi = w_in[l]
    w_qk = wi[:, :2 * DIFF_WIDTH].astype(BF16)
    w_vt = wi[:, 2 * DIFF_WIDTH:3 * DIFF_WIDTH].T.astype(BF16)
    w_fm = wi[:, 3 * DIFF_WIDTH:].astype(BF16)
    h, q, k, vt, f_in, mq = _ln0_inproj(x2, ln0_g.reshape(1, -1), ln0_b.reshape(1, -1), w_qk, w_vt, w_fm)

    T = ATT_TILE
    kk = jnp.arange(T, dtype=jnp.int32)[:, None]
    qq = jnp.arange(T, dtype=jnp.int32)[None, :]
    buckets = jnp.stack([_t5_bucket(d * T + kk - qq) for d in range(-2, 3)])
    lam_vecs = jnp.stack([lambda_q1[l], lambda_k1[l], lambda_q2[l], lambda_k2[l]]).astype(F32)
    far_buckets = _t5_bucket(jnp.array([-(T + 1), T + 1], jnp.int32))
    o_diff = _diff_attention(rel_bias.astype(F32), far_buckets, q, k, vt, buckets, lam_vecs,
                             subln_g[l].astype(F32).reshape(DIFF_VDIM, 1))

    dft, bdc, bds = (jnp.asarray(c) for c in _dft_constants())
    wf = w_fourier[l]
    bdw = jnp.zeros((FOURIER_WIDTH, FOURIER_WIDTH), F32)
    for g in range(FOURIER_GROUPS):
        sl = slice(g * FOURIER_GROUP_DIM, (g + 1) * FOURIER_GROUP_DIM)
        bdw = bdw.at[sl, sl].set(wf[g])
    z = _fourier_chan(f_in, bdc, bds, bdw.astype(BF16))
    o_four = _fourier_seq(dft, z.reshape(BATCH, 2 * SEQ, FOURIER_WIDTH))

    kcat, vcat = _mem_kv(mem.reshape(BATCH * MEM_LEN, D_MODEL), w_mem_kv[l].astype(BF16))
    o_mem = _mem_attention(mq, kcat, vcat)

    w_r = jnp.zeros((D_MODEL, ROUTER_LANES), F32)
    w_r = w_r.at[:, :N_EXPERTS].set(w_router[l].astype(F32))
    w_r = w_r.at[:, GROUP_LANE0:GROUP_LANE0 + N_GROUPS].set(w_group[l].astype(F32))
    wr_hi = w_r.astype(BF16)
    wr_lo = (w_r - wr_hi.astype(F32)).astype(BF16)
    b_r = jnp.zeros((1, ROUTER_LANES), F32)
    b_r = b_r.at[0, :N_EXPERTS].set(b_router[l].astype(F32))
    b_r = b_r.at[0, GROUP_LANE0:GROUP_LANE0 + N_GROUPS].set(b_group[l].astype(F32))
    h1, h1b, gates = _outproj_router(o_diff, o_four, o_mem, h, w_out[l].astype(BF16),
                                     ln1_g[l].reshape(1, -1), ln1_b[l].reshape(1, -1), wr_hi, wr_lo, b_r)

    out = _moe(h1b, h1, gates, w1[l].astype(BF16), w3[l].astype(BF16), w2[l].astype(BF16),
               ln2_g[l].reshape(1, -1), ln2_b[l].reshape(1, -1))
    return out.reshape(BATCH, SEQ, D_MODEL)
```

```python
import functools
import math

import numpy as np
import jax
import jax.numpy as jnp
from jax import lax
from jax.experimental import pallas as pl
from jax.experimental.pallas import tpu as pltpu

F32 = jnp.float32
BF16 = jnp.bfloat16

D_MODEL = 1024
BATCH = 4
SEQ = 4096
N_TOK = BATCH * SEQ
MEM_LEN = 256
MEM_HEADS = 4
MEM_HEAD_DIM = 64
MEM_WIDTH = 256
DIFF_HEADS = 4
DIFF_HEAD_DIM = 64
DIFF_VDIM = 2 * DIFF_HEAD_DIM
VT_ROWS = DIFF_VDIM + 16
DIFF_WIDTH = 512
FOURIER_GROUPS = 4
FOURIER_GROUP_DIM = 64
FOURIER_WIDTH = 256
N_BUCKETS = 32
MAX_DISTANCE = 128
N_GROUPS = 4
EXPERTS_PER_GROUP = 8
N_EXPERTS = 32
D_EXPERT = 256
LN_EPS = 1e-5
DEEPNORM_ALPHA = 2.0 ** 0.25
LAM_INIT = 0.8 - 0.6 * math.exp(-0.3 * 0)
LOG2E = 1.4426950408889634

ROW_TILE = 512
ATT_TILE = 256
MOE_TILE = 512
N_MOE_TILES = 2 * N_TOK // MOE_TILE + N_EXPERTS
N_SLOTS = N_MOE_TILES * MOE_TILE
ROW_SLABS = D_MODEL // 128
DISPATCH_TILE = 512
COMBINE_TILE = 256
DFT_ROW_TILE = 512
ROUTER_LANES = 128
GROUP_LANE0 = 32
NEG_BIG = -1e30

_NT = (((1,), (1,)), ((), ()))


def _vmem(nbytes):
    return pltpu.CompilerParams(vmem_limit_bytes=int(nbytes))


def _layer_norm(x, g, b):
    mu = jnp.mean(x, axis=-1, keepdims=True)
    xc = x - mu
    var = jnp.mean(xc * xc, axis=-1, keepdims=True)
    return xc * lax.rsqrt(var + LN_EPS) * g + b


def _t5_bucket(rel):
    nb = N_BUCKETS // 2
    max_exact = nb // 2
    ret = (rel > 0).astype(jnp.int32) * nb
    n = jnp.abs(rel)
    nf = jnp.maximum(n, 1).astype(F32)
    large = max_exact + (jnp.log(nf / max_exact) / math.log(MAX_DISTANCE / max_exact)
                         * (nb - max_exact)).astype(jnp.int32)
    large = jnp.minimum(large, nb - 1)
    return ret + jnp.where(n < max_exact, n, large)


def _ln0_inproj_kernel(x_ref, g_ref, b_ref, wqk_ref, wvt_ref, wfm_ref,
                       h_ref, q_ref, k_ref, vt_ref, f_ref, mq_ref):
    h = _layer_norm(x_ref[...], g_ref[...], b_ref[...])
    h_ref[...] = h
    hb = h.astype(BF16)
    qk = jnp.dot(hb, wqk_ref[...], preferred_element_type=F32)
    q_ref[...] = (qk[:, :DIFF_WIDTH] * (DIFF_HEAD_DIM ** -0.5 * LOG2E)).astype(BF16)
    k_ref[...] = qk[:, DIFF_WIDTH:].astype(BF16)
    vt = lax.dot_general(wvt_ref[...], hb, _NT, preferred_element_type=F32)
    ones = jnp.ones((VT_ROWS - DIFF_VDIM, ROW_TILE), BF16)
    for hh in range(DIFF_HEADS):
        vt_ref[0, hh, :DIFF_VDIM, :] = vt[hh * DIFF_VDIM:(hh + 1) * DIFF_VDIM, :].astype(BF16)
        vt_ref[0, hh, DIFF_VDIM:, :] = ones
    fm = jnp.dot(hb, wfm_ref[...], preferred_element_type=F32)
    f_ref[...] = fm[:, :FOURIER_WIDTH].astype(BF16)
    mq_ref[...] = (fm[:, FOURIER_WIDTH:] * MEM_HEAD_DIM ** -0.5).astype(BF16)


def _ln0_inproj(x2, ln_g, ln_b, w_qk, w_vt, w_fm):
    rt = SEQ // ROW_TILE
    row = lambda r: (r, 0)
    const = lambda r: (0, 0)
    return pl.pallas_call(
        _ln0_inproj_kernel,
        grid=(N_TOK // ROW_TILE,),
        in_specs=[
            pl.BlockSpec((ROW_TILE, D_MODEL), row),
            pl.BlockSpec((1, D_MODEL), const),
            pl.BlockSpec((1, D_MODEL), const),
            pl.BlockSpec((D_MODEL, 2 * DIFF_WIDTH), const),
            pl.BlockSpec((DIFF_WIDTH, D_MODEL), const),
            pl.BlockSpec((D_MODEL, FOURIER_WIDTH + MEM_WIDTH), const),
        ],
        out_specs=[
            pl.BlockSpec((ROW_TILE, D_MODEL), row),
            pl.BlockSpec((ROW_TILE, DIFF_WIDTH), row),
            pl.BlockSpec((ROW_TILE, DIFF_WIDTH), row),
            pl.BlockSpec((1, DIFF_HEADS, VT_ROWS, ROW_TILE), lambda r: (r // rt, 0, 0, r % rt)),
            pl.BlockSpec((ROW_TILE, FOURIER_WIDTH), row),
            pl.BlockSpec((ROW_TILE, MEM_WIDTH), row),
        ],
        out_shape=[
            jax.ShapeDtypeStruct((N_TOK, D_MODEL), F32),
            jax.ShapeDtypeStruct((N_TOK, DIFF_WIDTH), BF16),
            jax.ShapeDtypeStruct((N_TOK, DIFF_WIDTH), BF16),
            jax.ShapeDtypeStruct((BATCH, DIFF_HEADS, VT_ROWS, SEQ), BF16),
            jax.ShapeDtypeStruct((N_TOK, FOURIER_WIDTH), BF16),
            jax.ShapeDtypeStruct((N_TOK, MEM_WIDTH), BF16),
        ],
        compiler_params=_vmem(48 << 20),
        name="ln0_inproj",
    )(x2, ln_g, ln_b, w_qk, w_vt, w_fm)


def _diff_attn_kernel(tbl_ref, far_ref, q_ref, k_ref, vt_ref, bkt_ref, lam_ref, g_ref, o_ref,
                      bias_scr, s0_scr, s1_scr, p0_scr, p1_scr):
    T = ATT_TILE
    nk = SEQ // T
    h = pl.program_id(1)
    i = pl.program_id(2)

    @pl.when(i == 0)
    def _build_bias():
        for d in range(5):
            bk = bkt_ref[d]
            bias = jnp.zeros((T, T), F32)
            for n in range(N_BUCKETS):
                bias = jnp.where(bk == n, tbl_ref[n, h], bias)
            bias_scr[d] = bias * LOG2E

    q = q_ref[...]
    lane = lax.broadcasted_iota(jnp.int32, q.shape, 1)
    zero = jnp.zeros_like(q)
    q_comp = (jnp.where(lane < DIFF_HEAD_DIM, q, zero), jnp.where(lane >= DIFF_HEAD_DIM, q, zero))
    s_scr = (s0_scr, s1_scr)
    p_scr = (p0_scr, p1_scr)

    for c in range(2):
        s_scr[c][...] = lax.dot_general(k_ref[...], q_comp[c], _NT, preferred_element_type=F32)

    js = jnp.clip(i - 1, 0, nk - 3)
    for w in range(3):
        rows = pl.ds(pl.multiple_of((js + w) * T, T), T)
        bias = bias_scr[js + w - i + 2]
        for c in range(2):
            s_scr[c][rows, :] = s_scr[c][rows, :] + bias
    c_left = tbl_ref[far_ref[0], h] * LOG2E
    c_right = tbl_ref[far_ref[1], h] * LOG2E
    side = [jnp.where(j < js, c_left, jnp.where(j > js + 2, c_right, 0.0)) for j in range(nk)]

    m8 = [None, None]
    for j in range(nk):
        for c in range(2):
            t = jnp.max(s_scr[c][j * T:(j + 1) * T, :].reshape(T // 8, 8, T), axis=0) + side[j]
            m8[c] = t if m8[c] is None else jnp.maximum(m8[c], t)
    m = [jnp.max(m8[c], axis=0, keepdims=True) for c in range(2)]

    for j in range(nk):
        for c in range(2):
            x = s_scr[c][j * T:(j + 1) * T, :] + (side[j] - m[c])
            p_scr[c][j * T:(j + 1) * T, :] = jnp.exp2(x.astype(BF16))

    acc = [jnp.dot(vt_ref[0, 0], p_scr[c][...], preferred_element_type=F32) for c in range(2)]
    num = [a[:DIFF_VDIM] for a in acc]
    den = [a[DIFF_VDIM:DIFF_VDIM + 1] for a in acc]

    lam = (jnp.exp(jnp.sum(lam_ref[0:1, :] * lam_ref[1:2, :], axis=1, keepdims=True))
           - jnp.exp(jnp.sum(lam_ref[2:3, :] * lam_ref[3:4, :], axis=1, keepdims=True)) + LAM_INIT)
    o = num[0] * (1.0 / den[0]) - lam * (num[1] * (1.0 / den[1]))
    ms = jnp.mean(o * o, axis=0, keepdims=True)
    o = o * lax.rsqrt(ms + LN_EPS) * g_ref[...] * (1.0 - LAM_INIT)
    o_ref[...] = o.T.astype(BF16)


def _diff_attention(rel_bias, far_buckets, q, k, vt, buckets, lam_vecs, subln_col):
    T = ATT_TILE
    nq = SEQ // T
    return pl.pallas_call(
        _diff_attn_kernel,
        grid=(BATCH, DIFF_HEADS, nq),
        in_specs=[
            pl.BlockSpec(memory_space=pltpu.SMEM),
            pl.BlockSpec(memory_space=pltpu.SMEM),
            pl.BlockSpec((T, DIFF_VDIM), lambda b, h, i: (b * nq + i, h)),
            pl.BlockSpec((SEQ, DIFF_VDIM), lambda b, h, i: (b, h)),
            pl.BlockSpec((1, 1, VT_ROWS, SEQ), lambda b, h, i: (b, h, 0, 0)),
            pl.BlockSpec((5, T, T), lambda b, h, i: (0, 0, 0)),
            pl.BlockSpec((4, DIFF_HEAD_DIM), lambda b, h, i: (0, 0)),
            pl.BlockSpec((DIFF_VDIM, 1), lambda b, h, i: (0, 0)),
        ],
        out_specs=pl.BlockSpec((T, DIFF_VDIM), lambda b, h, i: (b * nq + i, h)),
        out_shape=jax.ShapeDtypeStruct((N_TOK, DIFF_WIDTH), BF16),
        scratch_shapes=[pltpu.VMEM((5, T, T), F32),
                        pltpu.VMEM((SEQ, T), F32), pltpu.VMEM((SEQ, T), F32),
                        pltpu.VMEM((SEQ, T), BF16), pltpu.VMEM((SEQ, T), BF16)],
        compiler_params=_vmem(40 << 20),
        name="diff_attn",
    )(rel_bias, far_buckets, q, k, vt, buckets, lam_vecs, subln_col)


def _fourier_chan_kernel(f_ref, bdc_ref, bds_ref, bdw_ref, z_ref):
    u = f_ref[...]
    w = bdw_ref[...]
    pc = jnp.dot(u, bdc_ref[...], preferred_element_type=F32).astype(BF16)
    ps = jnp.dot(u, bds_ref[...], preferred_element_type=F32).astype(BF16)
    z_ref[0, 0] = jnp.dot(pc, w, preferred_element_type=F32).astype(BF16)
    z_ref[0, 1] = jnp.dot(ps, w, preferred_element_type=F32).astype(BF16)


def _fourier_chan(f_in, bdc, bds, bdw):
    rt = SEQ // ROW_TILE
    const = lambda r: (0, 0)
    return pl.pallas_call(
        _fourier_chan_kernel,
        grid=(N_TOK // ROW_TILE,),
        in_specs=[
            pl.BlockSpec((ROW_TILE, FOURIER_WIDTH), lambda r: (r, 0)),
            pl.BlockSpec((FOURIER_WIDTH, FOURIER_WIDTH), const),
            pl.BlockSpec((FOURIER_WIDTH, FOURIER_WIDTH), const),
            pl.BlockSpec((FOURIER_WIDTH, FOURIER_WIDTH), const),
        ],
        out_specs=pl.BlockSpec((1, 2, ROW_TILE, FOURIER_WIDTH), lambda r: (r // rt, 0, r % rt, 0)),
        out_shape=jax.ShapeDtypeStruct((BATCH, 2, SEQ, FOURIER_WIDTH), BF16),
        name="fourier_chan",
    )(f_in, bdc, bds, bdw)


def _fourier_seq_kernel(dft_ref, z_ref, o_ref):
    o_ref[...] = jnp.dot(dft_ref[...], z_ref[0], preferred_element_type=F32).astype(BF16)


def _fourier_seq(dft, z):
    nr = SEQ // DFT_ROW_TILE
    return pl.pallas_call(
        _fourier_seq_kernel,
        grid=(nr, BATCH),
        in_specs=[
            pl.BlockSpec((DFT_ROW_TILE, 2 * SEQ), lambda r, b: (r, 0)),
            pl.BlockSpec((1, 2 * SEQ, FOURIER_WIDTH), lambda r, b: (b, 0, 0)),
        ],
        out_specs=pl.BlockSpec((DFT_ROW_TILE, FOURIER_WIDTH), lambda r, b: (b * nr + r, 0)),
        out_shape=jax.ShapeDtypeStruct((N_TOK, FOURIER_WIDTH), BF16),
        compiler_params=_vmem(48 << 20),
        name="fourier_seq",
    )(dft, z)


@functools.lru_cache(maxsize=None)
def _dft_constants():
    n = np.arange(SEQ, dtype=np.int64)
    ang = 2.0 * np.pi * ((n[:, None] * n[None, :]) % SEQ).astype(np.float64) / SEQ
    scale = 1.0 / math.sqrt(SEQ * FOURIER_GROUP_DIM)
    dft = np.concatenate([np.cos(ang), -np.sin(ang)], axis=1) * scale
    c = np.arange(FOURIER_GROUP_DIM, dtype=np.int64)
    angc = 2.0 * np.pi * ((c[:, None] * c[None, :]) % FOURIER_GROUP_DIM).astype(np.float64) / FOURIER_GROUP_DIM
    eye = np.eye(FOURIER_GROUPS)
    bdc = np.kron(eye, np.cos(angc))
    bds = np.kron(eye, np.sin(angc))
    return dft.astype(BF16), bdc.astype(BF16), bds.astype(BF16)


def _mem_kv_kernel(mem_ref, w_ref, kcat_ref, vcat_ref):
    kv = jnp.dot(mem_ref[...].astype(BF16), w_ref[...], preferred_element_type=F32)
    mk = kv[:, :MEM_WIDTH]
    mv = kv[:, MEM_WIDTH:]
    lane = lax.broadcasted_iota(jnp.int32, mk.shape, 1)
    for hh in range(MEM_HEADS):
        sel = (lane >= hh * MEM_HEAD_DIM) & (lane < (hh + 1) * MEM_HEAD_DIM)
        kcat_ref[0, hh * MEM_LEN:(hh + 1) * MEM_LEN, :] = jnp.where(sel, mk, 0.0).astype(BF16)
        vcat_ref[0, hh * MEM_LEN:(hh + 1) * MEM_LEN, :] = jnp.where(sel, mv, 0.0).astype(BF16)


def _mem_kv(mem2, w_kv):
    return pl.pallas_call(
        _mem_kv_kernel,
        grid=(BATCH,),
        in_specs=[
            pl.BlockSpec((MEM_LEN, D_MODEL), lambda b: (b, 0)),
            pl.BlockSpec((D_MODEL, 2 * MEM_WIDTH), lambda b: (0, 0)),
        ],
        out_specs=[
            pl.BlockSpec((1, MEM_HEADS * MEM_LEN, MEM_WIDTH), lambda b: (b, 0, 0)),
            pl.BlockSpec((1, MEM_HEADS * MEM_LEN, MEM_WIDTH), lambda b: (b, 0, 0)),
        ],
        out_shape=[
            jax.ShapeDtypeStruct((BATCH, MEM_HEADS * MEM_LEN, MEM_WIDTH), BF16),
            jax.ShapeDtypeStruct((BATCH, MEM_HEADS * MEM_LEN, MEM_WIDTH), BF16),
        ],
        name="mem_kv",
    )(mem2, w_kv)


def _mem_attn_kernel(mq_ref, kcat_ref, vcat_ref, o_ref):
    s = lax.dot_general(mq_ref[...], kcat_ref[0], _NT, preferred_element_type=F32)
    parts = []
    for hh in range(MEM_HEADS):
        sh = s[:, hh * MEM_LEN:(hh + 1) * MEM_LEN]
        p = jnp.exp(sh - jnp.max(sh, axis=1, keepdims=True))
        parts.append((p * (1.0 / jnp.sum(p, axis=1, keepdims=True))).astype(BF16))
    p_all = jnp.concatenate(parts, axis=1)
    o_ref[...] = jnp.dot(p_all, vcat_ref[0], preferred_element_type=F32).astype(BF16)


def _mem_attention(mq, kcat, vcat):
    rt = SEQ // ROW_TILE
    return pl.pallas_call(
        _mem_attn_kernel,
        grid=(N_TOK // ROW_TILE,),
        in_specs=[
            pl.BlockSpec((ROW_TILE, MEM_WIDTH), lambda r: (r, 0)),
            pl.BlockSpec((1, MEM_HEADS * MEM_LEN, MEM_WIDTH), lambda r: (r // rt, 0, 0)),
            pl.BlockSpec((1, MEM_HEADS * MEM_LEN, MEM_WIDTH), lambda r: (r // rt, 0, 0)),
        ],
        out_specs=pl.BlockSpec((ROW_TILE, MEM_WIDTH), lambda r: (r, 0)),
        out_shape=jax.ShapeDtypeStruct((N_TOK, MEM_WIDTH), BF16),
        name="mem_attn",
    )(mq, kcat, vcat)


def _route(logits):
    lane = lax.broadcasted_iota(jnp.int32, logits.shape, 1)
    big = jnp.int32(ROUTER_LANES)
    is_group = (lane >= GROUP_LANE0) & (lane < GROUP_LANE0 + N_GROUPS)
    gl = jnp.where(is_group, logits, -jnp.inf)
    gmax = jnp.max(gl, axis=1, keepdims=True)
    g_sel = jnp.min(jnp.where(gl == gmax, lane, big), axis=1, keepdims=True) - GROUP_LANE0
    g_gate = 1.0 / jnp.sum(jnp.where(is_group, jnp.exp(gl - gmax), 0.0), axis=1, keepdims=True)
    in_group = (lane >= g_sel * EXPERTS_PER_GROUP) & (lane < (g_sel + 1) * EXPERTS_PER_GROUP)
    el = jnp.where(in_group, logits, -jnp.inf)
    v1 = jnp.max(el, axis=1, keepdims=True)
    i1 = jnp.min(jnp.where(el == v1, lane, big), axis=1, keepdims=True)
    el2 = jnp.where(lane == i1, -jnp.inf, el)
    v2 = jnp.max(el2, axis=1, keepdims=True)
    i2 = jnp.min(jnp.where(el2 == v2, lane, big), axis=1, keepdims=True)
    e = jnp.exp(v2 - v1)
    w_first = g_gate / (1.0 + e)
    w_second = g_gate * e / (1.0 + e)
    return i1, i2, w_first, w_second


def _outproj_router_kernel(od_ref, of_ref, om_ref, h_ref, wo_ref, g_ref, b_ref,
                           wr_hi_ref, wr_lo_ref, br_ref,
                           h1_ref, route_ref, w0_ref, w1_ref, cnt_ref, cnt_scr):
    @pl.when(pl.program_id(0) == 0)
    def _():
        cnt_scr[...] = jnp.zeros_like(cnt_scr)

    o = jnp.concatenate([od_ref[...], of_ref[...], om_ref[...]], axis=1)
    a = jnp.dot(o, wo_ref[...], preferred_element_type=F32)
    h1 = _layer_norm(DEEPNORM_ALPHA * h_ref[...] + a, g_ref[...], b_ref[...])
    for s in range(ROW_SLABS):
        h1_ref[:, s, :] = h1[:, s * 128:(s + 1) * 128]
    hi = h1.astype(BF16)
    lo = (h1 - hi.astype(F32)).astype(BF16)
    w_hi = wr_hi_ref[...]
    logits = (jnp.dot(hi, w_hi, preferred_element_type=F32)
              + jnp.dot(hi, wr_lo_ref[...], preferred_element_type=F32)
              + jnp.dot(lo, w_hi, preferred_element_type=F32) + br_ref[...])
    i1, i2, w_first, w_second = _route(logits)

    lane = lax.broadcasted_iota(jnp.int32, logits.shape, 1)
    onehot = jnp.where(lane == i1, 1.0, jnp.where(lane == i2, 1.0, 0.0))
    r_id = lax.broadcasted_iota(jnp.int32, (ROW_TILE, ROW_TILE), 0)
    c_id = lax.broadcasted_iota(jnp.int32, (ROW_TILE, ROW_TILE), 1)
    tri = jnp.where(r_id > c_id, 1.0, 0.0).astype(BF16)
    before = jnp.dot(tri, onehot.astype(BF16), preferred_element_type=F32) + cnt_scr[...]
    rank1 = jnp.sum(jnp.where(lane == i1, before, 0.0), axis=1, keepdims=True)
    rank2 = jnp.sum(jnp.where(lane == i2, before, 0.0), axis=1, keepdims=True)
    cnt_scr[...] += jnp.sum(onehot, axis=0, keepdims=True)
    cnt_ref[...] = cnt_scr[...]

    packed = jnp.where(lane == 0, i1.astype(F32),
                       jnp.where(lane == 1, i2.astype(F32),
                                 jnp.where(lane == 2, rank1, jnp.where(lane == 3, rank2, 0.0))))
    route_ref[...] = packed.T[:8, :].astype(jnp.int32)
    w0_ref[...] = jnp.broadcast_to(w_first, (ROW_TILE, 128))
    w1_ref[...] = jnp.broadcast_to(w_second, (ROW_TILE, 128))


def _outproj_router(o_diff, o_four, o_mem, h, w_out, ln_g, ln_b, wr_hi, wr_lo, b_r):
    row = lambda r: (r, 0)
    const = lambda r: (0, 0)
    return pl.pallas_call(
        _outproj_router_kernel,
        grid=(N_TOK // ROW_TILE,),
        in_specs=[
            pl.BlockSpec((ROW_TILE, DIFF_WIDTH), row),
            pl.BlockSpec((ROW_TILE, FOURIER_WIDTH), row),
            pl.BlockSpec((ROW_TILE, MEM_WIDTH), row),
            pl.BlockSpec((ROW_TILE, D_MODEL), row),
            pl.BlockSpec((D_MODEL, D_MODEL), const),
            pl.BlockSpec((1, D_MODEL), const),
            pl.BlockSpec((1, D_MODEL), const),
            pl.BlockSpec((D_MODEL, ROUTER_LANES), const),
            pl.BlockSpec((D_MODEL, ROUTER_LANES), const),
            pl.BlockSpec((1, ROUTER_LANES), const),
        ],
        out_specs=[
            pl.BlockSpec((ROW_TILE, ROW_SLABS, 128), lambda r: (r, 0, 0)),
            pl.BlockSpec((8, ROW_TILE), lambda r: (0, r)),
            pl.BlockSpec((ROW_TILE, 128), row),
            pl.BlockSpec((ROW_TILE, 128), row),
            pl.BlockSpec((1, ROUTER_LANES), const),
        ],
        out_shape=[
            jax.ShapeDtypeStruct((N_TOK, ROW_SLABS, 128), F32),
            jax.ShapeDtypeStruct((8, N_TOK), jnp.int32),
            jax.ShapeDtypeStruct((N_TOK, 128), F32),
            jax.ShapeDtypeStruct((N_TOK, 128), F32),
            jax.ShapeDtypeStruct((1, ROUTER_LANES), F32),
        ],
        scratch_shapes=[pltpu.VMEM((1, ROUTER_LANES), F32)],
        compiler_params=_vmem(40 << 20),
        name="outproj_router",
    )(o_diff, o_four, o_mem, h, w_out, ln_g, ln_b, wr_hi, wr_lo, b_r)


def _dispatch_kernel(pos0_ref, pos1_ref, x_ref, xs_ref, sem):
    base = pl.program_id(0) * DISPATCH_TILE

    def issue(t, carry):
        pltpu.make_async_copy(x_ref.at[t], xs_ref.at[pos0_ref[base + t]], sem).start()
        pltpu.make_async_copy(x_ref.at[t], xs_ref.at[pos1_ref[base + t]], sem).start()
        return carry

    lax.fori_loop(0, DISPATCH_TILE, issue, 0, unroll=8)
    for _ in range(2):
        pltpu.make_async_copy(x_ref, xs_ref.at[pl.ds(0, DISPATCH_TILE)], sem).wait()


def _dispatch(pos0, pos1, h1_rows):
    return pl.pallas_call(
        _dispatch_kernel,
        grid_spec=pltpu.PrefetchScalarGridSpec(
            num_scalar_prefetch=2,
            grid=(N_TOK // DISPATCH_TILE,),
            in_specs=[pl.BlockSpec((DISPATCH_TILE, ROW_SLABS, 128), lambda i, p0, p1: (i, 0, 0))],
            out_specs=pl.BlockSpec(memory_space=pl.ANY),
            scratch_shapes=[pltpu.SemaphoreType.DMA(())],
        ),
        out_shape=jax.ShapeDtypeStruct((N_SLOTS, ROW_SLABS, 128), F32),
        name="moe_dispatch",
    )(pos0, pos1, h1_rows)


def _rows_to_2d(ref):
    return jnp.concatenate([ref[:, s, :] for s in range(ROW_SLABS)], axis=1)


def _expert_kernel(te_ref, nt_ref, xs_ref, w1_ref, w3_ref, w2_ref, ys_ref):
    @pl.when(pl.program_id(0) < nt_ref[0])
    def _():
        x = _rows_to_2d(xs_ref).astype(BF16)
        a = jnp.dot(x, w1_ref[0].astype(BF16), preferred_element_type=F32)
        b = jnp.dot(x, w3_ref[0].astype(BF16), preferred_element_type=F32)
        hid = (a * jax.nn.sigmoid(a) * b).astype(BF16)
        y = jnp.dot(hid, w2_ref[0].astype(BF16), preferred_element_type=F32)
        for s in range(ROW_SLABS):
            ys_ref[:, s, :] = y[:, s * 128:(s + 1) * 128]


def _experts(tile_expert, n_tiles, xs, w1, w3, w2):
    tile = lambda t, te, nt: (jnp.minimum(t, nt[0] - 1), 0, 0)
    wsel = lambda t, te, nt: (te[t], 0, 0)
    return pl.pallas_call(
        _expert_kernel,
        grid_spec=pltpu.PrefetchScalarGridSpec(
            num_scalar_prefetch=2,
            grid=(N_MOE_TILES,),
            in_specs=[
                pl.BlockSpec((MOE_TILE, ROW_SLABS, 128), tile),
                pl.BlockSpec((1, D_MODEL, D_EXPERT), wsel),
                pl.BlockSpec((1, D_MODEL, D_EXPERT), wsel),
                pl.BlockSpec((1, D_EXPERT, D_MODEL), wsel),
            ],
            out_specs=pl.BlockSpec((MOE_TILE, ROW_SLABS, 128), tile),
        ),
        out_shape=jax.ShapeDtypeStruct((N_SLOTS, ROW_SLABS, 128), F32),
        compiler_params=_vmem(40 << 20),
        name="moe_experts",
    )(tile_expert, n_tiles, xs, w1, w3, w2)


def _combine_kernel(pos0_ref, pos1_ref, ys_ref, h1_ref, w0_ref, w1_ref, g_ref, b_ref, o_ref,
                    y0_buf, y1_buf, sem):
    base = pl.program_id(0) * COMBINE_TILE

    def issue(t, carry):
        pltpu.make_async_copy(ys_ref.at[pos0_ref[base + t]], y0_buf.at[t], sem).start()
        pltpu.make_async_copy(ys_ref.at[pos1_ref[base + t]], y1_buf.at[t], sem).start()
        return carry

    lax.fori_loop(0, COMBINE_TILE, issue, 0, unroll=8)
    pltpu.make_async_copy(ys_ref.at[pl.ds(0, COMBINE_TILE)], y0_buf, sem).wait()
    pltpu.make_async_copy(ys_ref.at[pl.ds(0, COMBINE_TILE)], y1_buf, sem).wait()
    w0 = w0_ref[...]
    w1 = w1_ref[...]
    f = jnp.concatenate([w0 * y0_buf[:, s, :] + w1 * y1_buf[:, s, :] for s in range(ROW_SLABS)], axis=1)
    o_ref[...] = _layer_norm(DEEPNORM_ALPHA * _rows_to_2d(h1_ref) + f, g_ref[...], b_ref[...])


def _combine(pos0, pos1, ys, h1_rows, w0, w1, ln_g, ln_b):
    row = lambda i, p0, p1: (i, 0)
    const = lambda i, p0, p1: (0, 0)
    return pl.pallas_call(
        _combine_kernel,
        grid_spec=pltpu.PrefetchScalarGridSpec(
            num_scalar_prefetch=2,
            grid=(N_TOK // COMBINE_TILE,),
            in_specs=[
                pl.BlockSpec(memory_space=pl.ANY),
                pl.BlockSpec((COMBINE_TILE, ROW_SLABS, 128), lambda i, p0, p1: (i, 0, 0)),
                pl.BlockSpec((COMBINE_TILE, 128), row),
                pl.BlockSpec((COMBINE_TILE, 128), row),
                pl.BlockSpec((1, D_MODEL), const),
                pl.BlockSpec((1, D_MODEL), const),
            ],
            out_specs=pl.BlockSpec((COMBINE_TILE, D_MODEL), row),
            scratch_shapes=[pltpu.VMEM((COMBINE_TILE, ROW_SLABS, 128), F32),
                            pltpu.VMEM((COMBINE_TILE, ROW_SLABS, 128), F32),
                            pltpu.SemaphoreType.DMA(())],
        ),
        out_shape=jax.ShapeDtypeStruct((N_TOK, D_MODEL), F32),
        name="moe_combine",
    )(pos0, pos1, ys, h1_rows, w0, w1, ln_g, ln_b)


def kernel(x, mem, ln0_g, ln0_b, rel_bias, w_in, w_mem_kv, w_fourier, lambda_q1, lambda_k1, lambda_q2,
           lambda_k2, subln_g, w_out, ln1_g, ln1_b, w_group, b_group, w_router, b_router, w1, w3, w2,
           ln2_g, ln2_b):
    l = 0
    x2 = x.reshape(N_TOK, D_MODEL)
    wi = w_in[l]
    w_qk = wi[:, :2 * DIFF_WIDTH].astype(BF16)
    w_vt = wi[:, 2 * DIFF_WIDTH:3 * DIFF_WIDTH].T.astype(BF16)
    w_fm = wi[:, 3 * DIFF_WIDTH:].astype(BF16)
    h, q, k, vt, f_in, mq = _ln0_inproj(x2, ln0_g.reshape(1, -1), ln0_b.reshape(1, -1), w_qk, w_vt, w_fm)

    T = ATT_TILE
    kk = jnp.arange(T, dtype=jnp.int32)[:, None]
    qq = jnp.arange(T, dtype=jnp.int32)[None, :]
    buckets = jnp.stack([_t5_bucket(d * T + kk - qq) for d in range(-2, 3)])
    lam_vecs = jnp.stack([lambda_q1[l], lambda_k1[l], lambda_q2[l], lambda_k2[l]]).astype(F32)
    far_buckets = _t5_bucket(jnp.array([-(T + 1), T + 1], jnp.int32))
    o_diff = _diff_attention(rel_bias.astype(F32), far_buckets, q, k, vt, buckets, lam_vecs,
                             subln_g[l].astype(F32).reshape(DIFF_VDIM, 1))

    dft, bdc, bds = (jnp.asarray(c) for c in _dft_constants())
    wf = w_fourier[l]
    bdw = jnp.zeros((FOURIER_WIDTH, FOURIER_WIDTH), F32)
    for g in range(FOURIER_GROUPS):
        sl = slice(g * FOURIER_GROUP_DIM, (g + 1) * FOURIER_GROUP_DIM)
        bdw = bdw.at[sl, sl].set(wf[g])
    z = _fourier_chan(f_in, bdc, bds, bdw.astype(BF16))
    o_four = _fourier_seq(dft, z.reshape(BATCH, 2 * SEQ, FOURIER_WIDTH))

    kcat, vcat = _mem_kv(mem.reshape(BATCH * MEM_LEN, D_MODEL), w_mem_kv[l].astype(BF16))
    o_mem = _mem_attention(mq, kcat, vcat)

    w_r = jnp.zeros((D_MODEL, ROUTER_LANES), F32)
    w_r = w_r.at[:, :N_EXPERTS].set(w_router[l].astype(F32))
    w_r = w_r.at[:, GROUP_LANE0:GROUP_LANE0 + N_GROUPS].set(w_group[l].astype(F32))
    wr_hi = w_r.astype(BF16)
    wr_lo = (w_r - wr_hi.astype(F32)).astype(BF16)
    b_r = jnp.zeros((1, ROUTER_LANES), F32)
    b_r = b_r.at[0, :N_EXPERTS].set(b_router[l].astype(F32))
    b_r = b_r.at[0, GROUP_LANE0:GROUP_LANE0 + N_GROUPS].set(b_group[l].astype(F32))
    h1_rows, route, gate0, gate1, counts = _outproj_router(
        o_diff, o_four, o_mem, h, w_out[l].astype(BF16),
        ln1_g[l].reshape(1, -1), ln1_b[l].reshape(1, -1), wr_hi, wr_lo, b_r)

    cnt = counts[0, :N_EXPERTS].astype(jnp.int32)
    tiles_per_expert = (cnt + MOE_TILE - 1) // MOE_TILE
    tile_end = jnp.cumsum(tiles_per_expert)
    row_start = (tile_end - tiles_per_expert) * MOE_TILE
    pos0 = row_start[route[0]] + route[2]
    pos1 = row_start[route[1]] + route[3]
    tile_ids = jnp.arange(N_MOE_TILES, dtype=jnp.int32)
    tile_expert = jnp.minimum(jnp.sum((tile_ids[:, None] >= tile_end[None, :]).astype(jnp.int32), axis=1),
                              N_EXPERTS - 1)
    n_tiles = tile_end[-1:].astype(jnp.int32)

    xs = _dispatch(pos0, pos1, h1_rows)
    ys = _experts(tile_expert, n_tiles, xs, w1[l], w3[l], w2[l])
    out = _combine(pos0, pos1, ys, h1_rows, gate0, gate1, ln2_g[l].reshape(1, -1), ln2_b[l].reshape(1, -1))
    return out.reshape(BATCH, SEQ, D_MODEL)
```

```python
import functools
import math

import numpy as np
import jax
import jax.numpy as jnp
from jax import lax
from jax.experimental import pallas as pl
from jax.experimental.pallas import tpu as pltpu

F32 = jnp.float32
BF16 = jnp.bfloat16

D_MODEL = 1024
BATCH = 4
SEQ = 4096
N_TOK = BATCH * SEQ
MEM_LEN = 256
MEM_HEADS = 4
MEM_HEAD_DIM = 64
MEM_WIDTH = 256
DIFF_HEADS = 4
DIFF_HEAD_DIM = 64
DIFF_VDIM = 2 * DIFF_HEAD_DIM
VT_ROWS = DIFF_VDIM + 16
DIFF_WIDTH = 512
FOURIER_GROUPS = 4
FOURIER_GROUP_DIM = 64
FOURIER_WIDTH = 256
N_BUCKETS = 32
MAX_DISTANCE = 128
N_GROUPS = 4
EXPERTS_PER_GROUP = 8
N_EXPERTS = 32
D_EXPERT = 256
LN_EPS = 1e-5
DEEPNORM_ALPHA = 2.0 ** 0.25
LAM_INIT = 0.8 - 0.6 * math.exp(-0.3 * 0)
LOG2E = 1.4426950408889634

ROW_TILE = 512
ATT_TILE = 256
MOE_TILE = 512
N_MOE_TILES = 2 * N_TOK // MOE_TILE + N_EXPERTS
N_SLOTS = N_MOE_TILES * MOE_TILE
ROW_SLABS = D_MODEL // 128
DISPATCH_TILE = 512
COMBINE_TILE = 512
DFT_ROW_TILE = 512
ROUTER_LANES = 128
GROUP_LANE0 = 32
NEG_BIG = -1e30

_NT = (((1,), (1,)), ((), ()))


def _vmem(nbytes):
    return pltpu.CompilerParams(vmem_limit_bytes=int(nbytes))


def _layer_norm(x, g, b):
    mu = jnp.mean(x, axis=-1, keepdims=True)
    xc = x - mu
    var = jnp.mean(xc * xc, axis=-1, keepdims=True)
    return xc * lax.rsqrt(var + LN_EPS) * g + b


def _load_row_tiles(ref, rows):
    return jnp.concatenate([ref[pl.ds(s, rows, stride=ROW_SLABS), :] for s in range(ROW_SLABS)], axis=1)


def _store_row_tiles(ref, x):
    for s in range(ROW_SLABS):
        ref[pl.ds(s, x.shape[0], stride=ROW_SLABS), :] = x[:, s * 128:(s + 1) * 128]


def _row_tile(ref, t):
    return ref.at[pl.ds(pl.multiple_of(t * ROW_SLABS, ROW_SLABS), ROW_SLABS), :]


def _t5_bucket(rel):
    nb = N_BUCKETS // 2
    max_exact = nb // 2
    ret = (rel > 0).astype(jnp.int32) * nb
    n = jnp.abs(rel)
    nf = jnp.maximum(n, 1).astype(F32)
    large = max_exact + (jnp.log(nf / max_exact) / math.log(MAX_DISTANCE / max_exact)
                         * (nb - max_exact)).astype(jnp.int32)
    large = jnp.minimum(large, nb - 1)
    return ret + jnp.where(n < max_exact, n, large)


def _ln0_inproj_kernel(x_ref, g_ref, b_ref, wqk_ref, wvt_ref, wfm_ref,
                       h_ref, q_ref, k_ref, vt_ref, f_ref, mq_ref):
    h = _layer_norm(x_ref[...], g_ref[...], b_ref[...])
    h_ref[...] = h
    hb = h.astype(BF16)
    qk = jnp.dot(hb, wqk_ref[...], preferred_element_type=F32)
    q_ref[...] = (qk[:, :DIFF_WIDTH] * (DIFF_HEAD_DIM ** -0.5 * LOG2E)).astype(BF16)
    k_ref[...] = qk[:, DIFF_WIDTH:].astype(BF16)
    vt = lax.dot_general(wvt_ref[...], hb, _NT, preferred_element_type=F32)
    ones = jnp.ones((VT_ROWS - DIFF_VDIM, ROW_TILE), BF16)
    for hh in range(DIFF_HEADS):
        vt_ref[0, hh, :DIFF_VDIM, :] = vt[hh * DIFF_VDIM:(hh + 1) * DIFF_VDIM, :].astype(BF16)
        vt_ref[0, hh, DIFF_VDIM:, :] = ones
    fm = jnp.dot(hb, wfm_ref[...], preferred_element_type=F32)
    f_ref[...] = fm[:, :FOURIER_WIDTH].astype(BF16)
    mq_ref[...] = (fm[:, FOURIER_WIDTH:] * MEM_HEAD_DIM ** -0.5).astype(BF16)


def _ln0_inproj(x2, ln_g, ln_b, w_qk, w_vt, w_fm):
    rt = SEQ // ROW_TILE
    row = lambda r: (r, 0)
    const = lambda r: (0, 0)
    return pl.pallas_call(
        _ln0_inproj_kernel,
        grid=(N_TOK // ROW_TILE,),
        in_specs=[
            pl.BlockSpec((ROW_TILE, D_MODEL), row),
            pl.BlockSpec((1, D_MODEL), const),
            pl.BlockSpec((1, D_MODEL), const),
            pl.BlockSpec((D_MODEL, 2 * DIFF_WIDTH), const),
            pl.BlockSpec((DIFF_WIDTH, D_MODEL), const),
            pl.BlockSpec((D_MODEL, FOURIER_WIDTH + MEM_WIDTH), const),
        ],
        out_specs=[
            pl.BlockSpec((ROW_TILE, D_MODEL), row),
            pl.BlockSpec((ROW_TILE, DIFF_WIDTH), row),
            pl.BlockSpec((ROW_TILE, DIFF_WIDTH), row),
            pl.BlockSpec((1, DIFF_HEADS, VT_ROWS, ROW_TILE), lambda r: (r // rt, 0, 0, r % rt)),
            pl.BlockSpec((ROW_TILE, FOURIER_WIDTH), row),
            pl.BlockSpec((ROW_TILE, MEM_WIDTH), row),
        ],
        out_shape=[
            jax.ShapeDtypeStruct((N_TOK, D_MODEL), F32),
            jax.ShapeDtypeStruct((N_TOK, DIFF_WIDTH), BF16),
            jax.ShapeDtypeStruct((N_TOK, DIFF_WIDTH), BF16),
            jax.ShapeDtypeStruct((BATCH, DIFF_HEADS, VT_ROWS, SEQ), BF16),
            jax.ShapeDtypeStruct((N_TOK, FOURIER_WIDTH), BF16),
            jax.ShapeDtypeStruct((N_TOK, MEM_WIDTH), BF16),
        ],
        compiler_params=_vmem(48 << 20),
        name="ln0_inproj",
    )(x2, ln_g, ln_b, w_qk, w_vt, w_fm)


def _diff_attn_kernel(tbl_ref, far_ref, q_ref, k_ref, vt_ref, bkt_ref, lam_ref, g_ref, o_ref,
                      bias_scr, s0_scr, s1_scr, p0_scr, p1_scr):
    T = ATT_TILE
    nk = SEQ // T
    h = pl.program_id(1)
    i = pl.program_id(2)

    @pl.when(i == 0)
    def _build_bias():
        for d in range(5):
            bk = bkt_ref[d]
            bias = jnp.zeros((T, T), F32)
            for n in range(N_BUCKETS):
                bias = jnp.where(bk == n, tbl_ref[n, h], bias)
            bias_scr[d] = bias * LOG2E

    q = q_ref[...]
    lane = lax.broadcasted_iota(jnp.int32, q.shape, 1)
    zero = jnp.zeros_like(q)
    q_comp = (jnp.where(lane < DIFF_HEAD_DIM, q, zero), jnp.where(lane >= DIFF_HEAD_DIM, q, zero))
    s_scr = (s0_scr, s1_scr)
    p_scr = (p0_scr, p1_scr)

    for c in range(2):
        s_scr[c][...] = lax.dot_general(k_ref[...], q_comp[c], _NT, preferred_element_type=F32)

    js = jnp.clip(i - 1, 0, nk - 3)
    for w in range(3):
        rows = pl.ds(pl.multiple_of((js + w) * T, T), T)
        bias = bias_scr[js + w - i + 2]
        for c in range(2):
            s_scr[c][rows, :] = s_scr[c][rows, :] + bias
    c_left = tbl_ref[far_ref[0], h] * LOG2E
    c_right = tbl_ref[far_ref[1], h] * LOG2E
    side = [jnp.where(j < js, c_left, jnp.where(j > js + 2, c_right, 0.0)) for j in range(nk)]

    m8 = [None, None]
    for j in range(nk):
        for c in range(2):
            t = jnp.max(s_scr[c][j * T:(j + 1) * T, :].reshape(T // 8, 8, T), axis=0) + side[j]
            m8[c] = t if m8[c] is None else jnp.maximum(m8[c], t)
    m = [jnp.max(m8[c], axis=0, keepdims=True) for c in range(2)]

    for j in range(nk):
        for c in range(2):
            x = s_scr[c][j * T:(j + 1) * T, :] + (side[j] - m[c])
            p_scr[c][j * T:(j + 1) * T, :] = jnp.exp2(x.astype(BF16))

    acc = [jnp.dot(vt_ref[0, 0], p_scr[c][...], preferred_element_type=F32) for c in range(2)]
    num = [a[:DIFF_VDIM] for a in acc]
    den = [a[DIFF_VDIM:DIFF_VDIM + 1] for a in acc]

    lam = (jnp.exp(jnp.sum(lam_ref[0:1, :] * lam_ref[1:2, :], axis=1, keepdims=True))
           - jnp.exp(jnp.sum(lam_ref[2:3, :] * lam_ref[3:4, :], axis=1, keepdims=True)) + LAM_INIT)
    o = num[0] * (1.0 / den[0]) - lam * (num[1] * (1.0 / den[1]))
    ms = jnp.mean(o * o, axis=0, keepdims=True)
    o = o * lax.rsqrt(ms + LN_EPS) * g_ref[...] * (1.0 - LAM_INIT)
    o_ref[...] = o.T.astype(BF16)


def _diff_attention(rel_bias, far_buckets, q, k, vt, buckets, lam_vecs, subln_col):
    T = ATT_TILE
    nq = SEQ // T
    return pl.pallas_call(
        _diff_attn_kernel,
        grid=(BATCH, DIFF_HEADS, nq),
        in_specs=[
            pl.BlockSpec(memory_space=pltpu.SMEM),
            pl.BlockSpec(memory_space=pltpu.SMEM),
            pl.BlockSpec((T, DIFF_VDIM), lambda b, h, i: (b * nq + i, h)),
            pl.BlockSpec((SEQ, DIFF_VDIM), lambda b, h, i: (b, h)),
            pl.BlockSpec((1, 1, VT_ROWS, SEQ), lambda b, h, i: (b, h, 0, 0)),
            pl.BlockSpec((5, T, T), lambda b, h, i: (0, 0, 0)),
            pl.BlockSpec((4, DIFF_HEAD_DIM), lambda b, h, i: (0, 0)),
            pl.BlockSpec((DIFF_VDIM, 1), lambda b, h, i: (0, 0)),
        ],
        out_specs=pl.BlockSpec((T, DIFF_VDIM), lambda b, h, i: (b * nq + i, h)),
        out_shape=jax.ShapeDtypeStruct((N_TOK, DIFF_WIDTH), BF16),
        scratch_shapes=[pltpu.VMEM((5, T, T), F32),
                        pltpu.VMEM((SEQ, T), F32), pltpu.VMEM((SEQ, T), F32),
                        pltpu.VMEM((SEQ, T), BF16), pltpu.VMEM((SEQ, T), BF16)],
        compiler_params=_vmem(40 << 20),
        name="diff_attn",
    )(rel_bias, far_buckets, q, k, vt, buckets, lam_vecs, subln_col)


def _fourier_chan_kernel(f_ref, bdc_ref, bds_ref, bdw_ref, z_ref):
    u = f_ref[...]
    w = bdw_ref[...]
    pc = jnp.dot(u, bdc_ref[...], preferred_element_type=F32).astype(BF16)
    ps = jnp.dot(u, bds_ref[...], preferred_element_type=F32).astype(BF16)
    z_ref[0, 0] = jnp.dot(pc, w, preferred_element_type=F32).astype(BF16)
    z_ref[0, 1] = jnp.dot(ps, w, preferred_element_type=F32).astype(BF16)


def _fourier_chan(f_in, bdc, bds, bdw):
    rt = SEQ // ROW_TILE
    const = lambda r: (0, 0)
    return pl.pallas_call(
        _fourier_chan_kernel,
        grid=(N_TOK // ROW_TILE,),
        in_specs=[
            pl.BlockSpec((ROW_TILE, FOURIER_WIDTH), lambda r: (r, 0)),
            pl.BlockSpec((FOURIER_WIDTH, FOURIER_WIDTH), const),
            pl.BlockSpec((FOURIER_WIDTH, FOURIER_WIDTH), const),
            pl.BlockSpec((FOURIER_WIDTH, FOURIER_WIDTH), const),
        ],
        out_specs=pl.BlockSpec((1, 2, ROW_TILE, FOURIER_WIDTH), lambda r: (r // rt, 0, r % rt, 0)),
        out_shape=jax.ShapeDtypeStruct((BATCH, 2, SEQ, FOURIER_WIDTH), BF16),
        name="fourier_chan",
    )(f_in, bdc, bds, bdw)


def _fourier_seq_kernel(dft_ref, z_ref, o_ref):
    o_ref[...] = jnp.dot(dft_ref[...], z_ref[0], preferred_element_type=F32).astype(BF16)


def _fourier_seq(dft, z):
    nr = SEQ // DFT_ROW_TILE
    return pl.pallas_call(
        _fourier_seq_kernel,
        grid=(nr, BATCH),
        in_specs=[
            pl.BlockSpec((DFT_ROW_TILE, 2 * SEQ), lambda r, b: (r, 0)),
            pl.BlockSpec((1, 2 * SEQ, FOURIER_WIDTH), lambda r, b: (b, 0, 0)),
        ],
        out_specs=pl.BlockSpec((DFT_ROW_TILE, FOURIER_WIDTH), lambda r, b: (b * nr + r, 0)),
        out_shape=jax.ShapeDtypeStruct((N_TOK, FOURIER_WIDTH), BF16),
        compiler_params=_vmem(48 << 20),
        name="fourier_seq",
    )(dft, z)


@functools.lru_cache(maxsize=None)
def _dft_constants():
    n = np.arange(SEQ, dtype=np.int64)
    ang = 2.0 * np.pi * ((n[:, None] * n[None, :]) % SEQ).astype(np.float64) / SEQ
    scale = 1.0 / math.sqrt(SEQ * FOURIER_GROUP_DIM)
    dft = np.concatenate([np.cos(ang), -np.sin(ang)], axis=1) * scale
    c = np.arange(FOURIER_GROUP_DIM, dtype=np.int64)
    angc = 2.0 * np.pi * ((c[:, None] * c[None, :]) % FOURIER_GROUP_DIM).astype(np.float64) / FOURIER_GROUP_DIM
    eye = np.eye(FOURIER_GROUPS)
    bdc = np.kron(eye, np.cos(angc))
    bds = np.kron(eye, np.sin(angc))
    return dft.astype(BF16), bdc.astype(BF16), bds.astype(BF16)


def _mem_kv_kernel(mem_ref, w_ref, kcat_ref, vcat_ref):
    kv = jnp.dot(mem_ref[...].astype(BF16), w_ref[...], preferred_element_type=F32)
    mk = kv[:, :MEM_WIDTH]
    mv = kv[:, MEM_WIDTH:]
    lane = lax.broadcasted_iota(jnp.int32, mk.shape, 1)
    for hh in range(MEM_HEADS):
        sel = (lane >= hh * MEM_HEAD_DIM) & (lane < (hh + 1) * MEM_HEAD_DIM)
        kcat_ref[0, hh * MEM_LEN:(hh + 1) * MEM_LEN, :] = jnp.where(sel, mk, 0.0).astype(BF16)
        vcat_ref[0, hh * MEM_LEN:(hh + 1) * MEM_LEN, :] = jnp.where(sel, mv, 0.0).astype(BF16)


def _mem_kv(mem2, w_kv):
    return pl.pallas_call(
        _mem_kv_kernel,
        grid=(BATCH,),
        in_specs=[
            pl.BlockSpec((MEM_LEN, D_MODEL), lambda b: (b, 0)),
            pl.BlockSpec((D_MODEL, 2 * MEM_WIDTH), lambda b: (0, 0)),
        ],
        out_specs=[
            pl.BlockSpec((1, MEM_HEADS * MEM_LEN, MEM_WIDTH), lambda b: (b, 0, 0)),
            pl.BlockSpec((1, MEM_HEADS * MEM_LEN, MEM_WIDTH), lambda b: (b, 0, 0)),
        ],
        out_shape=[
            jax.ShapeDtypeStruct((BATCH, MEM_HEADS * MEM_LEN, MEM_WIDTH), BF16),
            jax.ShapeDtypeStruct((BATCH, MEM_HEADS * MEM_LEN, MEM_WIDTH), BF16),
        ],
        name="mem_kv",
    )(mem2, w_kv)


def _mem_attn_kernel(mq_ref, kcat_ref, vcat_ref, o_ref):
    s = lax.dot_general(mq_ref[...], kcat_ref[0], _NT, preferred_element_type=F32)
    parts = []
    for hh in range(MEM_HEADS):
        sh = s[:, hh * MEM_LEN:(hh + 1) * MEM_LEN]
        p = jnp.exp(sh - jnp.max(sh, axis=1, keepdims=True))
        parts.append((p * (1.0 / jnp.sum(p, axis=1, keepdims=True))).astype(BF16))
    p_all = jnp.concatenate(parts, axis=1)
    o_ref[...] = jnp.dot(p_all, vcat_ref[0], preferred_element_type=F32).astype(BF16)


def _mem_attention(mq, kcat, vcat):
    rt = SEQ // ROW_TILE
    return pl.pallas_call(
        _mem_attn_kernel,
        grid=(N_TOK // ROW_TILE,),
        in_specs=[
            pl.BlockSpec((ROW_TILE, MEM_WIDTH), lambda r: (r, 0)),
            pl.BlockSpec((1, MEM_HEADS * MEM_LEN, MEM_WIDTH), lambda r: (r // rt, 0, 0)),
            pl.BlockSpec((1, MEM_HEADS * MEM_LEN, MEM_WIDTH), lambda r: (r // rt, 0, 0)),
        ],
        out_specs=pl.BlockSpec((ROW_TILE, MEM_WIDTH), lambda r: (r, 0)),
        out_shape=jax.ShapeDtypeStruct((N_TOK, MEM_WIDTH), BF16),
        name="mem_attn",
    )(mq, kcat, vcat)


def _route(logits):
    lane = lax.broadcasted_iota(jnp.int32, logits.shape, 1)
    big = jnp.int32(ROUTER_LANES)
    is_group = (lane >= GROUP_LANE0) & (lane < GROUP_LANE0 + N_GROUPS)
    gl = jnp.where(is_group, logits, -jnp.inf)
    gmax = jnp.max(gl, axis=1, keepdims=True)
    g_sel = jnp.min(jnp.where(gl == gmax, lane, big), axis=1, keepdims=True) - GROUP_LANE0
    g_gate = 1.0 / jnp.sum(jnp.where(is_group, jnp.exp(gl - gmax), 0.0), axis=1, keepdims=True)
    in_group = (lane >= g_sel * EXPERTS_PER_GROUP) & (lane < (g_sel + 1) * EXPERTS_PER_GROUP)
    el = jnp.where(in_group, logits, -jnp.inf)
    v1 = jnp.max(el, axis=1, keepdims=True)
    i1 = jnp.min(jnp.where(el == v1, lane, big), axis=1, keepdims=True)
    el2 = jnp.where(lane == i1, -jnp.inf, el)
    v2 = jnp.max(el2, axis=1, keepdims=True)
    i2 = jnp.min(jnp.where(el2 == v2, lane, big), axis=1, keepdims=True)
    e = jnp.exp(v2 - v1)
    w_first = g_gate / (1.0 + e)
    w_second = g_gate * e / (1.0 + e)
    return i1, i2, w_first, w_second


def _outproj_router_kernel(od_ref, of_ref, om_ref, h_ref, wo_ref, g_ref, b_ref,
                           wr_hi_ref, wr_lo_ref, br_ref,
                           h1_ref, route_ref, w0_ref, w1_ref, cnt_ref, cnt_scr):
    @pl.when(pl.program_id(0) == 0)
    def _():
        cnt_scr[...] = jnp.zeros_like(cnt_scr)

    o = jnp.concatenate([od_ref[...], of_ref[...], om_ref[...]], axis=1)
    a = jnp.dot(o, wo_ref[...], preferred_element_type=F32)
    h1 = _layer_norm(DEEPNORM_ALPHA * h_ref[...] + a, g_ref[...], b_ref[...])
    _store_row_tiles(h1_ref, h1)
    hi = h1.astype(BF16)
    lo = (h1 - hi.astype(F32)).astype(BF16)
    w_hi = wr_hi_ref[...]
    logits = (jnp.dot(hi, w_hi, preferred_element_type=F32)
              + jnp.dot(hi, wr_lo_ref[...], preferred_element_type=F32)
              + jnp.dot(lo, w_hi, preferred_element_type=F32) + br_ref[...])
    i1, i2, w_first, w_second = _route(logits)

    lane = lax.broadcasted_iota(jnp.int32, logits.shape, 1)
    onehot = jnp.where(lane == i1, 1.0, jnp.where(lane == i2, 1.0, 0.0))
    r_id = lax.broadcasted_iota(jnp.int32, (ROW_TILE, ROW_TILE), 0)
    c_id = lax.broadcasted_iota(jnp.int32, (ROW_TILE, ROW_TILE), 1)
    tri = jnp.where(r_id > c_id, 1.0, 0.0).astype(BF16)
    before = jnp.dot(tri, onehot.astype(BF16), preferred_element_type=F32) + cnt_scr[...]
    rank1 = jnp.sum(jnp.where(lane == i1, before, 0.0), axis=1, keepdims=True)
    rank2 = jnp.sum(jnp.where(lane == i2, before, 0.0), axis=1, keepdims=True)
    cnt_scr[...] += jnp.sum(onehot, axis=0, keepdims=True)
    cnt_ref[...] = cnt_scr[...]

    packed = jnp.where(lane == 0, i1.astype(F32),
                       jnp.where(lane == 1, i2.astype(F32),
                                 jnp.where(lane == 2, rank1, jnp.where(lane == 3, rank2, 0.0))))
    route_ref[...] = packed.T[:8, :].astype(jnp.int32)
    w0_ref[...] = jnp.broadcast_to(w_first, (ROW_TILE, 128))
    w1_ref[...] = jnp.broadcast_to(w_second, (ROW_TILE, 128))


def _outproj_router(o_diff, o_four, o_mem, h, w_out, ln_g, ln_b, wr_hi, wr_lo, b_r):
    row = lambda r: (r, 0)
    const = lambda r: (0, 0)
    return pl.pallas_call(
        _outproj_router_kernel,
        grid=(N_TOK // ROW_TILE,),
        in_specs=[
            pl.BlockSpec((ROW_TILE, DIFF_WIDTH), row),
            pl.BlockSpec((ROW_TILE, FOURIER_WIDTH), row),
            pl.BlockSpec((ROW_TILE, MEM_WIDTH), row),
            pl.BlockSpec((ROW_TILE, D_MODEL), row),
            pl.BlockSpec((D_MODEL, D_MODEL), const),
            pl.BlockSpec((1, D_MODEL), const),
            pl.BlockSpec((1, D_MODEL), const),
            pl.BlockSpec((D_MODEL, ROUTER_LANES), const),
            pl.BlockSpec((D_MODEL, ROUTER_LANES), const),
            pl.BlockSpec((1, ROUTER_LANES), const),
        ],
        out_specs=[
            pl.BlockSpec((ROW_TILE * ROW_SLABS, 128), row),
            pl.BlockSpec((8, ROW_TILE), lambda r: (0, r)),
            pl.BlockSpec((ROW_TILE, 128), row),
            pl.BlockSpec((ROW_TILE, 128), row),
            pl.BlockSpec((1, ROUTER_LANES), const),
        ],
        out_shape=[
            jax.ShapeDtypeStruct((N_TOK * ROW_SLABS, 128), F32),
            jax.ShapeDtypeStruct((8, N_TOK), jnp.int32),
            jax.ShapeDtypeStruct((N_TOK, 128), F32),
            jax.ShapeDtypeStruct((N_TOK, 128), F32),
            jax.ShapeDtypeStruct((1, ROUTER_LANES), F32),
        ],
        scratch_shapes=[pltpu.VMEM((1, ROUTER_LANES), F32)],
        compiler_params=_vmem(40 << 20),
        name="outproj_router",
    )(o_diff, o_four, o_mem, h, w_out, ln_g, ln_b, wr_hi, wr_lo, b_r)


def _dispatch_kernel(pos0_ref, pos1_ref, x_ref, xs_ref, sem):
    base = pl.program_id(0) * DISPATCH_TILE

    def issue(t, carry):
        src = _row_tile(x_ref, t)
        pltpu.make_async_copy(src, _row_tile(xs_ref, pos0_ref[base + t]), sem).start(priority=0)
        pltpu.make_async_copy(src, _row_tile(xs_ref, pos1_ref[base + t]), sem).start(priority=1)
        return carry

    lax.fori_loop(0, DISPATCH_TILE, issue, 0, unroll=8)
    for _ in range(2):
        pltpu.make_async_copy(x_ref, xs_ref.at[pl.ds(0, DISPATCH_TILE * ROW_SLABS), :], sem).wait()


def _dispatch(pos0, pos1, h1_rows):
    return pl.pallas_call(
        _dispatch_kernel,
        grid_spec=pltpu.PrefetchScalarGridSpec(
            num_scalar_prefetch=2,
            grid=(N_TOK // DISPATCH_TILE,),
            in_specs=[pl.BlockSpec((DISPATCH_TILE * ROW_SLABS, 128), lambda i, p0, p1: (i, 0))],
            out_specs=pl.BlockSpec(memory_space=pl.ANY),
            scratch_shapes=[pltpu.SemaphoreType.DMA(())],
        ),
        out_shape=jax.ShapeDtypeStruct((N_SLOTS * ROW_SLABS, 128), F32),
        name="moe_dispatch",
    )(pos0, pos1, h1_rows)


def _expert_kernel(te_ref, nt_ref, xs_ref, w1_ref, w3_ref, w2_ref, ys_ref):
    @pl.when(pl.program_id(0) < nt_ref[0])
    def _():
        x = _load_row_tiles(xs_ref, MOE_TILE).astype(BF16)
        a = jnp.dot(x, w1_ref[0].astype(BF16), preferred_element_type=F32)
        b = jnp.dot(x, w3_ref[0].astype(BF16), preferred_element_type=F32)
        hid = (a * jax.nn.sigmoid(a) * b).astype(BF16)
        y = jnp.dot(hid, w2_ref[0].astype(BF16), preferred_element_type=F32)
        _store_row_tiles(ys_ref, y)


def _experts(tile_expert, n_tiles, xs, w1, w3, w2):
    tile = lambda t, te, nt: (jnp.minimum(t, nt[0] - 1), 0)
    wsel = lambda t, te, nt: (te[t], 0, 0)
    return pl.pallas_call(
        _expert_kernel,
        grid_spec=pltpu.PrefetchScalarGridSpec(
            num_scalar_prefetch=2,
            grid=(N_MOE_TILES,),
            in_specs=[
                pl.BlockSpec((MOE_TILE * ROW_SLABS, 128), tile),
                pl.BlockSpec((1, D_MODEL, D_EXPERT), wsel),
                pl.BlockSpec((1, D_MODEL, D_EXPERT), wsel),
                pl.BlockSpec((1, D_EXPERT, D_MODEL), wsel),
            ],
            out_specs=pl.BlockSpec((MOE_TILE * ROW_SLABS, 128), tile),
        ),
        out_shape=jax.ShapeDtypeStruct((N_SLOTS * ROW_SLABS, 128), F32),
        compiler_params=_vmem(40 << 20),
        name="moe_experts",
    )(tile_expert, n_tiles, xs, w1, w3, w2)


def _combine_kernel(pos0_ref, pos1_ref, ys_ref, h1_ref, w0_ref, w1_ref, g_ref, b_ref, o_ref,
                    y0_buf, y1_buf, sems):
    i = pl.program_id(0)
    n = pl.num_programs(0)

    def gather(tile, slot):
        base = tile * COMBINE_TILE

        def issue(t, carry):
            pltpu.make_async_copy(_row_tile(ys_ref, pos0_ref[base + t]), _row_tile(y0_buf.at[slot], t),
                                  sems.at[slot]).start(priority=0)
            pltpu.make_async_copy(_row_tile(ys_ref, pos1_ref[base + t]), _row_tile(y1_buf.at[slot], t),
                                  sems.at[slot]).start(priority=1)
            return carry

        lax.fori_loop(0, COMBINE_TILE, issue, 0, unroll=8)

    @pl.when(i == 0)
    def _():
        gather(0, 0)

    @pl.when(i + 1 < n)
    def _():
        gather(i + 1, (i + 1) % 2)

    slot = i % 2
    whole = ys_ref.at[pl.ds(0, COMBINE_TILE * ROW_SLABS), :]
    pltpu.make_async_copy(whole, y0_buf.at[slot], sems.at[slot]).wait()
    pltpu.make_async_copy(whole, y1_buf.at[slot], sems.at[slot]).wait()
    w0 = jnp.concatenate([w0_ref[...]] * ROW_SLABS, axis=1)
    w1 = jnp.concatenate([w1_ref[...]] * ROW_SLABS, axis=1)
    f = (w0 * _load_row_tiles(y0_buf.at[slot], COMBINE_TILE)
         + w1 * _load_row_tiles(y1_buf.at[slot], COMBINE_TILE))
    h1 = _load_row_tiles(h1_ref, COMBINE_TILE)
    o_ref[...] = _layer_norm(DEEPNORM_ALPHA * h1 + f, g_ref[...], b_ref[...])


def _combine(pos0, pos1, ys, h1_rows, w0, w1, ln_g, ln_b):
    row = lambda i, p0, p1: (i, 0)
    const = lambda i, p0, p1: (0, 0)
    return pl.pallas_call(
        _combine_kernel,
        grid_spec=pltpu.PrefetchScalarGridSpec(
            num_scalar_prefetch=2,
            grid=(N_TOK // COMBINE_TILE,),
            in_specs=[
                pl.BlockSpec(memory_space=pl.ANY),
                pl.BlockSpec((COMBINE_TILE * ROW_SLABS, 128), row),
                pl.BlockSpec((COMBINE_TILE, 128), row),
                pl.BlockSpec((COMBINE_TILE, 128), row),
                pl.BlockSpec((1, D_MODEL), const),
                pl.BlockSpec((1, D_MODEL), const),
            ],
            out_specs=pl.BlockSpec((COMBINE_TILE, D_MODEL), row),
            scratch_shapes=[pltpu.VMEM((2, COMBINE_TILE * ROW_SLABS, 128), F32),
                            pltpu.VMEM((2, COMBINE_TILE * ROW_SLABS, 128), F32),
                            pltpu.SemaphoreType.DMA((2,))],
        ),
        out_shape=jax.ShapeDtypeStruct((N_TOK, D_MODEL), F32),
        compiler_params=pltpu.CompilerParams(dimension_semantics=("arbitrary",), vmem_limit_bytes=40 << 20),
        name="moe_combine",
    )(pos0, pos1, ys, h1_rows, w0, w1, ln_g, ln_b)


def kernel(x, mem, ln0_g, ln0_b, rel_bias, w_in, w_mem_kv, w_fourier, lambda_q1, lambda_k1, lambda_q2,
           lambda_k2, subln_g, w_out, ln1_g, ln1_b, w_group, b_group, w_router, b_router, w1, w3, w2,
           ln2_g, ln2_b):
    l = 0
    x2 = x.reshape(N_TOK, D_MODEL)
    wi = w_in[l]
    w_qk = wi[:, :2 * DIFF_WIDTH].astype(BF16)
    w_vt = wi[:, 2 * DIFF_WIDTH:3 * DIFF_WIDTH].T.astype(BF16)
    w_fm = wi[:, 3 * DIFF_WIDTH:].astype(BF16)
    h, q, k, vt, f_in, mq = _ln0_inproj(x2, ln0_g.reshape(1, -1), ln0_b.reshape(1, -1), w_qk, w_vt, w_fm)

    T = ATT_TILE
    kk = jnp.arange(T, dtype=jnp.int32)[:, None]
    qq = jnp.arange(T, dtype=jnp.int32)[None, :]
    buckets = jnp.stack([_t5_bucket(d * T + kk - qq) for d in range(-2, 3)])
    lam_vecs = jnp.stack([lambda_q1[l], lambda_k1[l], lambda_q2[l], lambda_k2[l]]).astype(F32)
    far_buckets = _t5_bucket(jnp.array([-(T + 1), T + 1], jnp.int32))
    o_diff = _diff_attention(rel_bias.astype(F32), far_buckets, q, k, vt, buckets, lam_vecs,
                             subln_g[l].astype(F32).reshape(DIFF_VDIM, 1))

    dft, bdc, bds = (jnp.asarray(c) for c in _dft_constants())
    wf = w_fourier[l]
    bdw = jnp.zeros((FOURIER_WIDTH, FOURIER_WIDTH), F32)
    for g in range(FOURIER_GROUPS):
        sl = slice(g * FOURIER_GROUP_DIM, (g + 1) * FOURIER_GROUP_DIM)
        bdw = bdw.at[sl, sl].set(wf[g])
    z = _fourier_chan(f_in, bdc, bds, bdw.astype(BF16))
    o_four = _fourier_seq(dft, z.reshape(BATCH, 2 * SEQ, FOURIER_WIDTH))

    kcat, vcat = _mem_kv(mem.reshape(BATCH * MEM_LEN, D_MODEL), w_mem_kv[l].astype(BF16))
    o_mem = _mem_attention(mq, kcat, vcat)

    w_r = jnp.zeros((D_MODEL, ROUTER_LANES), F32)
    w_r = w_r.at[:, :N_EXPERTS].set(w_router[l].astype(F32))
    w_r = w_r.at[:, GROUP_LANE0:GROUP_LANE0 + N_GROUPS].set(w_group[l].astype(F32))
    wr_hi = w_r.astype(BF16)
    wr_lo = (w_r - wr_hi.astype(F32)).astype(BF16)
    b_r = jnp.zeros((1, ROUTER_LANES), F32)
    b_r = b_r.at[0, :N_EXPERTS].set(b_router[l].astype(F32))
    b_r = b_r.at[0, GROUP_LANE0:GROUP_LANE0 + N_GROUPS].set(b_group[l].astype(F32))
    h1_rows, route, gate0, gate1, counts = _outproj_router(
        o_diff, o_four, o_mem, h, w_out[l].astype(BF16),
        ln1_g[l].reshape(1, -1), ln1_b[l].reshape(1, -1), wr_hi, wr_lo, b_r)

    cnt = counts[0, :N_EXPERTS].astype(jnp.int32)
    tiles_per_expert = (cnt + MOE_TILE - 1) // MOE_TILE
    tile_end = jnp.cumsum(tiles_per_expert)
    row_start = (tile_end - tiles_per_expert) * MOE_TILE
    pos0 = row_start[route[0]] + route[2]
    pos1 = row_start[route[1]] + route[3]
    tile_ids = jnp.arange(N_MOE_TILES, dtype=jnp.int32)
    tile_expert = jnp.minimum(jnp.sum((tile_ids[:, None] >= tile_end[None, :]).astype(jnp.int32), axis=1),
                              N_EXPERTS - 1)
    n_tiles = tile_end[-1:].astype(jnp.int32)

    xs = _dispatch(pos0, pos1, h1_rows)
    ys = _experts(tile_expert, n_tiles, xs, w1[l], w3[l], w2[l])
    out = _combine(pos0, pos1, ys, h1_rows, gate0, gate1, ln2_g[l].reshape(1, -1), ln2_b[l].reshape(1, -1))
    return out.reshape(BATCH, SEQ, D_MODEL)
```

```python
import functools
import math

import numpy as np
import jax
import jax.numpy as jnp
from jax import lax
from jax.experimental import pallas as pl
from jax.experimental.pallas import tpu as pltpu

F32 = jnp.float32
BF16 = jnp.bfloat16

D_MODEL = 1024
BATCH = 4
SEQ = 4096
N_TOK = BATCH * SEQ
MEM_LEN = 256
MEM_HEADS = 4
MEM_HEAD_DIM = 64
MEM_WIDTH = 256
DIFF_HEADS = 4
DIFF_HEAD_DIM = 64
DIFF_VDIM = 2 * DIFF_HEAD_DIM
VT_ROWS = DIFF_VDIM + 16
DIFF_WIDTH = 512
FOURIER_GROUPS = 4
FOURIER_GROUP_DIM = 64
FOURIER_WIDTH = 256
N_BUCKETS = 32
MAX_DISTANCE = 128
N_GROUPS = 4
EXPERTS_PER_GROUP = 8
N_EXPERTS = 32
D_EXPERT = 256
LN_EPS = 1e-5
DEEPNORM_ALPHA = 2.0 ** 0.25
LAM_INIT = 0.8 - 0.6 * math.exp(-0.3 * 0)
LOG2E = 1.4426950408889634

ROW_TILE = 512
ATT_TILE = 256
MOE_TILE = 512
N_MOE_TILES = 2 * N_TOK // MOE_TILE + N_EXPERTS
N_SLOTS = N_MOE_TILES * MOE_TILE
ROW_SLABS = D_MODEL // 128
DISPATCH_TILE = 512
COMBINE_TILE = 512
FFT_RADIX = 64
ROUTER_LANES = 128
GROUP_LANE0 = 32
NEG_BIG = -1e30

_NT = (((1,), (1,)), ((), ()))


def _vmem(nbytes):
    return pltpu.CompilerParams(vmem_limit_bytes=int(nbytes))


def _layer_norm(x, g, b):
    mu = jnp.mean(x, axis=-1, keepdims=True)
    xc = x - mu
    var = jnp.mean(xc * xc, axis=-1, keepdims=True)
    return xc * lax.rsqrt(var + LN_EPS) * g + b


def _load_row_tiles(ref, rows):
    return jnp.concatenate([ref[pl.ds(s, rows, stride=ROW_SLABS), :] for s in range(ROW_SLABS)], axis=1)


def _store_row_tiles(ref, x):
    for s in range(ROW_SLABS):
        ref[pl.ds(s, x.shape[0], stride=ROW_SLABS), :] = x[:, s * 128:(s + 1) * 128]


def _row_tile(ref, t):
    return ref.at[pl.ds(pl.multiple_of(t * ROW_SLABS, ROW_SLABS), ROW_SLABS), :]


def _t5_bucket(rel):
    nb = N_BUCKETS // 2
    max_exact = nb // 2
    ret = (rel > 0).astype(jnp.int32) * nb
    n = jnp.abs(rel)
    nf = jnp.maximum(n, 1).astype(F32)
    large = max_exact + (jnp.log(nf / max_exact) / math.log(MAX_DISTANCE / max_exact)
                         * (nb - max_exact)).astype(jnp.int32)
    large = jnp.minimum(large, nb - 1)
    return ret + jnp.where(n < max_exact, n, large)


def _ln0_inproj_kernel(x_ref, g_ref, b_ref, wqk_ref, wvt_ref, wfm_ref,
                       h_ref, q_ref, k_ref, vt_ref, f_ref, mq_ref):
    h = _layer_norm(x_ref[...], g_ref[...], b_ref[...])
    h_ref[...] = h
    hb = h.astype(BF16)
    qk = jnp.dot(hb, wqk_ref[...], preferred_element_type=F32)
    q_ref[...] = (qk[:, :DIFF_WIDTH] * (DIFF_HEAD_DIM ** -0.5 * LOG2E)).astype(BF16)
    k_ref[...] = qk[:, DIFF_WIDTH:].astype(BF16)
    vt = lax.dot_general(wvt_ref[...], hb, _NT, preferred_element_type=F32)
    ones = jnp.ones((VT_ROWS - DIFF_VDIM, ROW_TILE), BF16)
    for hh in range(DIFF_HEADS):
        vt_ref[0, hh, :DIFF_VDIM, :] = vt[hh * DIFF_VDIM:(hh + 1) * DIFF_VDIM, :].astype(BF16)
        vt_ref[0, hh, DIFF_VDIM:, :] = ones
    fm = jnp.dot(hb, wfm_ref[...], preferred_element_type=F32)
    f_ref[...] = fm[:, :FOURIER_WIDTH].astype(BF16)
    mq_ref[...] = (fm[:, FOURIER_WIDTH:] * MEM_HEAD_DIM ** -0.5).astype(BF16)


def _ln0_inproj(x2, ln_g, ln_b, w_qk, w_vt, w_fm):
    rt = SEQ // ROW_TILE
    row = lambda r: (r, 0)
    const = lambda r: (0, 0)
    return pl.pallas_call(
        _ln0_inproj_kernel,
        grid=(N_TOK // ROW_TILE,),
        in_specs=[
            pl.BlockSpec((ROW_TILE, D_MODEL), row),
            pl.BlockSpec((1, D_MODEL), const),
            pl.BlockSpec((1, D_MODEL), const),
            pl.BlockSpec((D_MODEL, 2 * DIFF_WIDTH), const),
            pl.BlockSpec((DIFF_WIDTH, D_MODEL), const),
            pl.BlockSpec((D_MODEL, FOURIER_WIDTH + MEM_WIDTH), const),
        ],
        out_specs=[
            pl.BlockSpec((ROW_TILE, D_MODEL), row),
            pl.BlockSpec((ROW_TILE, DIFF_WIDTH), row),
            pl.BlockSpec((ROW_TILE, DIFF_WIDTH), row),
            pl.BlockSpec((1, DIFF_HEADS, VT_ROWS, ROW_TILE), lambda r: (r // rt, 0, 0, r % rt)),
            pl.BlockSpec((ROW_TILE, FOURIER_WIDTH), row),
            pl.BlockSpec((ROW_TILE, MEM_WIDTH), row),
        ],
        out_shape=[
            jax.ShapeDtypeStruct((N_TOK, D_MODEL), F32),
            jax.ShapeDtypeStruct((N_TOK, DIFF_WIDTH), BF16),
            jax.ShapeDtypeStruct((N_TOK, DIFF_WIDTH), BF16),
            jax.ShapeDtypeStruct((BATCH, DIFF_HEADS, VT_ROWS, SEQ), BF16),
            jax.ShapeDtypeStruct((N_TOK, FOURIER_WIDTH), BF16),
            jax.ShapeDtypeStruct((N_TOK, MEM_WIDTH), BF16),
        ],
        compiler_params=_vmem(48 << 20),
        name="ln0_inproj",
    )(x2, ln_g, ln_b, w_qk, w_vt, w_fm)


def _diff_attn_kernel(tbl_ref, far_ref, q_ref, k_ref, vt_ref, bkt_ref, lam_ref, g_ref, o_ref,
                      bias_scr, s0_scr, s1_scr, p0_scr, p1_scr):
    T = ATT_TILE
    nk = SEQ // T
    h = pl.program_id(0)
    i = pl.program_id(2)
    s_scr = (s0_scr, s1_scr)
    p_scr = (p0_scr, p1_scr)

    @pl.when((pl.program_id(1) == 0) & (i == 0))
    def _build_bias():
        for d in range(5):
            bk = bkt_ref[d]
            bias = jnp.zeros((T, T), F32)
            for n in range(N_BUCKETS):
                bias = jnp.where(bk == n, tbl_ref[n, h], bias)
            bias_scr[d] = bias * LOG2E

    q = q_ref[...]
    lane = lax.broadcasted_iota(jnp.int32, q.shape, 1)
    zero = jnp.zeros_like(q)
    q_comp = (jnp.where(lane < DIFF_HEAD_DIM, q, zero), jnp.where(lane >= DIFF_HEAD_DIM, q, zero))

    for c in range(2):
        s_scr[c][...] = lax.dot_general(k_ref[...], q_comp[c], _NT, preferred_element_type=F32)

    js = jnp.clip(i - 1, 0, nk - 3)
    for w in range(3):
        rows = pl.ds(pl.multiple_of((js + w) * T, T), T)
        bias = bias_scr[js + w - i + 2]
        for c in range(2):
            s_scr[c][rows, :] = s_scr[c][rows, :] + bias
    c_left = tbl_ref[far_ref[0], h] * LOG2E
    c_right = tbl_ref[far_ref[1], h] * LOG2E
    side = [jnp.where(j < js, c_left, jnp.where(j > js + 2, c_right, 0.0)) for j in range(nk)]

    m8 = [None, None]
    for j in range(nk):
        for c in range(2):
            t = jnp.max(s_scr[c][j * T:(j + 1) * T, :].reshape(T // 8, 8, T), axis=0) + side[j]
            m8[c] = t if m8[c] is None else jnp.maximum(m8[c], t)
    m = [jnp.max(m8[c], axis=0, keepdims=True) for c in range(2)]

    for j in range(nk):
        for c in range(2):
            x = s_scr[c][j * T:(j + 1) * T, :] + (side[j] - m[c])
            p_scr[c][j * T:(j + 1) * T, :] = jnp.exp2(x.astype(BF16))

    acc = [jnp.dot(vt_ref[0, 0], p_scr[c][...], preferred_element_type=F32) for c in range(2)]
    num = [a[:DIFF_VDIM] for a in acc]
    den = [a[DIFF_VDIM:DIFF_VDIM + 1] for a in acc]

    lam = (jnp.exp(jnp.sum(lam_ref[0:1, :] * lam_ref[1:2, :], axis=1, keepdims=True))
           - jnp.exp(jnp.sum(lam_ref[2:3, :] * lam_ref[3:4, :], axis=1, keepdims=True)) + LAM_INIT)
    o = num[0] * (1.0 / den[0]) - lam * (num[1] * (1.0 / den[1]))
    ms = jnp.mean(o * o, axis=0, keepdims=True)
    o = o * lax.rsqrt(ms + LN_EPS) * g_ref[...] * (1.0 - LAM_INIT)
    o_ref[...] = o.T.astype(BF16)


def _diff_attention(rel_bias, far_buckets, q, k, vt, buckets, lam_vecs, subln_col):
    T = ATT_TILE
    nq = SEQ // T
    return pl.pallas_call(
        _diff_attn_kernel,
        grid=(DIFF_HEADS, BATCH, nq),
        in_specs=[
            pl.BlockSpec(memory_space=pltpu.SMEM),
            pl.BlockSpec(memory_space=pltpu.SMEM),
            pl.BlockSpec((T, DIFF_VDIM), lambda h, b, i: (b * nq + i, h)),
            pl.BlockSpec((SEQ, DIFF_VDIM), lambda h, b, i: (b, h)),
            pl.BlockSpec((1, 1, VT_ROWS, SEQ), lambda h, b, i: (b, h, 0, 0)),
            pl.BlockSpec((5, T, T), lambda h, b, i: (0, 0, 0)),
            pl.BlockSpec((4, DIFF_HEAD_DIM), lambda h, b, i: (0, 0)),
            pl.BlockSpec((DIFF_VDIM, 1), lambda h, b, i: (0, 0)),
        ],
        out_specs=pl.BlockSpec((T, DIFF_VDIM), lambda h, b, i: (b * nq + i, h)),
        out_shape=jax.ShapeDtypeStruct((N_TOK, DIFF_WIDTH), BF16),
        scratch_shapes=[pltpu.VMEM((5, T, T), F32),
                        pltpu.VMEM((SEQ, T), F32), pltpu.VMEM((SEQ, T), F32),
                        pltpu.VMEM((SEQ, T), BF16), pltpu.VMEM((SEQ, T), BF16)],
        compiler_params=_vmem(40 << 20),
        name="diff_attn",
    )(rel_bias, far_buckets, q, k, vt, buckets, lam_vecs, subln_col)


def _fourier_kernel(f_ref, bdc_ref, bds_ref, bdw_ref, m1_ref, m2_ref, twc_ref, tws_ref, o_ref,
                    zr_scr, zi_scr, d_scr, a_scr):
    R = FFT_RADIX
    W = FOURIER_WIDTH
    u = f_ref[...]
    w = bdw_ref[...]
    pc = jnp.dot(u, bdc_ref[...], preferred_element_type=F32).astype(BF16)
    ps = jnp.dot(u, bds_ref[...], preferred_element_type=F32).astype(BF16)
    zr = jnp.dot(pc, w, preferred_element_type=F32)
    zi = jnp.dot(ps, w, preferred_element_type=F32)
    halves = range(W // 128)
    for hh in halves:
        zr_scr[hh] = zr[:, hh * 128:(hh + 1) * 128]
        zi_scr[hh] = zi[:, hh * 128:(hh + 1) * 128]

    for n2 in range(R):
        for hh in halves:
            cols = slice(n2 * W + hh * 128, n2 * W + (hh + 1) * 128)
            d_scr[0:R, cols] = zr_scr[hh, pl.ds(n2, R, stride=R), :].astype(BF16)
            d_scr[R:2 * R, cols] = zi_scr[hh, pl.ds(n2, R, stride=R), :].astype(BF16)
    a_scr[...] = jnp.dot(m1_ref[...], d_scr[...], preferred_element_type=F32)

    for n2 in range(R):
        tc = twc_ref[n2]
        ts = tws_ref[n2]
        for hh in halves:
            cols = slice(n2 * W + hh * 128, n2 * W + (hh + 1) * 128)
            ar = a_scr[0:R, cols]
            ai = a_scr[R:2 * R, cols]
            zr_scr[hh, pl.ds(n2, R, stride=R), :] = ar * tc + ai * ts
            zi_scr[hh, pl.ds(n2, R, stride=R), :] = ai * tc - ar * ts
    for k1 in range(R):
        for hh in halves:
            cols = slice(k1 * W + hh * 128, k1 * W + (hh + 1) * 128)
            d_scr[0:R, cols] = zr_scr[hh, k1 * R:(k1 + 1) * R, :].astype(BF16)
            d_scr[R:2 * R, cols] = zi_scr[hh, k1 * R:(k1 + 1) * R, :].astype(BF16)
    a_scr[0:R, :] = jnp.dot(m2_ref[...], d_scr[...], preferred_element_type=F32)
    for k1 in range(R):
        for hh in halves:
            o_ref[hh, pl.ds(k1, R, stride=R), :] = a_scr[0:R, k1 * W + hh * 128:k1 * W + (hh + 1) * 128]


def _fourier(f_in, bdc, bds, bdw, m1, m2, twc, tws):
    R = FFT_RADIX
    W = FOURIER_WIDTH
    const2 = lambda b: (0, 0)
    const3 = lambda b: (0, 0, 0)
    return pl.pallas_call(
        _fourier_kernel,
        grid=(BATCH,),
        in_specs=[
            pl.BlockSpec((SEQ, W), lambda b: (b, 0)),
            pl.BlockSpec((W, W), const2),
            pl.BlockSpec((W, W), const2),
            pl.BlockSpec((W, W), const2),
            pl.BlockSpec((2 * R, 2 * R), const2),
            pl.BlockSpec((R, 2 * R), const2),
            pl.BlockSpec((R, R, 128), const3),
            pl.BlockSpec((R, R, 128), const3),
        ],
        out_specs=pl.BlockSpec((W // 128, SEQ, 128), lambda b: (0, b, 0)),
        out_shape=jax.ShapeDtypeStruct((W // 128, N_TOK, 128), F32),
        scratch_shapes=[pltpu.VMEM((W // 128, SEQ, 128), F32), pltpu.VMEM((W // 128, SEQ, 128), F32),
                        pltpu.VMEM((2 * R, R * W), BF16), pltpu.VMEM((2 * R, R * W), F32)],
        compiler_params=_vmem(52 << 20),
        name="fourier",
    )(f_in, bdc, bds, bdw, m1, m2, twc, tws)


@functools.lru_cache(maxsize=None)
def _dft_constants():
    R = FFT_RADIX
    a = np.arange(R, dtype=np.int64)
    ang = 2.0 * np.pi * ((a[:, None] * a[None, :]) % R).astype(np.float64) / R
    c64, s64 = np.cos(ang), np.sin(ang)
    scale = 1.0 / math.sqrt(SEQ * FOURIER_GROUP_DIM)
    m1 = np.block([[c64, s64], [-s64, c64]])
    m2 = np.concatenate([c64, s64], axis=1) * scale
    tang = 2.0 * np.pi * (a[:, None] * a[None, :]).astype(np.float64) / SEQ
    twc = np.repeat(np.cos(tang)[:, :, None], 128, axis=2).astype(np.float32)
    tws = np.repeat(np.sin(tang)[:, :, None], 128, axis=2).astype(np.float32)
    assert FOURIER_GROUP_DIM == R
    eye = np.eye(FOURIER_GROUPS)
    bdc = np.kron(eye, c64)
    bds = np.kron(eye, -s64)
    return bdc.astype(BF16), bds.astype(BF16), m1.astype(BF16), m2.astype(BF16), twc, tws


def _mem_kv_kernel(mem_ref, w_ref, kcat_ref, vcat_ref):
    kv = jnp.dot(mem_ref[...].astype(BF16), w_ref[...], preferred_element_type=F32)
    mk = kv[:, :MEM_WIDTH]
    mv = kv[:, MEM_WIDTH:]
    lane = lax.broadcasted_iota(jnp.int32, mk.shape, 1)
    for hh in range(MEM_HEADS):
        sel = (lane >= hh * MEM_HEAD_DIM) & (lane < (hh + 1) * MEM_HEAD_DIM)
        kcat_ref[0, hh * MEM_LEN:(hh + 1) * MEM_LEN, :] = jnp.where(sel, mk, 0.0).astype(BF16)
        vcat_ref[0, hh * MEM_LEN:(hh + 1) * MEM_LEN, :] = jnp.where(sel, mv, 0.0).astype(BF16)


def _mem_kv(mem2, w_kv):
    return pl.pallas_call(
        _mem_kv_kernel,
        grid=(BATCH,),
        in_specs=[
            pl.BlockSpec((MEM_LEN, D_MODEL), lambda b: (b, 0)),
            pl.BlockSpec((D_MODEL, 2 * MEM_WIDTH), lambda b: (0, 0)),
        ],
        out_specs=[
            pl.BlockSpec((1, MEM_HEADS * MEM_LEN, MEM_WIDTH), lambda b: (b, 0, 0)),
            pl.BlockSpec((1, MEM_HEADS * MEM_LEN, MEM_WIDTH), lambda b: (b, 0, 0)),
        ],
        out_shape=[
            jax.ShapeDtypeStruct((BATCH, MEM_HEADS * MEM_LEN, MEM_WIDTH), BF16),
            jax.ShapeDtypeStruct((BATCH, MEM_HEADS * MEM_LEN, MEM_WIDTH), BF16),
        ],
        name="mem_kv",
    )(mem2, w_kv)


def _mem_attn_kernel(mq_ref, kcat_ref, vcat_ref, o_ref):
    s = lax.dot_general(mq_ref[...], kcat_ref[0], _NT, preferred_element_type=F32)
    parts = []
    for hh in range(MEM_HEADS):
        sh = s[:, hh * MEM_LEN:(hh + 1) * MEM_LEN]
        p = jnp.exp(sh - jnp.max(sh, axis=1, keepdims=True))
        parts.append((p * (1.0 / jnp.sum(p, axis=1, keepdims=True))).astype(BF16))
    p_all = jnp.concatenate(parts, axis=1)
    o_ref[...] = jnp.dot(p_all, vcat_ref[0], preferred_element_type=F32).astype(BF16)


def _mem_attention(mq, kcat, vcat):
    rt = SEQ // ROW_TILE
    return pl.pallas_call(
        _mem_attn_kernel,
        grid=(N_TOK // ROW_TILE,),
        in_specs=[
            pl.BlockSpec((ROW_TILE, MEM_WIDTH), lambda r: (r, 0)),
            pl.BlockSpec((1, MEM_HEADS * MEM_LEN, MEM_WIDTH), lambda r: (r // rt, 0, 0)),
            pl.BlockSpec((1, MEM_HEADS * MEM_LEN, MEM_WIDTH), lambda r: (r // rt, 0, 0)),
        ],
        out_specs=pl.BlockSpec((ROW_TILE, MEM_WIDTH), lambda r: (r, 0)),
        out_shape=jax.ShapeDtypeStruct((N_TOK, MEM_WIDTH), BF16),
        name="mem_attn",
    )(mq, kcat, vcat)


def _route(logits):
    lane = lax.broadcasted_iota(jnp.int32, logits.shape, 1)
    big = jnp.int32(ROUTER_LANES)
    is_group = (lane >= GROUP_LANE0) & (lane < GROUP_LANE0 + N_GROUPS)
    gl = jnp.where(is_group, logits, -jnp.inf)
    gmax = jnp.max(gl, axis=1, keepdims=True)
    g_sel = jnp.min(jnp.where(gl == gmax, lane, big), axis=1, keepdims=True) - GROUP_LANE0
    g_gate = 1.0 / jnp.sum(jnp.where(is_group, jnp.exp(gl - gmax), 0.0), axis=1, keepdims=True)
    in_group = (lane >= g_sel * EXPERTS_PER_GROUP) & (lane < (g_sel + 1) * EXPERTS_PER_GROUP)
    el = jnp.where(in_group, logits, -jnp.inf)
    v1 = jnp.max(el, axis=1, keepdims=True)
    i1 = jnp.min(jnp.where(el == v1, lane, big), axis=1, keepdims=True)
    el2 = jnp.where(lane == i1, -jnp.inf, el)
    v2 = jnp.max(el2, axis=1, keepdims=True)
    i2 = jnp.min(jnp.where(el2 == v2, lane, big), axis=1, keepdims=True)
    e = jnp.exp(v2 - v1)
    w_first = g_gate / (1.0 + e)
    w_second = g_gate * e / (1.0 + e)
    return i1, i2, w_first, w_second


def _outproj_router_kernel(od_ref, of_ref, om_ref, h_ref, wo_ref, g_ref, b_ref,
                           wr_ref, br_ref,
                           h1_ref, route_ref, w0_ref, w1_ref, cnt_ref, cnt_scr):
    @pl.when(pl.program_id(0) == 0)
    def _():
        cnt_scr[...] = jnp.zeros_like(cnt_scr)

    o_four = [of_ref[hh].astype(BF16) for hh in range(FOURIER_WIDTH // 128)]
    o = jnp.concatenate([od_ref[...]] + o_four + [om_ref[...]], axis=1)
    a = jnp.dot(o, wo_ref[...], preferred_element_type=F32)
    h1 = _layer_norm(DEEPNORM_ALPHA * h_ref[...] + a, g_ref[...], b_ref[...])
    _store_row_tiles(h1_ref, h1)
    hi = h1.astype(BF16)
    lo = (h1 - hi.astype(F32)).astype(BF16)
    hw = jnp.dot(hi, wr_ref[...], preferred_element_type=F32)
    logits = (hw[:, :ROUTER_LANES] + hw[:, ROUTER_LANES:]
              + jnp.dot(lo, wr_ref[:, :ROUTER_LANES], preferred_element_type=F32) + br_ref[...])
    i1, i2, w_first, w_second = _route(logits)

    lane = lax.broadcasted_iota(jnp.int32, logits.shape, 1)
    onehot = jnp.where(lane == i1, 1.0, jnp.where(lane == i2, 1.0, 0.0))
    r_id = lax.broadcasted_iota(jnp.int32, (ROW_TILE, ROW_TILE), 0)
    c_id = lax.broadcasted_iota(jnp.int32, (ROW_TILE, ROW_TILE), 1)
    tri = jnp.where(r_id > c_id, 1.0, 0.0).astype(BF16)
    before = jnp.dot(tri, onehot.astype(BF16), preferred_element_type=F32) + cnt_scr[...]
    rank1 = jnp.sum(jnp.where(lane == i1, before, 0.0), axis=1, keepdims=True)
    rank2 = jnp.sum(jnp.where(lane == i2, before, 0.0), axis=1, keepdims=True)
    cnt_scr[...] += jnp.sum(onehot, axis=0, keepdims=True)
    cnt_ref[...] = cnt_scr[...]

    packed = jnp.where(lane == 0, i1.astype(F32),
                       jnp.where(lane == 1, i2.astype(F32),
                                 jnp.where(lane == 2, rank1, jnp.where(lane == 3, rank2, 0.0))))
    route_ref[...] = packed.T[:8, :].astype(jnp.int32)
    w0_ref[...] = jnp.broadcast_to(w_first, (ROW_TILE, 128))
    w1_ref[...] = jnp.broadcast_to(w_second, (ROW_TILE, 128))


def _outproj_router(o_diff, o_four, o_mem, h, w_out, ln_g, ln_b, w_r, b_r):
    row = lambda r: (r, 0)
    const = lambda r: (0, 0)
    return pl.pallas_call(
        _outproj_router_kernel,
        grid=(N_TOK // ROW_TILE,),
        in_specs=[
            pl.BlockSpec((ROW_TILE, DIFF_WIDTH), row),
            pl.BlockSpec((FOURIER_WIDTH // 128, ROW_TILE, 128), lambda r: (0, r, 0)),
            pl.BlockSpec((ROW_TILE, MEM_WIDTH), row),
            pl.BlockSpec((ROW_TILE, D_MODEL), row),
            pl.BlockSpec((D_MODEL, D_MODEL), const),
            pl.BlockSpec((1, D_MODEL), const),
            pl.BlockSpec((1, D_MODEL), const),
            pl.BlockSpec((D_MODEL, 2 * ROUTER_LANES), const),
            pl.BlockSpec((1, ROUTER_LANES), const),
        ],
        out_specs=[
            pl.BlockSpec((ROW_TILE * ROW_SLABS, 128), row),
            pl.BlockSpec((8, ROW_TILE), lambda r: (0, r)),
            pl.BlockSpec((ROW_TILE, 128), row),
            pl.BlockSpec((ROW_TILE, 128), row),
            pl.BlockSpec((1, ROUTER_LANES), const),
        ],
        out_shape=[
            jax.ShapeDtypeStruct((N_TOK * ROW_SLABS, 128), F32),
            jax.ShapeDtypeStruct((8, N_TOK), jnp.int32),
            jax.ShapeDtypeStruct((N_TOK, 128), F32),
            jax.ShapeDtypeStruct((N_TOK, 128), F32),
            jax.ShapeDtypeStruct((1, ROUTER_LANES), F32),
        ],
        scratch_shapes=[pltpu.VMEM((1, ROUTER_LANES), F32)],
        compiler_params=_vmem(40 << 20),
        name="outproj_router",
    )(o_diff, o_four, o_mem, h, w_out, ln_g, ln_b, w_r, b_r)


def _dispatch_kernel(pos0_ref, pos1_ref, x_ref, xs_ref, sem):
    base = pl.program_id(0) * DISPATCH_TILE

    def issue(t, carry):
        src = _row_tile(x_ref, t)
        pltpu.make_async_copy(src, _row_tile(xs_ref, pos0_ref[base + t]), sem).start(priority=0)
        pltpu.make_async_copy(src, _row_tile(xs_ref, pos1_ref[base + t]), sem).start(priority=1)
        return carry

    lax.fori_loop(0, DISPATCH_TILE, issue, 0, unroll=8)
    for _ in range(2):
        pltpu.make_async_copy(x_ref, xs_ref.at[pl.ds(0, DISPATCH_TILE * ROW_SLABS), :], sem).wait()


def _dispatch(pos0, pos1, h1_rows):
    return pl.pallas_call(
        _dispatch_kernel,
        grid_spec=pltpu.PrefetchScalarGridSpec(
            num_scalar_prefetch=2,
            grid=(N_TOK // DISPATCH_TILE,),
            in_specs=[pl.BlockSpec((DISPATCH_TILE * ROW_SLABS, 128), lambda i, p0, p1: (i, 0))],
            out_specs=pl.BlockSpec(memory_space=pl.ANY),
            scratch_shapes=[pltpu.SemaphoreType.DMA(())],
        ),
        out_shape=jax.ShapeDtypeStruct((N_SLOTS * ROW_SLABS, 128), F32),
        name="moe_dispatch",
    )(pos0, pos1, h1_rows)


def _expert_kernel(tile_end_ref, xs_ref, w1_ref, w3_ref, w2_ref, ys_ref):
    @pl.when(pl.program_id(0) < tile_end_ref[N_EXPERTS - 1])
    def _():
        x = _load_row_tiles(xs_ref, MOE_TILE).astype(BF16)
        a = jnp.dot(x, w1_ref[0].astype(BF16), preferred_element_type=F32)
        b = jnp.dot(x, w3_ref[0].astype(BF16), preferred_element_type=F32)
        hid = (a * jax.nn.sigmoid(a) * b).astype(BF16)
        y = jnp.dot(hid, w2_ref[0].astype(BF16), preferred_element_type=F32)
        _store_row_tiles(ys_ref, y)


def _experts(tile_end, xs, w1, w3, w2):
    def tile(t, tile_end):
        return jnp.minimum(t, tile_end[N_EXPERTS - 1] - 1), 0

    def wsel(t, tile_end):
        e = jnp.int32(0)
        step = N_EXPERTS // 2
        while step:
            e = jnp.where(t >= tile_end[e + step - 1], e + step, e)
            step //= 2
        return e, 0, 0

    return pl.pallas_call(
        _expert_kernel,
        grid_spec=pltpu.PrefetchScalarGridSpec(
            num_scalar_prefetch=1,
            grid=(N_MOE_TILES,),
            in_specs=[
                pl.BlockSpec((MOE_TILE * ROW_SLABS, 128), tile),
                pl.BlockSpec((1, D_MODEL, D_EXPERT), wsel),
                pl.BlockSpec((1, D_MODEL, D_EXPERT), wsel),
                pl.BlockSpec((1, D_EXPERT, D_MODEL), wsel),
            ],
            out_specs=pl.BlockSpec((MOE_TILE * ROW_SLABS, 128), tile),
        ),
        out_shape=jax.ShapeDtypeStruct((N_SLOTS * ROW_SLABS, 128), F32),
        compiler_params=_vmem(40 << 20),
        name="moe_experts",
    )(tile_end, xs, w1, w3, w2)


def _combine_kernel(pos0_ref, pos1_ref, ys_ref, h1_ref, w0_ref, w1_ref, g_ref, b_ref, o_ref,
                    y0_buf, y1_buf, sems):
    i = pl.program_id(0)
    n = pl.num_programs(0)

    def gather(tile, slot):
        base = tile * COMBINE_TILE

        def issue(t, carry):
            pltpu.make_async_copy(_row_tile(ys_ref, pos0_ref[base + t]), _row_tile(y0_buf.at[slot], t),
                                  sems.at[slot]).start(priority=0)
            pltpu.make_async_copy(_row_tile(ys_ref, pos1_ref[base + t]), _row_tile(y1_buf.at[slot], t),
                                  sems.at[slot]).start(priority=1)
            return carry

        lax.fori_loop(0, COMBINE_TILE, issue, 0, unroll=8)

    @pl.when(i == 0)
    def _():
        gather(0, 0)

    @pl.when(i + 1 < n)
    def _():
        gather(i + 1, (i + 1) % 2)

    slot = i % 2
    whole = ys_ref.at[pl.ds(0, COMBINE_TILE * ROW_SLABS), :]
    pltpu.make_async_copy(whole, y0_buf.at[slot], sems.at[slot]).wait()
    pltpu.make_async_copy(whole, y1_buf.at[slot], sems.at[slot]).wait()
    w0 = jnp.concatenate([w0_ref[...]] * ROW_SLABS, axis=1)
    w1 = jnp.concatenate([w1_ref[...]] * ROW_SLABS, axis=1)
    f = (w0 * _load_row_tiles(y0_buf.at[slot], COMBINE_TILE)
         + w1 * _load_row_tiles(y1_buf.at[slot], COMBINE_TILE))
    h1 = _load_row_tiles(h1_ref, COMBINE_TILE)
    o_ref[...] = _layer_norm(DEEPNORM_ALPHA * h1 + f, g_ref[...], b_ref[...])


def _combine(pos0, pos1, ys, h1_rows, w0, w1, ln_g, ln_b):
    row = lambda i, p0, p1: (i, 0)
    const = lambda i, p0, p1: (0, 0)
    return pl.pallas_call(
        _combine_kernel,
        grid_spec=pltpu.PrefetchScalarGridSpec(
            num_scalar_prefetch=2,
            grid=(N_TOK // COMBINE_TILE,),
            in_specs=[
                pl.BlockSpec(memory_space=pl.ANY),
                pl.BlockSpec((COMBINE_TILE * ROW_SLABS, 128), row),
                pl.BlockSpec((COMBINE_TILE, 128), row),
                pl.BlockSpec((COMBINE_TILE, 128), row),
                pl.BlockSpec((1, D_MODEL), const),
                pl.BlockSpec((1, D_MODEL), const),
            ],
            out_specs=pl.BlockSpec((COMBINE_TILE, D_MODEL), row),
            scratch_shapes=[pltpu.VMEM((2, COMBINE_TILE * ROW_SLABS, 128), F32),
                            pltpu.VMEM((2, COMBINE_TILE * ROW_SLABS, 128), F32),
                            pltpu.SemaphoreType.DMA((2,))],
        ),
        out_shape=jax.ShapeDtypeStruct((N_TOK, D_MODEL), F32),
        compiler_params=pltpu.CompilerParams(dimension_semantics=("arbitrary",), vmem_limit_bytes=40 << 20),
        name="moe_combine",
    )(pos0, pos1, ys, h1_rows, w0, w1, ln_g, ln_b)


def kernel(x, mem, ln0_g, ln0_b, rel_bias, w_in, w_mem_kv, w_fourier, lambda_q1, lambda_k1, lambda_q2,
           lambda_k2, subln_g, w_out, ln1_g, ln1_b, w_group, b_group, w_router, b_router, w1, w3, w2,
           ln2_g, ln2_b):
    l = 0
    x2 = x.reshape(N_TOK, D_MODEL)
    wi = w_in[l]
    w_qk = wi[:, :2 * DIFF_WIDTH].astype(BF16)
    w_vt = wi[:, 2 * DIFF_WIDTH:3 * DIFF_WIDTH].T.astype(BF16)
    w_fm = wi[:, 3 * DIFF_WIDTH:].astype(BF16)
    h, q, k, vt, f_in, mq = _ln0_inproj(x2, ln0_g.reshape(1, -1), ln0_b.reshape(1, -1), w_qk, w_vt, w_fm)

    T = ATT_TILE
    kk = jnp.arange(T, dtype=jnp.int32)[:, None]
    qq = jnp.arange(T, dtype=jnp.int32)[None, :]
    buckets = jnp.stack([_t5_bucket(d * T + kk - qq) for d in range(-2, 3)])
    lam_vecs = jnp.stack([lambda_q1[l], lambda_k1[l], lambda_q2[l], lambda_k2[l]]).astype(F32)
    far_buckets = _t5_bucket(jnp.array([-(T + 1), T + 1], jnp.int32))
    o_diff = _diff_attention(rel_bias.astype(F32), far_buckets, q, k, vt, buckets, lam_vecs,
                             subln_g[l].astype(F32).reshape(DIFF_VDIM, 1))

    bdc, bds, m1, m2, twc, tws = (jnp.asarray(c) for c in _dft_constants())
    wf = w_fourier[l]
    bdw = jnp.zeros((FOURIER_WIDTH, FOURIER_WIDTH), F32)
    for g in range(FOURIER_GROUPS):
        sl = slice(g * FOURIER_GROUP_DIM, (g + 1) * FOURIER_GROUP_DIM)
        bdw = bdw.at[sl, sl].set(wf[g])
    o_four = _fourier(f_in, bdc, bds, bdw.astype(BF16), m1, m2, twc, tws)

    kcat, vcat = _mem_kv(mem.reshape(BATCH * MEM_LEN, D_MODEL), w_mem_kv[l].astype(BF16))
    o_mem = _mem_attention(mq, kcat, vcat)

    w_r = jnp.zeros((D_MODEL, ROUTER_LANES), F32)
    w_r = w_r.at[:, :N_EXPERTS].set(w_router[l].astype(F32))
    w_r = w_r.at[:, GROUP_LANE0:GROUP_LANE0 + N_GROUPS].set(w_group[l].astype(F32))
    wr_hi = w_r.astype(BF16)
    wr_lo = (w_r - wr_hi.astype(F32)).astype(BF16)
    b_r = jnp.zeros((1, ROUTER_LANES), F32)
    b_r = b_r.at[0, :N_EXPERTS].set(b_router[l].astype(F32))
    b_r = b_r.at[0, GROUP_LANE0:GROUP_LANE0 + N_GROUPS].set(b_group[l].astype(F32))
    h1_rows, route, gate0, gate1, counts = _outproj_router(
        o_diff, o_four, o_mem, h, w_out[l].astype(BF16),
        ln1_g[l].reshape(1, -1), ln1_b[l].reshape(1, -1), jnp.concatenate([wr_hi, wr_lo], axis=1), b_r)

    cnt = counts[0, :N_EXPERTS].astype(jnp.int32)
    tiles_per_expert = (cnt + MOE_TILE - 1) // MOE_TILE
    tile_end = jnp.cumsum(tiles_per_expert).astype(jnp.int32)
    row_start = (tile_end - tiles_per_expert) * MOE_TILE
    expert_ids = jnp.arange(N_EXPERTS, dtype=jnp.int32)[None, :]

    def slot_of(expert, rank):
        return jnp.sum(jnp.where(expert[:, None] == expert_ids, row_start[None, :], 0), axis=1) + rank

    pos0 = slot_of(route[0], route[2])
    pos1 = slot_of(route[1], route[3])

    xs = _dispatch(pos0, pos1, h1_rows)
    ys = _experts(tile_end, xs, w1[l], w3[l], w2[l])
    out = _combine(pos0, pos1, ys, h1_rows, gate0, gate1, ln2_g[l].reshape(1, -1), ln2_b[l].reshape(1, -1))
    return out.reshape(BATCH, SEQ, D_MODEL)
```

```python
import functools
import math

import numpy as np
import jax
import jax.numpy as jnp
from jax import lax
from jax.experimental import pallas as pl
from jax.experimental.pallas import tpu as pltpu

F32 = jnp.float32
BF16 = jnp.bfloat16

D_MODEL = 1024
BATCH = 4
SEQ = 4096
N_TOK = BATCH * SEQ
MEM_LEN = 256
MEM_HEADS = 4
MEM_HEAD_DIM = 64
MEM_WIDTH = 256
DIFF_HEADS = 4
DIFF_HEAD_DIM = 64
DIFF_VDIM = 2 * DIFF_HEAD_DIM
VT_ROWS = DIFF_VDIM + 16
DIFF_WIDTH = 512
FOURIER_GROUPS = 4
FOURIER_GROUP_DIM = 64
FOURIER_WIDTH = 256
N_BUCKETS = 32
MAX_DISTANCE = 128
N_GROUPS = 4
EXPERTS_PER_GROUP = 8
N_EXPERTS = 32
D_EXPERT = 256
LN_EPS = 1e-5
DEEPNORM_ALPHA = 2.0 ** 0.25
LAM_INIT = 0.8 - 0.6 * math.exp(-0.3 * 0)
LOG2E = 1.4426950408889634

ROW_TILE = 512
ATT_TILE = 256
ATT_Q = 256
MOE_TILE = 512
N_MOE_TILES = 2 * N_TOK // MOE_TILE + N_EXPERTS
N_SLOTS = N_MOE_TILES * MOE_TILE
ROW_SLABS = D_MODEL // 128
DISPATCH_TILE = 512
COMBINE_TILE = 512
FFT_RADIX = 64
FFT_PITCH = 72
ROUTER_LANES = 128
GROUP_LANE0 = 32
NEG_BIG = -1e30

_NT = (((1,), (1,)), ((), ()))


def _vmem(nbytes):
    return pltpu.CompilerParams(vmem_limit_bytes=int(nbytes))


def _layer_norm(x, g, b):
    mu = jnp.mean(x, axis=-1, keepdims=True)
    xc = x - mu
    var = jnp.mean(xc * xc, axis=-1, keepdims=True)
    return xc * lax.rsqrt(var + LN_EPS) * g + b


def _load_row_tiles(ref, rows):
    return jnp.concatenate([ref[pl.ds(s, rows, stride=ROW_SLABS), :] for s in range(ROW_SLABS)], axis=1)


def _store_row_tiles(ref, x):
    for s in range(ROW_SLABS):
        ref[pl.ds(s, x.shape[0], stride=ROW_SLABS), :] = x[:, s * 128:(s + 1) * 128]


def _row_tile(ref, t):
    return ref.at[pl.ds(pl.multiple_of(t * ROW_SLABS, ROW_SLABS), ROW_SLABS), :]


def _t5_bucket(rel):
    nb = N_BUCKETS // 2
    max_exact = nb // 2
    ret = (rel > 0).astype(jnp.int32) * nb
    n = jnp.abs(rel)
    nf = jnp.maximum(n, 1).astype(F32)
    large = max_exact + (jnp.log(nf / max_exact) / math.log(MAX_DISTANCE / max_exact)
                         * (nb - max_exact)).astype(jnp.int32)
    large = jnp.minimum(large, nb - 1)
    return ret + jnp.where(n < max_exact, n, large)


def _ln0_inproj_kernel(x_ref, g_ref, b_ref, wqk_ref, wvt_ref, wfm_ref,
                       h_ref, q_ref, k_ref, vt_ref, f_ref, mq_ref):
    h = _layer_norm(x_ref[...], g_ref[...], b_ref[...])
    h_ref[...] = h
    hb = h.astype(BF16)
    qk = jnp.dot(hb, wqk_ref[...], preferred_element_type=F32)
    q_ref[...] = (qk[:, :DIFF_WIDTH] * (DIFF_HEAD_DIM ** -0.5 * LOG2E)).astype(BF16)
    k_ref[...] = qk[:, DIFF_WIDTH:].astype(BF16)
    vt = lax.dot_general(wvt_ref[...], hb, _NT, preferred_element_type=F32)
    ones = jnp.ones((VT_ROWS - DIFF_VDIM, ROW_TILE), BF16)
    for hh in range(DIFF_HEADS):
        vt_ref[0, hh, :DIFF_VDIM, :] = vt[hh * DIFF_VDIM:(hh + 1) * DIFF_VDIM, :].astype(BF16)
        vt_ref[0, hh, DIFF_VDIM:, :] = ones
    fm = jnp.dot(hb, wfm_ref[...], preferred_element_type=F32)
    f_ref[...] = fm[:, :FOURIER_WIDTH].astype(BF16)
    mq_ref[...] = (fm[:, FOURIER_WIDTH:] * MEM_HEAD_DIM ** -0.5).astype(BF16)


def _ln0_inproj(x2, ln_g, ln_b, w_qk, w_vt, w_fm):
    rt = SEQ // ROW_TILE
    row = lambda r: (r, 0)
    const = lambda r: (0, 0)
    return pl.pallas_call(
        _ln0_inproj_kernel,
        grid=(N_TOK // ROW_TILE,),
        in_specs=[
            pl.BlockSpec((ROW_TILE, D_MODEL), row),
            pl.BlockSpec((1, D_MODEL), const),
            pl.BlockSpec((1, D_MODEL), const),
            pl.BlockSpec((D_MODEL, 2 * DIFF_WIDTH), const),
            pl.BlockSpec((DIFF_WIDTH, D_MODEL), const),
            pl.BlockSpec((D_MODEL, FOURIER_WIDTH + MEM_WIDTH), const),
        ],
        out_specs=[
            pl.BlockSpec((ROW_TILE, D_MODEL), row),
            pl.BlockSpec((ROW_TILE, DIFF_WIDTH), row),
            pl.BlockSpec((ROW_TILE, DIFF_WIDTH), row),
            pl.BlockSpec((1, DIFF_HEADS, VT_ROWS, ROW_TILE), lambda r: (r // rt, 0, 0, r % rt)),
            pl.BlockSpec((ROW_TILE, FOURIER_WIDTH), row),
            pl.BlockSpec((ROW_TILE, MEM_WIDTH), row),
        ],
        out_shape=[
            jax.ShapeDtypeStruct((N_TOK, D_MODEL), F32),
            jax.ShapeDtypeStruct((N_TOK, DIFF_WIDTH), BF16),
            jax.ShapeDtypeStruct((N_TOK, DIFF_WIDTH), BF16),
            jax.ShapeDtypeStruct((BATCH, DIFF_HEADS, VT_ROWS, SEQ), BF16),
            jax.ShapeDtypeStruct((N_TOK, FOURIER_WIDTH), BF16),
            jax.ShapeDtypeStruct((N_TOK, MEM_WIDTH), BF16),
        ],
        compiler_params=_vmem(48 << 20),
        name="ln0_inproj",
    )(x2, ln_g, ln_b, w_qk, w_vt, w_fm)


def _diff_attn_kernel(tbl_ref, far_ref, q_ref, k_ref, vt_ref, bkt_ref, lam_ref, g_ref, o_ref,
                      bias_scr, s0_scr, s1_scr, p0_scr, p1_scr):
    T = ATT_TILE
    TQ = ATT_Q
    r = TQ // T
    nk = SEQ // T
    h = pl.program_id(0)
    i = pl.program_id(2)
    s_scr = (s0_scr, s1_scr)
    p_scr = (p0_scr, p1_scr)

    @pl.when((pl.program_id(1) == 0) & (i == 0))
    def _build_bias():
        for d in range(r + 4):
            bk = bkt_ref[d]
            bias = jnp.zeros((T, TQ), F32)
            for n in range(N_BUCKETS):
                bias = jnp.where(bk == n, tbl_ref[n, h], bias)
            bias_scr[d] = bias * LOG2E

    q = q_ref[...]
    lane = lax.broadcasted_iota(jnp.int32, q.shape, 1)
    zero = jnp.zeros_like(q)
    q_comp = (jnp.where(lane < DIFF_HEAD_DIM, q, zero), jnp.where(lane >= DIFF_HEAD_DIM, q, zero))

    for c in range(2):
        s_scr[c][...] = lax.dot_general(k_ref[...], q_comp[c], _NT, preferred_element_type=F32)

    win = r + 2
    js = jnp.clip(r * i - 1, 0, nk - win)
    for w in range(win):
        rows = pl.ds(pl.multiple_of((js + w) * T, T), T)
        bias = bias_scr[js + w - r * i + 2]
        for c in range(2):
            s_scr[c][rows, :] = s_scr[c][rows, :] + bias
    c_left = tbl_ref[far_ref[0], h] * LOG2E
    c_right = tbl_ref[far_ref[1], h] * LOG2E
    side = [jnp.where(j < js, c_left, jnp.where(j > js + win - 1, c_right, 0.0)) for j in range(nk)]

    m8 = [None, None]
    for j in range(nk):
        for c in range(2):
            t = jnp.max(s_scr[c][j * T:(j + 1) * T, :].reshape(T // 8, 8, TQ), axis=0) + side[j]
            m8[c] = t if m8[c] is None else jnp.maximum(m8[c], t)
    m = [jnp.max(m8[c], axis=0, keepdims=True) for c in range(2)]

    for j in range(nk):
        for c in range(2):
            x = s_scr[c][j * T:(j + 1) * T, :] + (side[j] - m[c])
            p_scr[c][j * T:(j + 1) * T, :] = jnp.exp2(x.astype(BF16))

    acc = [jnp.dot(vt_ref[0, 0], p_scr[c][...], preferred_element_type=F32) for c in range(2)]
    num = [a[:DIFF_VDIM] for a in acc]
    den = [a[DIFF_VDIM:DIFF_VDIM + 1] for a in acc]

    lam = (jnp.exp(jnp.sum(lam_ref[0:1, :] * lam_ref[1:2, :], axis=1, keepdims=True))
           - jnp.exp(jnp.sum(lam_ref[2:3, :] * lam_ref[3:4, :], axis=1, keepdims=True)) + LAM_INIT)
    o = num[0] * (1.0 / den[0]) - lam * (num[1] * (1.0 / den[1]))
    ms = jnp.mean(o * o, axis=0, keepdims=True)
    o = o * lax.rsqrt(ms + LN_EPS) * g_ref[...] * (1.0 - LAM_INIT)
    o_ref[...] = o.T.astype(BF16)


def _diff_attention(rel_bias, far_buckets, q, k, vt, buckets, lam_vecs, subln_col):
    T = ATT_TILE
    TQ = ATT_Q
    nq = SEQ // TQ
    n_bias = TQ // T + 4
    return pl.pallas_call(
        _diff_attn_kernel,
        grid=(DIFF_HEADS, BATCH, nq),
        in_specs=[
            pl.BlockSpec(memory_space=pltpu.SMEM),
            pl.BlockSpec(memory_space=pltpu.SMEM),
            pl.BlockSpec((TQ, DIFF_VDIM), lambda h, b, i: (b * nq + i, h)),
            pl.BlockSpec((SEQ, DIFF_VDIM), lambda h, b, i: (b, h)),
            pl.BlockSpec((1, 1, VT_ROWS, SEQ), lambda h, b, i: (b, h, 0, 0)),
            pl.BlockSpec((n_bias, T, TQ), lambda h, b, i: (0, 0, 0)),
            pl.BlockSpec((4, DIFF_HEAD_DIM), lambda h, b, i: (0, 0)),
            pl.BlockSpec((DIFF_VDIM, 1), lambda h, b, i: (0, 0)),
        ],
        out_specs=pl.BlockSpec((TQ, DIFF_VDIM), lambda h, b, i: (b * nq + i, h)),
        out_shape=jax.ShapeDtypeStruct((N_TOK, DIFF_WIDTH), BF16),
        scratch_shapes=[pltpu.VMEM((n_bias, T, TQ), F32),
                        pltpu.VMEM((SEQ, TQ), F32), pltpu.VMEM((SEQ, TQ), F32),
                        pltpu.VMEM((SEQ, TQ), BF16), pltpu.VMEM((SEQ, TQ), BF16)],
        compiler_params=_vmem(40 << 20),
        name="diff_attn",
    )(rel_bias, far_buckets, q, k, vt, buckets, lam_vecs, subln_col)


def _fourier_kernel(f_ref, bdc_ref, bds_ref, bdw_ref, m1_ref, m2_ref, twc_ref, tws_ref, o_ref,
                    zr_scr, zi_scr, d_scr, a_scr):
    R = FFT_RADIX
    W = FOURIER_WIDTH
    u = f_ref[...]
    w = bdw_ref[...]
    pc = jnp.dot(u, bdc_ref[...], preferred_element_type=F32).astype(BF16)
    ps = jnp.dot(u, bds_ref[...], preferred_element_type=F32).astype(BF16)
    zr = jnp.dot(pc, w, preferred_element_type=F32)
    zi = jnp.dot(ps, w, preferred_element_type=F32)
    P = FFT_PITCH
    halves = range(W // 128)
    for hh in halves:
        for n1 in range(R):
            zr_scr[hh, n1 * P:n1 * P + R, :] = zr[n1 * R:(n1 + 1) * R, hh * 128:(hh + 1) * 128]
            zi_scr[hh, n1 * P:n1 * P + R, :] = zi[n1 * R:(n1 + 1) * R, hh * 128:(hh + 1) * 128]

    for n2 in range(R):
        for hh in halves:
            cols = slice(n2 * W + hh * 128, n2 * W + (hh + 1) * 128)
            d_scr[0:R, cols] = zr_scr[hh, pl.ds(n2, R, stride=P), :].astype(BF16)
            d_scr[R:2 * R, cols] = zi_scr[hh, pl.ds(n2, R, stride=P), :].astype(BF16)
    a_scr[...] = jnp.dot(m1_ref[...], d_scr[...], preferred_element_type=F32)

    for n2 in range(R):
        tc = twc_ref[n2]
        ts = tws_ref[n2]
        for hh in halves:
            cols = slice(n2 * W + hh * 128, n2 * W + (hh + 1) * 128)
            ar = a_scr[0:R, cols]
            ai = a_scr[R:2 * R, cols]
            zr_scr[hh, pl.ds(n2, R, stride=P), :] = ar * tc + ai * ts
            zi_scr[hh, pl.ds(n2, R, stride=P), :] = ai * tc - ar * ts
    for k1 in range(R):
        for hh in halves:
            cols = slice(k1 * W + hh * 128, k1 * W + (hh + 1) * 128)
            d_scr[0:R, cols] = zr_scr[hh, k1 * P:k1 * P + R, :].astype(BF16)
            d_scr[R:2 * R, cols] = zi_scr[hh, k1 * P:k1 * P + R, :].astype(BF16)
    a_scr[0:R, :] = jnp.dot(m2_ref[...], d_scr[...], preferred_element_type=F32)
    for k1 in range(R):
        for hh in halves:
            zr_scr[hh, pl.ds(k1, R, stride=P), :] = a_scr[0:R, k1 * W + hh * 128:k1 * W + (hh + 1) * 128]
    for k2 in range(R):
        for hh in halves:
            o_ref[hh, k2 * R:(k2 + 1) * R, :] = zr_scr[hh, k2 * P:k2 * P + R, :]


def _fourier(f_in, bdc, bds, bdw, m1, m2, twc, tws):
    R = FFT_RADIX
    W = FOURIER_WIDTH
    const2 = lambda b: (0, 0)
    const3 = lambda b: (0, 0, 0)
    return pl.pallas_call(
        _fourier_kernel,
        grid=(BATCH,),
        in_specs=[
            pl.BlockSpec((SEQ, W), lambda b: (b, 0)),
            pl.BlockSpec((W, W), const2),
            pl.BlockSpec((W, W), const2),
            pl.BlockSpec((W, W), const2),
            pl.BlockSpec((2 * R, 2 * R), const2),
            pl.BlockSpec((R, 2 * R), const2),
            pl.BlockSpec((R, R, 128), const3),
            pl.BlockSpec((R, R, 128), const3),
        ],
        out_specs=pl.BlockSpec((W // 128, SEQ, 128), lambda b: (0, b, 0)),
        out_shape=jax.ShapeDtypeStruct((W // 128, N_TOK, 128), F32),
        scratch_shapes=[pltpu.VMEM((W // 128, R * FFT_PITCH, 128), F32),
                        pltpu.VMEM((W // 128, R * FFT_PITCH, 128), F32),
                        pltpu.VMEM((2 * R, R * W), BF16), pltpu.VMEM((2 * R, R * W), F32)],
        compiler_params=_vmem(52 << 20),
        name="fourier",
    )(f_in, bdc, bds, bdw, m1, m2, twc, tws)


@functools.lru_cache(maxsize=None)
def _dft_constants():
    R = FFT_RADIX
    a = np.arange(R, dtype=np.int64)
    ang = 2.0 * np.pi * ((a[:, None] * a[None, :]) % R).astype(np.float64) / R
    c64, s64 = np.cos(ang), np.sin(ang)
    scale = 1.0 / math.sqrt(SEQ * FOURIER_GROUP_DIM)
    m1 = np.block([[c64, s64], [-s64, c64]])
    m2 = np.concatenate([c64, s64], axis=1) * scale
    tang = 2.0 * np.pi * (a[:, None] * a[None, :]).astype(np.float64) / SEQ
    twc = np.repeat(np.cos(tang)[:, :, None], 128, axis=2).astype(np.float32)
    tws = np.repeat(np.sin(tang)[:, :, None], 128, axis=2).astype(np.float32)
    assert FOURIER_GROUP_DIM == R
    eye = np.eye(FOURIER_GROUPS)
    bdc = np.kron(eye, c64)
    bds = np.kron(eye, -s64)
    return bdc.astype(BF16), bds.astype(BF16), m1.astype(BF16), m2.astype(BF16), twc, tws


def _mem_kv_kernel(mem_ref, w_ref, kcat_ref, vcat_ref):
    kv = jnp.dot(mem_ref[...].astype(BF16), w_ref[...], preferred_element_type=F32)
    mk = kv[:, :MEM_WIDTH]
    mv = kv[:, MEM_WIDTH:]
    lane = lax.broadcasted_iota(jnp.int32, mk.shape, 1)
    for hh in range(MEM_HEADS):
        sel = (lane >= hh * MEM_HEAD_DIM) & (lane < (hh + 1) * MEM_HEAD_DIM)
        kcat_ref[0, hh * MEM_LEN:(hh + 1) * MEM_LEN, :] = jnp.where(sel, mk, 0.0).astype(BF16)
        vcat_ref[0, hh * MEM_LEN:(hh + 1) * MEM_LEN, :] = jnp.where(sel, mv, 0.0).astype(BF16)


def _mem_kv(mem2, w_kv):
    return pl.pallas_call(
        _mem_kv_kernel,
        grid=(BATCH,),
        in_specs=[
            pl.BlockSpec((MEM_LEN, D_MODEL), lambda b: (b, 0)),
            pl.BlockSpec((D_MODEL, 2 * MEM_WIDTH), lambda b: (0, 0)),
        ],
        out_specs=[
            pl.BlockSpec((1, MEM_HEADS * MEM_LEN, MEM_WIDTH), lambda b: (b, 0, 0)),
            pl.BlockSpec((1, MEM_HEADS * MEM_LEN, MEM_WIDTH), lambda b: (b, 0, 0)),
        ],
        out_shape=[
            jax.ShapeDtypeStruct((BATCH, MEM_HEADS * MEM_LEN, MEM_WIDTH), BF16),
            jax.ShapeDtypeStruct((BATCH, MEM_HEADS * MEM_LEN, MEM_WIDTH), BF16),
        ],
        name="mem_kv",
    )(mem2, w_kv)


def _mem_attn_kernel(mq_ref, kcat_ref, vcat_ref, o_ref):
    s = lax.dot_general(mq_ref[...], kcat_ref[0], _NT, preferred_element_type=F32)
    parts = []
    for hh in range(MEM_HEADS):
        sh = s[:, hh * MEM_LEN:(hh + 1) * MEM_LEN]
        p = jnp.exp(sh - jnp.max(sh, axis=1, keepdims=True))
        parts.append((p * (1.0 / jnp.sum(p, axis=1, keepdims=True))).astype(BF16))
    p_all = jnp.concatenate(parts, axis=1)
    o_ref[...] = jnp.dot(p_all, vcat_ref[0], preferred_element_type=F32).astype(BF16)


def _mem_attention(mq, kcat, vcat):
    rt = SEQ // ROW_TILE
    return pl.pallas_call(
        _mem_attn_kernel,
        grid=(N_TOK // ROW_TILE,),
        in_specs=[
            pl.BlockSpec((ROW_TILE, MEM_WIDTH), lambda r: (r, 0)),
            pl.BlockSpec((1, MEM_HEADS * MEM_LEN, MEM_WIDTH), lambda r: (r // rt, 0, 0)),
            pl.BlockSpec((1, MEM_HEADS * MEM_LEN, MEM_WIDTH), lambda r: (r // rt, 0, 0)),
        ],
        out_specs=pl.BlockSpec((ROW_TILE, MEM_WIDTH), lambda r: (r, 0)),
        out_shape=jax.ShapeDtypeStruct((N_TOK, MEM_WIDTH), BF16),
        name="mem_attn",
    )(mq, kcat, vcat)


def _route(logits):
    lane = lax.broadcasted_iota(jnp.int32, logits.shape, 1)
    big = jnp.int32(ROUTER_LANES)
    is_group = (lane >= GROUP_LANE0) & (lane < GROUP_LANE0 + N_GROUPS)
    gl = jnp.where(is_group, logits, -jnp.inf)
    gmax = jnp.max(gl, axis=1, keepdims=True)
    g_sel = jnp.min(jnp.where(gl == gmax, lane, big), axis=1, keepdims=True) - GROUP_LANE0
    g_gate = 1.0 / jnp.sum(jnp.where(is_group, jnp.exp(gl - gmax), 0.0), axis=1, keepdims=True)
    in_group = (lane >= g_sel * EXPERTS_PER_GROUP) & (lane < (g_sel + 1) * EXPERTS_PER_GROUP)
    el = jnp.where(in_group, logits, -jnp.inf)
    v1 = jnp.max(el, axis=1, keepdims=True)
    i1 = jnp.min(jnp.where(el == v1, lane, big), axis=1, keepdims=True)
    el2 = jnp.where(lane == i1, -jnp.inf, el)
    v2 = jnp.max(el2, axis=1, keepdims=True)
    i2 = jnp.min(jnp.where(el2 == v2, lane, big), axis=1, keepdims=True)
    e = jnp.exp(v2 - v1)
    w_first = g_gate / (1.0 + e)
    w_second = g_gate * e / (1.0 + e)
    return i1, i2, w_first, w_second


def _outproj_router_kernel(od_ref, of_ref, om_ref, h_ref, wo_ref, g_ref, b_ref,
                           wr_ref, br_ref,
                           h1_ref, route_ref, w0_ref, w1_ref, cnt_ref, cnt_scr):
    @pl.when(pl.program_id(0) == 0)
    def _():
        cnt_scr[...] = jnp.zeros_like(cnt_scr)

    o_four = [of_ref[hh].astype(BF16) for hh in range(FOURIER_WIDTH // 128)]
    o = jnp.concatenate([od_ref[...]] + o_four + [om_ref[...]], axis=1)
    a = jnp.dot(o, wo_ref[...], preferred_element_type=F32)
    h1 = _layer_norm(DEEPNORM_ALPHA * h_ref[...] + a, g_ref[...], b_ref[...])
    _store_row_tiles(h1_ref, h1)
    hi = h1.astype(BF16)
    lo = (h1 - hi.astype(F32)).astype(BF16)
    hw = jnp.dot(hi, wr_ref[...], preferred_element_type=F32)
    logits = (hw[:, :ROUTER_LANES] + hw[:, ROUTER_LANES:]
              + jnp.dot(lo, wr_ref[:, :ROUTER_LANES], preferred_element_type=F32) + br_ref[...])
    i1, i2, w_first, w_second = _route(logits)

    lane = lax.broadcasted_iota(jnp.int32, logits.shape, 1)
    onehot = jnp.where(lane == i1, 1.0, jnp.where(lane == i2, 1.0, 0.0))
    r_id = lax.broadcasted_iota(jnp.int32, (ROW_TILE, ROW_TILE), 0)
    c_id = lax.broadcasted_iota(jnp.int32, (ROW_TILE, ROW_TILE), 1)
    tri = jnp.where(r_id > c_id, 1.0, 0.0).astype(BF16)
    before = jnp.dot(tri, onehot.astype(BF16), preferred_element_type=F32) + cnt_scr[...]
    rank1 = jnp.sum(jnp.where(lane == i1, before, 0.0), axis=1, keepdims=True)
    rank2 = jnp.sum(jnp.where(lane == i2, before, 0.0), axis=1, keepdims=True)
    cnt_scr[...] += jnp.sum(onehot, axis=0, keepdims=True)
    cnt_ref[...] = cnt_scr[...]

    packed = jnp.where(lane == 0, i1.astype(F32),
                       jnp.where(lane == 1, i2.astype(F32),
                                 jnp.where(lane == 2, rank1, jnp.where(lane == 3, rank2, 0.0))))
    route_ref[...] = packed.T[:8, :].astype(jnp.int32)
    w0_ref[...] = jnp.broadcast_to(w_first, (ROW_TILE, 128))
    w1_ref[...] = jnp.broadcast_to(w_second, (ROW_TILE, 128))


def _outproj_router(o_diff, o_four, o_mem, h, w_out, ln_g, ln_b, w_r, b_r):
    row = lambda r: (r, 0)
    const = lambda r: (0, 0)
    return pl.pallas_call(
        _outproj_router_kernel,
        grid=(N_TOK // ROW_TILE,),
        in_specs=[
            pl.BlockSpec((ROW_TILE, DIFF_WIDTH), row),
            pl.BlockSpec((FOURIER_WIDTH // 128, ROW_TILE, 128), lambda r: (0, r, 0)),
            pl.BlockSpec((ROW_TILE, MEM_WIDTH), row),
            pl.BlockSpec((ROW_TILE, D_MODEL), row),
            pl.BlockSpec((D_MODEL, D_MODEL), const),
            pl.BlockSpec((1, D_MODEL), const),
            pl.BlockSpec((1, D_MODEL), const),
            pl.BlockSpec((D_MODEL, 2 * ROUTER_LANES), const),
            pl.BlockSpec((1, ROUTER_LANES), const),
        ],
        out_specs=[
            pl.BlockSpec((ROW_TILE * ROW_SLABS, 128), row),
            pl.BlockSpec((8, ROW_TILE), lambda r: (0, r)),
            pl.BlockSpec((ROW_TILE, 128), row),
            pl.BlockSpec((ROW_TILE, 128), row),
            pl.BlockSpec((1, ROUTER_LANES), const),
        ],
        out_shape=[
            jax.ShapeDtypeStruct((N_TOK * ROW_SLABS, 128), F32),
            jax.ShapeDtypeStruct((8, N_TOK), jnp.int32),
            jax.ShapeDtypeStruct((N_TOK, 128), F32),
            jax.ShapeDtypeStruct((N_TOK, 128), F32),
            jax.ShapeDtypeStruct((1, ROUTER_LANES), F32),
        ],
        scratch_shapes=[pltpu.VMEM((1, ROUTER_LANES), F32)],
        compiler_params=_vmem(40 << 20),
        name="outproj_router",
    )(o_diff, o_four, o_mem, h, w_out, ln_g, ln_b, w_r, b_r)


def _dispatch_kernel(pos0_ref, pos1_ref, x_ref, xs_ref, sem):
    base = pl.program_id(0) * DISPATCH_TILE

    def issue(t, carry):
        src = _row_tile(x_ref, t)
        pltpu.make_async_copy(src, _row_tile(xs_ref, pos0_ref[base + t]), sem).start(priority=0)
        pltpu.make_async_copy(src, _row_tile(xs_ref, pos1_ref[base + t]), sem).start(priority=1)
        return carry

    lax.fori_loop(0, DISPATCH_TILE, issue, 0, unroll=8)
    for _ in range(2):
        pltpu.make_async_copy(x_ref, xs_ref.at[pl.ds(0, DISPATCH_TILE * ROW_SLABS), :], sem).wait()


def _dispatch(pos0, pos1, h1_rows):
    return pl.pallas_call(
        _dispatch_kernel,
        grid_spec=pltpu.PrefetchScalarGridSpec(
            num_scalar_prefetch=2,
            grid=(N_TOK // DISPATCH_TILE,),
            in_specs=[pl.BlockSpec((DISPATCH_TILE * ROW_SLABS, 128), lambda i, p0, p1: (i, 0))],
            out_specs=pl.BlockSpec(memory_space=pl.ANY),
            scratch_shapes=[pltpu.SemaphoreType.DMA(())],
        ),
        out_shape=jax.ShapeDtypeStruct((N_SLOTS * ROW_SLABS, 128), F32),
        name="moe_dispatch",
    )(pos0, pos1, h1_rows)


def _expert_kernel(tile_end_ref, xs_ref, w1_ref, w3_ref, w2_ref, ys_ref):
    @pl.when(pl.program_id(0) < tile_end_ref[N_EXPERTS - 1])
    def _():
        x = _load_row_tiles(xs_ref, MOE_TILE).astype(BF16)
        a = jnp.dot(x, w1_ref[0].astype(BF16), preferred_element_type=F32)
        b = jnp.dot(x, w3_ref[0].astype(BF16), preferred_element_type=F32)
        hid = (a * jax.nn.sigmoid(a) * b).astype(BF16)
        y = jnp.dot(hid, w2_ref[0].astype(BF16), preferred_element_type=F32)
        _store_row_tiles(ys_ref, y)


def _experts(tile_end, xs, w1, w3, w2):
    def tile(t, tile_end):
        return jnp.minimum(t, tile_end[N_EXPERTS - 1] - 1), 0

    def wsel(t, tile_end):
        e = jnp.int32(0)
        step = N_EXPERTS // 2
        while step:
            e = jnp.where(t >= tile_end[e + step - 1], e + step, e)
            step //= 2
        return e, 0, 0

    return pl.pallas_call(
        _expert_kernel,
        grid_spec=pltpu.PrefetchScalarGridSpec(
            num_scalar_prefetch=1,
            grid=(N_MOE_TILES,),
            in_specs=[
                pl.BlockSpec((MOE_TILE * ROW_SLABS, 128), tile),
                pl.BlockSpec((1, D_MODEL, D_EXPERT), wsel),
                pl.BlockSpec((1, D_MODEL, D_EXPERT), wsel),
                pl.BlockSpec((1, D_EXPERT, D_MODEL), wsel),
            ],
            out_specs=pl.BlockSpec((MOE_TILE * ROW_SLABS, 128), tile),
        ),
        out_shape=jax.ShapeDtypeStruct((N_SLOTS * ROW_SLABS, 128), F32),
        compiler_params=_vmem(40 << 20),
        name="moe_experts",
    )(tile_end, xs, w1, w3, w2)


def _combine_kernel(pos0_ref, pos1_ref, ys_ref, h1_ref, w0_ref, w1_ref, g_ref, b_ref, o_ref,
                    y0_buf, y1_buf, sems):
    i = pl.program_id(0)
    n = pl.num_programs(0)

    def gather(tile, slot):
        base = tile * COMBINE_TILE

        def issue(t, carry):
            pltpu.make_async_copy(_row_tile(ys_ref, pos0_ref[base + t]), _row_tile(y0_buf.at[slot], t),
                                  sems.at[slot]).start(priority=0)
            pltpu.make_async_copy(_row_tile(ys_ref, pos1_ref[base + t]), _row_tile(y1_buf.at[slot], t),
                                  sems.at[slot]).start(priority=1)
            return carry

        lax.fori_loop(0, COMBINE_TILE, issue, 0, unroll=8)

    @pl.when(i == 0)
    def _():
        gather(0, 0)

    @pl.when(i + 1 < n)
    def _():
        gather(i + 1, (i + 1) % 2)

    slot = i % 2
    whole = ys_ref.at[pl.ds(0, COMBINE_TILE * ROW_SLABS), :]
    pltpu.make_async_copy(whole, y0_buf.at[slot], sems.at[slot]).wait()
    pltpu.make_async_copy(whole, y1_buf.at[slot], sems.at[slot]).wait()
    w0 = jnp.concatenate([w0_ref[...]] * ROW_SLABS, axis=1)
    w1 = jnp.concatenate([w1_ref[...]] * ROW_SLABS, axis=1)
    f = (w0 * _load_row_tiles(y0_buf.at[slot], COMBINE_TILE)
         + w1 * _load_row_tiles(y1_buf.at[slot], COMBINE_TILE))
    h1 = _load_row_tiles(h1_ref, COMBINE_TILE)
    o_ref[...] = _layer_norm(DEEPNORM_ALPHA * h1 + f, g_ref[...], b_ref[...])


def _combine(pos0, pos1, ys, h1_rows, w0, w1, ln_g, ln_b):
    row = lambda i, p0, p1: (i, 0)
    const = lambda i, p0, p1: (0, 0)
    return pl.pallas_call(
        _combine_kernel,
        grid_spec=pltpu.PrefetchScalarGridSpec(
            num_scalar_prefetch=2,
            grid=(N_TOK // COMBINE_TILE,),
            in_specs=[
                pl.BlockSpec(memory_space=pl.ANY),
                pl.BlockSpec((COMBINE_TILE * ROW_SLABS, 128), row),
                pl.BlockSpec((COMBINE_TILE, 128), row),
                pl.BlockSpec((COMBINE_TILE, 128), row),
                pl.BlockSpec((1, D_MODEL), const),
                pl.BlockSpec((1, D_MODEL), const),
            ],
            out_specs=pl.BlockSpec((COMBINE_TILE, D_MODEL), row),
            scratch_shapes=[pltpu.VMEM((2, COMBINE_TILE * ROW_SLABS, 128), F32),
                            pltpu.VMEM((2, COMBINE_TILE * ROW_SLABS, 128), F32),
                            pltpu.SemaphoreType.DMA((2,))],
        ),
        out_shape=jax.ShapeDtypeStruct((N_TOK, D_MODEL), F32),
        compiler_params=pltpu.CompilerParams(dimension_semantics=("arbitrary",), vmem_limit_bytes=40 << 20),
        name="moe_combine",
    )(pos0, pos1, ys, h1_rows, w0, w1, ln_g, ln_b)


def kernel(x, mem, ln0_g, ln0_b, rel_bias, w_in, w_mem_kv, w_fourier, lambda_q1, lambda_k1, lambda_q2,
           lambda_k2, subln_g, w_out, ln1_g, ln1_b, w_group, b_group, w_router, b_router, w1, w3, w2,
           ln2_g, ln2_b):
    l = 0
    x2 = x.reshape(N_TOK, D_MODEL)
    wi = w_in[l]
    w_qk = wi[:, :2 * DIFF_WIDTH].astype(BF16)
    w_vt = wi[:, 2 * DIFF_WIDTH:3 * DIFF_WIDTH].T.astype(BF16)
    w_fm = wi[:, 3 * DIFF_WIDTH:].astype(BF16)
    h, q, k, vt, f_in, mq = _ln0_inproj(x2, ln0_g.reshape(1, -1), ln0_b.reshape(1, -1), w_qk, w_vt, w_fm)

    T = ATT_TILE
    kk = jnp.arange(T, dtype=jnp.int32)[:, None]
    qq = jnp.arange(ATT_Q, dtype=jnp.int32)[None, :]
    buckets = jnp.stack([_t5_bucket(d * T + kk - qq) for d in range(-2, ATT_Q // T + 2)])
    lam_vecs = jnp.stack([lambda_q1[l], lambda_k1[l], lambda_q2[l], lambda_k2[l]]).astype(F32)
    far_buckets = _t5_bucket(jnp.array([-(T + 1), T + 1], jnp.int32))
    o_diff = _diff_attention(rel_bias.astype(F32), far_buckets, q, k, vt, buckets, lam_vecs,
                             subln_g[l].astype(F32).reshape(DIFF_VDIM, 1))

    bdc, bds, m1, m2, twc, tws = (jnp.asarray(c) for c in _dft_constants())
    wf = w_fourier[l]
    bdw = jnp.zeros((FOURIER_WIDTH, FOURIER_WIDTH), F32)
    for g in range(FOURIER_GROUPS):
        sl = slice(g * FOURIER_GROUP_DIM, (g + 1) * FOURIER_GROUP_DIM)
        bdw = bdw.at[sl, sl].set(wf[g])
    o_four = _fourier(f_in, bdc, bds, bdw.astype(BF16), m1, m2, twc, tws)

    kcat, vcat = _mem_kv(mem.reshape(BATCH * MEM_LEN, D_MODEL), w_mem_kv[l].astype(BF16))
    o_mem = _mem_attention(mq, kcat, vcat)

    w_r = jnp.zeros((D_MODEL, ROUTER_LANES), F32)
    w_r = w_r.at[:, :N_EXPERTS].set(w_router[l].astype(F32))
    w_r = w_r.at[:, GROUP_LANE0:GROUP_LANE0 + N_GROUPS].set(w_group[l].astype(F32))
    wr_hi = w_r.astype(BF16)
    wr_lo = (w_r - wr_hi.astype(F32)).astype(BF16)
    b_r = jnp.zeros((1, ROUTER_LANES), F32)
    b_r = b_r.at[0, :N_EXPERTS].set(b_router[l].astype(F32))
    b_r = b_r.at[0, GROUP_LANE0:GROUP_LANE0 + N_GROUPS].set(b_group[l].astype(F32))
    h1_rows, route, gate0, gate1, counts = _outproj_router(
        o_diff, o_four, o_mem, h, w_out[l].astype(BF16),
        ln1_g[l].reshape(1, -1), ln1_b[l].reshape(1, -1), jnp.concatenate([wr_hi, wr_lo], axis=1), b_r)

    cnt = counts[0, :N_EXPERTS].astype(jnp.int32)
    tiles_per_expert = (cnt + MOE_TILE - 1) // MOE_TILE
    tile_end = jnp.cumsum(tiles_per_expert).astype(jnp.int32)
    row_start = (tile_end - tiles_per_expert) * MOE_TILE
    expert_ids = jnp.arange(N_EXPERTS, dtype=jnp.int32)[None, :]

    def slot_of(expert, rank):
        return jnp.sum(jnp.where(expert[:, None] == expert_ids, row_start[None, :], 0), axis=1) + rank

    pos0 = slot_of(route[0], route[2])
    pos1 = slot_of(route[1], route[3])

    xs = _dispatch(pos0, pos1, h1_rows)
    ys = _experts(tile_end, xs, w1[l], w3[l], w2[l])
    out = _combine(pos0, pos1, ys, h1_rows, gate0, gate1, ln2_g[l].reshape(1, -1), ln2_b[l].reshape(1, -1))
    return out.reshape(BATCH, SEQ, D_MODEL)
```

```python
import functools
import math

import numpy as np
import jax
import jax.numpy as jnp
from jax import lax
from jax.experimental import pallas as pl
from jax.experimental.pallas import tpu as pltpu

F32 = jnp.float32
BF16 = jnp.bfloat16

D_MODEL = 1024
BATCH = 4
SEQ = 4096
N_TOK = BATCH * SEQ
MEM_LEN = 256
MEM_HEADS = 4
MEM_HEAD_DIM = 64
MEM_WIDTH = 256
DIFF_HEADS = 4
DIFF_HEAD_DIM = 64
DIFF_VDIM = 2 * DIFF_HEAD_DIM
VT_ROWS = DIFF_VDIM + 16
DIFF_WIDTH = 512
FOURIER_GROUPS = 4
FOURIER_GROUP_DIM = 64
FOURIER_WIDTH = 256
N_BUCKETS = 32
MAX_DISTANCE = 128
N_GROUPS = 4
EXPERTS_PER_GROUP = 8
N_EXPERTS = 32
D_EXPERT = 256
LN_EPS = 1e-5
DEEPNORM_ALPHA = 2.0 ** 0.25
LAM_INIT = 0.8 - 0.6 * math.exp(-0.3 * 0)
LOG2E = 1.4426950408889634

ROW_TILE = 512
ATT_TILE = 256
ATT_Q = 256
MOE_TILE = 512
N_MOE_TILES = 2 * N_TOK // MOE_TILE + N_EXPERTS
N_SLOTS = N_MOE_TILES * MOE_TILE
ROW_SLABS = D_MODEL // 128
DISPATCH_TILE = 512
COMBINE_TILE = 512
FFT_RADIX = 64
FFT_PITCH = 72
ROUTER_LANES = 128
GROUP_LANE0 = 32
NEG_BIG = -1e30

_NT = (((1,), (1,)), ((), ()))


def _vmem(nbytes):
    return pltpu.CompilerParams(vmem_limit_bytes=int(nbytes))


def _layer_norm(x, g, b):
    mu = jnp.mean(x, axis=-1, keepdims=True)
    xc = x - mu
    var = jnp.mean(xc * xc, axis=-1, keepdims=True)
    return xc * lax.rsqrt(var + LN_EPS) * g + b


def _load_row_tiles(ref, rows):
    return jnp.concatenate([ref[pl.ds(s, rows, stride=ROW_SLABS), :] for s in range(ROW_SLABS)], axis=1)


def _store_row_tiles(ref, x):
    for s in range(ROW_SLABS):
        ref[pl.ds(s, x.shape[0], stride=ROW_SLABS), :] = x[:, s * 128:(s + 1) * 128]


def _row_tile(ref, t):
    return ref.at[pl.ds(pl.multiple_of(t * ROW_SLABS, ROW_SLABS), ROW_SLABS), :]


def _t5_bucket(rel):
    nb = N_BUCKETS // 2
    max_exact = nb // 2
    ret = (rel > 0).astype(jnp.int32) * nb
    n = jnp.abs(rel)
    nf = jnp.maximum(n, 1).astype(F32)
    large = max_exact + (jnp.log(nf / max_exact) / math.log(MAX_DISTANCE / max_exact)
                         * (nb - max_exact)).astype(jnp.int32)
    large = jnp.minimum(large, nb - 1)
    return ret + jnp.where(n < max_exact, n, large)


def _ln0_inproj_kernel(x_ref, g_ref, b_ref, wqk_ref, wvt_ref, wfm_ref,
                       h_ref, q_ref, k_ref, vt_ref, f_ref, mq_ref):
    h = _layer_norm(x_ref[...], g_ref[...], b_ref[...])
    h_ref[...] = h
    hb = h.astype(BF16)
    qk = jnp.dot(hb, wqk_ref[...], preferred_element_type=F32)
    q_ref[...] = (qk[:, :DIFF_WIDTH] * (DIFF_HEAD_DIM ** -0.5 * LOG2E)).astype(BF16)
    k_ref[...] = qk[:, DIFF_WIDTH:].astype(BF16)
    vt = lax.dot_general(wvt_ref[...], hb, _NT, preferred_element_type=F32)
    ones = jnp.ones((VT_ROWS - DIFF_VDIM, ROW_TILE), BF16)
    for hh in range(DIFF_HEADS):
        vt_ref[0, hh, :DIFF_VDIM, :] = vt[hh * DIFF_VDIM:(hh + 1) * DIFF_VDIM, :].astype(BF16)
        vt_ref[0, hh, DIFF_VDIM:, :] = ones
    fm = jnp.dot(hb, wfm_ref[...], preferred_element_type=F32)
    f_ref[...] = fm[:, :FOURIER_WIDTH].astype(BF16)
    mq_ref[...] = (fm[:, FOURIER_WIDTH:] * MEM_HEAD_DIM ** -0.5).astype(BF16)


def _ln0_inproj(x2, ln_g, ln_b, w_qk, w_vt, w_fm):
    rt = SEQ // ROW_TILE
    row = lambda r: (r, 0)
    const = lambda r: (0, 0)
    return pl.pallas_call(
        _ln0_inproj_kernel,
        grid=(N_TOK // ROW_TILE,),
        in_specs=[
            pl.BlockSpec((ROW_TILE, D_MODEL), row),
            pl.BlockSpec((1, D_MODEL), const),
            pl.BlockSpec((1, D_MODEL), const),
            pl.BlockSpec((D_MODEL, 2 * DIFF_WIDTH), const),
            pl.BlockSpec((DIFF_WIDTH, D_MODEL), const),
            pl.BlockSpec((D_MODEL, FOURIER_WIDTH + MEM_WIDTH), const),
        ],
        out_specs=[
            pl.BlockSpec((ROW_TILE, D_MODEL), row),
            pl.BlockSpec((ROW_TILE, DIFF_WIDTH), row),
            pl.BlockSpec((ROW_TILE, DIFF_WIDTH), row),
            pl.BlockSpec((1, DIFF_HEADS, VT_ROWS, ROW_TILE), lambda r: (r // rt, 0, 0, r % rt)),
            pl.BlockSpec((ROW_TILE, FOURIER_WIDTH), row),
            pl.BlockSpec((ROW_TILE, MEM_WIDTH), row),
        ],
        out_shape=[
            jax.ShapeDtypeStruct((N_TOK, D_MODEL), F32),
            jax.ShapeDtypeStruct((N_TOK, DIFF_WIDTH), BF16),
            jax.ShapeDtypeStruct((N_TOK, DIFF_WIDTH), BF16),
            jax.ShapeDtypeStruct((BATCH, DIFF_HEADS, VT_ROWS, SEQ), BF16),
            jax.ShapeDtypeStruct((N_TOK, FOURIER_WIDTH), BF16),
            jax.ShapeDtypeStruct((N_TOK, MEM_WIDTH), BF16),
        ],
        compiler_params=_vmem(48 << 20),
        name="ln0_inproj",
    )(x2, ln_g, ln_b, w_qk, w_vt, w_fm)


def _diff_attn_kernel(tbl_ref, far_ref, q_ref, k_ref, vt_ref, bkt_ref, lam_ref, g_ref, o_ref,
                      bias_scr, s0_scr, s1_scr, p0_scr, p1_scr):
    T = ATT_TILE
    TQ = ATT_Q
    r = TQ // T
    nk = SEQ // T
    h = pl.program_id(0)
    i = pl.program_id(2)
    s_scr = (s0_scr, s1_scr)
    p_scr = (p0_scr, p1_scr)

    @pl.when((pl.program_id(1) == 0) & (i == 0))
    def _build_bias():
        for d in range(r + 4):
            bk = bkt_ref[d]
            bias = jnp.zeros((T, TQ), F32)
            for n in range(N_BUCKETS):
                bias = jnp.where(bk == n, tbl_ref[n, h], bias)
            bias_scr[d] = bias * LOG2E

    q = q_ref[...]
    lane = lax.broadcasted_iota(jnp.int32, q.shape, 1)
    zero = jnp.zeros_like(q)
    q_comp = (jnp.where(lane < DIFF_HEAD_DIM, q, zero), jnp.where(lane >= DIFF_HEAD_DIM, q, zero))

    win = r + 2
    js = jnp.clip(r * i - 1, 0, nk - win)
    c_left = tbl_ref[far_ref[0], h] * LOG2E
    c_right = tbl_ref[far_ref[1], h] * LOG2E
    side = [jnp.where(j < js, c_left, jnp.where(j > js + win - 1, c_right, 0.0)) for j in range(nk)]
    side_or_out = [jnp.where((j >= js) & (j <= js + win - 1), NEG_BIG, side[j]) for j in range(nk)]

    m8 = [None, None]
    for c in range(2):
        s = lax.dot_general(k_ref[...], q_comp[c], _NT, preferred_element_type=F32)
        s_scr[c][...] = s
        for j in range(nk):
            t = jnp.max(s[j * T:(j + 1) * T, :].reshape(T // 8, 8, TQ), axis=0) + side_or_out[j]
            m8[c] = t if m8[c] is None else jnp.maximum(m8[c], t)

    for w in range(win):
        rows = pl.ds(pl.multiple_of((js + w) * T, T), T)
        bias = bias_scr[js + w - r * i + 2]
        for c in range(2):
            sb = s_scr[c][rows, :] + bias
            s_scr[c][rows, :] = sb
            m8[c] = jnp.maximum(m8[c], jnp.max(sb.reshape(T // 8, 8, TQ), axis=0))
    m = [jnp.max(m8[c], axis=0, keepdims=True) for c in range(2)]

    for j in range(nk):
        for c in range(2):
            x = s_scr[c][j * T:(j + 1) * T, :] + (side[j] - m[c])
            p_scr[c][j * T:(j + 1) * T, :] = jnp.exp2(x.astype(BF16))

    acc = [jnp.dot(vt_ref[0, 0], p_scr[c][...], preferred_element_type=F32) for c in range(2)]
    num = [a[:DIFF_VDIM] for a in acc]
    den = [a[DIFF_VDIM:DIFF_VDIM + 1] for a in acc]

    lam = (jnp.exp(jnp.sum(lam_ref[0:1, :] * lam_ref[1:2, :], axis=1, keepdims=True))
           - jnp.exp(jnp.sum(lam_ref[2:3, :] * lam_ref[3:4, :], axis=1, keepdims=True)) + LAM_INIT)
    o = num[0] * (1.0 / den[0]) - lam * (num[1] * (1.0 / den[1]))
    ms = jnp.mean(o * o, axis=0, keepdims=True)
    o = o * lax.rsqrt(ms + LN_EPS) * g_ref[...] * (1.0 - LAM_INIT)
    o_ref[...] = o.T.astype(BF16)


def _diff_attention(rel_bias, far_buckets, q, k, vt, buckets, lam_vecs, subln_col):
    T = ATT_TILE
    TQ = ATT_Q
    nq = SEQ // TQ
    n_bias = TQ // T + 4
    return pl.pallas_call(
        _diff_attn_kernel,
        grid=(DIFF_HEADS, BATCH, nq),
        in_specs=[
            pl.BlockSpec(memory_space=pltpu.SMEM),
            pl.BlockSpec(memory_space=pltpu.SMEM),
            pl.BlockSpec((TQ, DIFF_VDIM), lambda h, b, i: (b * nq + i, h)),
            pl.BlockSpec((SEQ, DIFF_VDIM), lambda h, b, i: (b, h)),
            pl.BlockSpec((1, 1, VT_ROWS, SEQ), lambda h, b, i: (b, h, 0, 0)),
            pl.BlockSpec((n_bias, T, TQ), lambda h, b, i: (0, 0, 0)),
            pl.BlockSpec((4, DIFF_HEAD_DIM), lambda h, b, i: (0, 0)),
            pl.BlockSpec((DIFF_VDIM, 1), lambda h, b, i: (0, 0)),
        ],
        out_specs=pl.BlockSpec((TQ, DIFF_VDIM), lambda h, b, i: (b * nq + i, h)),
        out_shape=jax.ShapeDtypeStruct((N_TOK, DIFF_WIDTH), BF16),
        scratch_shapes=[pltpu.VMEM((n_bias, T, TQ), F32),
                        pltpu.VMEM((SEQ, TQ), F32), pltpu.VMEM((SEQ, TQ), F32),
                        pltpu.VMEM((SEQ, TQ), BF16), pltpu.VMEM((SEQ, TQ), BF16)],
        compiler_params=_vmem(40 << 20),
        name="diff_attn",
    )(rel_bias, far_buckets, q, k, vt, buckets, lam_vecs, subln_col)


def _fourier_kernel(f_ref, bdc_ref, bds_ref, bdw_ref, m1_ref, m2_ref, twc_ref, tws_ref, o_ref,
                    zr_scr, zi_scr, d_scr, a_scr):
    R = FFT_RADIX
    W = FOURIER_WIDTH
    u = f_ref[...]
    w = bdw_ref[...]
    pc = jnp.dot(u, bdc_ref[...], preferred_element_type=F32).astype(BF16)
    ps = jnp.dot(u, bds_ref[...], preferred_element_type=F32).astype(BF16)
    zr = jnp.dot(pc, w, preferred_element_type=F32)
    zi = jnp.dot(ps, w, preferred_element_type=F32)
    P = FFT_PITCH
    halves = range(W // 128)
    for hh in halves:
        for n1 in range(R):
            zr_scr[hh, n1 * P:n1 * P + R, :] = zr[n1 * R:(n1 + 1) * R, hh * 128:(hh + 1) * 128]
            zi_scr[hh, n1 * P:n1 * P + R, :] = zi[n1 * R:(n1 + 1) * R, hh * 128:(hh + 1) * 128]

    for n2 in range(R):
        for hh in halves:
            cols = slice(n2 * W + hh * 128, n2 * W + (hh + 1) * 128)
            d_scr[0:R, cols] = zr_scr[hh, pl.ds(n2, R, stride=P), :].astype(BF16)
            d_scr[R:2 * R, cols] = zi_scr[hh, pl.ds(n2, R, stride=P), :].astype(BF16)
    a_scr[...] = jnp.dot(m1_ref[...], d_scr[...], preferred_element_type=F32)

    for n2 in range(R):
        tc = twc_ref[n2]
        ts = tws_ref[n2]
        for hh in halves:
            cols = slice(n2 * W + hh * 128, n2 * W + (hh + 1) * 128)
            ar = a_scr[0:R, cols]
            ai = a_scr[R:2 * R, cols]
            zr_scr[hh, pl.ds(n2, R, stride=P), :] = ar * tc + ai * ts
            zi_scr[hh, pl.ds(n2, R, stride=P), :] = ai * tc - ar * ts
    for k1 in range(R):
        for hh in halves:
            cols = slice(k1 * W + hh * 128, k1 * W + (hh + 1) * 128)
            d_scr[0:R, cols] = zr_scr[hh, k1 * P:k1 * P + R, :].astype(BF16)
            d_scr[R:2 * R, cols] = zi_scr[hh, k1 * P:k1 * P + R, :].astype(BF16)
    a_scr[0:R, :] = jnp.dot(m2_ref[...], d_scr[...], preferred_element_type=F32)
    for k1 in range(R):
        for hh in halves:
            zr_scr[hh, pl.ds(k1, R, stride=P), :] = a_scr[0:R, k1 * W + hh * 128:k1 * W + (hh + 1) * 128]
    for k2 in range(R):
        for hh in halves:
            o_ref[hh, k2 * R:(k2 + 1) * R, :] = zr_scr[hh, k2 * P:k2 * P + R, :]


def _fourier(f_in, bdc, bds, bdw, m1, m2, twc, tws):
    R = FFT_RADIX
    W = FOURIER_WIDTH
    const2 = lambda b: (0, 0)
    const3 = lambda b: (0, 0, 0)
    return pl.pallas_call(
        _fourier_kernel,
        grid=(BATCH,),
        in_specs=[
            pl.BlockSpec((SEQ, W), lambda b: (b, 0)),
            pl.BlockSpec((W, W), const2),
            pl.BlockSpec((W, W), const2),
            pl.BlockSpec((W, W), const2),
            pl.BlockSpec((2 * R, 2 * R), const2),
            pl.BlockSpec((R, 2 * R), const2),
            pl.BlockSpec((R, R, 128), const3),
            pl.BlockSpec((R, R, 128), const3),
        ],
        out_specs=pl.BlockSpec((W // 128, SEQ, 128), lambda b: (0, b, 0)),
        out_shape=jax.ShapeDtypeStruct((W // 128, N_TOK, 128), F32),
        scratch_shapes=[pltpu.VMEM((W // 128, R * FFT_PITCH, 128), F32),
                        pltpu.VMEM((W // 128, R * FFT_PITCH, 128), F32),
                        pltpu.VMEM((2 * R, R * W), BF16), pltpu.VMEM((2 * R, R * W), F32)],
        compiler_params=_vmem(52 << 20),
        name="fourier",
    )(f_in, bdc, bds, bdw, m1, m2, twc, tws)


@functools.lru_cache(maxsize=None)
def _dft_constants():
    R = FFT_RADIX
    a = np.arange(R, dtype=np.int64)
    ang = 2.0 * np.pi * ((a[:, None] * a[None, :]) % R).astype(np.float64) / R
    c64, s64 = np.cos(ang), np.sin(ang)
    scale = 1.0 / math.sqrt(SEQ * FOURIER_GROUP_DIM)
    m1 = np.block([[c64, s64], [-s64, c64]])
    m2 = np.concatenate([c64, s64], axis=1) * scale
    tang = 2.0 * np.pi * (a[:, None] * a[None, :]).astype(np.float64) / SEQ
    twc = np.repeat(np.cos(tang)[:, :, None], 128, axis=2).astype(np.float32)
    tws = np.repeat(np.sin(tang)[:, :, None], 128, axis=2).astype(np.float32)
    assert FOURIER_GROUP_DIM == R
    eye = np.eye(FOURIER_GROUPS)
    bdc = np.kron(eye, c64)
    bds = np.kron(eye, -s64)
    return bdc.astype(BF16), bds.astype(BF16), m1.astype(BF16), m2.astype(BF16), twc, tws


def _mem_kv_kernel(mem_ref, w_ref, kcat_ref, vcat_ref):
    kv = jnp.dot(mem_ref[...].astype(BF16), w_ref[...], preferred_element_type=F32)
    mk = kv[:, :MEM_WIDTH]
    mv = kv[:, MEM_WIDTH:]
    lane = lax.broadcasted_iota(jnp.int32, mk.shape, 1)
    for hh in range(MEM_HEADS):
        sel = (lane >= hh * MEM_HEAD_DIM) & (lane < (hh + 1) * MEM_HEAD_DIM)
        kcat_ref[0, hh * MEM_LEN:(hh + 1) * MEM_LEN, :] = jnp.where(sel, mk, 0.0).astype(BF16)
        vcat_ref[0, hh * MEM_LEN:(hh + 1) * MEM_LEN, :] = jnp.where(sel, mv, 0.0).astype(BF16)


def _mem_kv(mem2, w_kv):
    return pl.pallas_call(
        _mem_kv_kernel,
        grid=(BATCH,),
        in_specs=[
            pl.BlockSpec((MEM_LEN, D_MODEL), lambda b: (b, 0)),
            pl.BlockSpec((D_MODEL, 2 * MEM_WIDTH), lambda b: (0, 0)),
        ],
        out_specs=[
            pl.BlockSpec((1, MEM_HEADS * MEM_LEN, MEM_WIDTH), lambda b: (b, 0, 0)),
            pl.BlockSpec((1, MEM_HEADS * MEM_LEN, MEM_WIDTH), lambda b: (b, 0, 0)),
        ],
        out_shape=[
            jax.ShapeDtypeStruct((BATCH, MEM_HEADS * MEM_LEN, MEM_WIDTH), BF16),
            jax.ShapeDtypeStruct((BATCH, MEM_HEADS * MEM_LEN, MEM_WIDTH), BF16),
        ],
        name="mem_kv",
    )(mem2, w_kv)


def _mem_attn_kernel(mq_ref, kcat_ref, vcat_ref, o_ref):
    s = lax.dot_general(mq_ref[...], kcat_ref[0], _NT, preferred_element_type=F32)
    parts = []
    for hh in range(MEM_HEADS):
        sh = s[:, hh * MEM_LEN:(hh + 1) * MEM_LEN]
        p = jnp.exp(sh - jnp.max(sh, axis=1, keepdims=True))
        parts.append((p * (1.0 / jnp.sum(p, axis=1, keepdims=True))).astype(BF16))
    p_all = jnp.concatenate(parts, axis=1)
    o_ref[...] = jnp.dot(p_all, vcat_ref[0], preferred_element_type=F32).astype(BF16)


def _mem_attention(mq, kcat, vcat):
    rt = SEQ // ROW_TILE
    return pl.pallas_call(
        _mem_attn_kernel,
        grid=(N_TOK // ROW_TILE,),
        in_specs=[
            pl.BlockSpec((ROW_TILE, MEM_WIDTH), lambda r: (r, 0)),
            pl.BlockSpec((1, MEM_HEADS * MEM_LEN, MEM_WIDTH), lambda r: (r // rt, 0, 0)),
            pl.BlockSpec((1, MEM_HEADS * MEM_LEN, MEM_WIDTH), lambda r: (r // rt, 0, 0)),
        ],
        out_specs=pl.BlockSpec((ROW_TILE, MEM_WIDTH), lambda r: (r, 0)),
        out_shape=jax.ShapeDtypeStruct((N_TOK, MEM_WIDTH), BF16),
        name="mem_attn",
    )(mq, kcat, vcat)


def _route(logits):
    lane = lax.broadcasted_iota(jnp.int32, logits.shape, 1)
    big = jnp.int32(ROUTER_LANES)
    is_group = (lane >= GROUP_LANE0) & (lane < GROUP_LANE0 + N_GROUPS)
    gl = jnp.where(is_group, logits, -jnp.inf)
    gmax = jnp.max(gl, axis=1, keepdims=True)
    g_sel = jnp.min(jnp.where(gl == gmax, lane, big), axis=1, keepdims=True) - GROUP_LANE0
    g_gate = 1.0 / jnp.sum(jnp.where(is_group, jnp.exp(gl - gmax), 0.0), axis=1, keepdims=True)
    in_group = (lane >= g_sel * EXPERTS_PER_GROUP) & (lane < (g_sel + 1) * EXPERTS_PER_GROUP)
    el = jnp.where(in_group, logits, -jnp.inf)
    v1 = jnp.max(el, axis=1, keepdims=True)
    i1 = jnp.min(jnp.where(el == v1, lane, big), axis=1, keepdims=True)
    el2 = jnp.where(lane == i1, -jnp.inf, el)
    v2 = jnp.max(el2, axis=1, keepdims=True)
    i2 = jnp.min(jnp.where(el2 == v2, lane, big), axis=1, keepdims=True)
    e = jnp.exp(v2 - v1)
    w_first = g_gate / (1.0 + e)
    w_second = g_gate * e / (1.0 + e)
    return i1, i2, w_first, w_second


def _outproj_router_kernel(od_ref, of_ref, om_ref, h_ref, wo_ref, g_ref, b_ref,
                           wr_ref, br_ref,
                           h1_ref, route_ref, w0_ref, w1_ref, cnt_ref, cnt_scr):
    @pl.when(pl.program_id(0) == 0)
    def _():
        cnt_scr[...] = jnp.zeros_like(cnt_scr)

    o_four = [of_ref[hh].astype(BF16) for hh in range(FOURIER_WIDTH // 128)]
    o = jnp.concatenate([od_ref[...]] + o_four + [om_ref[...]], axis=1)
    a = jnp.dot(o, wo_ref[...], preferred_element_type=F32)
    h1 = _layer_norm(DEEPNORM_ALPHA * h_ref[...] + a, g_ref[...], b_ref[...])
    _store_row_tiles(h1_ref, h1)
    hi = h1.astype(BF16)
    lo = (h1 - hi.astype(F32)).astype(BF16)
    hw = jnp.dot(hi, wr_ref[...], preferred_element_type=F32)
    logits = (hw[:, :ROUTER_LANES] + hw[:, ROUTER_LANES:]
              + jnp.dot(lo, wr_ref[:, :ROUTER_LANES], preferred_element_type=F32) + br_ref[...])
    i1, i2, w_first, w_second = _route(logits)

    lane = lax.broadcasted_iota(jnp.int32, logits.shape, 1)
    onehot = jnp.where(lane == i1, 1.0, jnp.where(lane == i2, 1.0, 0.0))
    r_id = lax.broadcasted_iota(jnp.int32, (ROW_TILE, ROW_TILE), 0)
    c_id = lax.broadcasted_iota(jnp.int32, (ROW_TILE, ROW_TILE), 1)
    tri = jnp.where(r_id > c_id, 1.0, 0.0).astype(BF16)
    before = jnp.dot(tri, onehot.astype(BF16), preferred_element_type=F32) + cnt_scr[...]
    rank1 = jnp.sum(jnp.where(lane == i1, before, 0.0), axis=1, keepdims=True)
    rank2 = jnp.sum(jnp.where(lane == i2, before, 0.0), axis=1, keepdims=True)
    cnt_scr[...] += jnp.sum(onehot, axis=0, keepdims=True)
    cnt_ref[...] = cnt_scr[...]

    packed = jnp.where(lane == 0, i1.astype(F32),
                       jnp.where(lane == 1, i2.astype(F32),
                                 jnp.where(lane == 2, rank1, jnp.where(lane == 3, rank2, 0.0))))
    route_ref[...] = packed.T[:8, :].astype(jnp.int32)
    w0_ref[...] = jnp.broadcast_to(w_first, (ROW_TILE, 128))
    w1_ref[...] = jnp.broadcast_to(w_second, (ROW_TILE, 128))


def _outproj_router(o_diff, o_four, o_mem, h, w_out, ln_g, ln_b, w_r, b_r):
    row = lambda r: (r, 0)
    const = lambda r: (0, 0)
    return pl.pallas_call(
        _outproj_router_kernel,
        grid=(N_TOK // ROW_TILE,),
        in_specs=[
            pl.BlockSpec((ROW_TILE, DIFF_WIDTH), row),
            pl.BlockSpec((FOURIER_WIDTH // 128, ROW_TILE, 128), lambda r: (0, r, 0)),
            pl.BlockSpec((ROW_TILE, MEM_WIDTH), row),
            pl.BlockSpec((ROW_TILE, D_MODEL), row),
            pl.BlockSpec((D_MODEL, D_MODEL), const),
            pl.BlockSpec((1, D_MODEL), const),
            pl.BlockSpec((1, D_MODEL), const),
            pl.BlockSpec((D_MODEL, 2 * ROUTER_LANES), const),
            pl.BlockSpec((1, ROUTER_LANES), const),
        ],
        out_specs=[
            pl.BlockSpec((ROW_TILE * ROW_SLABS, 128), row),
            pl.BlockSpec((8, ROW_TILE), lambda r: (0, r)),
            pl.BlockSpec((ROW_TILE, 128), row),
            pl.BlockSpec((ROW_TILE, 128), row),
            pl.BlockSpec((1, ROUTER_LANES), const),
        ],
        out_shape=[
            jax.ShapeDtypeStruct((N_TOK * ROW_SLABS, 128), F32),
            jax.ShapeDtypeStruct((8, N_TOK), jnp.int32),
            jax.ShapeDtypeStruct((N_TOK, 128), F32),
            jax.ShapeDtypeStruct((N_TOK, 128), F32),
            jax.ShapeDtypeStruct((1, ROUTER_LANES), F32),
        ],
        scratch_shapes=[pltpu.VMEM((1, ROUTER_LANES), F32)],
        compiler_params=_vmem(40 << 20),
        name="outproj_router",
    )(o_diff, o_four, o_mem, h, w_out, ln_g, ln_b, w_r, b_r)


def _dispatch_kernel(pos0_ref, pos1_ref, x_ref, xs_ref, sem):
    base = pl.program_id(0) * DISPATCH_TILE

    def issue(t, carry):
        src = _row_tile(x_ref, t)
        pltpu.make_async_copy(src, _row_tile(xs_ref, pos0_ref[base + t]), sem).start(priority=0)
        pltpu.make_async_copy(src, _row_tile(xs_ref, pos1_ref[base + t]), sem).start(priority=1)
        return carry

    lax.fori_loop(0, DISPATCH_TILE, issue, 0, unroll=8)
    for _ in range(2):
        pltpu.make_async_copy(x_ref, xs_ref.at[pl.ds(0, DISPATCH_TILE * ROW_SLABS), :], sem).wait()


def _dispatch(pos0, pos1, h1_rows):
    return pl.pallas_call(
        _dispatch_kernel,
        grid_spec=pltpu.PrefetchScalarGridSpec(
            num_scalar_prefetch=2,
            grid=(N_TOK // DISPATCH_TILE,),
            in_specs=[pl.BlockSpec((DISPATCH_TILE * ROW_SLABS, 128), lambda i, p0, p1: (i, 0))],
            out_specs=pl.BlockSpec(memory_space=pl.ANY),
            scratch_shapes=[pltpu.SemaphoreType.DMA(())],
        ),
        out_shape=jax.ShapeDtypeStruct((N_SLOTS * ROW_SLABS, 128), F32),
        name="moe_dispatch",
    )(pos0, pos1, h1_rows)


def _expert_kernel(tile_end_ref, xs_ref, w1_ref, w3_ref, w2_ref, ys_ref):
    @pl.when(pl.program_id(0) < tile_end_ref[N_EXPERTS - 1])
    def _():
        x = _load_row_tiles(xs_ref, MOE_TILE).astype(BF16)
        a = jnp.dot(x, w1_ref[0].astype(BF16), preferred_element_type=F32)
        b = jnp.dot(x, w3_ref[0].astype(BF16), preferred_element_type=F32)
        hid = (a * jax.nn.sigmoid(a) * b).astype(BF16)
        y = jnp.dot(hid, w2_ref[0].astype(BF16), preferred_element_type=F32)
        _store_row_tiles(ys_ref, y)


def _experts(tile_end, xs, w1, w3, w2):
    def tile(t, tile_end):
        return jnp.minimum(t, tile_end[N_EXPERTS - 1] - 1), 0

    def wsel(t, tile_end):
        e = jnp.int32(0)
        step = N_EXPERTS // 2
        while step:
            e = jnp.where(t >= tile_end[e + step - 1], e + step, e)
            step //= 2
        return e, 0, 0

    return pl.pallas_call(
        _expert_kernel,
        grid_spec=pltpu.PrefetchScalarGridSpec(
            num_scalar_prefetch=1,
            grid=(N_MOE_TILES,),
            in_specs=[
                pl.BlockSpec((MOE_TILE * ROW_SLABS, 128), tile),
                pl.BlockSpec((1, D_MODEL, D_EXPERT), wsel),
                pl.BlockSpec((1, D_MODEL, D_EXPERT), wsel),
                pl.BlockSpec((1, D_EXPERT, D_MODEL), wsel),
            ],
            out_specs=pl.BlockSpec((MOE_TILE * ROW_SLABS, 128), tile),
        ),
        out_shape=jax.ShapeDtypeStruct((N_SLOTS * ROW_SLABS, 128), F32),
        compiler_params=_vmem(40 << 20),
        name="moe_experts",
    )(tile_end, xs, w1, w3, w2)


def _combine_kernel(pos0_ref, pos1_ref, ys_ref, h1_ref, w0_ref, w1_ref, g_ref, b_ref, o_ref,
                    y0_buf, y1_buf, sems):
    i = pl.program_id(0)
    n = pl.num_programs(0)

    def gather(tile, slot):
        base = tile * COMBINE_TILE

        def issue(t, carry):
            pltpu.make_async_copy(_row_tile(ys_ref, pos0_ref[base + t]), _row_tile(y0_buf.at[slot], t),
                                  sems.at[slot]).start(priority=0)
            pltpu.make_async_copy(_row_tile(ys_ref, pos1_ref[base + t]), _row_tile(y1_buf.at[slot], t),
                                  sems.at[slot]).start(priority=1)
            return carry

        lax.fori_loop(0, COMBINE_TILE, issue, 0, unroll=8)

    @pl.when(i == 0)
    def _():
        gather(0, 0)

    @pl.when(i + 1 < n)
    def _():
        gather(i + 1, (i + 1) % 2)

    slot = i % 2
    whole = ys_ref.at[pl.ds(0, COMBINE_TILE * ROW_SLABS), :]
    pltpu.make_async_copy(whole, y0_buf.at[slot], sems.at[slot]).wait()
    pltpu.make_async_copy(whole, y1_buf.at[slot], sems.at[slot]).wait()
    w0 = jnp.concatenate([w0_ref[...]] * ROW_SLABS, axis=1)
    w1 = jnp.concatenate([w1_ref[...]] * ROW_SLABS, axis=1)
    f = (w0 * _load_row_tiles(y0_buf.at[slot], COMBINE_TILE)
         + w1 * _load_row_tiles(y1_buf.at[slot], COMBINE_TILE))
    h1 = _load_row_tiles(h1_ref, COMBINE_TILE)
    o_ref[...] = _layer_norm(DEEPNORM_ALPHA * h1 + f, g_ref[...], b_ref[...])


def _combine(pos0, pos1, ys, h1_rows, w0, w1, ln_g, ln_b):
    row = lambda i, p0, p1: (i, 0)
    const = lambda i, p0, p1: (0, 0)
    return pl.pallas_call(
        _combine_kernel,
        grid_spec=pltpu.PrefetchScalarGridSpec(
            num_scalar_prefetch=2,
            grid=(N_TOK // COMBINE_TILE,),
            in_specs=[
                pl.BlockSpec(memory_space=pl.ANY),
                pl.BlockSpec((COMBINE_TILE * ROW_SLABS, 128), row),
                pl.BlockSpec((COMBINE_TILE, 128), row),
                pl.BlockSpec((COMBINE_TILE, 128), row),
                pl.BlockSpec((1, D_MODEL), const),
                pl.BlockSpec((1, D_MODEL), const),
            ],
            out_specs=pl.BlockSpec((COMBINE_TILE, D_MODEL), row),
            scratch_shapes=[pltpu.VMEM((2, COMBINE_TILE * ROW_SLABS, 128), F32),
                            pltpu.VMEM((2, COMBINE_TILE * ROW_SLABS, 128), F32),
                            pltpu.SemaphoreType.DMA((2,))],
        ),
        out_shape=jax.ShapeDtypeStruct((N_TOK, D_MODEL), F32),
        compiler_params=pltpu.CompilerParams(dimension_semantics=("arbitrary",), vmem_limit_bytes=40 << 20),
        name="moe_combine",
    )(pos0, pos1, ys, h1_rows, w0, w1, ln_g, ln_b)


def kernel(x, mem, ln0_g, ln0_b, rel_bias, w_in, w_mem_kv, w_fourier, lambda_q1, lambda_k1, lambda_q2,
           lambda_k2, subln_g, w_out, ln1_g, ln1_b, w_group, b_group, w_router, b_router, w1, w3, w2,
           ln2_g, ln2_b):
    l = 0
    x2 = x.reshape(N_TOK, D_MODEL)
    wi = w_in[l]
    w_qk = wi[:, :2 * DIFF_WIDTH].astype(BF16)
    w_vt = wi[:, 2 * DIFF_WIDTH:3 * DIFF_WIDTH].T.astype(BF16)
    w_fm = wi[:, 3 * DIFF_WIDTH:].astype(BF16)
    h, q, k, vt, f_in, mq = _ln0_inproj(x2, ln0_g.reshape(1, -1), ln0_b.reshape(1, -1), w_qk, w_vt, w_fm)

    T = ATT_TILE
    kk = jnp.arange(T, dtype=jnp.int32)[:, None]
    qq = jnp.arange(ATT_Q, dtype=jnp.int32)[None, :]
    buckets = jnp.stack([_t5_bucket(d * T + kk - qq) for d in range(-2, ATT_Q // T + 2)])
    lam_vecs = jnp.stack([lambda_q1[l], lambda_k1[l], lambda_q2[l], lambda_k2[l]]).astype(F32)
    far_buckets = _t5_bucket(jnp.array([-(T + 1), T + 1], jnp.int32))
    o_diff = _diff_attention(rel_bias.astype(F32), far_buckets, q, k, vt, buckets, lam_vecs,
                             subln_g[l].astype(F32).reshape(DIFF_VDIM, 1))

    bdc, bds, m1, m2, twc, tws = (jnp.asarray(c) for c in _dft_constants())
    wf = w_fourier[l]
    bdw = jnp.zeros((FOURIER_WIDTH, FOURIER_WIDTH), F32)
    for g in range(FOURIER_GROUPS):
        sl = slice(g * FOURIER_GROUP_DIM, (g + 1) * FOURIER_GROUP_DIM)
        bdw = bdw.at[sl, sl].set(wf[g])
    o_four = _fourier(f_in, bdc, bds, bdw.astype(BF16), m1, m2, twc, tws)

    kcat, vcat = _mem_kv(mem.reshape(BATCH * MEM_LEN, D_MODEL), w_mem_kv[l].astype(BF16))
    o_mem = _mem_attention(mq, kcat, vcat)

    w_r = jnp.zeros((D_MODEL, ROUTER_LANES), F32)
    w_r = w_r.at[:, :N_EXPERTS].set(w_router[l].astype(F32))
    w_r = w_r.at[:, GROUP_LANE0:GROUP_LANE0 + N_GROUPS].set(w_group[l].astype(F32))
    wr_hi = w_r.astype(BF16)
    wr_lo = (w_r - wr_hi.astype(F32)).astype(BF16)
    b_r = jnp.zeros((1, ROUTER_LANES), F32)
    b_r = b_r.at[0, :N_EXPERTS].set(b_router[l].astype(F32))
    b_r = b_r.at[0, GROUP_LANE0:GROUP_LANE0 + N_GROUPS].set(b_group[l].astype(F32))
    h1_rows, route, gate0, gate1, counts = _outproj_router(
        o_diff, o_four, o_mem, h, w_out[l].astype(BF16),
        ln1_g[l].reshape(1, -1), ln1_b[l].reshape(1, -1), jnp.concatenate([wr_hi, wr_lo], axis=1), b_r)

    cnt = counts[0, :N_EXPERTS].astype(jnp.int32)
    tiles_per_expert = (cnt + MOE_TILE - 1) // MOE_TILE
    tile_end = jnp.cumsum(tiles_per_expert).astype(jnp.int32)
    row_start = (tile_end - tiles_per_expert) * MOE_TILE
    expert_ids = jnp.arange(N_EXPERTS, dtype=jnp.int32)[None, :]

    def slot_of(expert, rank):
        return jnp.sum(jnp.where(expert[:, None] == expert_ids, row_start[None, :], 0), axis=1) + rank

    pos0 = slot_of(route[0], route[2])
    pos1 = slot_of(route[1], route[3])

    xs = _dispatch(pos0, pos1, h1_rows)
    ys = _experts(tile_end, xs, w1[l], w3[l], w2[l])
    out = _combine(pos0, pos1, ys, h1_rows, gate0, gate1, ln2_g[l].reshape(1, -1), ln2_b[l].reshape(1, -1))
    return out.reshape(BATCH, SEQ, D_MODEL)
```

```python
import functools
import math

import numpy as np
import jax
import jax.numpy as jnp
from jax import lax
from jax.experimental import pallas as pl
from jax.experimental.pallas import tpu as pltpu

F32 = jnp.float32
BF16 = jnp.bfloat16

D_MODEL = 1024
BATCH = 4
SEQ = 4096
N_TOK = BATCH * SEQ
MEM_LEN = 256
MEM_HEADS = 4
MEM_HEAD_DIM = 64
MEM_WIDTH = 256
DIFF_HEADS = 4
DIFF_HEAD_DIM = 64
DIFF_VDIM = 2 * DIFF_HEAD_DIM
VT_ROWS = DIFF_VDIM + 16
DIFF_WIDTH = 512
FOURIER_GROUPS = 4
FOURIER_GROUP_DIM = 64
FOURIER_WIDTH = 256
N_BUCKETS = 32
MAX_DISTANCE = 128
N_GROUPS = 4
EXPERTS_PER_GROUP = 8
N_EXPERTS = 32
D_EXPERT = 256
LN_EPS = 1e-5
DEEPNORM_ALPHA = 2.0 ** 0.25
LAM_INIT = 0.8 - 0.6 * math.exp(-0.3 * 0)
LOG2E = 1.4426950408889634

ROW_TILE = 512
ATT_TILE = 256
ATT_Q = 256
ATT_SUB = 256
MOE_TILE = 512
N_MOE_TILES = 2 * N_TOK // MOE_TILE + N_EXPERTS
N_SLOTS = N_MOE_TILES * MOE_TILE
ROW_SLABS = D_MODEL // 128
DISPATCH_TILE = 512
COMBINE_TILE = 512
FFT_RADIX = 64
FFT_PITCH = 72
ROUTER_LANES = 128
GROUP_LANE0 = 32

_NT = (((1,), (1,)), ((), ()))


def _vmem(nbytes):
    return pltpu.CompilerParams(vmem_limit_bytes=int(nbytes))


def _layer_norm(x, g, b):
    mu = jnp.mean(x, axis=-1, keepdims=True)
    xc = x - mu
    var = jnp.mean(xc * xc, axis=-1, keepdims=True)
    return xc * lax.rsqrt(var + LN_EPS) * g + b


def _load_row_tiles(ref, rows):
    return jnp.concatenate([ref[pl.ds(s, rows, stride=ROW_SLABS), :] for s in range(ROW_SLABS)], axis=1)


def _store_row_tiles(ref, x):
    for s in range(ROW_SLABS):
        ref[pl.ds(s, x.shape[0], stride=ROW_SLABS), :] = x[:, s * 128:(s + 1) * 128]


def _row_tile(ref, t):
    return ref.at[pl.ds(pl.multiple_of(t * ROW_SLABS, ROW_SLABS), ROW_SLABS), :]


def _t5_bucket(rel):
    nb = N_BUCKETS // 2
    max_exact = nb // 2
    ret = (rel > 0).astype(jnp.int32) * nb
    n = jnp.abs(rel)
    nf = jnp.maximum(n, 1).astype(F32)
    large = max_exact + (jnp.log(nf / max_exact) / math.log(MAX_DISTANCE / max_exact)
                         * (nb - max_exact)).astype(jnp.int32)
    large = jnp.minimum(large, nb - 1)
    return ret + jnp.where(n < max_exact, n, large)


def _ln0_inproj_kernel(x_ref, g_ref, b_ref, wqk_ref, wvt_ref, wfm_ref,
                       h_ref, q_ref, k_ref, vt_ref, f_ref, mq_ref):
    h = _layer_norm(x_ref[...], g_ref[...], b_ref[...])
    h_ref[...] = h
    hb = h.astype(BF16)
    qk = jnp.dot(hb, wqk_ref[...], preferred_element_type=F32)
    q_ref[...] = (qk[:, :DIFF_WIDTH] * (DIFF_HEAD_DIM ** -0.5 * LOG2E)).astype(BF16)
    k_ref[...] = qk[:, DIFF_WIDTH:].astype(BF16)
    vt = lax.dot_general(wvt_ref[...], hb, _NT, preferred_element_type=F32)
    ones = jnp.ones((VT_ROWS - DIFF_VDIM, ROW_TILE), BF16)
    for hh in range(DIFF_HEADS):
        vt_ref[0, hh, :DIFF_VDIM, :] = vt[hh * DIFF_VDIM:(hh + 1) * DIFF_VDIM, :].astype(BF16)
        vt_ref[0, hh, DIFF_VDIM:, :] = ones
    fm = jnp.dot(hb, wfm_ref[...], preferred_element_type=F32)
    f_ref[...] = fm[:, :FOURIER_WIDTH].astype(BF16)
    mq_ref[...] = (fm[:, FOURIER_WIDTH:] * MEM_HEAD_DIM ** -0.5).astype(BF16)


def _ln0_inproj(x2, ln_g, ln_b, w_qk, w_vt, w_fm):
    rt = SEQ // ROW_TILE
    row = lambda r: (r, 0)
    const = lambda r: (0, 0)
    return pl.pallas_call(
        _ln0_inproj_kernel,
        grid=(N_TOK // ROW_TILE,),
        in_specs=[
            pl.BlockSpec((ROW_TILE, D_MODEL), row),
            pl.BlockSpec((1, D_MODEL), const),
            pl.BlockSpec((1, D_MODEL), const),
            pl.BlockSpec((D_MODEL, 2 * DIFF_WIDTH), const),
            pl.BlockSpec((DIFF_WIDTH, D_MODEL), const),
            pl.BlockSpec((D_MODEL, FOURIER_WIDTH + MEM_WIDTH), const),
        ],
        out_specs=[
            pl.BlockSpec((ROW_TILE, D_MODEL), row),
            pl.BlockSpec((ROW_TILE, DIFF_WIDTH), row),
            pl.BlockSpec((ROW_TILE, DIFF_WIDTH), row),
            pl.BlockSpec((1, DIFF_HEADS, VT_ROWS, ROW_TILE), lambda r: (r // rt, 0, 0, r % rt)),
            pl.BlockSpec((ROW_TILE, FOURIER_WIDTH), row),
            pl.BlockSpec((ROW_TILE, MEM_WIDTH), row),
        ],
        out_shape=[
            jax.ShapeDtypeStruct((N_TOK, D_MODEL), F32),
            jax.ShapeDtypeStruct((N_TOK, DIFF_WIDTH), BF16),
            jax.ShapeDtypeStruct((N_TOK, DIFF_WIDTH), BF16),
            jax.ShapeDtypeStruct((BATCH, DIFF_HEADS, VT_ROWS, SEQ), BF16),
            jax.ShapeDtypeStruct((N_TOK, FOURIER_WIDTH), BF16),
            jax.ShapeDtypeStruct((N_TOK, MEM_WIDTH), BF16),
        ],
        compiler_params=_vmem(48 << 20),
        name="ln0_inproj",
    )(x2, ln_g, ln_b, w_qk, w_vt, w_fm)


def _diff_attn_kernel(tbl_ref, q_ref, k_ref, vt_ref, bkt_ref, lam_ref, g_ref, o_ref,
                      bias_scr, p0_scr, p1_scr):
    T = ATT_TILE
    TQ = ATT_Q
    r = TQ // T
    h = pl.program_id(0)
    i = pl.program_id(2)
    p_scr = (p0_scr, p1_scr)

    @pl.when((pl.program_id(1) == 0) & (i == 0))
    def _build_bias():
        for d in range(r + 4):
            bk = bkt_ref[d]
            bias = jnp.zeros((T, TQ), F32)
            for n in range(N_BUCKETS):
                bias = jnp.where(bk == n, tbl_ref[n, h], bias)
            bias_scr[d] = bias * LOG2E

    q = q_ref[...]
    lane = lax.broadcasted_iota(jnp.int32, q.shape, 1)
    zero = jnp.zeros_like(q)
    q_comp = (jnp.where(lane < DIFF_HEAD_DIM, q, zero), jnp.where(lane >= DIFF_HEAD_DIM, q, zero))

    def bias_tile(j):
        return bias_scr[jnp.clip(j - r * i, -2, r + 1) + 2]

    SUB = ATT_SUB
    n_sub = SEQ // SUB
    m_tile = [[None] * n_sub for _ in range(2)]
    for c in range(2):
        s = lax.dot_general(k_ref[...], q_comp[c], _NT, preferred_element_type=F32)
        for u in range(n_sub):
            if SUB >= T:
                bias = jnp.concatenate([bias_tile(u * SUB // T + t) for t in range(SUB // T)], axis=0)
            else:
                bias = bias_tile(u * SUB // T)[(u * SUB) % T:(u * SUB) % T + SUB, :]
            sb = s[u * SUB:(u + 1) * SUB, :] + bias
            mu = jnp.max(sb, axis=0, keepdims=True)
            p_scr[c][u * SUB:(u + 1) * SUB, :] = jnp.exp2((sb - mu).astype(BF16))
            m_tile[c][u] = mu

    acc = []
    for c in range(2):
        m = functools.reduce(jnp.maximum, m_tile[c])
        a = None
        for u in range(n_sub):
            pv = jnp.dot(vt_ref[0, 0, :, u * SUB:(u + 1) * SUB], p_scr[c][u * SUB:(u + 1) * SUB, :],
                         preferred_element_type=F32)
            pv = pv * jnp.exp2(m_tile[c][u] - m)
            a = pv if a is None else a + pv
        acc.append(a)
    num = [a[:DIFF_VDIM] for a in acc]
    den = [a[DIFF_VDIM:DIFF_VDIM + 1] for a in acc]

    lam = (jnp.exp(jnp.sum(lam_ref[0:1, :] * lam_ref[1:2, :], axis=1, keepdims=True))
           - jnp.exp(jnp.sum(lam_ref[2:3, :] * lam_ref[3:4, :], axis=1, keepdims=True)) + LAM_INIT)
    o = num[0] * (1.0 / den[0]) - lam * (num[1] * (1.0 / den[1]))
    ms = jnp.mean(o * o, axis=0, keepdims=True)
    o = o * lax.rsqrt(ms + LN_EPS) * g_ref[...] * (1.0 - LAM_INIT)
    o_ref[...] = o.T.astype(BF16)


def _diff_attention(rel_bias, q, k, vt, buckets, lam_vecs, subln_col):
    T = ATT_TILE
    TQ = ATT_Q
    nq = SEQ // TQ
    n_bias = TQ // T + 4
    return pl.pallas_call(
        _diff_attn_kernel,
        grid=(DIFF_HEADS, BATCH, nq),
        in_specs=[
            pl.BlockSpec(memory_space=pltpu.SMEM),
            pl.BlockSpec((TQ, DIFF_VDIM), lambda h, b, i: (b * nq + i, h)),
            pl.BlockSpec((SEQ, DIFF_VDIM), lambda h, b, i: (b, h)),
            pl.BlockSpec((1, 1, VT_ROWS, SEQ), lambda h, b, i: (b, h, 0, 0)),
            pl.BlockSpec((n_bias, T, TQ), lambda h, b, i: (0, 0, 0)),
            pl.BlockSpec((4, DIFF_HEAD_DIM), lambda h, b, i: (0, 0)),
            pl.BlockSpec((DIFF_VDIM, 1), lambda h, b, i: (0, 0)),
        ],
        out_specs=pl.BlockSpec((TQ, DIFF_VDIM), lambda h, b, i: (b * nq + i, h)),
        out_shape=jax.ShapeDtypeStruct((N_TOK, DIFF_WIDTH), BF16),
        scratch_shapes=[pltpu.VMEM((n_bias, T, TQ), F32),
                        pltpu.VMEM((SEQ, TQ), BF16), pltpu.VMEM((SEQ, TQ), BF16)],
        compiler_params=_vmem(40 << 20),
        name="diff_attn",
    )(rel_bias, q, k, vt, buckets, lam_vecs, subln_col)


def _fourier_kernel(f_ref, bdc_ref, bds_ref, bdw_ref, m1_ref, m2_ref, twc_ref, tws_ref, o_ref,
                    zr_scr, zi_scr, d_scr, a_scr):
    R = FFT_RADIX
    W = FOURIER_WIDTH
    u = f_ref[...]
    w = bdw_ref[...]
    pc = jnp.dot(u, bdc_ref[...], preferred_element_type=F32).astype(BF16)
    ps = jnp.dot(u, bds_ref[...], preferred_element_type=F32).astype(BF16)
    zr = jnp.dot(pc, w, preferred_element_type=F32)
    zi = jnp.dot(ps, w, preferred_element_type=F32)
    P = FFT_PITCH
    halves = range(W // 128)
    for hh in halves:
        for n1 in range(R):
            zr_scr[hh, n1 * P:n1 * P + R, :] = zr[n1 * R:(n1 + 1) * R, hh * 128:(hh + 1) * 128]
            zi_scr[hh, n1 * P:n1 * P + R, :] = zi[n1 * R:(n1 + 1) * R, hh * 128:(hh + 1) * 128]

    for n2 in range(R):
        for hh in halves:
            cols = slice(n2 * W + hh * 128, n2 * W + (hh + 1) * 128)
            d_scr[0:R, cols] = zr_scr[hh, pl.ds(n2, R, stride=P), :].astype(BF16)
            d_scr[R:2 * R, cols] = zi_scr[hh, pl.ds(n2, R, stride=P), :].astype(BF16)
    a_scr[...] = jnp.dot(m1_ref[...], d_scr[...], preferred_element_type=F32)

    for n2 in range(R):
        tc = twc_ref[n2]
        ts = tws_ref[n2]
        for hh in halves:
            cols = slice(n2 * W + hh * 128, n2 * W + (hh + 1) * 128)
            ar = a_scr[0:R, cols]
            ai = a_scr[R:2 * R, cols]
            zr_scr[hh, pl.ds(n2, R, stride=P), :] = ar * tc + ai * ts
            zi_scr[hh, pl.ds(n2, R, stride=P), :] = ai * tc - ar * ts
    for k1 in range(R):
        for hh in halves:
            cols = slice(k1 * W + hh * 128, k1 * W + (hh + 1) * 128)
            d_scr[0:R, cols] = zr_scr[hh, k1 * P:k1 * P + R, :].astype(BF16)
            d_scr[R:2 * R, cols] = zi_scr[hh, k1 * P:k1 * P + R, :].astype(BF16)
    a_scr[0:R, :] = jnp.dot(m2_ref[...], d_scr[...], preferred_element_type=F32)
    for k1 in range(R):
        for hh in halves:
            zr_scr[hh, pl.ds(k1, R, stride=P), :] = a_scr[0:R, k1 * W + hh * 128:k1 * W + (hh + 1) * 128]
    for k2 in range(R):
        for hh in halves:
            o_ref[hh, k2 * R:(k2 + 1) * R, :] = zr_scr[hh, k2 * P:k2 * P + R, :]


def _fourier(f_in, bdc, bds, bdw, m1, m2, twc, tws):
    R = FFT_RADIX
    W = FOURIER_WIDTH
    const2 = lambda b: (0, 0)
    const3 = lambda b: (0, 0, 0)
    return pl.pallas_call(
        _fourier_kernel,
        grid=(BATCH,),
        in_specs=[
            pl.BlockSpec((SEQ, W), lambda b: (b, 0)),
            pl.BlockSpec((W, W), const2),
            pl.BlockSpec((W, W), const2),
            pl.BlockSpec((W, W), const2),
            pl.BlockSpec((2 * R, 2 * R), const2),
            pl.BlockSpec((R, 2 * R), const2),
            pl.BlockSpec((R, R, 128), const3),
            pl.BlockSpec((R, R, 128), const3),
        ],
        out_specs=pl.BlockSpec((W // 128, SEQ, 128), lambda b: (0, b, 0)),
        out_shape=jax.ShapeDtypeStruct((W // 128, N_TOK, 128), F32),
        scratch_shapes=[pltpu.VMEM((W // 128, R * FFT_PITCH, 128), F32),
                        pltpu.VMEM((W // 128, R * FFT_PITCH, 128), F32),
                        pltpu.VMEM((2 * R, R * W), BF16), pltpu.VMEM((2 * R, R * W), F32)],
        compiler_params=_vmem(52 << 20),
        name="fourier",
    )(f_in, bdc, bds, bdw, m1, m2, twc, tws)


@functools.lru_cache(maxsize=None)
def _dft_constants():
    R = FFT_RADIX
    a = np.arange(R, dtype=np.int64)
    ang = 2.0 * np.pi * ((a[:, None] * a[None, :]) % R).astype(np.float64) / R
    c64, s64 = np.cos(ang), np.sin(ang)
    scale = 1.0 / math.sqrt(SEQ * FOURIER_GROUP_DIM)
    m1 = np.block([[c64, s64], [-s64, c64]])
    m2 = np.concatenate([c64, s64], axis=1) * scale
    tang = 2.0 * np.pi * (a[:, None] * a[None, :]).astype(np.float64) / SEQ
    twc = np.repeat(np.cos(tang)[:, :, None], 128, axis=2).astype(np.float32)
    tws = np.repeat(np.sin(tang)[:, :, None], 128, axis=2).astype(np.float32)
    assert FOURIER_GROUP_DIM == R
    eye = np.eye(FOURIER_GROUPS)
    bdc = np.kron(eye, c64)
    bds = np.kron(eye, -s64)
    return bdc.astype(BF16), bds.astype(BF16), m1.astype(BF16), m2.astype(BF16), twc, tws


def _mem_kv_kernel(mem_ref, w_ref, kcat_ref, vcat_ref):
    kv = jnp.dot(mem_ref[...].astype(BF16), w_ref[...], preferred_element_type=F32)
    mk = kv[:, :MEM_WIDTH]
    mv = kv[:, MEM_WIDTH:]
    lane = lax.broadcasted_iota(jnp.int32, mk.shape, 1)
    for hh in range(MEM_HEADS):
        sel = (lane >= hh * MEM_HEAD_DIM) & (lane < (hh + 1) * MEM_HEAD_DIM)
        kcat_ref[0, hh * MEM_LEN:(hh + 1) * MEM_LEN, :] = jnp.where(sel, mk, 0.0).astype(BF16)
        vcat_ref[0, hh * MEM_LEN:(hh + 1) * MEM_LEN, :] = jnp.where(sel, mv, 0.0).astype(BF16)


def _mem_kv(mem2, w_kv):
    return pl.pallas_call(
        _mem_kv_kernel,
        grid=(BATCH,),
        in_specs=[
            pl.BlockSpec((MEM_LEN, D_MODEL), lambda b: (b, 0)),
            pl.BlockSpec((D_MODEL, 2 * MEM_WIDTH), lambda b: (0, 0)),
        ],
        out_specs=[
            pl.BlockSpec((1, MEM_HEADS * MEM_LEN, MEM_WIDTH), lambda b: (b, 0, 0)),
            pl.BlockSpec((1, MEM_HEADS * MEM_LEN, MEM_WIDTH), lambda b: (b, 0, 0)),
        ],
        out_shape=[
            jax.ShapeDtypeStruct((BATCH, MEM_HEADS * MEM_LEN, MEM_WIDTH), BF16),
            jax.ShapeDtypeStruct((BATCH, MEM_HEADS * MEM_LEN, MEM_WIDTH), BF16),
        ],
        name="mem_kv",
    )(mem2, w_kv)


def _mem_attn_kernel(mq_ref, kcat_ref, vcat_ref, o_ref):
    s = lax.dot_general(mq_ref[...], kcat_ref[0], _NT, preferred_element_type=F32)
    parts = []
    for hh in range(MEM_HEADS):
        sh = s[:, hh * MEM_LEN:(hh + 1) * MEM_LEN]
        p = jnp.exp(sh - jnp.max(sh, axis=1, keepdims=True))
        parts.append((p * (1.0 / jnp.sum(p, axis=1, keepdims=True))).astype(BF16))
    p_all = jnp.concatenate(parts, axis=1)
    o_ref[...] = jnp.dot(p_all, vcat_ref[0], preferred_element_type=F32).astype(BF16)


def _mem_attention(mq, kcat, vcat):
    rt = SEQ // ROW_TILE
    return pl.pallas_call(
        _mem_attn_kernel,
        grid=(N_TOK // ROW_TILE,),
        in_specs=[
            pl.BlockSpec((ROW_TILE, MEM_WIDTH), lambda r: (r, 0)),
            pl.BlockSpec((1, MEM_HEADS * MEM_LEN, MEM_WIDTH), lambda r: (r // rt, 0, 0)),
            pl.BlockSpec((1, MEM_HEADS * MEM_LEN, MEM_WIDTH), lambda r: (r // rt, 0, 0)),
        ],
        out_specs=pl.BlockSpec((ROW_TILE, MEM_WIDTH), lambda r: (r, 0)),
        out_shape=jax.ShapeDtypeStruct((N_TOK, MEM_WIDTH), BF16),
        name="mem_attn",
    )(mq, kcat, vcat)


def _route(logits):
    lane = lax.broadcasted_iota(jnp.int32, logits.shape, 1)
    big = jnp.int32(ROUTER_LANES)
    is_group = (lane >= GROUP_LANE0) & (lane < GROUP_LANE0 + N_GROUPS)
    gl = jnp.where(is_group, logits, -jnp.inf)
    gmax = jnp.max(gl, axis=1, keepdims=True)
    g_sel = jnp.min(jnp.where(gl == gmax, lane, big), axis=1, keepdims=True) - GROUP_LANE0
    g_gate = 1.0 / jnp.sum(jnp.where(is_group, jnp.exp(gl - gmax), 0.0), axis=1, keepdims=True)
    in_group = (lane >= g_sel * EXPERTS_PER_GROUP) & (lane < (g_sel + 1) * EXPERTS_PER_GROUP)
    el = jnp.where(in_group, logits, -jnp.inf)
    v1 = jnp.max(el, axis=1, keepdims=True)
    i1 = jnp.min(jnp.where(el == v1, lane, big), axis=1, keepdims=True)
    el2 = jnp.where(lane == i1, -jnp.inf, el)
    v2 = jnp.max(el2, axis=1, keepdims=True)
    i2 = jnp.min(jnp.where(el2 == v2, lane, big), axis=1, keepdims=True)
    e = jnp.exp(v2 - v1)
    w_first = g_gate / (1.0 + e)
    w_second = g_gate * e / (1.0 + e)
    return i1, i2, w_first, w_second


def _outproj_router_kernel(od_ref, of_ref, om_ref, h_ref, wo_ref, g_ref, b_ref,
                           wr_ref, br_ref,
                           h1_ref, route_ref, w0_ref, w1_ref, cnt_ref, cnt_scr):
    @pl.when(pl.program_id(0) == 0)
    def _():
        cnt_scr[...] = jnp.zeros_like(cnt_scr)

    o_four = [of_ref[hh].astype(BF16) for hh in range(FOURIER_WIDTH // 128)]
    o = jnp.concatenate([od_ref[...]] + o_four + [om_ref[...]], axis=1)
    a = jnp.dot(o, wo_ref[...], preferred_element_type=F32)
    h1 = _layer_norm(DEEPNORM_ALPHA * h_ref[...] + a, g_ref[...], b_ref[...])
    _store_row_tiles(h1_ref, h1)
    hi = h1.astype(BF16)
    lo = (h1 - hi.astype(F32)).astype(BF16)
    hw = jnp.dot(hi, wr_ref[...], preferred_element_type=F32)
    logits = (hw[:, :ROUTER_LANES] + hw[:, ROUTER_LANES:]
              + jnp.dot(lo, wr_ref[:, :ROUTER_LANES], preferred_element_type=F32) + br_ref[...])
    i1, i2, w_first, w_second = _route(logits)

    lane = lax.broadcasted_iota(jnp.int32, logits.shape, 1)
    onehot = jnp.where(lane == i1, 1.0, jnp.where(lane == i2, 1.0, 0.0))
    r_id = lax.broadcasted_iota(jnp.int32, (ROW_TILE, ROW_TILE), 0)
    c_id = lax.broadcasted_iota(jnp.int32, (ROW_TILE, ROW_TILE), 1)
    tri = jnp.where(r_id > c_id, 1.0, 0.0).astype(BF16)
    before = jnp.dot(tri, onehot.astype(BF16), preferred_element_type=F32) + cnt_scr[...]
    rank1 = jnp.sum(jnp.where(lane == i1, before, 0.0), axis=1, keepdims=True)
    rank2 = jnp.sum(jnp.where(lane == i2, before, 0.0), axis=1, keepdims=True)
    cnt_scr[...] += jnp.sum(onehot, axis=0, keepdims=True)
    cnt_ref[...] = cnt_scr[...]

    packed = jnp.where(lane == 0, i1.astype(F32),
                       jnp.where(lane == 1, i2.astype(F32),
                                 jnp.where(lane == 2, rank1, jnp.where(lane == 3, rank2, 0.0))))
    route_ref[...] = packed.T[:8, :].astype(jnp.int32)
    w0_ref[...] = jnp.broadcast_to(w_first, (ROW_TILE, 128))
    w1_ref[...] = jnp.broadcast_to(w_second, (ROW_TILE, 128))


def _outproj_router(o_diff, o_four, o_mem, h, w_out, ln_g, ln_b, w_r, b_r):
    row = lambda r: (r, 0)
    const = lambda r: (0, 0)
    return pl.pallas_call(
        _outproj_router_kernel,
        grid=(N_TOK // ROW_TILE,),
        in_specs=[
            pl.BlockSpec((ROW_TILE, DIFF_WIDTH), row),
            pl.BlockSpec((FOURIER_WIDTH // 128, ROW_TILE, 128), lambda r: (0, r, 0)),
            pl.BlockSpec((ROW_TILE, MEM_WIDTH), row),
            pl.BlockSpec((ROW_TILE, D_MODEL), row),
            pl.BlockSpec((D_MODEL, D_MODEL), const),
            pl.BlockSpec((1, D_MODEL), const),
            pl.BlockSpec((1, D_MODEL), const),
            pl.BlockSpec((D_MODEL, 2 * ROUTER_LANES), const),
            pl.BlockSpec((1, ROUTER_LANES), const),
        ],
        out_specs=[
            pl.BlockSpec((ROW_TILE * ROW_SLABS, 128), row),
            pl.BlockSpec((8, ROW_TILE), lambda r: (0, r)),
            pl.BlockSpec((ROW_TILE, 128), row),
            pl.BlockSpec((ROW_TILE, 128), row),
            pl.BlockSpec((1, ROUTER_LANES), const),
        ],
        out_shape=[
            jax.ShapeDtypeStruct((N_TOK * ROW_SLABS, 128), F32),
            jax.ShapeDtypeStruct((8, N_TOK), jnp.int32),
            jax.ShapeDtypeStruct((N_TOK, 128), F32),
            jax.ShapeDtypeStruct((N_TOK, 128), F32),
            jax.ShapeDtypeStruct((1, ROUTER_LANES), F32),
        ],
        scratch_shapes=[pltpu.VMEM((1, ROUTER_LANES), F32)],
        compiler_params=_vmem(40 << 20),
        name="outproj_router",
    )(o_diff, o_four, o_mem, h, w_out, ln_g, ln_b, w_r, b_r)


def _dispatch_kernel(pos0_ref, pos1_ref, x_ref, xs_ref, sem):
    base = pl.program_id(0) * DISPATCH_TILE

    def issue(t, carry):
        src = _row_tile(x_ref, t)
        pltpu.make_async_copy(src, _row_tile(xs_ref, pos0_ref[base + t]), sem).start(priority=0)
        pltpu.make_async_copy(src, _row_tile(xs_ref, pos1_ref[base + t]), sem).start(priority=1)
        return carry

    lax.fori_loop(0, DISPATCH_TILE, issue, 0, unroll=8)
    for _ in range(2):
        pltpu.make_async_copy(x_ref, xs_ref.at[pl.ds(0, DISPATCH_TILE * ROW_SLABS), :], sem).wait()


def _dispatch(pos0, pos1, h1_rows):
    return pl.pallas_call(
        _dispatch_kernel,
        grid_spec=pltpu.PrefetchScalarGridSpec(
            num_scalar_prefetch=2,
            grid=(N_TOK // DISPATCH_TILE,),
            in_specs=[pl.BlockSpec((DISPATCH_TILE * ROW_SLABS, 128), lambda i, p0, p1: (i, 0))],
            out_specs=pl.BlockSpec(memory_space=pl.ANY),
            scratch_shapes=[pltpu.SemaphoreType.DMA(())],
        ),
        out_shape=jax.ShapeDtypeStruct((N_SLOTS * ROW_SLABS, 128), F32),
        name="moe_dispatch",
    )(pos0, pos1, h1_rows)


def _expert_kernel(tile_end_ref, xs_ref, w1_ref, w3_ref, w2_ref, ys_ref):
    @pl.when(pl.program_id(0) < tile_end_ref[N_EXPERTS - 1])
    def _():
        x = _load_row_tiles(xs_ref, MOE_TILE).astype(BF16)
        a = jnp.dot(x, w1_ref[0].astype(BF16), preferred_element_type=F32)
        b = jnp.dot(x, w3_ref[0].astype(BF16), preferred_element_type=F32)
        hid = (a * jax.nn.sigmoid(a) * b).astype(BF16)
        y = jnp.dot(hid, w2_ref[0].astype(BF16), preferred_element_type=F32)
        _store_row_tiles(ys_ref, y)


def _experts(tile_end, xs, w1, w3, w2):
    def tile(t, tile_end):
        return jnp.minimum(t, tile_end[N_EXPERTS - 1] - 1), 0

    def wsel(t, tile_end):
        e = jnp.int32(0)
        step = N_EXPERTS // 2
        while step:
            e = jnp.where(t >= tile_end[e + step - 1], e + step, e)
            step //= 2
        return e, 0, 0

    return pl.pallas_call(
        _expert_kernel,
        grid_spec=pltpu.PrefetchScalarGridSpec(
            num_scalar_prefetch=1,
            grid=(N_MOE_TILES,),
            in_specs=[
                pl.BlockSpec((MOE_TILE * ROW_SLABS, 128), tile),
                pl.BlockSpec((1, D_MODEL, D_EXPERT), wsel),
                pl.BlockSpec((1, D_MODEL, D_EXPERT), wsel),
                pl.BlockSpec((1, D_EXPERT, D_MODEL), wsel),
            ],
            out_specs=pl.BlockSpec((MOE_TILE * ROW_SLABS, 128), tile),
        ),
        out_shape=jax.ShapeDtypeStruct((N_SLOTS * ROW_SLABS, 128), F32),
        compiler_params=_vmem(40 << 20),
        name="moe_experts",
    )(tile_end, xs, w1, w3, w2)


def _combine_kernel(pos0_ref, pos1_ref, ys_ref, h1_ref, w0_ref, w1_ref, g_ref, b_ref, o_ref,
                    y0_buf, y1_buf, sems):
    i = pl.program_id(0)
    n = pl.num_programs(0)

    def gather(tile, slot):
        base = tile * COMBINE_TILE

        def issue(t, carry):
            pltpu.make_async_copy(_row_tile(ys_ref, pos0_ref[base + t]), _row_tile(y0_buf.at[slot], t),
                                  sems.at[slot]).start(priority=0)
            pltpu.make_async_copy(_row_tile(ys_ref, pos1_ref[base + t]), _row_tile(y1_buf.at[slot], t),
                                  sems.at[slot]).start(priority=1)
            return carry

        lax.fori_loop(0, COMBINE_TILE, issue, 0, unroll=8)

    @pl.when(i == 0)
    def _():
        gather(0, 0)

    @pl.when(i + 1 < n)
    def _():
        gather(i + 1, (i + 1) % 2)

    slot = i % 2
    whole = ys_ref.at[pl.ds(0, COMBINE_TILE * ROW_SLABS), :]
    pltpu.make_async_copy(whole, y0_buf.at[slot], sems.at[slot]).wait()
    pltpu.make_async_copy(whole, y1_buf.at[slot], sems.at[slot]).wait()
    w0 = jnp.concatenate([w0_ref[...]] * ROW_SLABS, axis=1)
    w1 = jnp.concatenate([w1_ref[...]] * ROW_SLABS, axis=1)
    f = (w0 * _load_row_tiles(y0_buf.at[slot], COMBINE_TILE)
         + w1 * _load_row_tiles(y1_buf.at[slot], COMBINE_TILE))
    h1 = _load_row_tiles(h1_ref, COMBINE_TILE)
    o_ref[...] = _layer_norm(DEEPNORM_ALPHA * h1 + f, g_ref[...], b_ref[...])


def _combine(pos0, pos1, ys, h1_rows, w0, w1, ln_g, ln_b):
    row = lambda i, p0, p1: (i, 0)
    const = lambda i, p0, p1: (0, 0)
    return pl.pallas_call(
        _combine_kernel,
        grid_spec=pltpu.PrefetchScalarGridSpec(
            num_scalar_prefetch=2,
            grid=(N_TOK // COMBINE_TILE,),
            in_specs=[
                pl.BlockSpec(memory_space=pl.ANY),
                pl.BlockSpec((COMBINE_TILE * ROW_SLABS, 128), row),
                pl.BlockSpec((COMBINE_TILE, 128), row),
                pl.BlockSpec((COMBINE_TILE, 128), row),
                pl.BlockSpec((1, D_MODEL), const),
                pl.BlockSpec((1, D_MODEL), const),
            ],
            out_specs=pl.BlockSpec((COMBINE_TILE, D_MODEL), row),
            scratch_shapes=[pltpu.VMEM((2, COMBINE_TILE * ROW_SLABS, 128), F32),
                            pltpu.VMEM((2, COMBINE_TILE * ROW_SLABS, 128), F32),
                            pltpu.SemaphoreType.DMA((2,))],
        ),
        out_shape=jax.ShapeDtypeStruct((N_TOK, D_MODEL), F32),
        compiler_params=pltpu.CompilerParams(dimension_semantics=("arbitrary",), vmem_limit_bytes=40 << 20),
        name="moe_combine",
    )(pos0, pos1, ys, h1_rows, w0, w1, ln_g, ln_b)


def kernel(x, mem, ln0_g, ln0_b, rel_bias, w_in, w_mem_kv, w_fourier, lambda_q1, lambda_k1, lambda_q2,
           lambda_k2, subln_g, w_out, ln1_g, ln1_b, w_group, b_group, w_router, b_router, w1, w3, w2,
           ln2_g, ln2_b):
    l = 0
    x2 = x.reshape(N_TOK, D_MODEL)
    wi = w_in[l]
    w_qk = wi[:, :2 * DIFF_WIDTH].astype(BF16)
    w_vt = wi[:, 2 * DIFF_WIDTH:3 * DIFF_WIDTH].T.astype(BF16)
    w_fm = wi[:, 3 * DIFF_WIDTH:].astype(BF16)
    h, q, k, vt, f_in, mq = _ln0_inproj(x2, ln0_g.reshape(1, -1), ln0_b.reshape(1, -1), w_qk, w_vt, w_fm)

    T = ATT_TILE
    kk = jnp.arange(T, dtype=jnp.int32)[:, None]
    qq = jnp.arange(ATT_Q, dtype=jnp.int32)[None, :]
    buckets = jnp.stack([_t5_bucket(d * T + kk - qq) for d in range(-2, ATT_Q // T + 2)])
    lam_vecs = jnp.stack([lambda_q1[l], lambda_k1[l], lambda_q2[l], lambda_k2[l]]).astype(F32)
    o_diff = _diff_attention(rel_bias.astype(F32), q, k, vt, buckets, lam_vecs,
                             subln_g[l].astype(F32).reshape(DIFF_VDIM, 1))

    bdc, bds, m1, m2, twc, tws = (jnp.asarray(c) for c in _dft_constants())
    wf = w_fourier[l]
    bdw = jnp.zeros((FOURIER_WIDTH, FOURIER_WIDTH), F32)
    for g in range(FOURIER_GROUPS):
        sl = slice(g * FOURIER_GROUP_DIM, (g + 1) * FOURIER_GROUP_DIM)
        bdw = bdw.at[sl, sl].set(wf[g])
    o_four = _fourier(f_in, bdc, bds, bdw.astype(BF16), m1, m2, twc, tws)

    kcat, vcat = _mem_kv(mem.reshape(BATCH * MEM_LEN, D_MODEL), w_mem_kv[l].astype(BF16))
    o_mem = _mem_attention(mq, kcat, vcat)

    w_r = jnp.zeros((D_MODEL, ROUTER_LANES), F32)
    w_r = w_r.at[:, :N_EXPERTS].set(w_router[l].astype(F32))
    w_r = w_r.at[:, GROUP_LANE0:GROUP_LANE0 + N_GROUPS].set(w_group[l].astype(F32))
    wr_hi = w_r.astype(BF16)
    wr_lo = (w_r - wr_hi.astype(F32)).astype(BF16)
    b_r = jnp.zeros((1, ROUTER_LANES), F32)
    b_r = b_r.at[0, :N_EXPERTS].set(b_router[l].astype(F32))
    b_r = b_r.at[0, GROUP_LANE0:GROUP_LANE0 + N_GROUPS].set(b_group[l].astype(F32))
    h1_rows, route, gate0, gate1, counts = _outproj_router(
        o_diff, o_four, o_mem, h, w_out[l].astype(BF16),
        ln1_g[l].reshape(1, -1), ln1_b[l].reshape(1, -1), jnp.concatenate([wr_hi, wr_lo], axis=1), b_r)

    cnt = counts[0, :N_EXPERTS].astype(jnp.int32)
    tiles_per_expert = (cnt + MOE_TILE - 1) // MOE_TILE
    tile_end = jnp.cumsum(tiles_per_expert).astype(jnp.int32)
    row_start = (tile_end - tiles_per_expert) * MOE_TILE
    expert_ids = jnp.arange(N_EXPERTS, dtype=jnp.int32)[None, :]

    def slot_of(expert, rank):
        return jnp.sum(jnp.where(expert[:, None] == expert_ids, row_start[None, :], 0), axis=1) + rank

    pos0 = slot_of(route[0], route[2])
    pos1 = slot_of(route[1], route[3])

    xs = _dispatch(pos0, pos1, h1_rows)
    ys = _experts(tile_end, xs, w1[l], w3[l], w2[l])
    out = _combine(pos0, pos1, ys, h1_rows, gate0, gate1, ln2_g[l].reshape(1, -1), ln2_b[l].reshape(1, -1))
    return out.reshape(BATCH, SEQ, D_MODEL)
```

```python
import functools
import math

import numpy as np
import jax
import jax.numpy as jnp
from jax import lax
from jax.experimental import pallas as pl
from jax.experimental.pallas import tpu as pltpu

F32 = jnp.float32
BF16 = jnp.bfloat16

D_MODEL = 1024
BATCH = 4
SEQ = 4096
N_TOK = BATCH * SEQ
MEM_LEN = 256
MEM_HEADS = 4
MEM_HEAD_DIM = 64
MEM_WIDTH = 256
DIFF_HEADS = 4
DIFF_HEAD_DIM = 64
DIFF_VDIM = 2 * DIFF_HEAD_DIM
VT_ROWS = DIFF_VDIM + 16
DIFF_WIDTH = 512
FOURIER_GROUPS = 4
FOURIER_GROUP_DIM = 64
FOURIER_WIDTH = 256
N_BUCKETS = 32
MAX_DISTANCE = 128
N_GROUPS = 4
EXPERTS_PER_GROUP = 8
N_EXPERTS = 32
D_EXPERT = 256
LN_EPS = 1e-5
DEEPNORM_ALPHA = 2.0 ** 0.25
LAM_INIT = 0.8 - 0.6 * math.exp(-0.3 * 0)
LOG2E = 1.4426950408889634

ROW_TILE = 512
ATT_TILE = 256
ATT_Q = 512
ATT_SUB = 256
MOE_TILE = 512
N_MOE_TILES = 2 * N_TOK // MOE_TILE + N_EXPERTS
N_SLOTS = N_MOE_TILES * MOE_TILE
ROW_SLABS = D_MODEL // 128
DISPATCH_TILE = 512
COMBINE_TILE = 512
FFT_RADIX = 64
FFT_PITCH = 72
ROUTER_LANES = 128
GROUP_LANE0 = 32

_NT = (((1,), (1,)), ((), ()))


def _vmem(nbytes):
    return pltpu.CompilerParams(vmem_limit_bytes=int(nbytes))


def _layer_norm(x, g, b):
    mu = jnp.mean(x, axis=-1, keepdims=True)
    xc = x - mu
    var = jnp.mean(xc * xc, axis=-1, keepdims=True)
    return xc * lax.rsqrt(var + LN_EPS) * g + b


def _load_row_tiles(ref, rows):
    return jnp.concatenate([ref[pl.ds(s, rows, stride=ROW_SLABS), :] for s in range(ROW_SLABS)], axis=1)


def _store_row_tiles(ref, x):
    for s in range(ROW_SLABS):
        ref[pl.ds(s, x.shape[0], stride=ROW_SLABS), :] = x[:, s * 128:(s + 1) * 128]


def _row_tile(ref, t):
    return ref.at[pl.ds(pl.multiple_of(t * ROW_SLABS, ROW_SLABS), ROW_SLABS), :]


def _t5_bucket(rel):
    nb = N_BUCKETS // 2
    max_exact = nb // 2
    ret = (rel > 0).astype(jnp.int32) * nb
    n = jnp.abs(rel)
    nf = jnp.maximum(n, 1).astype(F32)
    large = max_exact + (jnp.log(nf / max_exact) / math.log(MAX_DISTANCE / max_exact)
                         * (nb - max_exact)).astype(jnp.int32)
    large = jnp.minimum(large, nb - 1)
    return ret + jnp.where(n < max_exact, n, large)


def _ln0_inproj_kernel(x_ref, g_ref, b_ref, wqk_ref, wvt_ref, wfm_ref,
                       h_ref, q_ref, k_ref, vt_ref, f_ref, mq_ref):
    h = _layer_norm(x_ref[...], g_ref[...], b_ref[...])
    h_ref[...] = h
    hb = h.astype(BF16)
    qk = jnp.dot(hb, wqk_ref[...], preferred_element_type=F32)
    q_ref[...] = (qk[:, :DIFF_WIDTH] * (DIFF_HEAD_DIM ** -0.5 * LOG2E)).astype(BF16)
    k_ref[...] = qk[:, DIFF_WIDTH:].astype(BF16)
    vt = lax.dot_general(wvt_ref[...], hb, _NT, preferred_element_type=F32)
    ones = jnp.ones((VT_ROWS - DIFF_VDIM, ROW_TILE), BF16)
    for hh in range(DIFF_HEADS):
        vt_ref[0, hh, :DIFF_VDIM, :] = vt[hh * DIFF_VDIM:(hh + 1) * DIFF_VDIM, :].astype(BF16)
        vt_ref[0, hh, DIFF_VDIM:, :] = ones
    fm = jnp.dot(hb, wfm_ref[...], preferred_element_type=F32)
    f_ref[...] = fm[:, :FOURIER_WIDTH].astype(BF16)
    mq_ref[...] = (fm[:, FOURIER_WIDTH:] * MEM_HEAD_DIM ** -0.5).astype(BF16)


def _ln0_inproj(x2, ln_g, ln_b, w_qk, w_vt, w_fm):
    rt = SEQ // ROW_TILE
    row = lambda r: (r, 0)
    const = lambda r: (0, 0)
    return pl.pallas_call(
        _ln0_inproj_kernel,
        grid=(N_TOK // ROW_TILE,),
        in_specs=[
            pl.BlockSpec((ROW_TILE, D_MODEL), row),
            pl.BlockSpec((1, D_MODEL), const),
            pl.BlockSpec((1, D_MODEL), const),
            pl.BlockSpec((D_MODEL, 2 * DIFF_WIDTH), const),
            pl.BlockSpec((DIFF_WIDTH, D_MODEL), const),
            pl.BlockSpec((D_MODEL, FOURIER_WIDTH + MEM_WIDTH), const),
        ],
        out_specs=[
            pl.BlockSpec((ROW_TILE, D_MODEL), row),
            pl.BlockSpec((ROW_TILE, DIFF_WIDTH), row),
            pl.BlockSpec((ROW_TILE, DIFF_WIDTH), row),
            pl.BlockSpec((1, DIFF_HEADS, VT_ROWS, ROW_TILE), lambda r: (r // rt, 0, 0, r % rt)),
            pl.BlockSpec((ROW_TILE, FOURIER_WIDTH), row),
            pl.BlockSpec((ROW_TILE, MEM_WIDTH), row),
        ],
        out_shape=[
            jax.ShapeDtypeStruct((N_TOK, D_MODEL), F32),
            jax.ShapeDtypeStruct((N_TOK, DIFF_WIDTH), BF16),
            jax.ShapeDtypeStruct((N_TOK, DIFF_WIDTH), BF16),
            jax.ShapeDtypeStruct((BATCH, DIFF_HEADS, VT_ROWS, SEQ), BF16),
            jax.ShapeDtypeStruct((N_TOK, FOURIER_WIDTH), BF16),
            jax.ShapeDtypeStruct((N_TOK, MEM_WIDTH), BF16),
        ],
        compiler_params=_vmem(48 << 20),
        name="ln0_inproj",
    )(x2, ln_g, ln_b, w_qk, w_vt, w_fm)


def _diff_attn_kernel(tbl_ref, q_ref, k_ref, vt_ref, bkt_ref, lam_ref, g_ref, o_ref,
                      bias_scr, p0_scr, p1_scr):
    T = ATT_TILE
    TQ = ATT_Q
    r = TQ // T
    h = pl.program_id(0)
    i = pl.program_id(2)
    p_scr = (p0_scr, p1_scr)

    @pl.when((pl.program_id(1) == 0) & (i == 0))
    def _build_bias():
        for d in range(r + 4):
            far = d in (0, r + 3)
            bk = bkt_ref[d, 0:8, :] if far else bkt_ref[d]
            bias = jnp.zeros(bk.shape, F32)
            for n in range(N_BUCKETS):
                bias = jnp.where(bk == n, tbl_ref[n, h], bias)
            bias_scr[d] = (jnp.broadcast_to(bias[0:1, :], (T, TQ)) if far else bias) * LOG2E

    q = q_ref[...]
    lane = lax.broadcasted_iota(jnp.int32, q.shape, 1)
    zero = jnp.zeros_like(q)
    q_comp = (jnp.where(lane < DIFF_HEAD_DIM, q, zero), jnp.where(lane >= DIFF_HEAD_DIM, q, zero))

    def bias_tile(j):
        return bias_scr[jnp.clip(j - r * i, -2, r + 1) + 2]

    SUB = ATT_SUB
    n_sub = SEQ // SUB
    m_tile = [[None] * n_sub for _ in range(2)]
    for c in range(2):
        s = lax.dot_general(k_ref[...], q_comp[c], _NT, preferred_element_type=F32)
        for u in range(n_sub):
            if SUB >= T:
                bias = jnp.concatenate([bias_tile(u * SUB // T + t) for t in range(SUB // T)], axis=0)
            else:
                bias = bias_tile(u * SUB // T)[(u * SUB) % T:(u * SUB) % T + SUB, :]
            sb = s[u * SUB:(u + 1) * SUB, :] + bias
            mu = jnp.max(sb, axis=0, keepdims=True)
            p_scr[c][u * SUB:(u + 1) * SUB, :] = jnp.exp2((sb - mu).astype(BF16))
            m_tile[c][u] = mu

    acc = []
    for c in range(2):
        m = functools.reduce(jnp.maximum, m_tile[c])
        a = None
        for u in range(n_sub):
            pv = jnp.dot(vt_ref[0, 0, :, u * SUB:(u + 1) * SUB], p_scr[c][u * SUB:(u + 1) * SUB, :],
                         preferred_element_type=F32)
            pv = pv * jnp.exp2(m_tile[c][u] - m)
            a = pv if a is None else a + pv
        acc.append(a)
    num = [a[:DIFF_VDIM] for a in acc]
    den = [a[DIFF_VDIM:DIFF_VDIM + 1] for a in acc]

    lam = (jnp.exp(jnp.sum(lam_ref[0:1, :] * lam_ref[1:2, :], axis=1, keepdims=True))
           - jnp.exp(jnp.sum(lam_ref[2:3, :] * lam_ref[3:4, :], axis=1, keepdims=True)) + LAM_INIT)
    o = num[0] * (1.0 / den[0]) - lam * (num[1] * (1.0 / den[1]))
    ms = jnp.mean(o * o, axis=0, keepdims=True)
    o = o * lax.rsqrt(ms + LN_EPS) * g_ref[...] * (1.0 - LAM_INIT)
    o_ref[...] = o.T.astype(BF16)


def _diff_attention(rel_bias, q, k, vt, buckets, lam_vecs, subln_col):
    T = ATT_TILE
    TQ = ATT_Q
    nq = SEQ // TQ
    n_bias = TQ // T + 4
    return pl.pallas_call(
        _diff_attn_kernel,
        grid=(DIFF_HEADS, BATCH, nq),
        in_specs=[
            pl.BlockSpec(memory_space=pltpu.SMEM),
            pl.BlockSpec((TQ, DIFF_VDIM), lambda h, b, i: (b * nq + i, h)),
            pl.BlockSpec((SEQ, DIFF_VDIM), lambda h, b, i: (b, h)),
            pl.BlockSpec((1, 1, VT_ROWS, SEQ), lambda h, b, i: (b, h, 0, 0)),
            pl.BlockSpec((n_bias, T, TQ), lambda h, b, i: (0, 0, 0)),
            pl.BlockSpec((4, DIFF_HEAD_DIM), lambda h, b, i: (0, 0)),
            pl.BlockSpec((DIFF_VDIM, 1), lambda h, b, i: (0, 0)),
        ],
        out_specs=pl.BlockSpec((TQ, DIFF_VDIM), lambda h, b, i: (b * nq + i, h)),
        out_shape=jax.ShapeDtypeStruct((N_TOK, DIFF_WIDTH), BF16),
        scratch_shapes=[pltpu.VMEM((n_bias, T, TQ), F32),
                        pltpu.VMEM((SEQ, TQ), BF16), pltpu.VMEM((SEQ, TQ), BF16)],
        compiler_params=_vmem(40 << 20),
        name="diff_attn",
    )(rel_bias, q, k, vt, buckets, lam_vecs, subln_col)


def _fourier_kernel(f_ref, bdc_ref, bds_ref, bdw_ref, m1_ref, m2_ref, twc_ref, tws_ref, o_ref,
                    zr_scr, zi_scr, d_scr, a_scr):
    R = FFT_RADIX
    W = FOURIER_WIDTH
    u = f_ref[...]
    w = bdw_ref[...]
    pc = jnp.dot(u, bdc_ref[...], preferred_element_type=F32).astype(BF16)
    ps = jnp.dot(u, bds_ref[...], preferred_element_type=F32).astype(BF16)
    zr = jnp.dot(pc, w, preferred_element_type=F32)
    zi = jnp.dot(ps, w, preferred_element_type=F32)
    P = FFT_PITCH
    halves = range(W // 128)
    for hh in halves:
        for n1 in range(R):
            zr_scr[hh, n1 * P:n1 * P + R, :] = zr[n1 * R:(n1 + 1) * R, hh * 128:(hh + 1) * 128]
            zi_scr[hh, n1 * P:n1 * P + R, :] = zi[n1 * R:(n1 + 1) * R, hh * 128:(hh + 1) * 128]

    for n2 in range(R):
        for hh in halves:
            cols = slice(n2 * W + hh * 128, n2 * W + (hh + 1) * 128)
            d_scr[0:R, cols] = zr_scr[hh, pl.ds(n2, R, stride=P), :].astype(BF16)
            d_scr[R:2 * R, cols] = zi_scr[hh, pl.ds(n2, R, stride=P), :].astype(BF16)
    a_scr[...] = jnp.dot(m1_ref[...], d_scr[...], preferred_element_type=F32)

    for n2 in range(R):
        tc = twc_ref[n2]
        ts = tws_ref[n2]
        for hh in halves:
            cols = slice(n2 * W + hh * 128, n2 * W + (hh + 1) * 128)
            ar = a_scr[0:R, cols]
            ai = a_scr[R:2 * R, cols]
            zr_scr[hh, pl.ds(n2, R, stride=P), :] = ar * tc + ai * ts
            zi_scr[hh, pl.ds(n2, R, stride=P), :] = ai * tc - ar * ts
    for k1 in range(R):
        for hh in halves:
            cols = slice(k1 * W + hh * 128, k1 * W + (hh + 1) * 128)
            d_scr[0:R, cols] = zr_scr[hh, k1 * P:k1 * P + R, :].astype(BF16)
            d_scr[R:2 * R, cols] = zi_scr[hh, k1 * P:k1 * P + R, :].astype(BF16)
    a_scr[0:R, :] = jnp.dot(m2_ref[...], d_scr[...], preferred_element_type=F32)
    for k1 in range(R):
        for hh in halves:
            zr_scr[hh, pl.ds(k1, R, stride=P), :] = a_scr[0:R, k1 * W + hh * 128:k1 * W + (hh + 1) * 128]
    for k2 in range(R):
        for hh in halves:
            o_ref[hh, k2 * R:(k2 + 1) * R, :] = zr_scr[hh, k2 * P:k2 * P + R, :]


def _fourier(f_in, bdc, bds, bdw, m1, m2, twc, tws):
    R = FFT_RADIX
    W = FOURIER_WIDTH
    const2 = lambda b: (0, 0)
    const3 = lambda b: (0, 0, 0)
    return pl.pallas_call(
        _fourier_kernel,
        grid=(BATCH,),
        in_specs=[
            pl.BlockSpec((SEQ, W), lambda b: (b, 0)),
            pl.BlockSpec((W, W), const2),
            pl.BlockSpec((W, W), const2),
            pl.BlockSpec((W, W), const2),
            pl.BlockSpec((2 * R, 2 * R), const2),
            pl.BlockSpec((R, 2 * R), const2),
            pl.BlockSpec((R, R, 128), const3),
            pl.BlockSpec((R, R, 128), const3),
        ],
        out_specs=pl.BlockSpec((W // 128, SEQ, 128), lambda b: (0, b, 0)),
        out_shape=jax.ShapeDtypeStruct((W // 128, N_TOK, 128), F32),
        scratch_shapes=[pltpu.VMEM((W // 128, R * FFT_PITCH, 128), F32),
                        pltpu.VMEM((W // 128, R * FFT_PITCH, 128), F32),
                        pltpu.VMEM((2 * R, R * W), BF16), pltpu.VMEM((2 * R, R * W), F32)],
        compiler_params=_vmem(52 << 20),
        name="fourier",
    )(f_in, bdc, bds, bdw, m1, m2, twc, tws)


@functools.lru_cache(maxsize=None)
def _dft_constants():
    R = FFT_RADIX
    a = np.arange(R, dtype=np.int64)
    ang = 2.0 * np.pi * ((a[:, None] * a[None, :]) % R).astype(np.float64) / R
    c64, s64 = np.cos(ang), np.sin(ang)
    scale = 1.0 / math.sqrt(SEQ * FOURIER_GROUP_DIM)
    m1 = np.block([[c64, s64], [-s64, c64]])
    m2 = np.concatenate([c64, s64], axis=1) * scale
    tang = 2.0 * np.pi * (a[:, None] * a[None, :]).astype(np.float64) / SEQ
    twc = np.repeat(np.cos(tang)[:, :, None], 128, axis=2).astype(np.float32)
    tws = np.repeat(np.sin(tang)[:, :, None], 128, axis=2).astype(np.float32)
    assert FOURIER_GROUP_DIM == R
    eye = np.eye(FOURIER_GROUPS)
    bdc = np.kron(eye, c64)
    bds = np.kron(eye, -s64)
    return bdc.astype(BF16), bds.astype(BF16), m1.astype(BF16), m2.astype(BF16), twc, tws


def _mem_kv_kernel(mem_ref, w_ref, kcat_ref, vcat_ref):
    kv = jnp.dot(mem_ref[...].astype(BF16), w_ref[...], preferred_element_type=F32)
    mk = kv[:, :MEM_WIDTH]
    mv = kv[:, MEM_WIDTH:]
    lane = lax.broadcasted_iota(jnp.int32, mk.shape, 1)
    for hh in range(MEM_HEADS):
        sel = (lane >= hh * MEM_HEAD_DIM) & (lane < (hh + 1) * MEM_HEAD_DIM)
        kcat_ref[0, hh * MEM_LEN:(hh + 1) * MEM_LEN, :] = jnp.where(sel, mk, 0.0).astype(BF16)
        vcat_ref[0, hh * MEM_LEN:(hh + 1) * MEM_LEN, :] = jnp.where(sel, mv, 0.0).astype(BF16)


def _mem_kv(mem2, w_kv):
    return pl.pallas_call(
        _mem_kv_kernel,
        grid=(BATCH,),
        in_specs=[
            pl.BlockSpec((MEM_LEN, D_MODEL), lambda b: (b, 0)),
            pl.BlockSpec((D_MODEL, 2 * MEM_WIDTH), lambda b: (0, 0)),
        ],
        out_specs=[
            pl.BlockSpec((1, MEM_HEADS * MEM_LEN, MEM_WIDTH), lambda b: (b, 0, 0)),
            pl.BlockSpec((1, MEM_HEADS * MEM_LEN, MEM_WIDTH), lambda b: (b, 0, 0)),
        ],
        out_shape=[
            jax.ShapeDtypeStruct((BATCH, MEM_HEADS * MEM_LEN, MEM_WIDTH), BF16),
            jax.ShapeDtypeStruct((BATCH, MEM_HEADS * MEM_LEN, MEM_WIDTH), BF16),
        ],
        name="mem_kv",
    )(mem2, w_kv)


def _mem_attn_kernel(mq_ref, kcat_ref, vcat_ref, o_ref):
    s = lax.dot_general(mq_ref[...], kcat_ref[0], _NT, preferred_element_type=F32)
    parts = []
    for hh in range(MEM_HEADS):
        sh = s[:, hh * MEM_LEN:(hh + 1) * MEM_LEN]
        p = jnp.exp(sh - jnp.max(sh, axis=1, keepdims=True))
        parts.append((p * (1.0 / jnp.sum(p, axis=1, keepdims=True))).astype(BF16))
    p_all = jnp.concatenate(parts, axis=1)
    o_ref[...] = jnp.dot(p_all, vcat_ref[0], preferred_element_type=F32).astype(BF16)


def _mem_attention(mq, kcat, vcat):
    rt = SEQ // ROW_TILE
    return pl.pallas_call(
        _mem_attn_kernel,
        grid=(N_TOK // ROW_TILE,),
        in_specs=[
            pl.BlockSpec((ROW_TILE, MEM_WIDTH), lambda r: (r, 0)),
            pl.BlockSpec((1, MEM_HEADS * MEM_LEN, MEM_WIDTH), lambda r: (r // rt, 0, 0)),
            pl.BlockSpec((1, MEM_HEADS * MEM_LEN, MEM_WIDTH), lambda r: (r // rt, 0, 0)),
        ],
        out_specs=pl.BlockSpec((ROW_TILE, MEM_WIDTH), lambda r: (r, 0)),
        out_shape=jax.ShapeDtypeStruct((N_TOK, MEM_WIDTH), BF16),
        name="mem_attn",
    )(mq, kcat, vcat)


def _route(logits):
    lane = lax.broadcasted_iota(jnp.int32, logits.shape, 1)
    big = jnp.int32(ROUTER_LANES)
    is_group = (lane >= GROUP_LANE0) & (lane < GROUP_LANE0 + N_GROUPS)
    gl = jnp.where(is_group, logits, -jnp.inf)
    gmax = jnp.max(gl, axis=1, keepdims=True)
    g_sel = jnp.min(jnp.where(gl == gmax, lane, big), axis=1, keepdims=True) - GROUP_LANE0
    g_gate = 1.0 / jnp.sum(jnp.where(is_group, jnp.exp(gl - gmax), 0.0), axis=1, keepdims=True)
    in_group = (lane >= g_sel * EXPERTS_PER_GROUP) & (lane < (g_sel + 1) * EXPERTS_PER_GROUP)
    el = jnp.where(in_group, logits, -jnp.inf)
    v1 = jnp.max(el, axis=1, keepdims=True)
    i1 = jnp.min(jnp.where(el == v1, lane, big), axis=1, keepdims=True)
    el2 = jnp.where(lane == i1, -jnp.inf, el)
    v2 = jnp.max(el2, axis=1, keepdims=True)
    i2 = jnp.min(jnp.where(el2 == v2, lane, big), axis=1, keepdims=True)
    e = jnp.exp(v2 - v1)
    w_first = g_gate / (1.0 + e)
    w_second = g_gate * e / (1.0 + e)
    return i1, i2, w_first, w_second


def _outproj_router_kernel(od_ref, of_ref, om_ref, h_ref, wo_ref, g_ref, b_ref,
                           wr_ref, br_ref,
                           h1_ref, route_ref, w0_ref, w1_ref, cnt_ref, cnt_scr):
    @pl.when(pl.program_id(0) == 0)
    def _():
        cnt_scr[...] = jnp.zeros_like(cnt_scr)

    o_four = [of_ref[hh].astype(BF16) for hh in range(FOURIER_WIDTH // 128)]
    o = jnp.concatenate([od_ref[...]] + o_four + [om_ref[...]], axis=1)
    a = jnp.dot(o, wo_ref[...], preferred_element_type=F32)
    h1 = _layer_norm(DEEPNORM_ALPHA * h_ref[...] + a, g_ref[...], b_ref[...])
    _store_row_tiles(h1_ref, h1)
    hi = h1.astype(BF16)
    lo = (h1 - hi.astype(F32)).astype(BF16)
    hw = jnp.dot(hi, wr_ref[...], preferred_element_type=F32)
    logits = (hw[:, :ROUTER_LANES] + hw[:, ROUTER_LANES:]
              + jnp.dot(lo, wr_ref[:, :ROUTER_LANES], preferred_element_type=F32) + br_ref[...])
    i1, i2, w_first, w_second = _route(logits)

    lane = lax.broadcasted_iota(jnp.int32, logits.shape, 1)
    onehot = jnp.where(lane == i1, 1.0, jnp.where(lane == i2, 1.0, 0.0))
    r_id = lax.broadcasted_iota(jnp.int32, (ROW_TILE, ROW_TILE), 0)
    c_id = lax.broadcasted_iota(jnp.int32, (ROW_TILE, ROW_TILE), 1)
    tri = jnp.where(r_id > c_id, 1.0, 0.0).astype(BF16)
    before = jnp.dot(tri, onehot.astype(BF16), preferred_element_type=F32) + cnt_scr[...]
    rank1 = jnp.sum(jnp.where(lane == i1, before, 0.0), axis=1, keepdims=True)
    rank2 = jnp.sum(jnp.where(lane == i2, before, 0.0), axis=1, keepdims=True)
    cnt_scr[...] += jnp.sum(onehot, axis=0, keepdims=True)
    cnt_ref[...] = cnt_scr[...]

    packed = jnp.where(lane == 0, i1.astype(F32),
                       jnp.where(lane == 1, i2.astype(F32),
                                 jnp.where(lane == 2, rank1, jnp.where(lane == 3, rank2, 0.0))))
    route_ref[...] = packed.T[:8, :].astype(jnp.int32)
    w0_ref[...] = jnp.broadcast_to(w_first, (ROW_TILE, 128))
    w1_ref[...] = jnp.broadcast_to(w_second, (ROW_TILE, 128))


def _outproj_router(o_diff, o_four, o_mem, h, w_out, ln_g, ln_b, w_r, b_r):
    row = lambda r: (r, 0)
    const = lambda r: (0, 0)
    return pl.pallas_call(
        _outproj_router_kernel,
        grid=(N_TOK // ROW_TILE,),
        in_specs=[
            pl.BlockSpec((ROW_TILE, DIFF_WIDTH), row),
            pl.BlockSpec((FOURIER_WIDTH // 128, ROW_TILE, 128), lambda r: (0, r, 0)),
            pl.BlockSpec((ROW_TILE, MEM_WIDTH), row),
            pl.BlockSpec((ROW_TILE, D_MODEL), row),
            pl.BlockSpec((D_MODEL, D_MODEL), const),
            pl.BlockSpec((1, D_MODEL), const),
            pl.BlockSpec((1, D_MODEL), const),
            pl.BlockSpec((D_MODEL, 2 * ROUTER_LANES), const),
            pl.BlockSpec((1, ROUTER_LANES), const),
        ],
        out_specs=[
            pl.BlockSpec((ROW_TILE * ROW_SLABS, 128), row),
            pl.BlockSpec((8, ROW_TILE), lambda r: (0, r)),
            pl.BlockSpec((ROW_TILE, 128), row),
            pl.BlockSpec((ROW_TILE, 128), row),
            pl.BlockSpec((1, ROUTER_LANES), const),
        ],
        out_shape=[
            jax.ShapeDtypeStruct((N_TOK * ROW_SLABS, 128), F32),
            jax.ShapeDtypeStruct((8, N_TOK), jnp.int32),
            jax.ShapeDtypeStruct((N_TOK, 128), F32),
            jax.ShapeDtypeStruct((N_TOK, 128), F32),
            jax.ShapeDtypeStruct((1, ROUTER_LANES), F32),
        ],
        scratch_shapes=[pltpu.VMEM((1, ROUTER_LANES), F32)],
        compiler_params=_vmem(40 << 20),
        name="outproj_router",
    )(o_diff, o_four, o_mem, h, w_out, ln_g, ln_b, w_r, b_r)


def _dispatch_kernel(pos0_ref, pos1_ref, x_ref, xs_ref, sem):
    base = pl.program_id(0) * DISPATCH_TILE

    def issue(t, carry):
        src = _row_tile(x_ref, t)
        pltpu.make_async_copy(src, _row_tile(xs_ref, pos0_ref[base + t]), sem).start(priority=0)
        pltpu.make_async_copy(src, _row_tile(xs_ref, pos1_ref[base + t]), sem).start(priority=1)
        return carry

    lax.fori_loop(0, DISPATCH_TILE, issue, 0, unroll=8)
    for _ in range(2):
        pltpu.make_async_copy(x_ref, xs_ref.at[pl.ds(0, DISPATCH_TILE * ROW_SLABS), :], sem).wait()


def _dispatch(pos0, pos1, h1_rows):
    return pl.pallas_call(
        _dispatch_kernel,
        grid_spec=pltpu.PrefetchScalarGridSpec(
            num_scalar_prefetch=2,
            grid=(N_TOK // DISPATCH_TILE,),
            in_specs=[pl.BlockSpec((DISPATCH_TILE * ROW_SLABS, 128), lambda i, p0, p1: (i, 0))],
            out_specs=pl.BlockSpec(memory_space=pl.ANY),
            scratch_shapes=[pltpu.SemaphoreType.DMA(())],
        ),
        out_shape=jax.ShapeDtypeStruct((N_SLOTS * ROW_SLABS, 128), F32),
        name="moe_dispatch",
    )(pos0, pos1, h1_rows)


def _expert_kernel(tile_end_ref, xs_ref, w1_ref, w3_ref, w2_ref, ys_ref):
    @pl.when(pl.program_id(0) < tile_end_ref[N_EXPERTS - 1])
    def _():
        x = _load_row_tiles(xs_ref, MOE_TILE).astype(BF16)
        a = jnp.dot(x, w1_ref[0].astype(BF16), preferred_element_type=F32)
        b = jnp.dot(x, w3_ref[0].astype(BF16), preferred_element_type=F32)
        hid = (a * jax.nn.sigmoid(a) * b).astype(BF16)
        y = jnp.dot(hid, w2_ref[0].astype(BF16), preferred_element_type=F32)
        _store_row_tiles(ys_ref, y)


def _experts(tile_end, xs, w1, w3, w2):
    def tile(t, tile_end):
        return jnp.minimum(t, tile_end[N_EXPERTS - 1] - 1), 0

    def wsel(t, tile_end):
        e = jnp.int32(0)
        step = N_EXPERTS // 2
        while step:
            e = jnp.where(t >= tile_end[e + step - 1], e + step, e)
            step //= 2
        return e, 0, 0

    return pl.pallas_call(
        _expert_kernel,
        grid_spec=pltpu.PrefetchScalarGridSpec(
            num_scalar_prefetch=1,
            grid=(N_MOE_TILES,),
            in_specs=[
                pl.BlockSpec((MOE_TILE * ROW_SLABS, 128), tile),
                pl.BlockSpec((1, D_MODEL, D_EXPERT), wsel),
                pl.BlockSpec((1, D_MODEL, D_EXPERT), wsel),
                pl.BlockSpec((1, D_EXPERT, D_MODEL), wsel),
            ],
            out_specs=pl.BlockSpec((MOE_TILE * ROW_SLABS, 128), tile),
        ),
        out_shape=jax.ShapeDtypeStruct((N_SLOTS * ROW_SLABS, 128), F32),
        compiler_params=_vmem(40 << 20),
        name="moe_experts",
    )(tile_end, xs, w1, w3, w2)


def _combine_kernel(pos0_ref, pos1_ref, ys_ref, h1_ref, w0_ref, w1_ref, g_ref, b_ref, o_ref,
                    y0_buf, y1_buf, sems):
    i = pl.program_id(0)
    n = pl.num_programs(0)

    def gather(tile, slot):
        base = tile * COMBINE_TILE

        def issue(t, carry):
            pltpu.make_async_copy(_row_tile(ys_ref, pos0_ref[base + t]), _row_tile(y0_buf.at[slot], t),
                                  sems.at[slot]).start(priority=0)
            pltpu.make_async_copy(_row_tile(ys_ref, pos1_ref[base + t]), _row_tile(y1_buf.at[slot], t),
                                  sems.at[slot]).start(priority=1)
            return carry

        lax.fori_loop(0, COMBINE_TILE, issue, 0, unroll=8)

    @pl.when(i == 0)
    def _():
        gather(0, 0)

    @pl.when(i + 1 < n)
    def _():
        gather(i + 1, (i + 1) % 2)

    slot = i % 2
    whole = ys_ref.at[pl.ds(0, COMBINE_TILE * ROW_SLABS), :]
    pltpu.make_async_copy(whole, y0_buf.at[slot], sems.at[slot]).wait()
    pltpu.make_async_copy(whole, y1_buf.at[slot], sems.at[slot]).wait()
    w0 = jnp.concatenate([w0_ref[...]] * ROW_SLABS, axis=1)
    w1 = jnp.concatenate([w1_ref[...]] * ROW_SLABS, axis=1)
    f = (w0 * _load_row_tiles(y0_buf.at[slot], COMBINE_TILE)
         + w1 * _load_row_tiles(y1_buf.at[slot], COMBINE_TILE))
    h1 = _load_row_tiles(h1_ref, COMBINE_TILE)
    o_ref[...] = _layer_norm(DEEPNORM_ALPHA * h1 + f, g_ref[...], b_ref[...])


def _combine(pos0, pos1, ys, h1_rows, w0, w1, ln_g, ln_b):
    row = lambda i, p0, p1: (i, 0)
    const = lambda i, p0, p1: (0, 0)
    return pl.pallas_call(
        _combine_kernel,
        grid_spec=pltpu.PrefetchScalarGridSpec(
            num_scalar_prefetch=2,
            grid=(N_TOK // COMBINE_TILE,),
            in_specs=[
                pl.BlockSpec(memory_space=pl.ANY),
                pl.BlockSpec((COMBINE_TILE * ROW_SLABS, 128), row),
                pl.BlockSpec((COMBINE_TILE, 128), row),
                pl.BlockSpec((COMBINE_TILE, 128), row),
                pl.BlockSpec((1, D_MODEL), const),
                pl.BlockSpec((1, D_MODEL), const),
            ],
            out_specs=pl.BlockSpec((COMBINE_TILE, D_MODEL), row),
            scratch_shapes=[pltpu.VMEM((2, COMBINE_TILE * ROW_SLABS, 128), F32),
                            pltpu.VMEM((2, COMBINE_TILE * ROW_SLABS, 128), F32),
                            pltpu.SemaphoreType.DMA((2,))],
        ),
        out_shape=jax.ShapeDtypeStruct((N_TOK, D_MODEL), F32),
        compiler_params=pltpu.CompilerParams(dimension_semantics=("arbitrary",), vmem_limit_bytes=40 << 20),
        name="moe_combine",
    )(pos0, pos1, ys, h1_rows, w0, w1, ln_g, ln_b)


def kernel(x, mem, ln0_g, ln0_b, rel_bias, w_in, w_mem_kv, w_fourier, lambda_q1, lambda_k1, lambda_q2,
           lambda_k2, subln_g, w_out, ln1_g, ln1_b, w_group, b_group, w_router, b_router, w1, w3, w2,
           ln2_g, ln2_b):
    l = 0
    x2 = x.reshape(N_TOK, D_MODEL)
    wi = w_in[l]
    w_qk = wi[:, :2 * DIFF_WIDTH].astype(BF16)
    w_vt = wi[:, 2 * DIFF_WIDTH:3 * DIFF_WIDTH].T.astype(BF16)
    w_fm = wi[:, 3 * DIFF_WIDTH:].astype(BF16)
    h, q, k, vt, f_in, mq = _ln0_inproj(x2, ln0_g.reshape(1, -1), ln0_b.reshape(1, -1), w_qk, w_vt, w_fm)

    T = ATT_TILE
    kk = jnp.arange(T, dtype=jnp.int32)[:, None]
    qq = jnp.arange(ATT_Q, dtype=jnp.int32)[None, :]
    buckets = jnp.stack([_t5_bucket(d * T + kk - qq) for d in range(-2, ATT_Q // T + 2)])
    lam_vecs = jnp.stack([lambda_q1[l], lambda_k1[l], lambda_q2[l], lambda_k2[l]]).astype(F32)
    o_diff = _diff_attention(rel_bias.astype(F32), q, k, vt, buckets, lam_vecs,
                             subln_g[l].astype(F32).reshape(DIFF_VDIM, 1))

    bdc, bds, m1, m2, twc, tws = (jnp.asarray(c) for c in _dft_constants())
    wf = w_fourier[l]
    bdw = jnp.zeros((FOURIER_WIDTH, FOURIER_WIDTH), F32)
    for g in range(FOURIER_GROUPS):
        sl = slice(g * FOURIER_GROUP_DIM, (g + 1) * FOURIER_GROUP_DIM)
        bdw = bdw.at[sl, sl].set(wf[g])
    o_four = _fourier(f_in, bdc, bds, bdw.astype(BF16), m1, m2, twc, tws)

    kcat, vcat = _mem_kv(mem.reshape(BATCH * MEM_LEN, D_MODEL), w_mem_kv[l].astype(BF16))
    o_mem = _mem_attention(mq, kcat, vcat)

    w_r = jnp.zeros((D_MODEL, ROUTER_LANES), F32)
    w_r = w_r.at[:, :N_EXPERTS].set(w_router[l].astype(F32))
    w_r = w_r.at[:, GROUP_LANE0:GROUP_LANE0 + N_GROUPS].set(w_group[l].astype(F32))
    wr_hi = w_r.astype(BF16)
    wr_lo = (w_r - wr_hi.astype(F32)).astype(BF16)
    b_r = jnp.zeros((1, ROUTER_LANES), F32)
    b_r = b_r.at[0, :N_EXPERTS].set(b_router[l].astype(F32))
    b_r = b_r.at[0, GROUP_LANE0:GROUP_LANE0 + N_GROUPS].set(b_group[l].astype(F32))
    h1_rows, route, gate0, gate1, counts = _outproj_router(
        o_diff, o_four, o_mem, h, w_out[l].astype(BF16),
        ln1_g[l].reshape(1, -1), ln1_b[l].reshape(1, -1), jnp.concatenate([wr_hi, wr_lo], axis=1), b_r)

    cnt = counts[0, :N_EXPERTS].astype(jnp.int32)
    tiles_per_expert = (cnt + MOE_TILE - 1) // MOE_TILE
    tile_end = jnp.cumsum(tiles_per_expert).astype(jnp.int32)
    row_start = (tile_end - tiles_per_expert) * MOE_TILE
    expert_ids = jnp.arange(N_EXPERTS, dtype=jnp.int32)[None, :]

    def slot_of(expert, rank):
        return jnp.sum(jnp.where(expert[:, None] == expert_ids, row_start[None, :], 0), axis=1) + rank

    pos0 = slot_of(route[0], route[2])
    pos1 = slot_of(route[1], route[3])

    xs = _dispatch(pos0, pos1, h1_rows)
    ys = _experts(tile_end, xs, w1[l], w3[l], w2[l])
    out = _combine(pos0, pos1, ys, h1_rows, gate0, gate1, ln2_g[l].reshape(1, -1), ln2_b[l].reshape(1, -1))
    return out.reshape(BATCH, SEQ, D_MODEL)
```

```python
import functools
import math

import numpy as np
import jax
import jax.numpy as jnp
from jax import lax
from jax.experimental import pallas as pl
from jax.experimental.pallas import tpu as pltpu

F32 = jnp.float32
BF16 = jnp.bfloat16

D_MODEL = 1024
BATCH = 4
SEQ = 4096
N_TOK = BATCH * SEQ
MEM_LEN = 256
MEM_HEADS = 4
MEM_HEAD_DIM = 64
MEM_WIDTH = 256
DIFF_HEADS = 4
DIFF_HEAD_DIM = 64
DIFF_VDIM = 2 * DIFF_HEAD_DIM
VT_ROWS = DIFF_VDIM + 16
DIFF_WIDTH = 512
FOURIER_GROUPS = 4
FOURIER_GROUP_DIM = 64
FOURIER_WIDTH = 256
N_BUCKETS = 32
MAX_DISTANCE = 128
N_GROUPS = 4
EXPERTS_PER_GROUP = 8
N_EXPERTS = 32
D_EXPERT = 256
LN_EPS = 1e-5
DEEPNORM_ALPHA = 2.0 ** 0.25
LAM_INIT = 0.8 - 0.6 * math.exp(-0.3 * 0)
LOG2E = 1.4426950408889634

ROW_TILE = 512
ATT_TILE = 256
ATT_Q = 512
ATT_SUB = 256
MOE_TILE = 512
N_MOE_TILES = 2 * N_TOK // MOE_TILE + N_EXPERTS
N_SLOTS = N_MOE_TILES * MOE_TILE
ROW_SLABS = D_MODEL // 128
DISPATCH_TILE = 512
COMBINE_TILE = 512
COMBINE_PARTS = 4
FFT_RADIX = 64
FFT_PITCH = 72
ROUTER_LANES = 128
GROUP_LANE0 = 32

_NT = (((1,), (1,)), ((), ()))


def _vmem(nbytes):
    return pltpu.CompilerParams(vmem_limit_bytes=int(nbytes))


def _layer_norm(x, g, b):
    mu = jnp.mean(x, axis=-1, keepdims=True)
    xc = x - mu
    var = jnp.mean(xc * xc, axis=-1, keepdims=True)
    return xc * lax.rsqrt(var + LN_EPS) * g + b


def _load_row_tiles(ref, rows):
    return jnp.concatenate([ref[pl.ds(s, rows, stride=ROW_SLABS), :] for s in range(ROW_SLABS)], axis=1)


def _store_row_tiles(ref, x):
    for s in range(ROW_SLABS):
        ref[pl.ds(s, x.shape[0], stride=ROW_SLABS), :] = x[:, s * 128:(s + 1) * 128]


def _row_tile(ref, t):
    return ref.at[pl.ds(pl.multiple_of(t * ROW_SLABS, ROW_SLABS), ROW_SLABS), :]


def _t5_bucket(rel):
    nb = N_BUCKETS // 2
    max_exact = nb // 2
    ret = (rel > 0).astype(jnp.int32) * nb
    n = jnp.abs(rel)
    nf = jnp.maximum(n, 1).astype(F32)
    large = max_exact + (jnp.log(nf / max_exact) / math.log(MAX_DISTANCE / max_exact)
                         * (nb - max_exact)).astype(jnp.int32)
    large = jnp.minimum(large, nb - 1)
    return ret + jnp.where(n < max_exact, n, large)


def _ln0_inproj_kernel(x_ref, g_ref, b_ref, wqk_ref, wvt_ref, wfm_ref,
                       h_ref, q_ref, k_ref, vt_ref, f_ref, mq_ref):
    h = _layer_norm(x_ref[...], g_ref[...], b_ref[...])
    h_ref[...] = h
    hb = h.astype(BF16)
    qk = jnp.dot(hb, wqk_ref[...], preferred_element_type=F32)
    q_ref[...] = (qk[:, :DIFF_WIDTH] * (DIFF_HEAD_DIM ** -0.5 * LOG2E)).astype(BF16)
    k_ref[...] = qk[:, DIFF_WIDTH:].astype(BF16)
    vt = lax.dot_general(wvt_ref[...], hb, _NT, preferred_element_type=F32)
    ones = jnp.ones((VT_ROWS - DIFF_VDIM, ROW_TILE), BF16)
    for hh in range(DIFF_HEADS):
        vt_ref[0, hh, :DIFF_VDIM, :] = vt[hh * DIFF_VDIM:(hh + 1) * DIFF_VDIM, :].astype(BF16)
        vt_ref[0, hh, DIFF_VDIM:, :] = ones
    fm = jnp.dot(hb, wfm_ref[...], preferred_element_type=F32)
    f_ref[...] = fm[:, :FOURIER_WIDTH].astype(BF16)
    mq_ref[...] = (fm[:, FOURIER_WIDTH:] * MEM_HEAD_DIM ** -0.5).astype(BF16)


def _ln0_inproj(x2, ln_g, ln_b, w_qk, w_vt, w_fm):
    rt = SEQ // ROW_TILE
    row = lambda r: (r, 0)
    const = lambda r: (0, 0)
    return pl.pallas_call(
        _ln0_inproj_kernel,
        grid=(N_TOK // ROW_TILE,),
        in_specs=[
            pl.BlockSpec((ROW_TILE, D_MODEL), row),
            pl.BlockSpec((1, D_MODEL), const),
            pl.BlockSpec((1, D_MODEL), const),
            pl.BlockSpec((D_MODEL, 2 * DIFF_WIDTH), const),
            pl.BlockSpec((DIFF_WIDTH, D_MODEL), const),
            pl.BlockSpec((D_MODEL, FOURIER_WIDTH + MEM_WIDTH), const),
        ],
        out_specs=[
            pl.BlockSpec((ROW_TILE, D_MODEL), row),
            pl.BlockSpec((ROW_TILE, DIFF_WIDTH), row),
            pl.BlockSpec((ROW_TILE, DIFF_WIDTH), row),
            pl.BlockSpec((1, DIFF_HEADS, VT_ROWS, ROW_TILE), lambda r: (r // rt, 0, 0, r % rt)),
            pl.BlockSpec((ROW_TILE, FOURIER_WIDTH), row),
            pl.BlockSpec((ROW_TILE, MEM_WIDTH), row),
        ],
        out_shape=[
            jax.ShapeDtypeStruct((N_TOK, D_MODEL), F32),
            jax.ShapeDtypeStruct((N_TOK, DIFF_WIDTH), BF16),
            jax.ShapeDtypeStruct((N_TOK, DIFF_WIDTH), BF16),
            jax.ShapeDtypeStruct((BATCH, DIFF_HEADS, VT_ROWS, SEQ), BF16),
            jax.ShapeDtypeStruct((N_TOK, FOURIER_WIDTH), BF16),
            jax.ShapeDtypeStruct((N_TOK, MEM_WIDTH), BF16),
        ],
        compiler_params=_vmem(48 << 20),
        name="ln0_inproj",
    )(x2, ln_g, ln_b, w_qk, w_vt, w_fm)


def _diff_attn_kernel(tbl_ref, q_ref, k_ref, vt_ref, bkt_ref, lam_ref, g_ref, o_ref,
                      bias_scr, p0_scr, p1_scr):
    T = ATT_TILE
    TQ = ATT_Q
    r = TQ // T
    h = pl.program_id(0)
    i = pl.program_id(2)
    p_scr = (p0_scr, p1_scr)

    @pl.when((pl.program_id(1) == 0) & (i == 0))
    def _build_bias():
        for d in range(r + 4):
            far = d in (0, r + 3)
            bk = bkt_ref[d, 0:8, :] if far else bkt_ref[d]
            bias = jnp.zeros(bk.shape, F32)
            for n in range(N_BUCKETS):
                bias = jnp.where(bk == n, tbl_ref[n, h], bias)
            bias_scr[d] = (jnp.broadcast_to(bias[0:1, :], (T, TQ)) if far else bias) * LOG2E

    q = q_ref[...]
    lane = lax.broadcasted_iota(jnp.int32, q.shape, 1)
    zero = jnp.zeros_like(q)
    q_comp = (jnp.where(lane < DIFF_HEAD_DIM, q, zero), jnp.where(lane >= DIFF_HEAD_DIM, q, zero))

    def bias_tile(j):
        return bias_scr[jnp.clip(j - r * i, -2, r + 1) + 2]

    SUB = ATT_SUB
    n_sub = SEQ // SUB
    m_tile = [[None] * n_sub for _ in range(2)]
    for c in range(2):
        s = lax.dot_general(k_ref[...], q_comp[c], _NT, preferred_element_type=F32)
        for u in range(n_sub):
            if SUB >= T:
                bias = jnp.concatenate([bias_tile(u * SUB // T + t) for t in range(SUB // T)], axis=0)
            else:
                bias = bias_tile(u * SUB // T)[(u * SUB) % T:(u * SUB) % T + SUB, :]
            sb = s[u * SUB:(u + 1) * SUB, :] + bias
            mu = jnp.max(sb, axis=0, keepdims=True)
            p_scr[c][u * SUB:(u + 1) * SUB, :] = jnp.exp2((sb - mu).astype(BF16))
            m_tile[c][u] = mu

    acc = []
    for c in range(2):
        m = functools.reduce(jnp.maximum, m_tile[c])
        a = None
        for u in range(n_sub):
            pv = jnp.dot(vt_ref[0, 0, :, u * SUB:(u + 1) * SUB], p_scr[c][u * SUB:(u + 1) * SUB, :],
                         preferred_element_type=F32)
            pv = pv * jnp.exp2(m_tile[c][u] - m)
            a = pv if a is None else a + pv
        acc.append(a)
    num = [a[:DIFF_VDIM] for a in acc]
    den = [a[DIFF_VDIM:DIFF_VDIM + 1] for a in acc]

    lam = (jnp.exp(jnp.sum(lam_ref[0:1, :] * lam_ref[1:2, :], axis=1, keepdims=True))
           - jnp.exp(jnp.sum(lam_ref[2:3, :] * lam_ref[3:4, :], axis=1, keepdims=True)) + LAM_INIT)
    o = num[0] * (1.0 / den[0]) - lam * (num[1] * (1.0 / den[1]))
    ms = jnp.mean(o * o, axis=0, keepdims=True)
    o = o * lax.rsqrt(ms + LN_EPS) * g_ref[...] * (1.0 - LAM_INIT)
    o_ref[...] = o.T.astype(BF16)


def _diff_attention(rel_bias, q, k, vt, buckets, lam_vecs, subln_col):
    T = ATT_TILE
    TQ = ATT_Q
    nq = SEQ // TQ
    n_bias = TQ // T + 4
    return pl.pallas_call(
        _diff_attn_kernel,
        grid=(DIFF_HEADS, BATCH, nq),
        in_specs=[
            pl.BlockSpec(memory_space=pltpu.SMEM),
            pl.BlockSpec((TQ, DIFF_VDIM), lambda h, b, i: (b * nq + i, h)),
            pl.BlockSpec((SEQ, DIFF_VDIM), lambda h, b, i: (b, h)),
            pl.BlockSpec((1, 1, VT_ROWS, SEQ), lambda h, b, i: (b, h, 0, 0)),
            pl.BlockSpec((n_bias, T, TQ), lambda h, b, i: (0, 0, 0)),
            pl.BlockSpec((4, DIFF_HEAD_DIM), lambda h, b, i: (0, 0)),
            pl.BlockSpec((DIFF_VDIM, 1), lambda h, b, i: (0, 0)),
        ],
        out_specs=pl.BlockSpec((TQ, DIFF_VDIM), lambda h, b, i: (b * nq + i, h)),
        out_shape=jax.ShapeDtypeStruct((N_TOK, DIFF_WIDTH), BF16),
        scratch_shapes=[pltpu.VMEM((n_bias, T, TQ), F32),
                        pltpu.VMEM((SEQ, TQ), BF16), pltpu.VMEM((SEQ, TQ), BF16)],
        compiler_params=_vmem(40 << 20),
        name="diff_attn",
    )(rel_bias, q, k, vt, buckets, lam_vecs, subln_col)


def _fourier_kernel(f_ref, bdc_ref, bds_ref, bdw_ref, m1_ref, m2_ref, twc_ref, tws_ref, o_ref,
                    zr_scr, zi_scr, d_scr, a_scr):
    R = FFT_RADIX
    W = FOURIER_WIDTH
    u = f_ref[...]
    w = bdw_ref[...]
    pc = jnp.dot(u, bdc_ref[...], preferred_element_type=F32).astype(BF16)
    ps = jnp.dot(u, bds_ref[...], preferred_element_type=F32).astype(BF16)
    zr = jnp.dot(pc, w, preferred_element_type=F32)
    zi = jnp.dot(ps, w, preferred_element_type=F32)
    P = FFT_PITCH
    halves = range(W // 128)
    for hh in halves:
        for n1 in range(R):
            zr_scr[hh, n1 * P:n1 * P + R, :] = zr[n1 * R:(n1 + 1) * R, hh * 128:(hh + 1) * 128]
            zi_scr[hh, n1 * P:n1 * P + R, :] = zi[n1 * R:(n1 + 1) * R, hh * 128:(hh + 1) * 128]

    for n2 in range(R):
        for hh in halves:
            cols = slice(n2 * W + hh * 128, n2 * W + (hh + 1) * 128)
            d_scr[0:R, cols] = zr_scr[hh, pl.ds(n2, R, stride=P), :].astype(BF16)
            d_scr[R:2 * R, cols] = zi_scr[hh, pl.ds(n2, R, stride=P), :].astype(BF16)
    a_scr[...] = jnp.dot(m1_ref[...], d_scr[...], preferred_element_type=F32)

    for n2 in range(R):
        tc = twc_ref[n2]
        ts = tws_ref[n2]
        for hh in halves:
            cols = slice(n2 * W + hh * 128, n2 * W + (hh + 1) * 128)
            ar = a_scr[0:R, cols]
            ai = a_scr[R:2 * R, cols]
            zr_scr[hh, pl.ds(n2, R, stride=P), :] = ar * tc + ai * ts
            zi_scr[hh, pl.ds(n2, R, stride=P), :] = ai * tc - ar * ts
    for k1 in range(R):
        for hh in halves:
            cols = slice(k1 * W + hh * 128, k1 * W + (hh + 1) * 128)
            d_scr[0:R, cols] = zr_scr[hh, k1 * P:k1 * P + R, :].astype(BF16)
            d_scr[R:2 * R, cols] = zi_scr[hh, k1 * P:k1 * P + R, :].astype(BF16)
    a_scr[0:R, :] = jnp.dot(m2_ref[...], d_scr[...], preferred_element_type=F32)
    for k1 in range(R):
        for hh in halves:
            zr_scr[hh, pl.ds(k1, R, stride=P), :] = a_scr[0:R, k1 * W + hh * 128:k1 * W + (hh + 1) * 128]
    for k2 in range(R):
        for hh in halves:
            o_ref[hh, k2 * R:(k2 + 1) * R, :] = zr_scr[hh, k2 * P:k2 * P + R, :]


def _fourier(f_in, bdc, bds, bdw, m1, m2, twc, tws):
    R = FFT_RADIX
    W = FOURIER_WIDTH
    const2 = lambda b: (0, 0)
    const3 = lambda b: (0, 0, 0)
    return pl.pallas_call(
        _fourier_kernel,
        grid=(BATCH,),
        in_specs=[
            pl.BlockSpec((SEQ, W), lambda b: (b, 0)),
            pl.BlockSpec((W, W), const2),
            pl.BlockSpec((W, W), const2),
            pl.BlockSpec((W, W), const2),
            pl.BlockSpec((2 * R, 2 * R), const2),
            pl.BlockSpec((R, 2 * R), const2),
            pl.BlockSpec((R, R, 128), const3),
            pl.BlockSpec((R, R, 128), const3),
        ],
        out_specs=pl.BlockSpec((W // 128, SEQ, 128), lambda b: (0, b, 0)),
        out_shape=jax.ShapeDtypeStruct((W // 128, N_TOK, 128), F32),
        scratch_shapes=[pltpu.VMEM((W // 128, R * FFT_PITCH, 128), F32),
                        pltpu.VMEM((W // 128, R * FFT_PITCH, 128), F32),
                        pltpu.VMEM((2 * R, R * W), BF16), pltpu.VMEM((2 * R, R * W), F32)],
        compiler_params=_vmem(52 << 20),
        name="fourier",
    )(f_in, bdc, bds, bdw, m1, m2, twc, tws)


@functools.lru_cache(maxsize=None)
def _dft_constants():
    R = FFT_RADIX
    a = np.arange(R, dtype=np.int64)
    ang = 2.0 * np.pi * ((a[:, None] * a[None, :]) % R).astype(np.float64) / R
    c64, s64 = np.cos(ang), np.sin(ang)
    scale = 1.0 / math.sqrt(SEQ * FOURIER_GROUP_DIM)
    m1 = np.block([[c64, s64], [-s64, c64]])
    m2 = np.concatenate([c64, s64], axis=1) * scale
    tang = 2.0 * np.pi * (a[:, None] * a[None, :]).astype(np.float64) / SEQ
    twc = np.repeat(np.cos(tang)[:, :, None], 128, axis=2).astype(np.float32)
    tws = np.repeat(np.sin(tang)[:, :, None], 128, axis=2).astype(np.float32)
    assert FOURIER_GROUP_DIM == R
    eye = np.eye(FOURIER_GROUPS)
    bdc = np.kron(eye, c64)
    bds = np.kron(eye, -s64)
    return bdc.astype(BF16), bds.astype(BF16), m1.astype(BF16), m2.astype(BF16), twc, tws


def _mem_kv_kernel(mem_ref, w_ref, kcat_ref, vcat_ref):
    kv = jnp.dot(mem_ref[...].astype(BF16), w_ref[...], preferred_element_type=F32)
    mk = kv[:, :MEM_WIDTH]
    mv = kv[:, MEM_WIDTH:]
    lane = lax.broadcasted_iota(jnp.int32, mk.shape, 1)
    for hh in range(MEM_HEADS):
        sel = (lane >= hh * MEM_HEAD_DIM) & (lane < (hh + 1) * MEM_HEAD_DIM)
        kcat_ref[0, hh * MEM_LEN:(hh + 1) * MEM_LEN, :] = jnp.where(sel, mk, 0.0).astype(BF16)
        vcat_ref[0, hh * MEM_LEN:(hh + 1) * MEM_LEN, :] = jnp.where(sel, mv, 0.0).astype(BF16)


def _mem_kv(mem2, w_kv):
    return pl.pallas_call(
        _mem_kv_kernel,
        grid=(BATCH,),
        in_specs=[
            pl.BlockSpec((MEM_LEN, D_MODEL), lambda b: (b, 0)),
            pl.BlockSpec((D_MODEL, 2 * MEM_WIDTH), lambda b: (0, 0)),
        ],
        out_specs=[
            pl.BlockSpec((1, MEM_HEADS * MEM_LEN, MEM_WIDTH), lambda b: (b, 0, 0)),
            pl.BlockSpec((1, MEM_HEADS * MEM_LEN, MEM_WIDTH), lambda b: (b, 0, 0)),
        ],
        out_shape=[
            jax.ShapeDtypeStruct((BATCH, MEM_HEADS * MEM_LEN, MEM_WIDTH), BF16),
            jax.ShapeDtypeStruct((BATCH, MEM_HEADS * MEM_LEN, MEM_WIDTH), BF16),
        ],
        name="mem_kv",
    )(mem2, w_kv)


def _mem_attn_kernel(mq_ref, kcat_ref, vcat_ref, o_ref):
    s = lax.dot_general(mq_ref[...], kcat_ref[0], _NT, preferred_element_type=F32)
    parts = []
    for hh in range(MEM_HEADS):
        sh = s[:, hh * MEM_LEN:(hh + 1) * MEM_LEN]
        p = jnp.exp(sh - jnp.max(sh, axis=1, keepdims=True))
        parts.append((p * (1.0 / jnp.sum(p, axis=1, keepdims=True))).astype(BF16))
    p_all = jnp.concatenate(parts, axis=1)
    o_ref[...] = jnp.dot(p_all, vcat_ref[0], preferred_element_type=F32).astype(BF16)


def _mem_attention(mq, kcat, vcat):
    rt = SEQ // ROW_TILE
    return pl.pallas_call(
        _mem_attn_kernel,
        grid=(N_TOK // ROW_TILE,),
        in_specs=[
            pl.BlockSpec((ROW_TILE, MEM_WIDTH), lambda r: (r, 0)),
            pl.BlockSpec((1, MEM_HEADS * MEM_LEN, MEM_WIDTH), lambda r: (r // rt, 0, 0)),
            pl.BlockSpec((1, MEM_HEADS * MEM_LEN, MEM_WIDTH), lambda r: (r // rt, 0, 0)),
        ],
        out_specs=pl.BlockSpec((ROW_TILE, MEM_WIDTH), lambda r: (r, 0)),
        out_shape=jax.ShapeDtypeStruct((N_TOK, MEM_WIDTH), BF16),
        name="mem_attn",
    )(mq, kcat, vcat)


def _route(logits):
    lane = lax.broadcasted_iota(jnp.int32, logits.shape, 1)
    big = jnp.int32(ROUTER_LANES)
    is_group = (lane >= GROUP_LANE0) & (lane < GROUP_LANE0 + N_GROUPS)
    gl = jnp.where(is_group, logits, -jnp.inf)
    gmax = jnp.max(gl, axis=1, keepdims=True)
    g_sel = jnp.min(jnp.where(gl == gmax, lane, big), axis=1, keepdims=True) - GROUP_LANE0
    g_gate = 1.0 / jnp.sum(jnp.where(is_group, jnp.exp(gl - gmax), 0.0), axis=1, keepdims=True)
    in_group = (lane >= g_sel * EXPERTS_PER_GROUP) & (lane < (g_sel + 1) * EXPERTS_PER_GROUP)
    el = jnp.where(in_group, logits, -jnp.inf)
    v1 = jnp.max(el, axis=1, keepdims=True)
    i1 = jnp.min(jnp.where(el == v1, lane, big), axis=1, keepdims=True)
    el2 = jnp.where(lane == i1, -jnp.inf, el)
    v2 = jnp.max(el2, axis=1, keepdims=True)
    i2 = jnp.min(jnp.where(el2 == v2, lane, big), axis=1, keepdims=True)
    e = jnp.exp(v2 - v1)
    w_first = g_gate / (1.0 + e)
    w_second = g_gate * e / (1.0 + e)
    return i1, i2, w_first, w_second


def _outproj_router_kernel(od_ref, of_ref, om_ref, h_ref, wo_ref, g_ref, b_ref,
                           wr_ref, br_ref,
                           h1_ref, route_ref, w0_ref, w1_ref, cnt_ref, cnt_scr):
    @pl.when(pl.program_id(0) == 0)
    def _():
        cnt_scr[...] = jnp.zeros_like(cnt_scr)

    o_four = [of_ref[hh].astype(BF16) for hh in range(FOURIER_WIDTH // 128)]
    o = jnp.concatenate([od_ref[...]] + o_four + [om_ref[...]], axis=1)
    a = jnp.dot(o, wo_ref[...], preferred_element_type=F32)
    h1 = _layer_norm(DEEPNORM_ALPHA * h_ref[...] + a, g_ref[...], b_ref[...])
    _store_row_tiles(h1_ref, h1)
    hi = h1.astype(BF16)
    lo = (h1 - hi.astype(F32)).astype(BF16)
    hw = jnp.dot(hi, wr_ref[...], preferred_element_type=F32)
    logits = (hw[:, :ROUTER_LANES] + hw[:, ROUTER_LANES:]
              + jnp.dot(lo, wr_ref[:, :ROUTER_LANES], preferred_element_type=F32) + br_ref[...])
    i1, i2, w_first, w_second = _route(logits)

    lane = lax.broadcasted_iota(jnp.int32, logits.shape, 1)
    onehot = jnp.where(lane == i1, 1.0, jnp.where(lane == i2, 1.0, 0.0))
    r_id = lax.broadcasted_iota(jnp.int32, (ROW_TILE, ROW_TILE), 0)
    c_id = lax.broadcasted_iota(jnp.int32, (ROW_TILE, ROW_TILE), 1)
    tri = jnp.where(r_id > c_id, 1.0, 0.0).astype(BF16)
    before = jnp.dot(tri, onehot.astype(BF16), preferred_element_type=F32) + cnt_scr[...]
    rank1 = jnp.sum(jnp.where(lane == i1, before, 0.0), axis=1, keepdims=True)
    rank2 = jnp.sum(jnp.where(lane == i2, before, 0.0), axis=1, keepdims=True)
    cnt_scr[...] += jnp.sum(onehot, axis=0, keepdims=True)
    cnt_ref[...] = cnt_scr[...]

    packed = jnp.where(lane == 0, i1.astype(F32),
                       jnp.where(lane == 1, i2.astype(F32),
                                 jnp.where(lane == 2, rank1, jnp.where(lane == 3, rank2, 0.0))))
    route_ref[...] = packed.T[:8, :].astype(jnp.int32)
    w0_ref[...] = jnp.broadcast_to(w_first, (ROW_TILE, 128))
    w1_ref[...] = jnp.broadcast_to(w_second, (ROW_TILE, 128))


def _outproj_router(o_diff, o_four, o_mem, h, w_out, ln_g, ln_b, w_r, b_r):
    row = lambda r: (r, 0)
    const = lambda r: (0, 0)
    return pl.pallas_call(
        _outproj_router_kernel,
        grid=(N_TOK // ROW_TILE,),
        in_specs=[
            pl.BlockSpec((ROW_TILE, DIFF_WIDTH), row),
            pl.BlockSpec((FOURIER_WIDTH // 128, ROW_TILE, 128), lambda r: (0, r, 0)),
            pl.BlockSpec((ROW_TILE, MEM_WIDTH), row),
            pl.BlockSpec((ROW_TILE, D_MODEL), row),
            pl.BlockSpec((D_MODEL, D_MODEL), const),
            pl.BlockSpec((1, D_MODEL), const),
            pl.BlockSpec((1, D_MODEL), const),
            pl.BlockSpec((D_MODEL, 2 * ROUTER_LANES), const),
            pl.BlockSpec((1, ROUTER_LANES), const),
        ],
        out_specs=[
            pl.BlockSpec((ROW_TILE * ROW_SLABS, 128), row),
            pl.BlockSpec((8, ROW_TILE), lambda r: (0, r)),
            pl.BlockSpec((ROW_TILE, 128), row),
            pl.BlockSpec((ROW_TILE, 128), row),
            pl.BlockSpec((1, ROUTER_LANES), const),
        ],
        out_shape=[
            jax.ShapeDtypeStruct((N_TOK * ROW_SLABS, 128), F32),
            jax.ShapeDtypeStruct((8, N_TOK), jnp.int32),
            jax.ShapeDtypeStruct((N_TOK, 128), F32),
            jax.ShapeDtypeStruct((N_TOK, 128), F32),
            jax.ShapeDtypeStruct((1, ROUTER_LANES), F32),
        ],
        scratch_shapes=[pltpu.VMEM((1, ROUTER_LANES), F32)],
        compiler_params=_vmem(40 << 20),
        name="outproj_router",
    )(o_diff, o_four, o_mem, h, w_out, ln_g, ln_b, w_r, b_r)


def _dispatch_kernel(pos0_ref, pos1_ref, x_ref, xs_ref, sem):
    base = pl.program_id(0) * DISPATCH_TILE

    def issue(t, carry):
        src = _row_tile(x_ref, t)
        pltpu.make_async_copy(src, _row_tile(xs_ref, pos0_ref[base + t]), sem).start(priority=0)
        pltpu.make_async_copy(src, _row_tile(xs_ref, pos1_ref[base + t]), sem).start(priority=1)
        return carry

    lax.fori_loop(0, DISPATCH_TILE, issue, 0, unroll=8)
    for _ in range(2):
        pltpu.make_async_copy(x_ref, xs_ref.at[pl.ds(0, DISPATCH_TILE * ROW_SLABS), :], sem).wait()


def _dispatch(pos0, pos1, h1_rows):
    return pl.pallas_call(
        _dispatch_kernel,
        grid_spec=pltpu.PrefetchScalarGridSpec(
            num_scalar_prefetch=2,
            grid=(N_TOK // DISPATCH_TILE,),
            in_specs=[pl.BlockSpec((DISPATCH_TILE * ROW_SLABS, 128), lambda i, p0, p1: (i, 0))],
            out_specs=pl.BlockSpec(memory_space=pl.ANY),
            scratch_shapes=[pltpu.SemaphoreType.DMA(())],
        ),
        out_shape=jax.ShapeDtypeStruct((N_SLOTS * ROW_SLABS, 128), F32),
        name="moe_dispatch",
    )(pos0, pos1, h1_rows)


def _expert_kernel(tile_end_ref, xs_ref, w1_ref, w3_ref, w2_ref, ys_ref):
    @pl.when(pl.program_id(0) < tile_end_ref[N_EXPERTS - 1])
    def _():
        x = _load_row_tiles(xs_ref, MOE_TILE).astype(BF16)
        a = jnp.dot(x, w1_ref[0].astype(BF16), preferred_element_type=F32)
        b = jnp.dot(x, w3_ref[0].astype(BF16), preferred_element_type=F32)
        hid = (a * jax.nn.sigmoid(a) * b).astype(BF16)
        y = jnp.dot(hid, w2_ref[0].astype(BF16), preferred_element_type=F32)
        _store_row_tiles(ys_ref, y)


def _experts(tile_end, xs, w1, w3, w2):
    def tile(t, tile_end):
        return jnp.minimum(t, tile_end[N_EXPERTS - 1] - 1), 0

    def wsel(t, tile_end):
        e = jnp.int32(0)
        step = N_EXPERTS // 2
        while step:
            e = jnp.where(t >= tile_end[e + step - 1], e + step, e)
            step //= 2
        return e, 0, 0

    return pl.pallas_call(
        _expert_kernel,
        grid_spec=pltpu.PrefetchScalarGridSpec(
            num_scalar_prefetch=1,
            grid=(N_MOE_TILES,),
            in_specs=[
                pl.BlockSpec((MOE_TILE * ROW_SLABS, 128), tile),
                pl.BlockSpec((1, D_MODEL, D_EXPERT), wsel),
                pl.BlockSpec((1, D_MODEL, D_EXPERT), wsel),
                pl.BlockSpec((1, D_EXPERT, D_MODEL), wsel),
            ],
            out_specs=pl.BlockSpec((MOE_TILE * ROW_SLABS, 128), tile),
        ),
        out_shape=jax.ShapeDtypeStruct((N_SLOTS * ROW_SLABS, 128), F32),
        compiler_params=_vmem(40 << 20),
        name="moe_experts",
    )(tile_end, xs, w1, w3, w2)


def _combine_kernel(pos0_ref, pos1_ref, ys_ref, h1_ref, w0_ref, w1_ref, g_ref, b_ref, o_ref,
                    y0_buf, y1_buf, sems):
    i = pl.program_id(0)
    n = pl.num_programs(0)

    part = COMBINE_TILE // COMBINE_PARTS

    def gather(tile, slot, first, count):
        base = tile * COMBINE_TILE

        def issue(t, carry):
            pltpu.make_async_copy(_row_tile(ys_ref, pos0_ref[base + t]), _row_tile(y0_buf.at[slot], t),
                                  sems.at[slot]).start(priority=0)
            pltpu.make_async_copy(_row_tile(ys_ref, pos1_ref[base + t]), _row_tile(y1_buf.at[slot], t),
                                  sems.at[slot]).start(priority=1)
            return carry

        lax.fori_loop(first, first + count, issue, 0, unroll=8)

    @pl.when(i == 0)
    def _():
        gather(0, 0, 0, COMBINE_TILE)

    slot = i % 2
    for p in range(COMBINE_PARTS):
        @pl.when(i + 1 < n)
        def _():
            gather(i + 1, 1 - slot, p * part, part)

        if p == 0:
            whole = ys_ref.at[pl.ds(0, COMBINE_TILE * ROW_SLABS), :]
            pltpu.make_async_copy(whole, y0_buf.at[slot], sems.at[slot]).wait()
            pltpu.make_async_copy(whole, y1_buf.at[slot], sems.at[slot]).wait()
        rows = pl.ds(p * part, part)
        tiles = pl.ds(p * part * ROW_SLABS, part * ROW_SLABS)
        w0 = jnp.concatenate([w0_ref[rows, :]] * ROW_SLABS, axis=1)
        w1 = jnp.concatenate([w1_ref[rows, :]] * ROW_SLABS, axis=1)
        f = (w0 * _load_row_tiles(y0_buf.at[slot, tiles], part)
             + w1 * _load_row_tiles(y1_buf.at[slot, tiles], part))
        h1 = _load_row_tiles(h1_ref.at[tiles], part)
        o_ref[rows, :] = _layer_norm(DEEPNORM_ALPHA * h1 + f, g_ref[...], b_ref[...])


def _combine(pos0, pos1, ys, h1_rows, w0, w1, ln_g, ln_b):
    row = lambda i, p0, p1: (i, 0)
    const = lambda i, p0, p1: (0, 0)
    return pl.pallas_call(
        _combine_kernel,
        grid_spec=pltpu.PrefetchScalarGridSpec(
            num_scalar_prefetch=2,
            grid=(N_TOK // COMBINE_TILE,),
            in_specs=[
                pl.BlockSpec(memory_space=pl.ANY),
                pl.BlockSpec((COMBINE_TILE * ROW_SLABS, 128), row),
                pl.BlockSpec((COMBINE_TILE, 128), row),
                pl.BlockSpec((COMBINE_TILE, 128), row),
                pl.BlockSpec((1, D_MODEL), const),
                pl.BlockSpec((1, D_MODEL), const),
            ],
            out_specs=pl.BlockSpec((COMBINE_TILE, D_MODEL), row),
            scratch_shapes=[pltpu.VMEM((2, COMBINE_TILE * ROW_SLABS, 128), F32),
                            pltpu.VMEM((2, COMBINE_TILE * ROW_SLABS, 128), F32),
                            pltpu.SemaphoreType.DMA((2,))],
        ),
        out_shape=jax.ShapeDtypeStruct((N_TOK, D_MODEL), F32),
        compiler_params=pltpu.CompilerParams(dimension_semantics=("arbitrary",), vmem_limit_bytes=40 << 20),
        name="moe_combine",
    )(pos0, pos1, ys, h1_rows, w0, w1, ln_g, ln_b)


def kernel(x, mem, ln0_g, ln0_b, rel_bias, w_in, w_mem_kv, w_fourier, lambda_q1, lambda_k1, lambda_q2,
           lambda_k2, subln_g, w_out, ln1_g, ln1_b, w_group, b_group, w_router, b_router, w1, w3, w2,
           ln2_g, ln2_b):
    l = 0
    x2 = x.reshape(N_TOK, D_MODEL)
    wi = w_in[l]
    w_qk = wi[:, :2 * DIFF_WIDTH].astype(BF16)
    w_vt = wi[:, 2 * DIFF_WIDTH:3 * DIFF_WIDTH].T.astype(BF16)
    w_fm = wi[:, 3 * DIFF_WIDTH:].astype(BF16)
    h, q, k, vt, f_in, mq = _ln0_inproj(x2, ln0_g.reshape(1, -1), ln0_b.reshape(1, -1), w_qk, w_vt, w_fm)

    T = ATT_TILE
    kk = jnp.arange(T, dtype=jnp.int32)[:, None]
    qq = jnp.arange(ATT_Q, dtype=jnp.int32)[None, :]
    buckets = jnp.stack([_t5_bucket(d * T + kk - qq) for d in range(-2, ATT_Q // T + 2)])
    lam_vecs = jnp.stack([lambda_q1[l], lambda_k1[l], lambda_q2[l], lambda_k2[l]]).astype(F32)
    o_diff = _diff_attention(rel_bias.astype(F32), q, k, vt, buckets, lam_vecs,
                             subln_g[l].astype(F32).reshape(DIFF_VDIM, 1))

    bdc, bds, m1, m2, twc, tws = (jnp.asarray(c) for c in _dft_constants())
    wf = w_fourier[l]
    bdw = jnp.zeros((FOURIER_WIDTH, FOURIER_WIDTH), F32)
    for g in range(FOURIER_GROUPS):
        sl = slice(g * FOURIER_GROUP_DIM, (g + 1) * FOURIER_GROUP_DIM)
        bdw = bdw.at[sl, sl].set(wf[g])
    o_four = _fourier(f_in, bdc, bds, bdw.astype(BF16), m1, m2, twc, tws)

    kcat, vcat = _mem_kv(mem.reshape(BATCH * MEM_LEN, D_MODEL), w_mem_kv[l].astype(BF16))
    o_mem = _mem_attention(mq, kcat, vcat)

    w_r = jnp.zeros((D_MODEL, ROUTER_LANES), F32)
    w_r = w_r.at[:, :N_EXPERTS].set(w_router[l].astype(F32))
    w_r = w_r.at[:, GROUP_LANE0:GROUP_LANE0 + N_GROUPS].set(w_group[l].astype(F32))
    wr_hi = w_r.astype(BF16)
    wr_lo = (w_r - wr_hi.astype(F32)).astype(BF16)
    b_r = jnp.zeros((1, ROUTER_LANES), F32)
    b_r = b_r.at[0, :N_EXPERTS].set(b_router[l].astype(F32))
    b_r = b_r.at[0, GROUP_LANE0:GROUP_LANE0 + N_GROUPS].set(b_group[l].astype(F32))
    h1_rows, route, gate0, gate1, counts = _outproj_router(
        o_diff, o_four, o_mem, h, w_out[l].astype(BF16),
        ln1_g[l].reshape(1, -1), ln1_b[l].reshape(1, -1), jnp.concatenate([wr_hi, wr_lo], axis=1), b_r)

    cnt = counts[0, :N_EXPERTS].astype(jnp.int32)
    tiles_per_expert = (cnt + MOE_TILE - 1) // MOE_TILE
    tile_end = jnp.cumsum(tiles_per_expert).astype(jnp.int32)
    row_start = (tile_end - tiles_per_expert) * MOE_TILE
    expert_ids = jnp.arange(N_EXPERTS, dtype=jnp.int32)[None, :]

    def slot_of(expert, rank):
        return jnp.sum(jnp.where(expert[:, None] == expert_ids, row_start[None, :], 0), axis=1) + rank

    pos0 = slot_of(route[0], route[2])
    pos1 = slot_of(route[1], route[3])

    xs = _dispatch(pos0, pos1, h1_rows)
    ys = _experts(tile_end, xs, w1[l], w3[l], w2[l])
    out = _combine(pos0, pos1, ys, h1_rows, gate0, gate1, ln2_g[l].reshape(1, -1), ln2_b[l].reshape(1, -1))
    return out.reshape(BATCH, SEQ, D_MODEL)
```

```python
import functools
import math

import numpy as np
import jax
import jax.numpy as jnp
from jax import lax
from jax.experimental import pallas as pl
from jax.experimental.pallas import tpu as pltpu

F32 = jnp.float32
BF16 = jnp.bfloat16

D_MODEL = 1024
BATCH = 4
SEQ = 4096
N_TOK = BATCH * SEQ
MEM_LEN = 256
MEM_HEADS = 4
MEM_HEAD_DIM = 64
MEM_WIDTH = 256
DIFF_HEADS = 4
DIFF_HEAD_DIM = 64
DIFF_VDIM = 2 * DIFF_HEAD_DIM
VT_ROWS = DIFF_VDIM + 16
DIFF_WIDTH = 512
FOURIER_GROUPS = 4
FOURIER_GROUP_DIM = 64
FOURIER_WIDTH = 256
N_BUCKETS = 32
MAX_DISTANCE = 128
N_GROUPS = 4
EXPERTS_PER_GROUP = 8
N_EXPERTS = 32
D_EXPERT = 256
LN_EPS = 1e-5
DEEPNORM_ALPHA = 2.0 ** 0.25
LAM_INIT = 0.8 - 0.6 * math.exp(-0.3 * 0)
LOG2E = 1.4426950408889634

ROW_TILE = 512
ATT_TILE = 256
ATT_Q = 512
ATT_SUB = 256
MOE_TILE = 512
N_MOE_TILES = 2 * N_TOK // MOE_TILE + N_EXPERTS
N_SLOTS = N_MOE_TILES * MOE_TILE
ROW_SLABS = D_MODEL // 256
U32 = jnp.uint32
DISPATCH_TILE = 512
COMBINE_TILE = 512
FFT_RADIX = 64
FFT_PITCH = 72
ROUTER_LANES = 128
GROUP_LANE0 = 32

_NT = (((1,), (1,)), ((), ()))


def _vmem(nbytes):
    return pltpu.CompilerParams(vmem_limit_bytes=int(nbytes))


def _layer_norm(x, g, b):
    mu = jnp.mean(x, axis=-1, keepdims=True)
    xc = x - mu
    var = jnp.mean(xc * xc, axis=-1, keepdims=True)
    return xc * lax.rsqrt(var + LN_EPS) * g + b


def _load_row_tiles(ref, rows):
    w = jnp.concatenate([ref[pl.ds(s, rows, stride=ROW_SLABS), :] for s in range(ROW_SLABS)], axis=1)
    lo = lax.bitcast_convert_type(lax.shift_left(w, jnp.uint32(16)), F32)
    hi = lax.bitcast_convert_type(w & jnp.uint32(0xFFFF0000), F32)
    return jnp.concatenate([lo, hi], axis=1)


def _store_row_tiles(ref, x):
    half = D_MODEL // 2
    lo = lax.bitcast_convert_type(x[:, :half].astype(BF16).astype(F32), jnp.uint32)
    hi = lax.bitcast_convert_type(x[:, half:].astype(BF16).astype(F32), jnp.uint32)
    w = lax.shift_right_logical(lo, jnp.uint32(16)) | hi
    for s in range(ROW_SLABS):
        ref[pl.ds(s, x.shape[0], stride=ROW_SLABS), :] = w[:, s * 128:(s + 1) * 128]


def _row_tile(ref, t):
    return ref.at[pl.ds(pl.multiple_of(t * ROW_SLABS, ROW_SLABS), ROW_SLABS), :]


def _t5_bucket(rel):
    nb = N_BUCKETS // 2
    max_exact = nb // 2
    ret = (rel > 0).astype(jnp.int32) * nb
    n = jnp.abs(rel)
    nf = jnp.maximum(n, 1).astype(F32)
    large = max_exact + (jnp.log(nf / max_exact) / math.log(MAX_DISTANCE / max_exact)
                         * (nb - max_exact)).astype(jnp.int32)
    large = jnp.minimum(large, nb - 1)
    return ret + jnp.where(n < max_exact, n, large)


def _ln0_inproj_kernel(x_ref, g_ref, b_ref, wqk_ref, wvt_ref, wfm_ref,
                       h_ref, q_ref, k_ref, vt_ref, f_ref, mq_ref):
    h = _layer_norm(x_ref[...], g_ref[...], b_ref[...])
    h_ref[...] = h
    hb = h.astype(BF16)
    qk = jnp.dot(hb, wqk_ref[...], preferred_element_type=F32)
    q_ref[...] = (qk[:, :DIFF_WIDTH] * (DIFF_HEAD_DIM ** -0.5 * LOG2E)).astype(BF16)
    k_ref[...] = qk[:, DIFF_WIDTH:].astype(BF16)
    vt = lax.dot_general(wvt_ref[...], hb, _NT, preferred_element_type=F32)
    ones = jnp.ones((VT_ROWS - DIFF_VDIM, ROW_TILE), BF16)
    for hh in range(DIFF_HEADS):
        vt_ref[0, hh, :DIFF_VDIM, :] = vt[hh * DIFF_VDIM:(hh + 1) * DIFF_VDIM, :].astype(BF16)
        vt_ref[0, hh, DIFF_VDIM:, :] = ones
    fm = jnp.dot(hb, wfm_ref[...], preferred_element_type=F32)
    f_ref[...] = fm[:, :FOURIER_WIDTH].astype(BF16)
    mq_ref[...] = (fm[:, FOURIER_WIDTH:] * MEM_HEAD_DIM ** -0.5).astype(BF16)


def _ln0_inproj(x2, ln_g, ln_b, w_qk, w_vt, w_fm):
    rt = SEQ // ROW_TILE
    row = lambda r: (r, 0)
    const = lambda r: (0, 0)
    return pl.pallas_call(
        _ln0_inproj_kernel,
        grid=(N_TOK // ROW_TILE,),
        in_specs=[
            pl.BlockSpec((ROW_TILE, D_MODEL), row),
            pl.BlockSpec((1, D_MODEL), const),
            pl.BlockSpec((1, D_MODEL), const),
            pl.BlockSpec((D_MODEL, 2 * DIFF_WIDTH), const),
            pl.BlockSpec((DIFF_WIDTH, D_MODEL), const),
            pl.BlockSpec((D_MODEL, FOURIER_WIDTH + MEM_WIDTH), const),
        ],
        out_specs=[
            pl.BlockSpec((ROW_TILE, D_MODEL), row),
            pl.BlockSpec((ROW_TILE, DIFF_WIDTH), row),
            pl.BlockSpec((ROW_TILE, DIFF_WIDTH), row),
            pl.BlockSpec((1, DIFF_HEADS, VT_ROWS, ROW_TILE), lambda r: (r // rt, 0, 0, r % rt)),
            pl.BlockSpec((ROW_TILE, FOURIER_WIDTH), row),
            pl.BlockSpec((ROW_TILE, MEM_WIDTH), row),
        ],
        out_shape=[
            jax.ShapeDtypeStruct((N_TOK, D_MODEL), F32),
            jax.ShapeDtypeStruct((N_TOK, DIFF_WIDTH), BF16),
            jax.ShapeDtypeStruct((N_TOK, DIFF_WIDTH), BF16),
            jax.ShapeDtypeStruct((BATCH, DIFF_HEADS, VT_ROWS, SEQ), BF16),
            jax.ShapeDtypeStruct((N_TOK, FOURIER_WIDTH), BF16),
            jax.ShapeDtypeStruct((N_TOK, MEM_WIDTH), BF16),
        ],
        compiler_params=_vmem(48 << 20),
        name="ln0_inproj",
    )(x2, ln_g, ln_b, w_qk, w_vt, w_fm)


def _diff_attn_kernel(tbl_ref, q_ref, k_ref, vt_ref, bkt_ref, lam_ref, g_ref, o_ref,
                      bias_scr, p0_scr, p1_scr):
    T = ATT_TILE
    TQ = ATT_Q
    r = TQ // T
    h = pl.program_id(0)
    i = pl.program_id(2)
    p_scr = (p0_scr, p1_scr)

    @pl.when((pl.program_id(1) == 0) & (i == 0))
    def _build_bias():
        for d in range(r + 4):
            far = d in (0, r + 3)
            bk = bkt_ref[d, 0:8, :] if far else bkt_ref[d]
            bias = jnp.zeros(bk.shape, F32)
            for n in range(N_BUCKETS):
                bias = jnp.where(bk == n, tbl_ref[n, h], bias)
            bias_scr[d] = (jnp.broadcast_to(bias[0:1, :], (T, TQ)) if far else bias) * LOG2E

    q = q_ref[...]
    lane = lax.broadcasted_iota(jnp.int32, q.shape, 1)
    zero = jnp.zeros_like(q)
    q_comp = (jnp.where(lane < DIFF_HEAD_DIM, q, zero), jnp.where(lane >= DIFF_HEAD_DIM, q, zero))

    def bias_tile(j):
        return bias_scr[jnp.clip(j - r * i, -2, r + 1) + 2]

    SUB = ATT_SUB
    n_sub = SEQ // SUB
    m_tile = [[None] * n_sub for _ in range(2)]
    for c in range(2):
        s = lax.dot_general(k_ref[...], q_comp[c], _NT, preferred_element_type=F32)
        for u in range(n_sub):
            if SUB >= T:
                bias = jnp.concatenate([bias_tile(u * SUB // T + t) for t in range(SUB // T)], axis=0)
            else:
                bias = bias_tile(u * SUB // T)[(u * SUB) % T:(u * SUB) % T + SUB, :]
            sb = s[u * SUB:(u + 1) * SUB, :] + bias
            mu = jnp.max(sb, axis=0, keepdims=True)
            p_scr[c][u * SUB:(u + 1) * SUB, :] = jnp.exp2((sb - mu).astype(BF16))
            m_tile[c][u] = mu

    acc = []
    for c in range(2):
        m = functools.reduce(jnp.maximum, m_tile[c])
        a = None
        for u in range(n_sub):
            pv = jnp.dot(vt_ref[0, 0, :, u * SUB:(u + 1) * SUB], p_scr[c][u * SUB:(u + 1) * SUB, :],
                         preferred_element_type=F32)
            pv = pv * jnp.exp2(m_tile[c][u] - m)
            a = pv if a is None else a + pv
        acc.append(a)
    num = [a[:DIFF_VDIM] for a in acc]
    den = [a[DIFF_VDIM:DIFF_VDIM + 1] for a in acc]

    lam = (jnp.exp(jnp.sum(lam_ref[0:1, :] * lam_ref[1:2, :], axis=1, keepdims=True))
           - jnp.exp(jnp.sum(lam_ref[2:3, :] * lam_ref[3:4, :], axis=1, keepdims=True)) + LAM_INIT)
    o = num[0] * (1.0 / den[0]) - lam * (num[1] * (1.0 / den[1]))
    ms = jnp.mean(o * o, axis=0, keepdims=True)
    o = o * lax.rsqrt(ms + LN_EPS) * g_ref[...] * (1.0 - LAM_INIT)
    o_ref[...] = o.T.astype(BF16)


def _diff_attention(rel_bias, q, k, vt, buckets, lam_vecs, subln_col):
    T = ATT_TILE
    TQ = ATT_Q
    nq = SEQ // TQ
    n_bias = TQ // T + 4
    return pl.pallas_call(
        _diff_attn_kernel,
        grid=(DIFF_HEADS, BATCH, nq),
        in_specs=[
            pl.BlockSpec(memory_space=pltpu.SMEM),
            pl.BlockSpec((TQ, DIFF_VDIM), lambda h, b, i: (b * nq + i, h)),
            pl.BlockSpec((SEQ, DIFF_VDIM), lambda h, b, i: (b, h)),
            pl.BlockSpec((1, 1, VT_ROWS, SEQ), lambda h, b, i: (b, h, 0, 0)),
            pl.BlockSpec((n_bias, T, TQ), lambda h, b, i: (0, 0, 0)),
            pl.BlockSpec((4, DIFF_HEAD_DIM), lambda h, b, i: (0, 0)),
            pl.BlockSpec((DIFF_VDIM, 1), lambda h, b, i: (0, 0)),
        ],
        out_specs=pl.BlockSpec((TQ, DIFF_VDIM), lambda h, b, i: (b * nq + i, h)),
        out_shape=jax.ShapeDtypeStruct((N_TOK, DIFF_WIDTH), BF16),
        scratch_shapes=[pltpu.VMEM((n_bias, T, TQ), F32),
                        pltpu.VMEM((SEQ, TQ), BF16), pltpu.VMEM((SEQ, TQ), BF16)],
        compiler_params=_vmem(40 << 20),
        name="diff_attn",
    )(rel_bias, q, k, vt, buckets, lam_vecs, subln_col)


def _fourier_kernel(f_ref, bdc_ref, bds_ref, bdw_ref, m1_ref, m2_ref, twc_ref, tws_ref, o_ref,
                    zr_scr, zi_scr, d_scr, a_scr):
    R = FFT_RADIX
    W = FOURIER_WIDTH
    u = f_ref[...]
    w = bdw_ref[...]
    pc = jnp.dot(u, bdc_ref[...], preferred_element_type=F32).astype(BF16)
    ps = jnp.dot(u, bds_ref[...], preferred_element_type=F32).astype(BF16)
    zr = jnp.dot(pc, w, preferred_element_type=F32)
    zi = jnp.dot(ps, w, preferred_element_type=F32)
    P = FFT_PITCH
    halves = range(W // 128)
    for hh in halves:
        for n1 in range(R):
            zr_scr[hh, n1 * P:n1 * P + R, :] = zr[n1 * R:(n1 + 1) * R, hh * 128:(hh + 1) * 128]
            zi_scr[hh, n1 * P:n1 * P + R, :] = zi[n1 * R:(n1 + 1) * R, hh * 128:(hh + 1) * 128]

    for n2 in range(R):
        for hh in halves:
            cols = slice(n2 * W + hh * 128, n2 * W + (hh + 1) * 128)
            d_scr[0:R, cols] = zr_scr[hh, pl.ds(n2, R, stride=P), :].astype(BF16)
            d_scr[R:2 * R, cols] = zi_scr[hh, pl.ds(n2, R, stride=P), :].astype(BF16)
    a_scr[...] = jnp.dot(m1_ref[...], d_scr[...], preferred_element_type=F32)

    for n2 in range(R):
        tc = twc_ref[n2]
        ts = tws_ref[n2]
        for hh in halves:
            cols = slice(n2 * W + hh * 128, n2 * W + (hh + 1) * 128)
            ar = a_scr[0:R, cols]
            ai = a_scr[R:2 * R, cols]
            zr_scr[hh, pl.ds(n2, R, stride=P), :] = ar * tc + ai * ts
            zi_scr[hh, pl.ds(n2, R, stride=P), :] = ai * tc - ar * ts
    for k1 in range(R):
        for hh in halves:
            cols = slice(k1 * W + hh * 128, k1 * W + (hh + 1) * 128)
            d_scr[0:R, cols] = zr_scr[hh, k1 * P:k1 * P + R, :].astype(BF16)
            d_scr[R:2 * R, cols] = zi_scr[hh, k1 * P:k1 * P + R, :].astype(BF16)
    a_scr[0:R, :] = jnp.dot(m2_ref[...], d_scr[...], preferred_element_type=F32)
    for k1 in range(R):
        for hh in halves:
            zr_scr[hh, pl.ds(k1, R, stride=P), :] = a_scr[0:R, k1 * W + hh * 128:k1 * W + (hh + 1) * 128]
    for k2 in range(R):
        for hh in halves:
            o_ref[hh, k2 * R:(k2 + 1) * R, :] = zr_scr[hh, k2 * P:k2 * P + R, :]


def _fourier(f_in, bdc, bds, bdw, m1, m2, twc, tws):
    R = FFT_RADIX
    W = FOURIER_WIDTH
    const2 = lambda b: (0, 0)
    const3 = lambda b: (0, 0, 0)
    return pl.pallas_call(
        _fourier_kernel,
        grid=(BATCH,),
        in_specs=[
            pl.BlockSpec((SEQ, W), lambda b: (b, 0)),
            pl.BlockSpec((W, W), const2),
            pl.BlockSpec((W, W), const2),
            pl.BlockSpec((W, W), const2),
            pl.BlockSpec((2 * R, 2 * R), const2),
            pl.BlockSpec((R, 2 * R), const2),
            pl.BlockSpec((R, R, 128), const3),
            pl.BlockSpec((R, R, 128), const3),
        ],
        out_specs=pl.BlockSpec((W // 128, SEQ, 128), lambda b: (0, b, 0)),
        out_shape=jax.ShapeDtypeStruct((W // 128, N_TOK, 128), F32),
        scratch_shapes=[pltpu.VMEM((W // 128, R * FFT_PITCH, 128), F32),
                        pltpu.VMEM((W // 128, R * FFT_PITCH, 128), F32),
                        pltpu.VMEM((2 * R, R * W), BF16), pltpu.VMEM((2 * R, R * W), F32)],
        compiler_params=_vmem(52 << 20),
        name="fourier",
    )(f_in, bdc, bds, bdw, m1, m2, twc, tws)


@functools.lru_cache(maxsize=None)
def _dft_constants():
    R = FFT_RADIX
    a = np.arange(R, dtype=np.int64)
    ang = 2.0 * np.pi * ((a[:, None] * a[None, :]) % R).astype(np.float64) / R
    c64, s64 = np.cos(ang), np.sin(ang)
    scale = 1.0 / math.sqrt(SEQ * FOURIER_GROUP_DIM)
    m1 = np.block([[c64, s64], [-s64, c64]])
    m2 = np.concatenate([c64, s64], axis=1) * scale
    tang = 2.0 * np.pi * (a[:, None] * a[None, :]).astype(np.float64) / SEQ
    twc = np.repeat(np.cos(tang)[:, :, None], 128, axis=2).astype(np.float32)
    tws = np.repeat(np.sin(tang)[:, :, None], 128, axis=2).astype(np.float32)
    assert FOURIER_GROUP_DIM == R
    eye = np.eye(FOURIER_GROUPS)
    bdc = np.kron(eye, c64)
    bds = np.kron(eye, -s64)
    return bdc.astype(BF16), bds.astype(BF16), m1.astype(BF16), m2.astype(BF16), twc, tws


def _mem_kv_kernel(mem_ref, w_ref, kcat_ref, vcat_ref):
    kv = jnp.dot(mem_ref[...].astype(BF16), w_ref[...], preferred_element_type=F32)
    mk = kv[:, :MEM_WIDTH]
    mv = kv[:, MEM_WIDTH:]
    lane = lax.broadcasted_iota(jnp.int32, mk.shape, 1)
    for hh in range(MEM_HEADS):
        sel = (lane >= hh * MEM_HEAD_DIM) & (lane < (hh + 1) * MEM_HEAD_DIM)
        kcat_ref[0, hh * MEM_LEN:(hh + 1) * MEM_LEN, :] = jnp.where(sel, mk, 0.0).astype(BF16)
        vcat_ref[0, hh * MEM_LEN:(hh + 1) * MEM_LEN, :] = jnp.where(sel, mv, 0.0).astype(BF16)


def _mem_kv(mem2, w_kv):
    return pl.pallas_call(
        _mem_kv_kernel,
        grid=(BATCH,),
        in_specs=[
            pl.BlockSpec((MEM_LEN, D_MODEL), lambda b: (b, 0)),
            pl.BlockSpec((D_MODEL, 2 * MEM_WIDTH), lambda b: (0, 0)),
        ],
        out_specs=[
            pl.BlockSpec((1, MEM_HEADS * MEM_LEN, MEM_WIDTH), lambda b: (b, 0, 0)),
            pl.BlockSpec((1, MEM_HEADS * MEM_LEN, MEM_WIDTH), lambda b: (b, 0, 0)),
        ],
        out_shape=[
            jax.ShapeDtypeStruct((BATCH, MEM_HEADS * MEM_LEN, MEM_WIDTH), BF16),
            jax.ShapeDtypeStruct((BATCH, MEM_HEADS * MEM_LEN, MEM_WIDTH), BF16),
        ],
        name="mem_kv",
    )(mem2, w_kv)


def _mem_attn_kernel(mq_ref, kcat_ref, vcat_ref, o_ref):
    s = lax.dot_general(mq_ref[...], kcat_ref[0], _NT, preferred_element_type=F32)
    parts = []
    for hh in range(MEM_HEADS):
        sh = s[:, hh * MEM_LEN:(hh + 1) * MEM_LEN]
        p = jnp.exp(sh - jnp.max(sh, axis=1, keepdims=True))
        parts.append((p * (1.0 / jnp.sum(p, axis=1, keepdims=True))).astype(BF16))
    p_all = jnp.concatenate(parts, axis=1)
    o_ref[...] = jnp.dot(p_all, vcat_ref[0], preferred_element_type=F32).astype(BF16)


def _mem_attention(mq, kcat, vcat):
    rt = SEQ // ROW_TILE
    return pl.pallas_call(
        _mem_attn_kernel,
        grid=(N_TOK // ROW_TILE,),
        in_specs=[
            pl.BlockSpec((ROW_TILE, MEM_WIDTH), lambda r: (r, 0)),
            pl.BlockSpec((1, MEM_HEADS * MEM_LEN, MEM_WIDTH), lambda r: (r // rt, 0, 0)),
            pl.BlockSpec((1, MEM_HEADS * MEM_LEN, MEM_WIDTH), lambda r: (r // rt, 0, 0)),
        ],
        out_specs=pl.BlockSpec((ROW_TILE, MEM_WIDTH), lambda r: (r, 0)),
        out_shape=jax.ShapeDtypeStruct((N_TOK, MEM_WIDTH), BF16),
        name="mem_attn",
    )(mq, kcat, vcat)


def _route(logits):
    lane = lax.broadcasted_iota(jnp.int32, logits.shape, 1)
    big = jnp.int32(ROUTER_LANES)
    is_group = (lane >= GROUP_LANE0) & (lane < GROUP_LANE0 + N_GROUPS)
    gl = jnp.where(is_group, logits, -jnp.inf)
    gmax = jnp.max(gl, axis=1, keepdims=True)
    g_sel = jnp.min(jnp.where(gl == gmax, lane, big), axis=1, keepdims=True) - GROUP_LANE0
    g_gate = 1.0 / jnp.sum(jnp.where(is_group, jnp.exp(gl - gmax), 0.0), axis=1, keepdims=True)
    in_group = (lane >= g_sel * EXPERTS_PER_GROUP) & (lane < (g_sel + 1) * EXPERTS_PER_GROUP)
    el = jnp.where(in_group, logits, -jnp.inf)
    v1 = jnp.max(el, axis=1, keepdims=True)
    i1 = jnp.min(jnp.where(el == v1, lane, big), axis=1, keepdims=True)
    el2 = jnp.where(lane == i1, -jnp.inf, el)
    v2 = jnp.max(el2, axis=1, keepdims=True)
    i2 = jnp.min(jnp.where(el2 == v2, lane, big), axis=1, keepdims=True)
    e = jnp.exp(v2 - v1)
    w_first = g_gate / (1.0 + e)
    w_second = g_gate * e / (1.0 + e)
    return i1, i2, w_first, w_second


def _outproj_router_kernel(od_ref, of_ref, om_ref, h_ref, wo_ref, g_ref, b_ref,
                           wr_ref, br_ref,
                           h1_ref, h1_rows_ref, route_ref, w0_ref, w1_ref, cnt_ref, cnt_scr):
    @pl.when(pl.program_id(0) == 0)
    def _():
        cnt_scr[...] = jnp.zeros_like(cnt_scr)

    o_four = [of_ref[hh].astype(BF16) for hh in range(FOURIER_WIDTH // 128)]
    o = jnp.concatenate([od_ref[...]] + o_four + [om_ref[...]], axis=1)
    a = jnp.dot(o, wo_ref[...], preferred_element_type=F32)
    h1 = _layer_norm(DEEPNORM_ALPHA * h_ref[...] + a, g_ref[...], b_ref[...])
    h1_ref[...] = h1
    _store_row_tiles(h1_rows_ref, h1)
    hi = h1.astype(BF16)
    lo = (h1 - hi.astype(F32)).astype(BF16)
    hw = jnp.dot(hi, wr_ref[...], preferred_element_type=F32)
    logits = (hw[:, :ROUTER_LANES] + hw[:, ROUTER_LANES:]
              + jnp.dot(lo, wr_ref[:, :ROUTER_LANES], preferred_element_type=F32) + br_ref[...])
    i1, i2, w_first, w_second = _route(logits)

    lane = lax.broadcasted_iota(jnp.int32, logits.shape, 1)
    onehot = jnp.where(lane == i1, 1.0, jnp.where(lane == i2, 1.0, 0.0))
    r_id = lax.broadcasted_iota(jnp.int32, (ROW_TILE, ROW_TILE), 0)
    c_id = lax.broadcasted_iota(jnp.int32, (ROW_TILE, ROW_TILE), 1)
    tri = jnp.where(r_id > c_id, 1.0, 0.0).astype(BF16)
    before = jnp.dot(tri, onehot.astype(BF16), preferred_element_type=F32) + cnt_scr[...]
    rank1 = jnp.sum(jnp.where(lane == i1, before, 0.0), axis=1, keepdims=True)
    rank2 = jnp.sum(jnp.where(lane == i2, before, 0.0), axis=1, keepdims=True)
    cnt_scr[...] += jnp.sum(onehot, axis=0, keepdims=True)
    cnt_ref[...] = cnt_scr[...]

    packed = jnp.where(lane == 0, i1.astype(F32),
                       jnp.where(lane == 1, i2.astype(F32),
                                 jnp.where(lane == 2, rank1, jnp.where(lane == 3, rank2, 0.0))))
    route_ref[...] = packed.T[:8, :].astype(jnp.int32)
    w0_ref[...] = jnp.broadcast_to(w_first, (ROW_TILE, 128))
    w1_ref[...] = jnp.broadcast_to(w_second, (ROW_TILE, 128))


def _outproj_router(o_diff, o_four, o_mem, h, w_out, ln_g, ln_b, w_r, b_r):
    row = lambda r: (r, 0)
    const = lambda r: (0, 0)
    return pl.pallas_call(
        _outproj_router_kernel,
        grid=(N_TOK // ROW_TILE,),
        in_specs=[
            pl.BlockSpec((ROW_TILE, DIFF_WIDTH), row),
            pl.BlockSpec((FOURIER_WIDTH // 128, ROW_TILE, 128), lambda r: (0, r, 0)),
            pl.BlockSpec((ROW_TILE, MEM_WIDTH), row),
            pl.BlockSpec((ROW_TILE, D_MODEL), row),
            pl.BlockSpec((D_MODEL, D_MODEL), const),
            pl.BlockSpec((1, D_MODEL), const),
            pl.BlockSpec((1, D_MODEL), const),
            pl.BlockSpec((D_MODEL, 2 * ROUTER_LANES), const),
            pl.BlockSpec((1, ROUTER_LANES), const),
        ],
        out_specs=[
            pl.BlockSpec((ROW_TILE, D_MODEL), row),
            pl.BlockSpec((ROW_TILE * ROW_SLABS, 128), row),
            pl.BlockSpec((8, ROW_TILE), lambda r: (0, r)),
            pl.BlockSpec((ROW_TILE, 128), row),
            pl.BlockSpec((ROW_TILE, 128), row),
            pl.BlockSpec((1, ROUTER_LANES), const),
        ],
        out_shape=[
            jax.ShapeDtypeStruct((N_TOK, D_MODEL), F32),
            jax.ShapeDtypeStruct((N_TOK * ROW_SLABS, 128), U32),
            jax.ShapeDtypeStruct((8, N_TOK), jnp.int32),
            jax.ShapeDtypeStruct((N_TOK, 128), F32),
            jax.ShapeDtypeStruct((N_TOK, 128), F32),
            jax.ShapeDtypeStruct((1, ROUTER_LANES), F32),
        ],
        scratch_shapes=[pltpu.VMEM((1, ROUTER_LANES), F32)],
        compiler_params=_vmem(40 << 20),
        name="outproj_router",
    )(o_diff, o_four, o_mem, h, w_out, ln_g, ln_b, w_r, b_r)


def _dispatch_kernel(pos0_ref, pos1_ref, x_ref, xs_ref, sem):
    base = pl.program_id(0) * DISPATCH_TILE

    def issue(t, carry):
        src = _row_tile(x_ref, t)
        pltpu.make_async_copy(src, _row_tile(xs_ref, pos0_ref[base + t]), sem).start(priority=0)
        pltpu.make_async_copy(src, _row_tile(xs_ref, pos1_ref[base + t]), sem).start(priority=1)
        return carry

    lax.fori_loop(0, DISPATCH_TILE, issue, 0, unroll=8)
    for _ in range(2):
        pltpu.make_async_copy(x_ref, xs_ref.at[pl.ds(0, DISPATCH_TILE * ROW_SLABS), :], sem).wait()


def _dispatch(pos0, pos1, h1_rows):
    return pl.pallas_call(
        _dispatch_kernel,
        grid_spec=pltpu.PrefetchScalarGridSpec(
            num_scalar_prefetch=2,
            grid=(N_TOK // DISPATCH_TILE,),
            in_specs=[pl.BlockSpec((DISPATCH_TILE * ROW_SLABS, 128), lambda i, p0, p1: (i, 0))],
            out_specs=pl.BlockSpec(memory_space=pl.ANY),
            scratch_shapes=[pltpu.SemaphoreType.DMA(())],
        ),
        out_shape=jax.ShapeDtypeStruct((N_SLOTS * ROW_SLABS, 128), U32),
        name="moe_dispatch",
    )(pos0, pos1, h1_rows)


def _expert_kernel(tile_end_ref, xs_ref, w1_ref, w3_ref, w2_ref, ys_ref):
    @pl.when(pl.program_id(0) < tile_end_ref[N_EXPERTS - 1])
    def _():
        x = _load_row_tiles(xs_ref, MOE_TILE).astype(BF16)
        a = jnp.dot(x, w1_ref[0].astype(BF16), preferred_element_type=F32)
        b = jnp.dot(x, w3_ref[0].astype(BF16), preferred_element_type=F32)
        hid = (a * jax.nn.sigmoid(a) * b).astype(BF16)
        y = jnp.dot(hid, w2_ref[0].astype(BF16), preferred_element_type=F32)
        _store_row_tiles(ys_ref, y)


def _experts(tile_end, xs, w1, w3, w2):
    def tile(t, tile_end):
        return jnp.minimum(t, tile_end[N_EXPERTS - 1] - 1), 0

    def wsel(t, tile_end):
        e = jnp.int32(0)
        step = N_EXPERTS // 2
        while step:
            e = jnp.where(t >= tile_end[e + step - 1], e + step, e)
            step //= 2
        return e, 0, 0

    return pl.pallas_call(
        _expert_kernel,
        grid_spec=pltpu.PrefetchScalarGridSpec(
            num_scalar_prefetch=1,
            grid=(N_MOE_TILES,),
            in_specs=[
                pl.BlockSpec((MOE_TILE * ROW_SLABS, 128), tile),
                pl.BlockSpec((1, D_MODEL, D_EXPERT), wsel),
                pl.BlockSpec((1, D_MODEL, D_EXPERT), wsel),
                pl.BlockSpec((1, D_EXPERT, D_MODEL), wsel),
            ],
            out_specs=pl.BlockSpec((MOE_TILE * ROW_SLABS, 128), tile),
        ),
        out_shape=jax.ShapeDtypeStruct((N_SLOTS * ROW_SLABS, 128), U32),
        compiler_params=_vmem(40 << 20),
        name="moe_experts",
    )(tile_end, xs, w1, w3, w2)


def _combine_kernel(pos0_ref, pos1_ref, ys_ref, h1_ref, w0_ref, w1_ref, g_ref, b_ref, o_ref,
                    y0_buf, y1_buf, sems):
    i = pl.program_id(0)
    n = pl.num_programs(0)

    def gather(tile, slot):
        base = tile * COMBINE_TILE

        def issue(t, carry):
            pltpu.make_async_copy(_row_tile(ys_ref, pos0_ref[base + t]), _row_tile(y0_buf.at[slot], t),
                                  sems.at[slot]).start(priority=0)
            pltpu.make_async_copy(_row_tile(ys_ref, pos1_ref[base + t]), _row_tile(y1_buf.at[slot], t),
                                  sems.at[slot]).start(priority=1)
            return carry

        lax.fori_loop(0, COMBINE_TILE, issue, 0, unroll=8)

    @pl.when(i == 0)
    def _():
        gather(0, 0)

    @pl.when(i + 1 < n)
    def _():
        gather(i + 1, (i + 1) % 2)

    slot = i % 2
    whole = ys_ref.at[pl.ds(0, COMBINE_TILE * ROW_SLABS), :]
    pltpu.make_async_copy(whole, y0_buf.at[slot], sems.at[slot]).wait()
    pltpu.make_async_copy(whole, y1_buf.at[slot], sems.at[slot]).wait()
    w0 = jnp.concatenate([w0_ref[...]] * (D_MODEL // 128), axis=1)
    w1 = jnp.concatenate([w1_ref[...]] * (D_MODEL // 128), axis=1)
    f = (w0 * _load_row_tiles(y0_buf.at[slot], COMBINE_TILE)
         + w1 * _load_row_tiles(y1_buf.at[slot], COMBINE_TILE))
    o_ref[...] = _layer_norm(DEEPNORM_ALPHA * h1_ref[...] + f, g_ref[...], b_ref[...])


def _combine(pos0, pos1, ys, h1, w0, w1, ln_g, ln_b):
    row = lambda i, p0, p1: (i, 0)
    const = lambda i, p0, p1: (0, 0)
    return pl.pallas_call(
        _combine_kernel,
        grid_spec=pltpu.PrefetchScalarGridSpec(
            num_scalar_prefetch=2,
            grid=(N_TOK // COMBINE_TILE,),
            in_specs=[
                pl.BlockSpec(memory_space=pl.ANY),
                pl.BlockSpec((COMBINE_TILE, D_MODEL), row),
                pl.BlockSpec((COMBINE_TILE, 128), row),
                pl.BlockSpec((COMBINE_TILE, 128), row),
                pl.BlockSpec((1, D_MODEL), const),
                pl.BlockSpec((1, D_MODEL), const),
            ],
            out_specs=pl.BlockSpec((COMBINE_TILE, D_MODEL), row),
            scratch_shapes=[pltpu.VMEM((2, COMBINE_TILE * ROW_SLABS, 128), U32),
                            pltpu.VMEM((2, COMBINE_TILE * ROW_SLABS, 128), U32),
                            pltpu.SemaphoreType.DMA((2,))],
        ),
        out_shape=jax.ShapeDtypeStruct((N_TOK, D_MODEL), F32),
        compiler_params=pltpu.CompilerParams(dimension_semantics=("arbitrary",), vmem_limit_bytes=40 << 20),
        name="moe_combine",
    )(pos0, pos1, ys, h1, w0, w1, ln_g, ln_b)


def kernel(x, mem, ln0_g, ln0_b, rel_bias, w_in, w_mem_kv, w_fourier, lambda_q1, lambda_k1, lambda_q2,
           lambda_k2, subln_g, w_out, ln1_g, ln1_b, w_group, b_group, w_router, b_router, w1, w3, w2,
           ln2_g, ln2_b):
    l = 0
    x2 = x.reshape(N_TOK, D_MODEL)
    wi = w_in[l]
    w_qk = wi[:, :2 * DIFF_WIDTH].astype(BF16)
    w_vt = wi[:, 2 * DIFF_WIDTH:3 * DIFF_WIDTH].T.astype(BF16)
    w_fm = wi[:, 3 * DIFF_WIDTH:].astype(BF16)
    h, q, k, vt, f_in, mq = _ln0_inproj(x2, ln0_g.reshape(1, -1), ln0_b.reshape(1, -1), w_qk, w_vt, w_fm)

    T = ATT_TILE
    kk = jnp.arange(T, dtype=jnp.int32)[:, None]
    qq = jnp.arange(ATT_Q, dtype=jnp.int32)[None, :]
    buckets = jnp.stack([_t5_bucket(d * T + kk - qq) for d in range(-2, ATT_Q // T + 2)])
    lam_vecs = jnp.stack([lambda_q1[l], lambda_k1[l], lambda_q2[l], lambda_k2[l]]).astype(F32)
    o_diff = _diff_attention(rel_bias.astype(F32), q, k, vt, buckets, lam_vecs,
                             subln_g[l].astype(F32).reshape(DIFF_VDIM, 1))

    bdc, bds, m1, m2, twc, tws = (jnp.asarray(c) for c in _dft_constants())
    wf = w_fourier[l]
    bdw = jnp.zeros((FOURIER_WIDTH, FOURIER_WIDTH), F32)
    for g in range(FOURIER_GROUPS):
        sl = slice(g * FOURIER_GROUP_DIM, (g + 1) * FOURIER_GROUP_DIM)
        bdw = bdw.at[sl, sl].set(wf[g])
    o_four = _fourier(f_in, bdc, bds, bdw.astype(BF16), m1, m2, twc, tws)

    kcat, vcat = _mem_kv(mem.reshape(BATCH * MEM_LEN, D_MODEL), w_mem_kv[l].astype(BF16))
    o_mem = _mem_attention(mq, kcat, vcat)

    w_r = jnp.zeros((D_MODEL, ROUTER_LANES), F32)
    w_r = w_r.at[:, :N_EXPERTS].set(w_router[l].astype(F32))
    w_r = w_r.at[:, GROUP_LANE0:GROUP_LANE0 + N_GROUPS].set(w_group[l].astype(F32))
    wr_hi = w_r.astype(BF16)
    wr_lo = (w_r - wr_hi.astype(F32)).astype(BF16)
    b_r = jnp.zeros((1, ROUTER_LANES), F32)
    b_r = b_r.at[0, :N_EXPERTS].set(b_router[l].astype(F32))
    b_r = b_r.at[0, GROUP_LANE0:GROUP_LANE0 + N_GROUPS].set(b_group[l].astype(F32))
    h1, h1_rows, route, gate0, gate1, counts = _outproj_router(
        o_diff, o_four, o_mem, h, w_out[l].astype(BF16),
        ln1_g[l].reshape(1, -1), ln1_b[l].reshape(1, -1), jnp.concatenate([wr_hi, wr_lo], axis=1), b_r)

    cnt = counts[0, :N_EXPERTS].astype(jnp.int32)
    tiles_per_expert = (cnt + MOE_TILE - 1) // MOE_TILE
    tile_end = jnp.cumsum(tiles_per_expert).astype(jnp.int32)
    row_start = (tile_end - tiles_per_expert) * MOE_TILE
    expert_ids = jnp.arange(N_EXPERTS, dtype=jnp.int32)[None, :]

    def slot_of(expert, rank):
        return jnp.sum(jnp.where(expert[:, None] == expert_ids, row_start[None, :], 0), axis=1) + rank

    pos0 = slot_of(route[0], route[2])
    pos1 = slot_of(route[1], route[3])

    xs = _dispatch(pos0, pos1, h1_rows)
    ys = _experts(tile_end, xs, w1[l], w3[l], w2[l])
    out = _combine(pos0, pos1, ys, h1, gate0, gate1, ln2_g[l].reshape(1, -1), ln2_b[l].reshape(1, -1))
    return out.reshape(BATCH, SEQ, D_MODEL)
```

```python
import functools
import math

import numpy as np
import jax
import jax.numpy as jnp
from jax import lax
from jax.experimental import pallas as pl
from jax.experimental.pallas import tpu as pltpu

F32 = jnp.float32
BF16 = jnp.bfloat16

D_MODEL = 1024
BATCH = 4
SEQ = 4096
N_TOK = BATCH * SEQ
MEM_LEN = 256
MEM_HEADS = 4
MEM_HEAD_DIM = 64
MEM_WIDTH = 256
DIFF_HEADS = 4
DIFF_HEAD_DIM = 64
DIFF_VDIM = 2 * DIFF_HEAD_DIM
VT_ROWS = DIFF_VDIM + 16
DIFF_WIDTH = 512
FOURIER_GROUPS = 4
FOURIER_GROUP_DIM = 64
FOURIER_WIDTH = 256
N_BUCKETS = 32
MAX_DISTANCE = 128
N_GROUPS = 4
EXPERTS_PER_GROUP = 8
N_EXPERTS = 32
D_EXPERT = 256
LN_EPS = 1e-5
DEEPNORM_ALPHA = 2.0 ** 0.25
LAM_INIT = 0.8 - 0.6 * math.exp(-0.3 * 0)
LOG2E = 1.4426950408889634

ROW_TILE = 1024
ATT_TILE = 256
ATT_Q = 512
ATT_SUB = 256
MOE_TILE = 512
N_MOE_TILES = 2 * N_TOK // MOE_TILE + N_EXPERTS
N_SLOTS = N_MOE_TILES * MOE_TILE
ROW_SLABS = D_MODEL // 256
U32 = jnp.uint32
DISPATCH_TILE = 512
COMBINE_TILE = 512
FFT_RADIX = 64
FFT_PITCH = 72
ROUTER_LANES = 128
GROUP_LANE0 = 32

_NT = (((1,), (1,)), ((), ()))


def _vmem(nbytes):
    return pltpu.CompilerParams(vmem_limit_bytes=int(nbytes))


def _layer_norm(x, g, b):
    mu = jnp.mean(x, axis=-1, keepdims=True)
    xc = x - mu
    var = jnp.mean(xc * xc, axis=-1, keepdims=True)
    return xc * lax.rsqrt(var + LN_EPS) * g + b


def _load_row_tiles(ref, rows):
    w = jnp.concatenate([ref[pl.ds(s, rows, stride=ROW_SLABS), :] for s in range(ROW_SLABS)], axis=1)
    lo = lax.bitcast_convert_type(lax.shift_left(w, jnp.uint32(16)), F32)
    hi = lax.bitcast_convert_type(w & jnp.uint32(0xFFFF0000), F32)
    return jnp.concatenate([lo, hi], axis=1)


def _store_row_tiles(ref, x):
    half = D_MODEL // 2
    lo = lax.bitcast_convert_type(x[:, :half].astype(BF16).astype(F32), jnp.uint32)
    hi = lax.bitcast_convert_type(x[:, half:].astype(BF16).astype(F32), jnp.uint32)
    w = lax.shift_right_logical(lo, jnp.uint32(16)) | hi
    for s in range(ROW_SLABS):
        ref[pl.ds(s, x.shape[0], stride=ROW_SLABS), :] = w[:, s * 128:(s + 1) * 128]


def _row_tile(ref, t):
    return ref.at[pl.ds(pl.multiple_of(t * ROW_SLABS, ROW_SLABS), ROW_SLABS), :]


def _t5_bucket(rel):
    nb = N_BUCKETS // 2
    max_exact = nb // 2
    ret = (rel > 0).astype(jnp.int32) * nb
    n = jnp.abs(rel)
    nf = jnp.maximum(n, 1).astype(F32)
    large = max_exact + (jnp.log(nf / max_exact) / math.log(MAX_DISTANCE / max_exact)
                         * (nb - max_exact)).astype(jnp.int32)
    large = jnp.minimum(large, nb - 1)
    return ret + jnp.where(n < max_exact, n, large)


def _ln0_inproj_kernel(x_ref, g_ref, b_ref, wqk_ref, wvt_ref, wfm_ref,
                       h_ref, q_ref, k_ref, vt_ref, f_ref, mq_ref):
    h = _layer_norm(x_ref[...], g_ref[...], b_ref[...])
    h_ref[...] = h
    hb = h.astype(BF16)
    qk = jnp.dot(hb, wqk_ref[...], preferred_element_type=F32)
    q_ref[...] = (qk[:, :DIFF_WIDTH] * (DIFF_HEAD_DIM ** -0.5 * LOG2E)).astype(BF16)
    k_ref[...] = qk[:, DIFF_WIDTH:].astype(BF16)
    vt = lax.dot_general(wvt_ref[...], hb, _NT, preferred_element_type=F32)
    ones = jnp.ones((VT_ROWS - DIFF_VDIM, ROW_TILE), BF16)
    for hh in range(DIFF_HEADS):
        vt_ref[0, hh, :DIFF_VDIM, :] = vt[hh * DIFF_VDIM:(hh + 1) * DIFF_VDIM, :].astype(BF16)
        vt_ref[0, hh, DIFF_VDIM:, :] = ones
    fm = jnp.dot(hb, wfm_ref[...], preferred_element_type=F32)
    f_ref[...] = fm[:, :FOURIER_WIDTH].astype(BF16)
    mq_ref[...] = (fm[:, FOURIER_WIDTH:] * MEM_HEAD_DIM ** -0.5).astype(BF16)


def _ln0_inproj(x2, ln_g, ln_b, w_qk, w_vt, w_fm):
    rt = SEQ // ROW_TILE
    row = lambda r: (r, 0)
    const = lambda r: (0, 0)
    return pl.pallas_call(
        _ln0_inproj_kernel,
        grid=(N_TOK // ROW_TILE,),
        in_specs=[
            pl.BlockSpec((ROW_TILE, D_MODEL), row),
            pl.BlockSpec((1, D_MODEL), const),
            pl.BlockSpec((1, D_MODEL), const),
            pl.BlockSpec((D_MODEL, 2 * DIFF_WIDTH), const),
            pl.BlockSpec((DIFF_WIDTH, D_MODEL), const),
            pl.BlockSpec((D_MODEL, FOURIER_WIDTH + MEM_WIDTH), const),
        ],
        out_specs=[
            pl.BlockSpec((ROW_TILE, D_MODEL), row),
            pl.BlockSpec((ROW_TILE, DIFF_WIDTH), row),
            pl.BlockSpec((ROW_TILE, DIFF_WIDTH), row),
            pl.BlockSpec((1, DIFF_HEADS, VT_ROWS, ROW_TILE), lambda r: (r // rt, 0, 0, r % rt)),
            pl.BlockSpec((ROW_TILE, FOURIER_WIDTH), row),
            pl.BlockSpec((ROW_TILE, MEM_WIDTH), row),
        ],
        out_shape=[
            jax.ShapeDtypeStruct((N_TOK, D_MODEL), F32),
            jax.ShapeDtypeStruct((N_TOK, DIFF_WIDTH), BF16),
            jax.ShapeDtypeStruct((N_TOK, DIFF_WIDTH), BF16),
            jax.ShapeDtypeStruct((BATCH, DIFF_HEADS, VT_ROWS, SEQ), BF16),
            jax.ShapeDtypeStruct((N_TOK, FOURIER_WIDTH), BF16),
            jax.ShapeDtypeStruct((N_TOK, MEM_WIDTH), BF16),
        ],
        compiler_params=_vmem(48 << 20),
        name="ln0_inproj",
    )(x2, ln_g, ln_b, w_qk, w_vt, w_fm)


def _diff_attn_kernel(tbl_ref, q_ref, k_ref, vt_ref, bkt_ref, lam_ref, g_ref, o_ref,
                      bias_scr, p0_scr, p1_scr):
    T = ATT_TILE
    TQ = ATT_Q
    r = TQ // T
    h = pl.program_id(0)
    i = pl.program_id(2)
    p_scr = (p0_scr, p1_scr)

    @pl.when((pl.program_id(1) == 0) & (i == 0))
    def _build_bias():
        for d in range(r + 4):
            far = d in (0, r + 3)
            bk = bkt_ref[d, 0:8, :] if far else bkt_ref[d]
            bias = jnp.zeros(bk.shape, F32)
            for n in range(N_BUCKETS):
                bias = jnp.where(bk == n, tbl_ref[n, h], bias)
            bias_scr[d] = (jnp.broadcast_to(bias[0:1, :], (T, TQ)) if far else bias) * LOG2E

    q = q_ref[...]
    lane = lax.broadcasted_iota(jnp.int32, q.shape, 1)
    zero = jnp.zeros_like(q)
    q_comp = (jnp.where(lane < DIFF_HEAD_DIM, q, zero), jnp.where(lane >= DIFF_HEAD_DIM, q, zero))

    def bias_tile(j):
        return bias_scr[jnp.clip(j - r * i, -2, r + 1) + 2]

    SUB = ATT_SUB
    n_sub = SEQ // SUB
    m_tile = [[None] * n_sub for _ in range(2)]
    for c in range(2):
        s = lax.dot_general(k_ref[...], q_comp[c], _NT, preferred_element_type=F32)
        for u in range(n_sub):
            if SUB >= T:
                bias = jnp.concatenate([bias_tile(u * SUB // T + t) for t in range(SUB // T)], axis=0)
            else:
                bias = bias_tile(u * SUB // T)[(u * SUB) % T:(u * SUB) % T + SUB, :]
            sb = s[u * SUB:(u + 1) * SUB, :] + bias
            mu = jnp.max(sb, axis=0, keepdims=True)
            p_scr[c][u * SUB:(u + 1) * SUB, :] = jnp.exp2((sb - mu).astype(BF16))
            m_tile[c][u] = mu

    acc = []
    for c in range(2):
        m = functools.reduce(jnp.maximum, m_tile[c])
        a = None
        for u in range(n_sub):
            pv = jnp.dot(vt_ref[0, 0, :, u * SUB:(u + 1) * SUB], p_scr[c][u * SUB:(u + 1) * SUB, :],
                         preferred_element_type=F32)
            pv = pv * jnp.exp2(m_tile[c][u] - m)
            a = pv if a is None else a + pv
        acc.append(a)
    num = [a[:DIFF_VDIM] for a in acc]
    den = [a[DIFF_VDIM:DIFF_VDIM + 1] for a in acc]

    lam = (jnp.exp(jnp.sum(lam_ref[0:1, :] * lam_ref[1:2, :], axis=1, keepdims=True))
           - jnp.exp(jnp.sum(lam_ref[2:3, :] * lam_ref[3:4, :], axis=1, keepdims=True)) + LAM_INIT)
    o = num[0] * (1.0 / den[0]) - lam * (num[1] * (1.0 / den[1]))
    ms = jnp.mean(o * o, axis=0, keepdims=True)
    o = o * lax.rsqrt(ms + LN_EPS) * g_ref[...] * (1.0 - LAM_INIT)
    o_ref[...] = o.T.astype(BF16)


def _diff_attention(rel_bias, q, k, vt, buckets, lam_vecs, subln_col):
    T = ATT_TILE
    TQ = ATT_Q
    nq = SEQ // TQ
    n_bias = TQ // T + 4
    return pl.pallas_call(
        _diff_attn_kernel,
        grid=(DIFF_HEADS, BATCH, nq),
        in_specs=[
            pl.BlockSpec(memory_space=pltpu.SMEM),
            pl.BlockSpec((TQ, DIFF_VDIM), lambda h, b, i: (b * nq + i, h)),
            pl.BlockSpec((SEQ, DIFF_VDIM), lambda h, b, i: (b, h)),
            pl.BlockSpec((1, 1, VT_ROWS, SEQ), lambda h, b, i: (b, h, 0, 0)),
            pl.BlockSpec((n_bias, T, TQ), lambda h, b, i: (0, 0, 0)),
            pl.BlockSpec((4, DIFF_HEAD_DIM), lambda h, b, i: (0, 0)),
            pl.BlockSpec((DIFF_VDIM, 1), lambda h, b, i: (0, 0)),
        ],
        out_specs=pl.BlockSpec((TQ, DIFF_VDIM), lambda h, b, i: (b * nq + i, h)),
        out_shape=jax.ShapeDtypeStruct((N_TOK, DIFF_WIDTH), BF16),
        scratch_shapes=[pltpu.VMEM((n_bias, T, TQ), F32),
                        pltpu.VMEM((SEQ, TQ), BF16), pltpu.VMEM((SEQ, TQ), BF16)],
        compiler_params=_vmem(40 << 20),
        name="diff_attn",
    )(rel_bias, q, k, vt, buckets, lam_vecs, subln_col)


def _fourier_kernel(f_ref, bdc_ref, bds_ref, bdw_ref, m1_ref, m2_ref, twc_ref, tws_ref, o_ref,
                    zr_scr, zi_scr, d_scr, a_scr):
    R = FFT_RADIX
    W = FOURIER_WIDTH
    u = f_ref[...]
    w = bdw_ref[...]
    pc = jnp.dot(u, bdc_ref[...], preferred_element_type=F32).astype(BF16)
    ps = jnp.dot(u, bds_ref[...], preferred_element_type=F32).astype(BF16)
    zr = jnp.dot(pc, w, preferred_element_type=F32)
    zi = jnp.dot(ps, w, preferred_element_type=F32)
    P = FFT_PITCH
    halves = range(W // 128)
    for hh in halves:
        for n1 in range(R):
            zr_scr[hh, n1 * P:n1 * P + R, :] = zr[n1 * R:(n1 + 1) * R, hh * 128:(hh + 1) * 128]
            zi_scr[hh, n1 * P:n1 * P + R, :] = zi[n1 * R:(n1 + 1) * R, hh * 128:(hh + 1) * 128]

    for n2 in range(R):
        for hh in halves:
            cols = slice(n2 * W + hh * 128, n2 * W + (hh + 1) * 128)
            d_scr[0:R, cols] = zr_scr[hh, pl.ds(n2, R, stride=P), :].astype(BF16)
            d_scr[R:2 * R, cols] = zi_scr[hh, pl.ds(n2, R, stride=P), :].astype(BF16)
    a_scr[...] = jnp.dot(m1_ref[...], d_scr[...], preferred_element_type=F32)

    for n2 in range(R):
        tc = twc_ref[n2]
        ts = tws_ref[n2]
        for hh in halves:
            cols = slice(n2 * W + hh * 128, n2 * W + (hh + 1) * 128)
            ar = a_scr[0:R, cols]
            ai = a_scr[R:2 * R, cols]
            zr_scr[hh, pl.ds(n2, R, stride=P), :] = ar * tc + ai * ts
            zi_scr[hh, pl.ds(n2, R, stride=P), :] = ai * tc - ar * ts
    for k1 in range(R):
        for hh in halves:
            cols = slice(k1 * W + hh * 128, k1 * W + (hh + 1) * 128)
            d_scr[0:R, cols] = zr_scr[hh, k1 * P:k1 * P + R, :].astype(BF16)
            d_scr[R:2 * R, cols] = zi_scr[hh, k1 * P:k1 * P + R, :].astype(BF16)
    a_scr[0:R, :] = jnp.dot(m2_ref[...], d_scr[...], preferred_element_type=F32)
    for k1 in range(R):
        for hh in halves:
            zr_scr[hh, pl.ds(k1, R, stride=P), :] = a_scr[0:R, k1 * W + hh * 128:k1 * W + (hh + 1) * 128]
    for k2 in range(R):
        for hh in halves:
            o_ref[hh, k2 * R:(k2 + 1) * R, :] = zr_scr[hh, k2 * P:k2 * P + R, :]


def _fourier(f_in, bdc, bds, bdw, m1, m2, twc, tws):
    R = FFT_RADIX
    W = FOURIER_WIDTH
    const2 = lambda b: (0, 0)
    const3 = lambda b: (0, 0, 0)
    return pl.pallas_call(
        _fourier_kernel,
        grid=(BATCH,),
        in_specs=[
            pl.BlockSpec((SEQ, W), lambda b: (b, 0)),
            pl.BlockSpec((W, W), const2),
            pl.BlockSpec((W, W), const2),
            pl.BlockSpec((W, W), const2),
            pl.BlockSpec((2 * R, 2 * R), const2),
            pl.BlockSpec((R, 2 * R), const2),
            pl.BlockSpec((R, R, 128), const3),
            pl.BlockSpec((R, R, 128), const3),
        ],
        out_specs=pl.BlockSpec((W // 128, SEQ, 128), lambda b: (0, b, 0)),
        out_shape=jax.ShapeDtypeStruct((W // 128, N_TOK, 128), F32),
        scratch_shapes=[pltpu.VMEM((W // 128, R * FFT_PITCH, 128), F32),
                        pltpu.VMEM((W // 128, R * FFT_PITCH, 128), F32),
                        pltpu.VMEM((2 * R, R * W), BF16), pltpu.VMEM((2 * R, R * W), F32)],
        compiler_params=_vmem(52 << 20),
        name="fourier",
    )(f_in, bdc, bds, bdw, m1, m2, twc, tws)


@functools.lru_cache(maxsize=None)
def _dft_constants():
    R = FFT_RADIX
    a = np.arange(R, dtype=np.int64)
    ang = 2.0 * np.pi * ((a[:, None] * a[None, :]) % R).astype(np.float64) / R
    c64, s64 = np.cos(ang), np.sin(ang)
    scale = 1.0 / math.sqrt(SEQ * FOURIER_GROUP_DIM)
    m1 = np.block([[c64, s64], [-s64, c64]])
    m2 = np.concatenate([c64, s64], axis=1) * scale
    tang = 2.0 * np.pi * (a[:, None] * a[None, :]).astype(np.float64) / SEQ
    twc = np.repeat(np.cos(tang)[:, :, None], 128, axis=2).astype(np.float32)
    tws = np.repeat(np.sin(tang)[:, :, None], 128, axis=2).astype(np.float32)
    assert FOURIER_GROUP_DIM == R
    eye = np.eye(FOURIER_GROUPS)
    bdc = np.kron(eye, c64)
    bds = np.kron(eye, -s64)
    return bdc.astype(BF16), bds.astype(BF16), m1.astype(BF16), m2.astype(BF16), twc, tws


def _mem_kv_kernel(mem_ref, w_ref, kcat_ref, vcat_ref):
    kv = jnp.dot(mem_ref[...].astype(BF16), w_ref[...], preferred_element_type=F32)
    mk = kv[:, :MEM_WIDTH]
    mv = kv[:, MEM_WIDTH:]
    lane = lax.broadcasted_iota(jnp.int32, mk.shape, 1)
    for hh in range(MEM_HEADS):
        sel = (lane >= hh * MEM_HEAD_DIM) & (lane < (hh + 1) * MEM_HEAD_DIM)
        kcat_ref[0, hh * MEM_LEN:(hh + 1) * MEM_LEN, :] = jnp.where(sel, mk, 0.0).astype(BF16)
        vcat_ref[0, hh * MEM_LEN:(hh + 1) * MEM_LEN, :] = jnp.where(sel, mv, 0.0).astype(BF16)


def _mem_kv(mem2, w_kv):
    return pl.pallas_call(
        _mem_kv_kernel,
        grid=(BATCH,),
        in_specs=[
            pl.BlockSpec((MEM_LEN, D_MODEL), lambda b: (b, 0)),
            pl.BlockSpec((D_MODEL, 2 * MEM_WIDTH), lambda b: (0, 0)),
        ],
        out_specs=[
            pl.BlockSpec((1, MEM_HEADS * MEM_LEN, MEM_WIDTH), lambda b: (b, 0, 0)),
            pl.BlockSpec((1, MEM_HEADS * MEM_LEN, MEM_WIDTH), lambda b: (b, 0, 0)),
        ],
        out_shape=[
            jax.ShapeDtypeStruct((BATCH, MEM_HEADS * MEM_LEN, MEM_WIDTH), BF16),
            jax.ShapeDtypeStruct((BATCH, MEM_HEADS * MEM_LEN, MEM_WIDTH), BF16),
        ],
        name="mem_kv",
    )(mem2, w_kv)


def _mem_attn_kernel(mq_ref, kcat_ref, vcat_ref, o_ref):
    s = lax.dot_general(mq_ref[...], kcat_ref[0], _NT, preferred_element_type=F32)
    parts = []
    for hh in range(MEM_HEADS):
        sh = s[:, hh * MEM_LEN:(hh + 1) * MEM_LEN]
        p = jnp.exp(sh - jnp.max(sh, axis=1, keepdims=True))
        parts.append((p * (1.0 / jnp.sum(p, axis=1, keepdims=True))).astype(BF16))
    p_all = jnp.concatenate(parts, axis=1)
    o_ref[...] = jnp.dot(p_all, vcat_ref[0], preferred_element_type=F32).astype(BF16)


def _mem_attention(mq, kcat, vcat):
    rt = SEQ // ROW_TILE
    return pl.pallas_call(
        _mem_attn_kernel,
        grid=(N_TOK // ROW_TILE,),
        in_specs=[
            pl.BlockSpec((ROW_TILE, MEM_WIDTH), lambda r: (r, 0)),
            pl.BlockSpec((1, MEM_HEADS * MEM_LEN, MEM_WIDTH), lambda r: (r // rt, 0, 0)),
            pl.BlockSpec((1, MEM_HEADS * MEM_LEN, MEM_WIDTH), lambda r: (r // rt, 0, 0)),
        ],
        out_specs=pl.BlockSpec((ROW_TILE, MEM_WIDTH), lambda r: (r, 0)),
        out_shape=jax.ShapeDtypeStruct((N_TOK, MEM_WIDTH), BF16),
        name="mem_attn",
    )(mq, kcat, vcat)


def _route(logits):
    lane = lax.broadcasted_iota(jnp.int32, logits.shape, 1)
    big = jnp.int32(ROUTER_LANES)
    is_group = (lane >= GROUP_LANE0) & (lane < GROUP_LANE0 + N_GROUPS)
    gl = jnp.where(is_group, logits, -jnp.inf)
    gmax = jnp.max(gl, axis=1, keepdims=True)
    g_sel = jnp.min(jnp.where(gl == gmax, lane, big), axis=1, keepdims=True) - GROUP_LANE0
    g_gate = 1.0 / jnp.sum(jnp.where(is_group, jnp.exp(gl - gmax), 0.0), axis=1, keepdims=True)
    in_group = (lane >= g_sel * EXPERTS_PER_GROUP) & (lane < (g_sel + 1) * EXPERTS_PER_GROUP)
    el = jnp.where(in_group, logits, -jnp.inf)
    v1 = jnp.max(el, axis=1, keepdims=True)
    i1 = jnp.min(jnp.where(el == v1, lane, big), axis=1, keepdims=True)
    el2 = jnp.where(lane == i1, -jnp.inf, el)
    v2 = jnp.max(el2, axis=1, keepdims=True)
    i2 = jnp.min(jnp.where(el2 == v2, lane, big), axis=1, keepdims=True)
    e = jnp.exp(v2 - v1)
    w_first = g_gate / (1.0 + e)
    w_second = g_gate * e / (1.0 + e)
    return i1, i2, w_first, w_second


def _outproj_router_kernel(od_ref, of_ref, om_ref, h_ref, wo_ref, g_ref, b_ref,
                           wr_ref, br_ref,
                           h1_ref, h1_rows_ref, route_ref, w0_ref, w1_ref, cnt_ref, cnt_scr):
    @pl.when(pl.program_id(0) == 0)
    def _():
        cnt_scr[...] = jnp.zeros_like(cnt_scr)

    o_four = [of_ref[hh].astype(BF16) for hh in range(FOURIER_WIDTH // 128)]
    o = jnp.concatenate([od_ref[...]] + o_four + [om_ref[...]], axis=1)
    a = jnp.dot(o, wo_ref[...], preferred_element_type=F32)
    h1 = _layer_norm(DEEPNORM_ALPHA * h_ref[...] + a, g_ref[...], b_ref[...])
    h1_ref[...] = h1
    _store_row_tiles(h1_rows_ref, h1)
    hi = h1.astype(BF16)
    lo = (h1 - hi.astype(F32)).astype(BF16)
    hw = jnp.dot(hi, wr_ref[...], preferred_element_type=F32)
    logits = (hw[:, :ROUTER_LANES] + hw[:, ROUTER_LANES:]
              + jnp.dot(lo, wr_ref[:, :ROUTER_LANES], preferred_element_type=F32) + br_ref[...])
    i1, i2, w_first, w_second = _route(logits)

    lane = lax.broadcasted_iota(jnp.int32, logits.shape, 1)
    onehot = jnp.where(lane == i1, 1.0, jnp.where(lane == i2, 1.0, 0.0))
    r_id = lax.broadcasted_iota(jnp.int32, (ROW_TILE, ROW_TILE), 0)
    c_id = lax.broadcasted_iota(jnp.int32, (ROW_TILE, ROW_TILE), 1)
    tri = jnp.where(r_id > c_id, 1.0, 0.0).astype(BF16)
    before = jnp.dot(tri, onehot.astype(BF16), preferred_element_type=F32) + cnt_scr[...]
    rank1 = jnp.sum(jnp.where(lane == i1, before, 0.0), axis=1, keepdims=True)
    rank2 = jnp.sum(jnp.where(lane == i2, before, 0.0), axis=1, keepdims=True)
    cnt_scr[...] += jnp.sum(onehot, axis=0, keepdims=True)
    cnt_ref[...] = cnt_scr[...]

    packed = jnp.where(lane == 0, i1.astype(F32),
                       jnp.where(lane == 1, i2.astype(F32),
                                 jnp.where(lane == 2, rank1, jnp.where(lane == 3, rank2, 0.0))))
    route_ref[...] = packed.T[:8, :].astype(jnp.int32)
    w0_ref[...] = jnp.broadcast_to(w_first, (ROW_TILE, 128))
    w1_ref[...] = jnp.broadcast_to(w_second, (ROW_TILE, 128))


def _outproj_router(o_diff, o_four, o_mem, h, w_out, ln_g, ln_b, w_r, b_r):
    row = lambda r: (r, 0)
    const = lambda r: (0, 0)
    return pl.pallas_call(
        _outproj_router_kernel,
        grid=(N_TOK // ROW_TILE,),
        in_specs=[
            pl.BlockSpec((ROW_TILE, DIFF_WIDTH), row),
            pl.BlockSpec((FOURIER_WIDTH // 128, ROW_TILE, 128), lambda r: (0, r, 0)),
            pl.BlockSpec((ROW_TILE, MEM_WIDTH), row),
            pl.BlockSpec((ROW_TILE, D_MODEL), row),
            pl.BlockSpec((D_MODEL, D_MODEL), const),
            pl.BlockSpec((1, D_MODEL), const),
            pl.BlockSpec((1, D_MODEL), const),
            pl.BlockSpec((D_MODEL, 2 * ROUTER_LANES), const),
            pl.BlockSpec((1, ROUTER_LANES), const),
        ],
        out_specs=[
            pl.BlockSpec((ROW_TILE, D_MODEL), row),
            pl.BlockSpec((ROW_TILE * ROW_SLABS, 128), row),
            pl.BlockSpec((8, ROW_TILE), lambda r: (0, r)),
            pl.BlockSpec((ROW_TILE, 128), row),
            pl.BlockSpec((ROW_TILE, 128), row),
            pl.BlockSpec((1, ROUTER_LANES), const),
        ],
        out_shape=[
            jax.ShapeDtypeStruct((N_TOK, D_MODEL), F32),
            jax.ShapeDtypeStruct((N_TOK * ROW_SLABS, 128), U32),
            jax.ShapeDtypeStruct((8, N_TOK), jnp.int32),
            jax.ShapeDtypeStruct((N_TOK, 128), F32),
            jax.ShapeDtypeStruct((N_TOK, 128), F32),
            jax.ShapeDtypeStruct((1, ROUTER_LANES), F32),
        ],
        scratch_shapes=[pltpu.VMEM((1, ROUTER_LANES), F32)],
        compiler_params=_vmem(40 << 20),
        name="outproj_router",
    )(o_diff, o_four, o_mem, h, w_out, ln_g, ln_b, w_r, b_r)


def _dispatch_kernel(pos0_ref, pos1_ref, x_ref, xs_ref, sem):
    base = pl.program_id(0) * DISPATCH_TILE

    def issue(t, carry):
        src = _row_tile(x_ref, t)
        pltpu.make_async_copy(src, _row_tile(xs_ref, pos0_ref[base + t]), sem).start(priority=0)
        pltpu.make_async_copy(src, _row_tile(xs_ref, pos1_ref[base + t]), sem).start(priority=1)
        return carry

    lax.fori_loop(0, DISPATCH_TILE, issue, 0, unroll=8)
    for _ in range(2):
        pltpu.make_async_copy(x_ref, xs_ref.at[pl.ds(0, DISPATCH_TILE * ROW_SLABS), :], sem).wait()


def _dispatch(pos0, pos1, h1_rows):
    return pl.pallas_call(
        _dispatch_kernel,
        grid_spec=pltpu.PrefetchScalarGridSpec(
            num_scalar_prefetch=2,
            grid=(N_TOK // DISPATCH_TILE,),
            in_specs=[pl.BlockSpec((DISPATCH_TILE * ROW_SLABS, 128), lambda i, p0, p1: (i, 0))],
            out_specs=pl.BlockSpec(memory_space=pl.ANY),
            scratch_shapes=[pltpu.SemaphoreType.DMA(())],
        ),
        out_shape=jax.ShapeDtypeStruct((N_SLOTS * ROW_SLABS, 128), U32),
        name="moe_dispatch",
    )(pos0, pos1, h1_rows)


def _expert_kernel(tile_end_ref, xs_ref, w1_ref, w3_ref, w2_ref, ys_ref):
    @pl.when(pl.program_id(0) < tile_end_ref[N_EXPERTS - 1])
    def _():
        x = _load_row_tiles(xs_ref, MOE_TILE).astype(BF16)
        a = jnp.dot(x, w1_ref[0].astype(BF16), preferred_element_type=F32)
        b = jnp.dot(x, w3_ref[0].astype(BF16), preferred_element_type=F32)
        hid = (a * jax.nn.sigmoid(a) * b).astype(BF16)
        y = jnp.dot(hid, w2_ref[0].astype(BF16), preferred_element_type=F32)
        _store_row_tiles(ys_ref, y)


def _experts(tile_end, xs, w1, w3, w2):
    def tile(t, tile_end):
        return jnp.minimum(t, tile_end[N_EXPERTS - 1] - 1), 0

    def wsel(t, tile_end):
        e = jnp.int32(0)
        step = N_EXPERTS // 2
        while step:
            e = jnp.where(t >= tile_end[e + step - 1], e + step, e)
            step //= 2
        return e, 0, 0

    return pl.pallas_call(
        _expert_kernel,
        grid_spec=pltpu.PrefetchScalarGridSpec(
            num_scalar_prefetch=1,
            grid=(N_MOE_TILES,),
            in_specs=[
                pl.BlockSpec((MOE_TILE * ROW_SLABS, 128), tile),
                pl.BlockSpec((1, D_MODEL, D_EXPERT), wsel),
                pl.BlockSpec((1, D_MODEL, D_EXPERT), wsel),
                pl.BlockSpec((1, D_EXPERT, D_MODEL), wsel),
            ],
            out_specs=pl.BlockSpec((MOE_TILE * ROW_SLABS, 128), tile),
        ),
        out_shape=jax.ShapeDtypeStruct((N_SLOTS * ROW_SLABS, 128), U32),
        compiler_params=_vmem(40 << 20),
        name="moe_experts",
    )(tile_end, xs, w1, w3, w2)


def _combine_kernel(pos0_ref, pos1_ref, ys_ref, h1_ref, w0_ref, w1_ref, g_ref, b_ref, o_ref,
                    y0_buf, y1_buf, sems):
    i = pl.program_id(0)
    n = pl.num_programs(0)

    def gather(tile, slot):
        base = tile * COMBINE_TILE

        def issue(t, carry):
            pltpu.make_async_copy(_row_tile(ys_ref, pos0_ref[base + t]), _row_tile(y0_buf.at[slot], t),
                                  sems.at[slot]).start(priority=0)
            pltpu.make_async_copy(_row_tile(ys_ref, pos1_ref[base + t]), _row_tile(y1_buf.at[slot], t),
                                  sems.at[slot]).start(priority=1)
            return carry

        lax.fori_loop(0, COMBINE_TILE, issue, 0, unroll=8)

    @pl.when(i == 0)
    def _():
        gather(0, 0)

    @pl.when(i + 1 < n)
    def _():
        gather(i + 1, (i + 1) % 2)

    slot = i % 2
    whole = ys_ref.at[pl.ds(0, COMBINE_TILE * ROW_SLABS), :]
    pltpu.make_async_copy(whole, y0_buf.at[slot], sems.at[slot]).wait()
    pltpu.make_async_copy(whole, y1_buf.at[slot], sems.at[slot]).wait()
    w0 = jnp.concatenate([w0_ref[...]] * (D_MODEL // 128), axis=1)
    w1 = jnp.concatenate([w1_ref[...]] * (D_MODEL // 128), axis=1)
    f = (w0 * _load_row_tiles(y0_buf.at[slot], COMBINE_TILE)
         + w1 * _load_row_tiles(y1_buf.at[slot], COMBINE_TILE))
    o_ref[...] = _layer_norm(DEEPNORM_ALPHA * h1_ref[...] + f, g_ref[...], b_ref[...])


def _combine(pos0, pos1, ys, h1, w0, w1, ln_g, ln_b):
    row = lambda i, p0, p1: (i, 0)
    const = lambda i, p0, p1: (0, 0)
    return pl.pallas_call(
        _combine_kernel,
        grid_spec=pltpu.PrefetchScalarGridSpec(
            num_scalar_prefetch=2,
            grid=(N_TOK // COMBINE_TILE,),
            in_specs=[
                pl.BlockSpec(memory_space=pl.ANY),
                pl.BlockSpec((COMBINE_TILE, D_MODEL), row),
                pl.BlockSpec((COMBINE_TILE, 128), row),
                pl.BlockSpec((COMBINE_TILE, 128), row),
                pl.BlockSpec((1, D_MODEL), const),
                pl.BlockSpec((1, D_MODEL), const),
            ],
            out_specs=pl.BlockSpec((COMBINE_TILE, D_MODEL), row),
            scratch_shapes=[pltpu.VMEM((2, COMBINE_TILE * ROW_SLABS, 128), U32),
                            pltpu.VMEM((2, COMBINE_TILE * ROW_SLABS, 128), U32),
                            pltpu.SemaphoreType.DMA((2,))],
        ),
        out_shape=jax.ShapeDtypeStruct((N_TOK, D_MODEL), F32),
        compiler_params=pltpu.CompilerParams(dimension_semantics=("arbitrary",), vmem_limit_bytes=40 << 20),
        name="moe_combine",
    )(pos0, pos1, ys, h1, w0, w1, ln_g, ln_b)


def kernel(x, mem, ln0_g, ln0_b, rel_bias, w_in, w_mem_kv, w_fourier, lambda_q1, lambda_k1, lambda_q2,
           lambda_k2, subln_g, w_out, ln1_g, ln1_b, w_group, b_group, w_router, b_router, w1, w3, w2,
           ln2_g, ln2_b):
    l = 0
    x2 = x.reshape(N_TOK, D_MODEL)
    wi = w_in[l]
    w_qk = wi[:, :2 * DIFF_WIDTH].astype(BF16)
    w_vt = wi[:, 2 * DIFF_WIDTH:3 * DIFF_WIDTH].T.astype(BF16)
    w_fm = wi[:, 3 * DIFF_WIDTH:].astype(BF16)
    h, q, k, vt, f_in, mq = _ln0_inproj(x2, ln0_g.reshape(1, -1), ln0_b.reshape(1, -1), w_qk, w_vt, w_fm)

    T = ATT_TILE
    kk = jnp.arange(T, dtype=jnp.int32)[:, None]
    qq = jnp.arange(ATT_Q, dtype=jnp.int32)[None, :]
    buckets = jnp.stack([_t5_bucket(d * T + kk - qq) for d in range(-2, ATT_Q // T + 2)])
    lam_vecs = jnp.stack([lambda_q1[l], lambda_k1[l], lambda_q2[l], lambda_k2[l]]).astype(F32)
    o_diff = _diff_attention(rel_bias.astype(F32), q, k, vt, buckets, lam_vecs,
                             subln_g[l].astype(F32).reshape(DIFF_VDIM, 1))

    bdc, bds, m1, m2, twc, tws = (jnp.asarray(c) for c in _dft_constants())
    wf = w_fourier[l]
    bdw = jnp.zeros((FOURIER_WIDTH, FOURIER_WIDTH), F32)
    for g in range(FOURIER_GROUPS):
        sl = slice(g * FOURIER_GROUP_DIM, (g + 1) * FOURIER_GROUP_DIM)
        bdw = bdw.at[sl, sl].set(wf[g])
    o_four = _fourier(f_in, bdc, bds, bdw.astype(BF16), m1, m2, twc, tws)

    kcat, vcat = _mem_kv(mem.reshape(BATCH * MEM_LEN, D_MODEL), w_mem_kv[l].astype(BF16))
    o_mem = _mem_attention(mq, kcat, vcat)

    w_r = jnp.zeros((D_MODEL, ROUTER_LANES), F32)
    w_r = w_r.at[:, :N_EXPERTS].set(w_router[l].astype(F32))
    w_r = w_r.at[:, GROUP_LANE0:GROUP_LANE0 + N_GROUPS].set(w_group[l].astype(F32))
    wr_hi = w_r.astype(BF16)
    wr_lo = (w_r - wr_hi.astype(F32)).astype(BF16)
    b_r = jnp.zeros((1, ROUTER_LANES), F32)
    b_r = b_r.at[0, :N_EXPERTS].set(b_router[l].astype(F32))
    b_r = b_r.at[0, GROUP_LANE0:GROUP_LANE0 + N_GROUPS].set(b_group[l].astype(F32))
    h1, h1_rows, route, gate0, gate1, counts = _outproj_router(
        o_diff, o_four, o_mem, h, w_out[l].astype(BF16),
        ln1_g[l].reshape(1, -1), ln1_b[l].reshape(1, -1), jnp.concatenate([wr_hi, wr_lo], axis=1), b_r)

    cnt = counts[0, :N_EXPERTS].astype(jnp.int32)
    tiles_per_expert = (cnt + MOE_TILE - 1) // MOE_TILE
    tile_end = jnp.cumsum(tiles_per_expert).astype(jnp.int32)
    row_start = (tile_end - tiles_per_expert) * MOE_TILE
    expert_ids = jnp.arange(N_EXPERTS, dtype=jnp.int32)[None, :]

    def slot_of(expert, rank):
        return jnp.sum(jnp.where(expert[:, None] == expert_ids, row_start[None, :], 0), axis=1) + rank

    pos0 = slot_of(route[0], route[2])
    pos1 = slot_of(route[1], route[3])

    xs = _dispatch(pos0, pos1, h1_rows)
    ys = _experts(tile_end, xs, w1[l], w3[l], w2[l])
    out = _combine(pos0, pos1, ys, h1, gate0, gate1, ln2_g[l].reshape(1, -1), ln2_b[l].reshape(1, -1))
    return out.reshape(BATCH, SEQ, D_MODEL)
```

```python
import functools
import math

import numpy as np
import jax
import jax.numpy as jnp
from jax import lax
from jax.experimental import pallas as pl
from jax.experimental.pallas import tpu as pltpu

F32 = jnp.float32
BF16 = jnp.bfloat16

D_MODEL = 1024
BATCH = 4
SEQ = 4096
N_TOK = BATCH * SEQ
MEM_LEN = 256
MEM_HEADS = 4
MEM_HEAD_DIM = 64
MEM_WIDTH = 256
DIFF_HEADS = 4
DIFF_HEAD_DIM = 64
DIFF_VDIM = 2 * DIFF_HEAD_DIM
VT_ROWS = DIFF_VDIM + 16
DIFF_WIDTH = 512
FOURIER_GROUPS = 4
FOURIER_GROUP_DIM = 64
FOURIER_WIDTH = 256
N_BUCKETS = 32
MAX_DISTANCE = 128
N_GROUPS = 4
EXPERTS_PER_GROUP = 8
N_EXPERTS = 32
D_EXPERT = 256
LN_EPS = 1e-5
DEEPNORM_ALPHA = 2.0 ** 0.25
LAM_INIT = 0.8 - 0.6 * math.exp(-0.3 * 0)
LOG2E = 1.4426950408889634

ROW_TILE = 1024
ATT_TILE = 256
ATT_Q = 512
ATT_SUB = 256
MOE_TILE = 512
N_MOE_TILES = 2 * N_TOK // MOE_TILE + N_EXPERTS
N_SLOTS = N_MOE_TILES * MOE_TILE
ROW_SLABS = D_MODEL // 256
U32 = jnp.uint32
DISPATCH_TILE = 4096
COMBINE_TILE = 1024
FFT_RADIX = 64
FFT_PITCH = 72
ROUTER_LANES = 128
GROUP_LANE0 = 32

_NT = (((1,), (1,)), ((), ()))


def _vmem(nbytes):
    return pltpu.CompilerParams(vmem_limit_bytes=int(nbytes))


def _layer_norm(x, g, b):
    mu = jnp.mean(x, axis=-1, keepdims=True)
    xc = x - mu
    var = jnp.mean(xc * xc, axis=-1, keepdims=True)
    return xc * lax.rsqrt(var + LN_EPS) * g + b


def _load_row_tiles(ref, rows):
    w = jnp.concatenate([ref[pl.ds(s, rows, stride=ROW_SLABS), :] for s in range(ROW_SLABS)], axis=1)
    lo = lax.bitcast_convert_type(lax.shift_left(w, jnp.uint32(16)), F32)
    hi = lax.bitcast_convert_type(w & jnp.uint32(0xFFFF0000), F32)
    return jnp.concatenate([lo, hi], axis=1)


def _store_row_tiles(ref, x):
    half = D_MODEL // 2
    lo = lax.bitcast_convert_type(x[:, :half].astype(BF16).astype(F32), jnp.uint32)
    hi = lax.bitcast_convert_type(x[:, half:].astype(BF16).astype(F32), jnp.uint32)
    w = lax.shift_right_logical(lo, jnp.uint32(16)) | hi
    for s in range(ROW_SLABS):
        ref[pl.ds(s, x.shape[0], stride=ROW_SLABS), :] = w[:, s * 128:(s + 1) * 128]


def _row_tile(ref, t):
    return ref.at[pl.ds(pl.multiple_of(t * ROW_SLABS, ROW_SLABS), ROW_SLABS), :]


def _t5_bucket(rel):
    nb = N_BUCKETS // 2
    max_exact = nb // 2
    ret = (rel > 0).astype(jnp.int32) * nb
    n = jnp.abs(rel)
    nf = jnp.maximum(n, 1).astype(F32)
    large = max_exact + (jnp.log(nf / max_exact) / math.log(MAX_DISTANCE / max_exact)
                         * (nb - max_exact)).astype(jnp.int32)
    large = jnp.minimum(large, nb - 1)
    return ret + jnp.where(n < max_exact, n, large)


def _ln0_inproj_kernel(x_ref, g_ref, b_ref, wqk_ref, wvt_ref, wfm_ref,
                       h_ref, q_ref, k_ref, vt_ref, f_ref, mq_ref):
    h = _layer_norm(x_ref[...], g_ref[...], b_ref[...])
    h_ref[...] = h
    hb = h.astype(BF16)
    qk = jnp.dot(hb, wqk_ref[...], preferred_element_type=F32)
    q_ref[...] = (qk[:, :DIFF_WIDTH] * (DIFF_HEAD_DIM ** -0.5 * LOG2E)).astype(BF16)
    k_ref[...] = qk[:, DIFF_WIDTH:].astype(BF16)
    vt = lax.dot_general(wvt_ref[...], hb, _NT, preferred_element_type=F32)
    ones = jnp.ones((VT_ROWS - DIFF_VDIM, ROW_TILE), BF16)
    for hh in range(DIFF_HEADS):
        vt_ref[0, hh, :DIFF_VDIM, :] = vt[hh * DIFF_VDIM:(hh + 1) * DIFF_VDIM, :].astype(BF16)
        vt_ref[0, hh, DIFF_VDIM:, :] = ones
    fm = jnp.dot(hb, wfm_ref[...], preferred_element_type=F32)
    f_ref[...] = fm[:, :FOURIER_WIDTH].astype(BF16)
    mq_ref[...] = (fm[:, FOURIER_WIDTH:] * MEM_HEAD_DIM ** -0.5).astype(BF16)


def _ln0_inproj(x2, ln_g, ln_b, w_qk, w_vt, w_fm):
    rt = SEQ // ROW_TILE
    row = lambda r: (r, 0)
    const = lambda r: (0, 0)
    return pl.pallas_call(
        _ln0_inproj_kernel,
        grid=(N_TOK // ROW_TILE,),
        in_specs=[
            pl.BlockSpec((ROW_TILE, D_MODEL), row),
            pl.BlockSpec((1, D_MODEL), const),
            pl.BlockSpec((1, D_MODEL), const),
            pl.BlockSpec((D_MODEL, 2 * DIFF_WIDTH), const),
            pl.BlockSpec((DIFF_WIDTH, D_MODEL), const),
            pl.BlockSpec((D_MODEL, FOURIER_WIDTH + MEM_WIDTH), const),
        ],
        out_specs=[
            pl.BlockSpec((ROW_TILE, D_MODEL), row),
            pl.BlockSpec((ROW_TILE, DIFF_WIDTH), row),
            pl.BlockSpec((ROW_TILE, DIFF_WIDTH), row),
            pl.BlockSpec((1, DIFF_HEADS, VT_ROWS, ROW_TILE), lambda r: (r // rt, 0, 0, r % rt)),
            pl.BlockSpec((ROW_TILE, FOURIER_WIDTH), row),
            pl.BlockSpec((ROW_TILE, MEM_WIDTH), row),
        ],
        out_shape=[
            jax.ShapeDtypeStruct((N_TOK, D_MODEL), F32),
            jax.ShapeDtypeStruct((N_TOK, DIFF_WIDTH), BF16),
            jax.ShapeDtypeStruct((N_TOK, DIFF_WIDTH), BF16),
            jax.ShapeDtypeStruct((BATCH, DIFF_HEADS, VT_ROWS, SEQ), BF16),
            jax.ShapeDtypeStruct((N_TOK, FOURIER_WIDTH), BF16),
            jax.ShapeDtypeStruct((N_TOK, MEM_WIDTH), BF16),
        ],
        compiler_params=_vmem(48 << 20),
        name="ln0_inproj",
    )(x2, ln_g, ln_b, w_qk, w_vt, w_fm)


def _diff_attn_kernel(tbl_ref, q_ref, k_ref, vt_ref, bkt_ref, lam_ref, g_ref, o_ref,
                      bias_scr, p0_scr, p1_scr):
    T = ATT_TILE
    TQ = ATT_Q
    r = TQ // T
    h = pl.program_id(0)
    i = pl.program_id(2)
    p_scr = (p0_scr, p1_scr)

    @pl.when((pl.program_id(1) == 0) & (i == 0))
    def _build_bias():
        for d in range(r + 4):
            far = d in (0, r + 3)
            bk = bkt_ref[d, 0:8, :] if far else bkt_ref[d]
            bias = jnp.zeros(bk.shape, F32)
            for n in range(N_BUCKETS):
                bias = jnp.where(bk == n, tbl_ref[n, h], bias)
            bias_scr[d] = (jnp.broadcast_to(bias[0:1, :], (T, TQ)) if far else bias) * LOG2E

    q = q_ref[...]
    lane = lax.broadcasted_iota(jnp.int32, q.shape, 1)
    zero = jnp.zeros_like(q)
    q_comp = (jnp.where(lane < DIFF_HEAD_DIM, q, zero), jnp.where(lane >= DIFF_HEAD_DIM, q, zero))

    def bias_tile(j):
        return bias_scr[jnp.clip(j - r * i, -2, r + 1) + 2]

    SUB = ATT_SUB
    n_sub = SEQ // SUB
    m_tile = [[None] * n_sub for _ in range(2)]
    for c in range(2):
        s = lax.dot_general(k_ref[...], q_comp[c], _NT, preferred_element_type=F32)
        for u in range(n_sub):
            if SUB >= T:
                bias = jnp.concatenate([bias_tile(u * SUB // T + t) for t in range(SUB // T)], axis=0)
            else:
                bias = bias_tile(u * SUB // T)[(u * SUB) % T:(u * SUB) % T + SUB, :]
            sb = s[u * SUB:(u + 1) * SUB, :] + bias
            mu = jnp.max(sb, axis=0, keepdims=True)
            p_scr[c][u * SUB:(u + 1) * SUB, :] = jnp.exp2((sb - mu).astype(BF16))
            m_tile[c][u] = mu

    acc = []
    for c in range(2):
        m = functools.reduce(jnp.maximum, m_tile[c])
        a = None
        for u in range(n_sub):
            pv = jnp.dot(vt_ref[0, 0, :, u * SUB:(u + 1) * SUB], p_scr[c][u * SUB:(u + 1) * SUB, :],
                         preferred_element_type=F32)
            pv = pv * jnp.exp2(m_tile[c][u] - m)
            a = pv if a is None else a + pv
        acc.append(a)
    num = [a[:DIFF_VDIM] for a in acc]
    den = [a[DIFF_VDIM:DIFF_VDIM + 1] for a in acc]

    lam = (jnp.exp(jnp.sum(lam_ref[0:1, :] * lam_ref[1:2, :], axis=1, keepdims=True))
           - jnp.exp(jnp.sum(lam_ref[2:3, :] * lam_ref[3:4, :], axis=1, keepdims=True)) + LAM_INIT)
    o = num[0] * (1.0 / den[0]) - lam * (num[1] * (1.0 / den[1]))
    ms = jnp.mean(o * o, axis=0, keepdims=True)
    o = o * lax.rsqrt(ms + LN_EPS) * g_ref[...] * (1.0 - LAM_INIT)
    o_ref[...] = o.T.astype(BF16)


def _diff_attention(rel_bias, q, k, vt, buckets, lam_vecs, subln_col):
    T = ATT_TILE
    TQ = ATT_Q
    nq = SEQ // TQ
    n_bias = TQ // T + 4
    return pl.pallas_call(
        _diff_attn_kernel,
        grid=(DIFF_HEADS, BATCH, nq),
        in_specs=[
            pl.BlockSpec(memory_space=pltpu.SMEM),
            pl.BlockSpec((TQ, DIFF_VDIM), lambda h, b, i: (b * nq + i, h)),
            pl.BlockSpec((SEQ, DIFF_VDIM), lambda h, b, i: (b, h)),
            pl.BlockSpec((1, 1, VT_ROWS, SEQ), lambda h, b, i: (b, h, 0, 0)),
            pl.BlockSpec((n_bias, T, TQ), lambda h, b, i: (0, 0, 0)),
            pl.BlockSpec((4, DIFF_HEAD_DIM), lambda h, b, i: (0, 0)),
            pl.BlockSpec((DIFF_VDIM, 1), lambda h, b, i: (0, 0)),
        ],
        out_specs=pl.BlockSpec((TQ, DIFF_VDIM), lambda h, b, i: (b * nq + i, h)),
        out_shape=jax.ShapeDtypeStruct((N_TOK, DIFF_WIDTH), BF16),
        scratch_shapes=[pltpu.VMEM((n_bias, T, TQ), F32),
                        pltpu.VMEM((SEQ, TQ), BF16), pltpu.VMEM((SEQ, TQ), BF16)],
        compiler_params=_vmem(40 << 20),
        name="diff_attn",
    )(rel_bias, q, k, vt, buckets, lam_vecs, subln_col)


def _fourier_kernel(f_ref, bdc_ref, bds_ref, bdw_ref, m1_ref, m2_ref, twc_ref, tws_ref, o_ref,
                    zr_scr, zi_scr, d_scr, a_scr):
    R = FFT_RADIX
    W = FOURIER_WIDTH
    u = f_ref[...]
    w = bdw_ref[...]
    pc = jnp.dot(u, bdc_ref[...], preferred_element_type=F32).astype(BF16)
    ps = jnp.dot(u, bds_ref[...], preferred_element_type=F32).astype(BF16)
    zr = jnp.dot(pc, w, preferred_element_type=F32)
    zi = jnp.dot(ps, w, preferred_element_type=F32)
    P = FFT_PITCH
    halves = range(W // 128)
    for hh in halves:
        for n1 in range(R):
            zr_scr[hh, n1 * P:n1 * P + R, :] = zr[n1 * R:(n1 + 1) * R, hh * 128:(hh + 1) * 128]
            zi_scr[hh, n1 * P:n1 * P + R, :] = zi[n1 * R:(n1 + 1) * R, hh * 128:(hh + 1) * 128]

    for n2 in range(R):
        for hh in halves:
            cols = slice(n2 * W + hh * 128, n2 * W + (hh + 1) * 128)
            d_scr[0:R, cols] = zr_scr[hh, pl.ds(n2, R, stride=P), :].astype(BF16)
            d_scr[R:2 * R, cols] = zi_scr[hh, pl.ds(n2, R, stride=P), :].astype(BF16)
    a_scr[...] = jnp.dot(m1_ref[...], d_scr[...], preferred_element_type=F32)

    for n2 in range(R):
        tc = twc_ref[n2]
        ts = tws_ref[n2]
        for hh in halves:
            cols = slice(n2 * W + hh * 128, n2 * W + (hh + 1) * 128)
            ar = a_scr[0:R, cols]
            ai = a_scr[R:2 * R, cols]
            zr_scr[hh, pl.ds(n2, R, stride=P), :] = ar * tc + ai * ts
            zi_scr[hh, pl.ds(n2, R, stride=P), :] = ai * tc - ar * ts
    for k1 in range(R):
        for hh in halves:
            cols = slice(k1 * W + hh * 128, k1 * W + (hh + 1) * 128)
            d_scr[0:R, cols] = zr_scr[hh, k1 * P:k1 * P + R, :].astype(BF16)
            d_scr[R:2 * R, cols] = zi_scr[hh, k1 * P:k1 * P + R, :].astype(BF16)
    a_scr[0:R, :] = jnp.dot(m2_ref[...], d_scr[...], preferred_element_type=F32)
    for k1 in range(R):
        for hh in halves:
            zr_scr[hh, pl.ds(k1, R, stride=P), :] = a_scr[0:R, k1 * W + hh * 128:k1 * W + (hh + 1) * 128]
    for k2 in range(R):
        for hh in halves:
            o_ref[hh, k2 * R:(k2 + 1) * R, :] = zr_scr[hh, k2 * P:k2 * P + R, :]


def _fourier(f_in, bdc, bds, bdw, m1, m2, twc, tws):
    R = FFT_RADIX
    W = FOURIER_WIDTH
    const2 = lambda b: (0, 0)
    const3 = lambda b: (0, 0, 0)
    return pl.pallas_call(
        _fourier_kernel,
        grid=(BATCH,),
        in_specs=[
            pl.BlockSpec((SEQ, W), lambda b: (b, 0)),
            pl.BlockSpec((W, W), const2),
            pl.BlockSpec((W, W), const2),
            pl.BlockSpec((W, W), const2),
            pl.BlockSpec((2 * R, 2 * R), const2),
            pl.BlockSpec((R, 2 * R), const2),
            pl.BlockSpec((R, R, 128), const3),
            pl.BlockSpec((R, R, 128), const3),
        ],
        out_specs=pl.BlockSpec((W // 128, SEQ, 128), lambda b: (0, b, 0)),
        out_shape=jax.ShapeDtypeStruct((W // 128, N_TOK, 128), F32),
        scratch_shapes=[pltpu.VMEM((W // 128, R * FFT_PITCH, 128), F32),
                        pltpu.VMEM((W // 128, R * FFT_PITCH, 128), F32),
                        pltpu.VMEM((2 * R, R * W), BF16), pltpu.VMEM((2 * R, R * W), F32)],
        compiler_params=_vmem(52 << 20),
        name="fourier",
    )(f_in, bdc, bds, bdw, m1, m2, twc, tws)


@functools.lru_cache(maxsize=None)
def _dft_constants():
    R = FFT_RADIX
    a = np.arange(R, dtype=np.int64)
    ang = 2.0 * np.pi * ((a[:, None] * a[None, :]) % R).astype(np.float64) / R
    c64, s64 = np.cos(ang), np.sin(ang)
    scale = 1.0 / math.sqrt(SEQ * FOURIER_GROUP_DIM)
    m1 = np.block([[c64, s64], [-s64, c64]])
    m2 = np.concatenate([c64, s64], axis=1) * scale
    tang = 2.0 * np.pi * (a[:, None] * a[None, :]).astype(np.float64) / SEQ
    twc = np.repeat(np.cos(tang)[:, :, None], 128, axis=2).astype(np.float32)
    tws = np.repeat(np.sin(tang)[:, :, None], 128, axis=2).astype(np.float32)
    assert FOURIER_GROUP_DIM == R
    eye = np.eye(FOURIER_GROUPS)
    bdc = np.kron(eye, c64)
    bds = np.kron(eye, -s64)
    return bdc.astype(BF16), bds.astype(BF16), m1.astype(BF16), m2.astype(BF16), twc, tws


def _mem_kv_kernel(mem_ref, w_ref, kcat_ref, vcat_ref):
    kv = jnp.dot(mem_ref[...].astype(BF16), w_ref[...], preferred_element_type=F32)
    mk = kv[:, :MEM_WIDTH]
    mv = kv[:, MEM_WIDTH:]
    lane = lax.broadcasted_iota(jnp.int32, mk.shape, 1)
    for hh in range(MEM_HEADS):
        sel = (lane >= hh * MEM_HEAD_DIM) & (lane < (hh + 1) * MEM_HEAD_DIM)
        kcat_ref[0, hh * MEM_LEN:(hh + 1) * MEM_LEN, :] = jnp.where(sel, mk, 0.0).astype(BF16)
        vcat_ref[0, hh * MEM_LEN:(hh + 1) * MEM_LEN, :] = jnp.where(sel, mv, 0.0).astype(BF16)


def _mem_kv(mem2, w_kv):
    return pl.pallas_call(
        _mem_kv_kernel,
        grid=(BATCH,),
        in_specs=[
            pl.BlockSpec((MEM_LEN, D_MODEL), lambda b: (b, 0)),
            pl.BlockSpec((D_MODEL, 2 * MEM_WIDTH), lambda b: (0, 0)),
        ],
        out_specs=[
            pl.BlockSpec((1, MEM_HEADS * MEM_LEN, MEM_WIDTH), lambda b: (b, 0, 0)),
            pl.BlockSpec((1, MEM_HEADS * MEM_LEN, MEM_WIDTH), lambda b: (b, 0, 0)),
        ],
        out_shape=[
            jax.ShapeDtypeStruct((BATCH, MEM_HEADS * MEM_LEN, MEM_WIDTH), BF16),
            jax.ShapeDtypeStruct((BATCH, MEM_HEADS * MEM_LEN, MEM_WIDTH), BF16),
        ],
        name="mem_kv",
    )(mem2, w_kv)


def _mem_attn_kernel(mq_ref, kcat_ref, vcat_ref, o_ref):
    s = lax.dot_general(mq_ref[...], kcat_ref[0], _NT, preferred_element_type=F32)
    parts = []
    for hh in range(MEM_HEADS):
        sh = s[:, hh * MEM_LEN:(hh + 1) * MEM_LEN]
        p = jnp.exp(sh - jnp.max(sh, axis=1, keepdims=True))
        parts.append((p * (1.0 / jnp.sum(p, axis=1, keepdims=True))).astype(BF16))
    p_all = jnp.concatenate(parts, axis=1)
    o_ref[...] = jnp.dot(p_all, vcat_ref[0], preferred_element_type=F32).astype(BF16)


def _mem_attention(mq, kcat, vcat):
    rt = SEQ // ROW_TILE
    return pl.pallas_call(
        _mem_attn_kernel,
        grid=(N_TOK // ROW_TILE,),
        in_specs=[
            pl.BlockSpec((ROW_TILE, MEM_WIDTH), lambda r: (r, 0)),
            pl.BlockSpec((1, MEM_HEADS * MEM_LEN, MEM_WIDTH), lambda r: (r // rt, 0, 0)),
            pl.BlockSpec((1, MEM_HEADS * MEM_LEN, MEM_WIDTH), lambda r: (r // rt, 0, 0)),
        ],
        out_specs=pl.BlockSpec((ROW_TILE, MEM_WIDTH), lambda r: (r, 0)),
        out_shape=jax.ShapeDtypeStruct((N_TOK, MEM_WIDTH), BF16),
        name="mem_attn",
    )(mq, kcat, vcat)


def _route(logits):
    lane = lax.broadcasted_iota(jnp.int32, logits.shape, 1)
    big = jnp.int32(ROUTER_LANES)
    is_group = (lane >= GROUP_LANE0) & (lane < GROUP_LANE0 + N_GROUPS)
    gl = jnp.where(is_group, logits, -jnp.inf)
    gmax = jnp.max(gl, axis=1, keepdims=True)
    g_sel = jnp.min(jnp.where(gl == gmax, lane, big), axis=1, keepdims=True) - GROUP_LANE0
    g_gate = 1.0 / jnp.sum(jnp.where(is_group, jnp.exp(gl - gmax), 0.0), axis=1, keepdims=True)
    in_group = (lane >= g_sel * EXPERTS_PER_GROUP) & (lane < (g_sel + 1) * EXPERTS_PER_GROUP)
    el = jnp.where(in_group, logits, -jnp.inf)
    v1 = jnp.max(el, axis=1, keepdims=True)
    i1 = jnp.min(jnp.where(el == v1, lane, big), axis=1, keepdims=True)
    el2 = jnp.where(lane == i1, -jnp.inf, el)
    v2 = jnp.max(el2, axis=1, keepdims=True)
    i2 = jnp.min(jnp.where(el2 == v2, lane, big), axis=1, keepdims=True)
    e = jnp.exp(v2 - v1)
    w_first = g_gate / (1.0 + e)
    w_second = g_gate * e / (1.0 + e)
    return i1, i2, w_first, w_second


def _outproj_router_kernel(od_ref, of_ref, om_ref, h_ref, wo_ref, g_ref, b_ref,
                           wr_ref, br_ref,
                           h1_ref, h1_rows_ref, route_ref, w0_ref, w1_ref, cnt_ref, cnt_scr):
    @pl.when(pl.program_id(0) == 0)
    def _():
        cnt_scr[...] = jnp.zeros_like(cnt_scr)

    o_four = [of_ref[hh].astype(BF16) for hh in range(FOURIER_WIDTH // 128)]
    o = jnp.concatenate([od_ref[...]] + o_four + [om_ref[...]], axis=1)
    a = jnp.dot(o, wo_ref[...], preferred_element_type=F32)
    h1 = _layer_norm(DEEPNORM_ALPHA * h_ref[...] + a, g_ref[...], b_ref[...])
    h1_ref[...] = h1
    _store_row_tiles(h1_rows_ref, h1)
    hi = h1.astype(BF16)
    lo = (h1 - hi.astype(F32)).astype(BF16)
    hw = jnp.dot(hi, wr_ref[...], preferred_element_type=F32)
    logits = (hw[:, :ROUTER_LANES] + hw[:, ROUTER_LANES:]
              + jnp.dot(lo, wr_ref[:, :ROUTER_LANES], preferred_element_type=F32) + br_ref[...])
    i1, i2, w_first, w_second = _route(logits)

    lane = lax.broadcasted_iota(jnp.int32, logits.shape, 1)
    onehot = jnp.where(lane == i1, 1.0, jnp.where(lane == i2, 1.0, 0.0))
    r_id = lax.broadcasted_iota(jnp.int32, (ROW_TILE, ROW_TILE), 0)
    c_id = lax.broadcasted_iota(jnp.int32, (ROW_TILE, ROW_TILE), 1)
    tri = jnp.where(r_id > c_id, 1.0, 0.0).astype(BF16)
    before = jnp.dot(tri, onehot.astype(BF16), preferred_element_type=F32) + cnt_scr[...]
    rank1 = jnp.sum(jnp.where(lane == i1, before, 0.0), axis=1, keepdims=True)
    rank2 = jnp.sum(jnp.where(lane == i2, before, 0.0), axis=1, keepdims=True)
    cnt_scr[...] += jnp.sum(onehot, axis=0, keepdims=True)
    cnt_ref[...] = cnt_scr[...]

    packed = jnp.where(lane == 0, i1.astype(F32),
                       jnp.where(lane == 1, i2.astype(F32),
                                 jnp.where(lane == 2, rank1, jnp.where(lane == 3, rank2, 0.0))))
    route_ref[...] = packed.T[:8, :].astype(jnp.int32)
    w0_ref[...] = jnp.broadcast_to(w_first, (ROW_TILE, 128))
    w1_ref[...] = jnp.broadcast_to(w_second, (ROW_TILE, 128))


def _outproj_router(o_diff, o_four, o_mem, h, w_out, ln_g, ln_b, w_r, b_r):
    row = lambda r: (r, 0)
    const = lambda r: (0, 0)
    return pl.pallas_call(
        _outproj_router_kernel,
        grid=(N_TOK // ROW_TILE,),
        in_specs=[
            pl.BlockSpec((ROW_TILE, DIFF_WIDTH), row),
            pl.BlockSpec((FOURIER_WIDTH // 128, ROW_TILE, 128), lambda r: (0, r, 0)),
            pl.BlockSpec((ROW_TILE, MEM_WIDTH), row),
            pl.BlockSpec((ROW_TILE, D_MODEL), row),
            pl.BlockSpec((D_MODEL, D_MODEL), const),
            pl.BlockSpec((1, D_MODEL), const),
            pl.BlockSpec((1, D_MODEL), const),
            pl.BlockSpec((D_MODEL, 2 * ROUTER_LANES), const),
            pl.BlockSpec((1, ROUTER_LANES), const),
        ],
        out_specs=[
            pl.BlockSpec((ROW_TILE, D_MODEL), row),
            pl.BlockSpec((ROW_TILE * ROW_SLABS, 128), row),
            pl.BlockSpec((8, ROW_TILE), lambda r: (0, r)),
            pl.BlockSpec((ROW_TILE, 128), row),
            pl.BlockSpec((ROW_TILE, 128), row),
            pl.BlockSpec((1, ROUTER_LANES), const),
        ],
        out_shape=[
            jax.ShapeDtypeStruct((N_TOK, D_MODEL), F32),
            jax.ShapeDtypeStruct((N_TOK * ROW_SLABS, 128), U32),
            jax.ShapeDtypeStruct((8, N_TOK), jnp.int32),
            jax.ShapeDtypeStruct((N_TOK, 128), F32),
            jax.ShapeDtypeStruct((N_TOK, 128), F32),
            jax.ShapeDtypeStruct((1, ROUTER_LANES), F32),
        ],
        scratch_shapes=[pltpu.VMEM((1, ROUTER_LANES), F32)],
        compiler_params=_vmem(40 << 20),
        name="outproj_router",
    )(o_diff, o_four, o_mem, h, w_out, ln_g, ln_b, w_r, b_r)


def _dispatch_kernel(pos0_ref, pos1_ref, x_ref, xs_ref, sem):
    base = pl.program_id(0) * DISPATCH_TILE

    def issue(t, carry):
        src = _row_tile(x_ref, t)
        pltpu.make_async_copy(src, _row_tile(xs_ref, pos0_ref[base + t]), sem).start(priority=0)
        pltpu.make_async_copy(src, _row_tile(xs_ref, pos1_ref[base + t]), sem).start(priority=1)
        return carry

    lax.fori_loop(0, DISPATCH_TILE, issue, 0, unroll=8)
    for _ in range(2):
        pltpu.make_async_copy(x_ref, xs_ref.at[pl.ds(0, DISPATCH_TILE * ROW_SLABS), :], sem).wait()


def _dispatch(pos0, pos1, h1_rows):
    return pl.pallas_call(
        _dispatch_kernel,
        grid_spec=pltpu.PrefetchScalarGridSpec(
            num_scalar_prefetch=2,
            grid=(N_TOK // DISPATCH_TILE,),
            in_specs=[pl.BlockSpec((DISPATCH_TILE * ROW_SLABS, 128), lambda i, p0, p1: (i, 0))],
            out_specs=pl.BlockSpec(memory_space=pl.ANY),
            scratch_shapes=[pltpu.SemaphoreType.DMA(())],
        ),
        out_shape=jax.ShapeDtypeStruct((N_SLOTS * ROW_SLABS, 128), U32),
        name="moe_dispatch",
    )(pos0, pos1, h1_rows)


def _expert_kernel(tile_end_ref, xs_ref, w1_ref, w3_ref, w2_ref, ys_ref):
    @pl.when(pl.program_id(0) < tile_end_ref[N_EXPERTS - 1])
    def _():
        x = _load_row_tiles(xs_ref, MOE_TILE).astype(BF16)
        a = jnp.dot(x, w1_ref[0].astype(BF16), preferred_element_type=F32)
        b = jnp.dot(x, w3_ref[0].astype(BF16), preferred_element_type=F32)
        hid = (a * jax.nn.sigmoid(a) * b).astype(BF16)
        y = jnp.dot(hid, w2_ref[0].astype(BF16), preferred_element_type=F32)
        _store_row_tiles(ys_ref, y)


def _experts(tile_end, xs, w1, w3, w2):
    def tile(t, tile_end):
        return jnp.minimum(t, tile_end[N_EXPERTS - 1] - 1), 0

    def wsel(t, tile_end):
        e = jnp.int32(0)
        step = N_EXPERTS // 2
        while step:
            e = jnp.where(t >= tile_end[e + step - 1], e + step, e)
            step //= 2
        return e, 0, 0

    return pl.pallas_call(
        _expert_kernel,
        grid_spec=pltpu.PrefetchScalarGridSpec(
            num_scalar_prefetch=1,
            grid=(N_MOE_TILES,),
            in_specs=[
                pl.BlockSpec((MOE_TILE * ROW_SLABS, 128), tile),
                pl.BlockSpec((1, D_MODEL, D_EXPERT), wsel),
                pl.BlockSpec((1, D_MODEL, D_EXPERT), wsel),
                pl.BlockSpec((1, D_EXPERT, D_MODEL), wsel),
            ],
            out_specs=pl.BlockSpec((MOE_TILE * ROW_SLABS, 128), tile),
        ),
        out_shape=jax.ShapeDtypeStruct((N_SLOTS * ROW_SLABS, 128), U32),
        compiler_params=_vmem(40 << 20),
        name="moe_experts",
    )(tile_end, xs, w1, w3, w2)


def _combine_kernel(pos0_ref, pos1_ref, ys_ref, h1_ref, w0_ref, w1_ref, g_ref, b_ref, o_ref,
                    y0_buf, y1_buf, sems):
    i = pl.program_id(0)
    n = pl.num_programs(0)

    def gather(tile, slot):
        base = tile * COMBINE_TILE

        def issue(t, carry):
            pltpu.make_async_copy(_row_tile(ys_ref, pos0_ref[base + t]), _row_tile(y0_buf.at[slot], t),
                                  sems.at[slot]).start(priority=0)
            pltpu.make_async_copy(_row_tile(ys_ref, pos1_ref[base + t]), _row_tile(y1_buf.at[slot], t),
                                  sems.at[slot]).start(priority=1)
            return carry

        lax.fori_loop(0, COMBINE_TILE, issue, 0, unroll=8)

    @pl.when(i == 0)
    def _():
        gather(0, 0)

    @pl.when(i + 1 < n)
    def _():
        gather(i + 1, (i + 1) % 2)

    slot = i % 2
    whole = ys_ref.at[pl.ds(0, COMBINE_TILE * ROW_SLABS), :]
    pltpu.make_async_copy(whole, y0_buf.at[slot], sems.at[slot]).wait()
    pltpu.make_async_copy(whole, y1_buf.at[slot], sems.at[slot]).wait()
    w0 = jnp.concatenate([w0_ref[...]] * (D_MODEL // 128), axis=1)
    w1 = jnp.concatenate([w1_ref[...]] * (D_MODEL // 128), axis=1)
    f = (w0 * _load_row_tiles(y0_buf.at[slot], COMBINE_TILE)
         + w1 * _load_row_tiles(y1_buf.at[slot], COMBINE_TILE))
    o_ref[...] = _layer_norm(DEEPNORM_ALPHA * h1_ref[...] + f, g_ref[...], b_ref[...])


def _combine(pos0, pos1, ys, h1, w0, w1, ln_g, ln_b):
    row = lambda i, p0, p1: (i, 0)
    const = lambda i, p0, p1: (0, 0)
    return pl.pallas_call(
        _combine_kernel,
        grid_spec=pltpu.PrefetchScalarGridSpec(
            num_scalar_prefetch=2,
            grid=(N_TOK // COMBINE_TILE,),
            in_specs=[
                pl.BlockSpec(memory_space=pl.ANY),
                pl.BlockSpec((COMBINE_TILE, D_MODEL), row),
                pl.BlockSpec((COMBINE_TILE, 128), row),
                pl.BlockSpec((COMBINE_TILE, 128), row),
                pl.BlockSpec((1, D_MODEL), const),
                pl.BlockSpec((1, D_MODEL), const),
            ],
            out_specs=pl.BlockSpec((COMBINE_TILE, D_MODEL), row),
            scratch_shapes=[pltpu.VMEM((2, COMBINE_TILE * ROW_SLABS, 128), U32),
                            pltpu.VMEM((2, COMBINE_TILE * ROW_SLABS, 128), U32),
                            pltpu.SemaphoreType.DMA((2,))],
        ),
        out_shape=jax.ShapeDtypeStruct((N_TOK, D_MODEL), F32),
        compiler_params=pltpu.CompilerParams(dimension_semantics=("arbitrary",), vmem_limit_bytes=40 << 20),
        name="moe_combine",
    )(pos0, pos1, ys, h1, w0, w1, ln_g, ln_b)


def kernel(x, mem, ln0_g, ln0_b, rel_bias, w_in, w_mem_kv, w_fourier, lambda_q1, lambda_k1, lambda_q2,
           lambda_k2, subln_g, w_out, ln1_g, ln1_b, w_group, b_group, w_router, b_router, w1, w3, w2,
           ln2_g, ln2_b):
    l = 0
    x2 = x.reshape(N_TOK, D_MODEL)
    wi = w_in[l]
    w_qk = wi[:, :2 * DIFF_WIDTH].astype(BF16)
    w_vt = wi[:, 2 * DIFF_WIDTH:3 * DIFF_WIDTH].T.astype(BF16)
    w_fm = wi[:, 3 * DIFF_WIDTH:].astype(BF16)
    h, q, k, vt, f_in, mq = _ln0_inproj(x2, ln0_g.reshape(1, -1), ln0_b.reshape(1, -1), w_qk, w_vt, w_fm)

    T = ATT_TILE
    kk = jnp.arange(T, dtype=jnp.int32)[:, None]
    qq = jnp.arange(ATT_Q, dtype=jnp.int32)[None, :]
    buckets = jnp.stack([_t5_bucket(d * T + kk - qq) for d in range(-2, ATT_Q // T + 2)])
    lam_vecs = jnp.stack([lambda_q1[l], lambda_k1[l], lambda_q2[l], lambda_k2[l]]).astype(F32)
    o_diff = _diff_attention(rel_bias.astype(F32), q, k, vt, buckets, lam_vecs,
                             subln_g[l].astype(F32).reshape(DIFF_VDIM, 1))

    bdc, bds, m1, m2, twc, tws = (jnp.asarray(c) for c in _dft_constants())
    wf = w_fourier[l]
    bdw = jnp.zeros((FOURIER_WIDTH, FOURIER_WIDTH), F32)
    for g in range(FOURIER_GROUPS):
        sl = slice(g * FOURIER_GROUP_DIM, (g + 1) * FOURIER_GROUP_DIM)
        bdw = bdw.at[sl, sl].set(wf[g])
    o_four = _fourier(f_in, bdc, bds, bdw.astype(BF16), m1, m2, twc, tws)

    kcat, vcat = _mem_kv(mem.reshape(BATCH * MEM_LEN, D_MODEL), w_mem_kv[l].astype(BF16))
    o_mem = _mem_attention(mq, kcat, vcat)

    w_r = jnp.zeros((D_MODEL, ROUTER_LANES), F32)
    w_r = w_r.at[:, :N_EXPERTS].set(w_router[l].astype(F32))
    w_r = w_r.at[:, GROUP_LANE0:GROUP_LANE0 + N_GROUPS].set(w_group[l].astype(F32))
    wr_hi = w_r.astype(BF16)
    wr_lo = (w_r - wr_hi.astype(F32)).astype(BF16)
    b_r = jnp.zeros((1, ROUTER_LANES), F32)
    b_r = b_r.at[0, :N_EXPERTS].set(b_router[l].astype(F32))
    b_r = b_r.at[0, GROUP_LANE0:GROUP_LANE0 + N_GROUPS].set(b_group[l].astype(F32))
    h1, h1_rows, route, gate0, gate1, counts = _outproj_router(
        o_diff, o_four, o_mem, h, w_out[l].astype(BF16),
        ln1_g[l].reshape(1, -1), ln1_b[l].reshape(1, -1), jnp.concatenate([wr_hi, wr_lo], axis=1), b_r)

    cnt = counts[0, :N_EXPERTS].astype(jnp.int32)
    tiles_per_expert = (cnt + MOE_TILE - 1) // MOE_TILE
    tile_end = jnp.cumsum(tiles_per_expert).astype(jnp.int32)
    row_start = (tile_end - tiles_per_expert) * MOE_TILE
    expert_ids = jnp.arange(N_EXPERTS, dtype=jnp.int32)[None, :]

    def slot_of(expert, rank):
        return jnp.sum(jnp.where(expert[:, None] == expert_ids, row_start[None, :], 0), axis=1) + rank

    pos0 = slot_of(route[0], route[2])
    pos1 = slot_of(route[1], route[3])

    xs = _dispatch(pos0, pos1, h1_rows)
    ys = _experts(tile_end, xs, w1[l], w3[l], w2[l])
    out = _combine(pos0, pos1, ys, h1, gate0, gate1, ln2_g[l].reshape(1, -1), ln2_b[l].reshape(1, -1))
    return out.reshape(BATCH, SEQ, D_MODEL)
```

```python
import functools
import math

import numpy as np
import jax
import jax.numpy as jnp
from jax import lax
from jax.experimental import pallas as pl
from jax.experimental.pallas import tpu as pltpu

F32 = jnp.float32
BF16 = jnp.bfloat16

D_MODEL = 1024
BATCH = 4
SEQ = 4096
N_TOK = BATCH * SEQ
MEM_LEN = 256
MEM_HEADS = 4
MEM_HEAD_DIM = 64
MEM_WIDTH = 256
DIFF_HEADS = 4
DIFF_HEAD_DIM = 64
DIFF_VDIM = 2 * DIFF_HEAD_DIM
VT_ROWS = DIFF_VDIM + 16
DIFF_WIDTH = 512
FOURIER_GROUPS = 4
FOURIER_GROUP_DIM = 64
FOURIER_WIDTH = 256
N_BUCKETS = 32
MAX_DISTANCE = 128
N_GROUPS = 4
EXPERTS_PER_GROUP = 8
N_EXPERTS = 32
D_EXPERT = 256
LN_EPS = 1e-5
DEEPNORM_ALPHA = 2.0 ** 0.25
LAM_INIT = 0.8 - 0.6 * math.exp(-0.3 * 0)
LOG2E = 1.4426950408889634

ROW_TILE = 1024
ATT_TILE = 256
ATT_Q = 512
ATT_SUB = 256
MOE_TILE = 512
N_MOE_TILES = 2 * N_TOK // MOE_TILE + N_EXPERTS
N_SLOTS = N_MOE_TILES * MOE_TILE
ROW_SLABS = D_MODEL // 256
U32 = jnp.uint32
DISPATCH_TILE = 4096
COMBINE_TILE = 512
FFT_RADIX = 64
FFT_PITCH = 72
ROUTER_LANES = 128
GROUP_LANE0 = 32

_NT = (((1,), (1,)), ((), ()))


def _vmem(nbytes):
    return pltpu.CompilerParams(vmem_limit_bytes=int(nbytes))


def _layer_norm(x, g, b):
    mu = jnp.mean(x, axis=-1, keepdims=True)
    xc = x - mu
    var = jnp.mean(xc * xc, axis=-1, keepdims=True)
    return xc * lax.rsqrt(var + LN_EPS) * g + b


def _load_row_tiles(ref, rows):
    w = jnp.concatenate([ref[pl.ds(s, rows, stride=ROW_SLABS), :] for s in range(ROW_SLABS)], axis=1)
    lo = lax.bitcast_convert_type(lax.shift_left(w, jnp.uint32(16)), F32)
    hi = lax.bitcast_convert_type(w & jnp.uint32(0xFFFF0000), F32)
    return jnp.concatenate([lo, hi], axis=1)


def _store_row_tiles(ref, x):
    half = D_MODEL // 2
    lo = lax.bitcast_convert_type(x[:, :half].astype(BF16).astype(F32), jnp.uint32)
    hi = lax.bitcast_convert_type(x[:, half:].astype(BF16).astype(F32), jnp.uint32)
    w = lax.shift_right_logical(lo, jnp.uint32(16)) | hi
    for s in range(ROW_SLABS):
        ref[pl.ds(s, x.shape[0], stride=ROW_SLABS), :] = w[:, s * 128:(s + 1) * 128]


def _row_tile(ref, t):
    return ref.at[pl.ds(pl.multiple_of(t * ROW_SLABS, ROW_SLABS), ROW_SLABS), :]


def _t5_bucket(rel):
    nb = N_BUCKETS // 2
    max_exact = nb // 2
    ret = (rel > 0).astype(jnp.int32) * nb
    n = jnp.abs(rel)
    nf = jnp.maximum(n, 1).astype(F32)
    large = max_exact + (jnp.log(nf / max_exact) / math.log(MAX_DISTANCE / max_exact)
                         * (nb - max_exact)).astype(jnp.int32)
    large = jnp.minimum(large, nb - 1)
    return ret + jnp.where(n < max_exact, n, large)


def _ln0_inproj_kernel(x_ref, g_ref, b_ref, wqk_ref, wvt_ref, wfm_ref,
                       h_ref, q_ref, k_ref, vt_ref, f_ref, mq_ref):
    h = _layer_norm(x_ref[...], g_ref[...], b_ref[...])
    h_ref[...] = h
    hb = h.astype(BF16)
    qk = jnp.dot(hb, wqk_ref[...], preferred_element_type=F32)
    q_ref[...] = (qk[:, :DIFF_WIDTH] * (DIFF_HEAD_DIM ** -0.5 * LOG2E)).astype(BF16)
    k_ref[...] = qk[:, DIFF_WIDTH:].astype(BF16)
    vt = lax.dot_general(wvt_ref[...], hb, _NT, preferred_element_type=F32)
    ones = jnp.ones((VT_ROWS - DIFF_VDIM, ROW_TILE), BF16)
    for hh in range(DIFF_HEADS):
        vt_ref[0, hh, :DIFF_VDIM, :] = vt[hh * DIFF_VDIM:(hh + 1) * DIFF_VDIM, :].astype(BF16)
        vt_ref[0, hh, DIFF_VDIM:, :] = ones
    fm = jnp.dot(hb, wfm_ref[...], preferred_element_type=F32)
    f_ref[...] = fm[:, :FOURIER_WIDTH].astype(BF16)
    mq_ref[...] = (fm[:, FOURIER_WIDTH:] * MEM_HEAD_DIM ** -0.5).astype(BF16)


def _ln0_inproj(x2, ln_g, ln_b, w_qk, w_vt, w_fm):
    rt = SEQ // ROW_TILE
    row = lambda r: (r, 0)
    const = lambda r: (0, 0)
    return pl.pallas_call(
        _ln0_inproj_kernel,
        grid=(N_TOK // ROW_TILE,),
        in_specs=[
            pl.BlockSpec((ROW_TILE, D_MODEL), row),
            pl.BlockSpec((1, D_MODEL), const),
            pl.BlockSpec((1, D_MODEL), const),
            pl.BlockSpec((D_MODEL, 2 * DIFF_WIDTH), const),
            pl.BlockSpec((DIFF_WIDTH, D_MODEL), const),
            pl.BlockSpec((D_MODEL, FOURIER_WIDTH + MEM_WIDTH), const),
        ],
        out_specs=[
            pl.BlockSpec((ROW_TILE, D_MODEL), row),
            pl.BlockSpec((ROW_TILE, DIFF_WIDTH), row),
            pl.BlockSpec((ROW_TILE, DIFF_WIDTH), row),
            pl.BlockSpec((1, DIFF_HEADS, VT_ROWS, ROW_TILE), lambda r: (r // rt, 0, 0, r % rt)),
            pl.BlockSpec((ROW_TILE, FOURIER_WIDTH), row),
            pl.BlockSpec((ROW_TILE, MEM_WIDTH), row),
        ],
        out_shape=[
            jax.ShapeDtypeStruct((N_TOK, D_MODEL), F32),
            jax.ShapeDtypeStruct((N_TOK, DIFF_WIDTH), BF16),
            jax.ShapeDtypeStruct((N_TOK, DIFF_WIDTH), BF16),
            jax.ShapeDtypeStruct((BATCH, DIFF_HEADS, VT_ROWS, SEQ), BF16),
            jax.ShapeDtypeStruct((N_TOK, FOURIER_WIDTH), BF16),
            jax.ShapeDtypeStruct((N_TOK, MEM_WIDTH), BF16),
        ],
        compiler_params=_vmem(48 << 20),
        name="ln0_inproj",
    )(x2, ln_g, ln_b, w_qk, w_vt, w_fm)


def _diff_attn_kernel(tbl_ref, q_ref, k_ref, vt_ref, bkt_ref, lam_ref, g_ref, o_ref,
                      bias_scr, p0_scr, p1_scr):
    T = ATT_TILE
    TQ = ATT_Q
    r = TQ // T
    h = pl.program_id(0)
    i = pl.program_id(2)
    p_scr = (p0_scr, p1_scr)

    @pl.when((pl.program_id(1) == 0) & (i == 0))
    def _build_bias():
        for d in range(r + 4):
            far = d in (0, r + 3)
            bk = bkt_ref[d, 0:8, :] if far else bkt_ref[d]
            bias = jnp.zeros(bk.shape, F32)
            for n in range(N_BUCKETS):
                bias = jnp.where(bk == n, tbl_ref[n, h], bias)
            bias_scr[d] = (jnp.broadcast_to(bias[0:1, :], (T, TQ)) if far else bias) * LOG2E

    q = q_ref[...]
    lane = lax.broadcasted_iota(jnp.int32, q.shape, 1)
    zero = jnp.zeros_like(q)
    q_comp = (jnp.where(lane < DIFF_HEAD_DIM, q, zero), jnp.where(lane >= DIFF_HEAD_DIM, q, zero))

    def bias_tile(j):
        return bias_scr[jnp.clip(j - r * i, -2, r + 1) + 2]

    SUB = ATT_SUB
    n_sub = SEQ // SUB
    m_tile = [[None] * n_sub for _ in range(2)]
    for c in range(2):
        s = lax.dot_general(k_ref[...], q_comp[c], _NT, preferred_element_type=F32)
        for u in range(n_sub):
            if SUB >= T:
                bias = jnp.concatenate([bias_tile(u * SUB // T + t) for t in range(SUB // T)], axis=0)
            else:
                bias = bias_tile(u * SUB // T)[(u * SUB) % T:(u * SUB) % T + SUB, :]
            sb = s[u * SUB:(u + 1) * SUB, :] + bias
            mu = jnp.max(sb, axis=0, keepdims=True)
            p_scr[c][u * SUB:(u + 1) * SUB, :] = jnp.exp2((sb - mu).astype(BF16))
            m_tile[c][u] = mu

    acc = []
    for c in range(2):
        m = functools.reduce(jnp.maximum, m_tile[c])
        a = None
        for u in range(n_sub):
            pv = jnp.dot(vt_ref[0, 0, :, u * SUB:(u + 1) * SUB], p_scr[c][u * SUB:(u + 1) * SUB, :],
                         preferred_element_type=F32)
            pv = pv * jnp.exp2(m_tile[c][u] - m)
            a = pv if a is None else a + pv
        acc.append(a)
    num = [a[:DIFF_VDIM] for a in acc]
    den = [a[DIFF_VDIM:DIFF_VDIM + 1] for a in acc]

    lam = (jnp.exp(jnp.sum(lam_ref[0:1, :] * lam_ref[1:2, :], axis=1, keepdims=True))
           - jnp.exp(jnp.sum(lam_ref[2:3, :] * lam_ref[3:4, :], axis=1, keepdims=True)) + LAM_INIT)
    o = num[0] * (1.0 / den[0]) - lam * (num[1] * (1.0 / den[1]))
    ms = jnp.mean(o * o, axis=0, keepdims=True)
    o = o * lax.rsqrt(ms + LN_EPS) * g_ref[...] * (1.0 - LAM_INIT)
    o_ref[...] = o.T.astype(BF16)


def _diff_attention(rel_bias, q, k, vt, buckets, lam_vecs, subln_col):
    T = ATT_TILE
    TQ = ATT_Q
    nq = SEQ // TQ
    n_bias = TQ // T + 4
    return pl.pallas_call(
        _diff_attn_kernel,
        grid=(DIFF_HEADS, BATCH, nq),
        in_specs=[
            pl.BlockSpec(memory_space=pltpu.SMEM),
            pl.BlockSpec((TQ, DIFF_VDIM), lambda h, b, i: (b * nq + i, h)),
            pl.BlockSpec((SEQ, DIFF_VDIM), lambda h, b, i: (b, h)),
            pl.BlockSpec((1, 1, VT_ROWS, SEQ), lambda h, b, i: (b, h, 0, 0)),
            pl.BlockSpec((n_bias, T, TQ), lambda h, b, i: (0, 0, 0)),
            pl.BlockSpec((4, DIFF_HEAD_DIM), lambda h, b, i: (0, 0)),
            pl.BlockSpec((DIFF_VDIM, 1), lambda h, b, i: (0, 0)),
        ],
        out_specs=pl.BlockSpec((TQ, DIFF_VDIM), lambda h, b, i: (b * nq + i, h)),
        out_shape=jax.ShapeDtypeStruct((N_TOK, DIFF_WIDTH), BF16),
        scratch_shapes=[pltpu.VMEM((n_bias, T, TQ), F32),
                        pltpu.VMEM((SEQ, TQ), BF16), pltpu.VMEM((SEQ, TQ), BF16)],
        compiler_params=_vmem(40 << 20),
        name="diff_attn",
    )(rel_bias, q, k, vt, buckets, lam_vecs, subln_col)


def _fourier_kernel(f_ref, bdc_ref, bds_ref, bdw_ref, m1_ref, m2_ref, twc_ref, tws_ref, o_ref,
                    zr_scr, zi_scr, d_scr, a_scr):
    R = FFT_RADIX
    W = FOURIER_WIDTH
    u = f_ref[...]
    w = bdw_ref[...]
    pc = jnp.dot(u, bdc_ref[...], preferred_element_type=F32).astype(BF16)
    ps = jnp.dot(u, bds_ref[...], preferred_element_type=F32).astype(BF16)
    zr = jnp.dot(pc, w, preferred_element_type=F32)
    zi = jnp.dot(ps, w, preferred_element_type=F32)
    P = FFT_PITCH
    halves = range(W // 128)
    for hh in halves:
        for n1 in range(R):
            zr_scr[hh, n1 * P:n1 * P + R, :] = zr[n1 * R:(n1 + 1) * R, hh * 128:(hh + 1) * 128]
            zi_scr[hh, n1 * P:n1 * P + R, :] = zi[n1 * R:(n1 + 1) * R, hh * 128:(hh + 1) * 128]

    for n2 in range(R):
        for hh in halves:
            cols = slice(n2 * W + hh * 128, n2 * W + (hh + 1) * 128)
            d_scr[0:R, cols] = zr_scr[hh, pl.ds(n2, R, stride=P), :].astype(BF16)
            d_scr[R:2 * R, cols] = zi_scr[hh, pl.ds(n2, R, stride=P), :].astype(BF16)
    a_scr[...] = jnp.dot(m1_ref[...], d_scr[...], preferred_element_type=F32)

    for n2 in range(R):
        tc = twc_ref[n2]
        ts = tws_ref[n2]
        for hh in halves:
            cols = slice(n2 * W + hh * 128, n2 * W + (hh + 1) * 128)
            ar = a_scr[0:R, cols]
            ai = a_scr[R:2 * R, cols]
            zr_scr[hh, pl.ds(n2, R, stride=P), :] = ar * tc + ai * ts
            zi_scr[hh, pl.ds(n2, R, stride=P), :] = ai * tc - ar * ts
    for k1 in range(R):
        for hh in halves:
            cols = slice(k1 * W + hh * 128, k1 * W + (hh + 1) * 128)
            d_scr[0:R, cols] = zr_scr[hh, k1 * P:k1 * P + R, :].astype(BF16)
            d_scr[R:2 * R, cols] = zi_scr[hh, k1 * P:k1 * P + R, :].astype(BF16)
    a_scr[0:R, :] = jnp.dot(m2_ref[...], d_scr[...], preferred_element_type=F32)
    for k1 in range(R):
        for hh in halves:
            zr_scr[hh, pl.ds(k1, R, stride=P), :] = a_scr[0:R, k1 * W + hh * 128:k1 * W + (hh + 1) * 128]
    for k2 in range(R):
        for hh in halves:
            o_ref[hh, k2 * R:(k2 + 1) * R, :] = zr_scr[hh, k2 * P:k2 * P + R, :]


def _fourier(f_in, bdc, bds, bdw, m1, m2, twc, tws):
    R = FFT_RADIX
    W = FOURIER_WIDTH
    const2 = lambda b: (0, 0)
    const3 = lambda b: (0, 0, 0)
    return pl.pallas_call(
        _fourier_kernel,
        grid=(BATCH,),
        in_specs=[
            pl.BlockSpec((SEQ, W), lambda b: (b, 0)),
            pl.BlockSpec((W, W), const2),
            pl.BlockSpec((W, W), const2),
            pl.BlockSpec((W, W), const2),
            pl.BlockSpec((2 * R, 2 * R), const2),
            pl.BlockSpec((R, 2 * R), const2),
            pl.BlockSpec((R, R, 128), const3),
            pl.BlockSpec((R, R, 128), const3),
        ],
        out_specs=pl.BlockSpec((W // 128, SEQ, 128), lambda b: (0, b, 0)),
        out_shape=jax.ShapeDtypeStruct((W // 128, N_TOK, 128), F32),
        scratch_shapes=[pltpu.VMEM((W // 128, R * FFT_PITCH, 128), F32),
                        pltpu.VMEM((W // 128, R * FFT_PITCH, 128), F32),
                        pltpu.VMEM((2 * R, R * W), BF16), pltpu.VMEM((2 * R, R * W), F32)],
        compiler_params=_vmem(52 << 20),
        name="fourier",
    )(f_in, bdc, bds, bdw, m1, m2, twc, tws)


@functools.lru_cache(maxsize=None)
def _dft_constants():
    R = FFT_RADIX
    a = np.arange(R, dtype=np.int64)
    ang = 2.0 * np.pi * ((a[:, None] * a[None, :]) % R).astype(np.float64) / R
    c64, s64 = np.cos(ang), np.sin(ang)
    scale = 1.0 / math.sqrt(SEQ * FOURIER_GROUP_DIM)
    m1 = np.block([[c64, s64], [-s64, c64]])
    m2 = np.concatenate([c64, s64], axis=1) * scale
    tang = 2.0 * np.pi * (a[:, None] * a[None, :]).astype(np.float64) / SEQ
    twc = np.repeat(np.cos(tang)[:, :, None], 128, axis=2).astype(np.float32)
    tws = np.repeat(np.sin(tang)[:, :, None], 128, axis=2).astype(np.float32)
    assert FOURIER_GROUP_DIM == R
    eye = np.eye(FOURIER_GROUPS)
    bdc = np.kron(eye, c64)
    bds = np.kron(eye, -s64)
    return bdc.astype(BF16), bds.astype(BF16), m1.astype(BF16), m2.astype(BF16), twc, tws


def _mem_kv_kernel(mem_ref, w_ref, kcat_ref, vcat_ref):
    kv = jnp.dot(mem_ref[...].astype(BF16), w_ref[...], preferred_element_type=F32)
    mk = kv[:, :MEM_WIDTH]
    mv = kv[:, MEM_WIDTH:]
    lane = lax.broadcasted_iota(jnp.int32, mk.shape, 1)
    for hh in range(MEM_HEADS):
        sel = (lane >= hh * MEM_HEAD_DIM) & (lane < (hh + 1) * MEM_HEAD_DIM)
        kcat_ref[0, hh * MEM_LEN:(hh + 1) * MEM_LEN, :] = jnp.where(sel, mk, 0.0).astype(BF16)
        vcat_ref[0, hh * MEM_LEN:(hh + 1) * MEM_LEN, :] = jnp.where(sel, mv, 0.0).astype(BF16)


def _mem_kv(mem2, w_kv):
    return pl.pallas_call(
        _mem_kv_kernel,
        grid=(BATCH,),
        in_specs=[
            pl.BlockSpec((MEM_LEN, D_MODEL), lambda b: (b, 0)),
            pl.BlockSpec((D_MODEL, 2 * MEM_WIDTH), lambda b: (0, 0)),
        ],
        out_specs=[
            pl.BlockSpec((1, MEM_HEADS * MEM_LEN, MEM_WIDTH), lambda b: (b, 0, 0)),
            pl.BlockSpec((1, MEM_HEADS * MEM_LEN, MEM_WIDTH), lambda b: (b, 0, 0)),
        ],
        out_shape=[
            jax.ShapeDtypeStruct((BATCH, MEM_HEADS * MEM_LEN, MEM_WIDTH), BF16),
            jax.ShapeDtypeStruct((BATCH, MEM_HEADS * MEM_LEN, MEM_WIDTH), BF16),
        ],
        name="mem_kv",
    )(mem2, w_kv)


def _mem_attn_kernel(mq_ref, kcat_ref, vcat_ref, o_ref):
    s = lax.dot_general(mq_ref[...], kcat_ref[0], _NT, preferred_element_type=F32)
    parts = []
    for hh in range(MEM_HEADS):
        sh = s[:, hh * MEM_LEN:(hh + 1) * MEM_LEN]
        p = jnp.exp(sh - jnp.max(sh, axis=1, keepdims=True))
        parts.append((p * (1.0 / jnp.sum(p, axis=1, keepdims=True))).astype(BF16))
    p_all = jnp.concatenate(parts, axis=1)
    o_ref[...] = jnp.dot(p_all, vcat_ref[0], preferred_element_type=F32).astype(BF16)


def _mem_attention(mq, kcat, vcat):
    rt = SEQ // ROW_TILE
    return pl.pallas_call(
        _mem_attn_kernel,
        grid=(N_TOK // ROW_TILE,),
        in_specs=[
            pl.BlockSpec((ROW_TILE, MEM_WIDTH), lambda r: (r, 0)),
            pl.BlockSpec((1, MEM_HEADS * MEM_LEN, MEM_WIDTH), lambda r: (r // rt, 0, 0)),
            pl.BlockSpec((1, MEM_HEADS * MEM_LEN, MEM_WIDTH), lambda r: (r // rt, 0, 0)),
        ],
        out_specs=pl.BlockSpec((ROW_TILE, MEM_WIDTH), lambda r: (r, 0)),
        out_shape=jax.ShapeDtypeStruct((N_TOK, MEM_WIDTH), BF16),
        name="mem_attn",
    )(mq, kcat, vcat)


def _route(logits):
    lane = lax.broadcasted_iota(jnp.int32, logits.shape, 1)
    big = jnp.int32(ROUTER_LANES)
    is_group = (lane >= GROUP_LANE0) & (lane < GROUP_LANE0 + N_GROUPS)
    gl = jnp.where(is_group, logits, -jnp.inf)
    gmax = jnp.max(gl, axis=1, keepdims=True)
    g_sel = jnp.min(jnp.where(gl == gmax, lane, big), axis=1, keepdims=True) - GROUP_LANE0
    g_gate = 1.0 / jnp.sum(jnp.where(is_group, jnp.exp(gl - gmax), 0.0), axis=1, keepdims=True)
    in_group = (lane >= g_sel * EXPERTS_PER_GROUP) & (lane < (g_sel + 1) * EXPERTS_PER_GROUP)
    el = jnp.where(in_group, logits, -jnp.inf)
    v1 = jnp.max(el, axis=1, keepdims=True)
    i1 = jnp.min(jnp.where(el == v1, lane, big), axis=1, keepdims=True)
    el2 = jnp.where(lane == i1, -jnp.inf, el)
    v2 = jnp.max(el2, axis=1, keepdims=True)
    i2 = jnp.min(jnp.where(el2 == v2, lane, big), axis=1, keepdims=True)
    e = jnp.exp(v2 - v1)
    w_first = g_gate / (1.0 + e)
    w_second = g_gate * e / (1.0 + e)
    return i1, i2, w_first, w_second


def _outproj_router_kernel(od_ref, of_ref, om_ref, h_ref, wo_ref, g_ref, b_ref,
                           wr_ref, br_ref,
                           h1_ref, h1_rows_ref, route_ref, w0_ref, w1_ref, cnt_ref, cnt_scr):
    @pl.when(pl.program_id(0) == 0)
    def _():
        cnt_scr[...] = jnp.zeros_like(cnt_scr)

    o_four = [of_ref[hh].astype(BF16) for hh in range(FOURIER_WIDTH // 128)]
    o = jnp.concatenate([od_ref[...]] + o_four + [om_ref[...]], axis=1)
    a = jnp.dot(o, wo_ref[...], preferred_element_type=F32)
    h1 = _layer_norm(DEEPNORM_ALPHA * h_ref[...] + a, g_ref[...], b_ref[...])
    h1_ref[...] = h1
    _store_row_tiles(h1_rows_ref, h1)
    hi = h1.astype(BF16)
    lo = (h1 - hi.astype(F32)).astype(BF16)
    hw = jnp.dot(hi, wr_ref[...], preferred_element_type=F32)
    logits = (hw[:, :ROUTER_LANES] + hw[:, ROUTER_LANES:]
              + jnp.dot(lo, wr_ref[:, :ROUTER_LANES], preferred_element_type=F32) + br_ref[...])
    i1, i2, w_first, w_second = _route(logits)

    lane = lax.broadcasted_iota(jnp.int32, logits.shape, 1)
    onehot = jnp.where(lane == i1, 1.0, jnp.where(lane == i2, 1.0, 0.0))
    r_id = lax.broadcasted_iota(jnp.int32, (ROW_TILE, ROW_TILE), 0)
    c_id = lax.broadcasted_iota(jnp.int32, (ROW_TILE, ROW_TILE), 1)
    tri = jnp.where(r_id > c_id, 1.0, 0.0).astype(BF16)
    before = jnp.dot(tri, onehot.astype(BF16), preferred_element_type=F32) + cnt_scr[...]
    rank1 = jnp.sum(jnp.where(lane == i1, before, 0.0), axis=1, keepdims=True)
    rank2 = jnp.sum(jnp.where(lane == i2, before, 0.0), axis=1, keepdims=True)
    cnt_scr[...] += jnp.sum(onehot, axis=0, keepdims=True)
    cnt_ref[...] = cnt_scr[...]

    packed = jnp.where(lane == 0, i1.astype(F32),
                       jnp.where(lane == 1, i2.astype(F32),
                                 jnp.where(lane == 2, rank1, jnp.where(lane == 3, rank2, 0.0))))
    route_ref[...] = packed.T[:8, :].astype(jnp.int32)
    w0_ref[...] = jnp.broadcast_to(w_first, (ROW_TILE, 128))
    w1_ref[...] = jnp.broadcast_to(w_second, (ROW_TILE, 128))


def _outproj_router(o_diff, o_four, o_mem, h, w_out, ln_g, ln_b, w_r, b_r):
    row = lambda r: (r, 0)
    const = lambda r: (0, 0)
    return pl.pallas_call(
        _outproj_router_kernel,
        grid=(N_TOK // ROW_TILE,),
        in_specs=[
            pl.BlockSpec((ROW_TILE, DIFF_WIDTH), row),
            pl.BlockSpec((FOURIER_WIDTH // 128, ROW_TILE, 128), lambda r: (0, r, 0)),
            pl.BlockSpec((ROW_TILE, MEM_WIDTH), row),
            pl.BlockSpec((ROW_TILE, D_MODEL), row),
            pl.BlockSpec((D_MODEL, D_MODEL), const),
            pl.BlockSpec((1, D_MODEL), const),
            pl.BlockSpec((1, D_MODEL), const),
            pl.BlockSpec((D_MODEL, 2 * ROUTER_LANES), const),
            pl.BlockSpec((1, ROUTER_LANES), const),
        ],
        out_specs=[
            pl.BlockSpec((ROW_TILE, D_MODEL), row),
            pl.BlockSpec((ROW_TILE * ROW_SLABS, 128), row),
            pl.BlockSpec((8, ROW_TILE), lambda r: (0, r)),
            pl.BlockSpec((ROW_TILE, 128), row),
            pl.BlockSpec((ROW_TILE, 128), row),
            pl.BlockSpec((1, ROUTER_LANES), const),
        ],
        out_shape=[
            jax.ShapeDtypeStruct((N_TOK, D_MODEL), F32),
            jax.ShapeDtypeStruct((N_TOK * ROW_SLABS, 128), U32),
            jax.ShapeDtypeStruct((8, N_TOK), jnp.int32),
            jax.ShapeDtypeStruct((N_TOK, 128), F32),
            jax.ShapeDtypeStruct((N_TOK, 128), F32),
            jax.ShapeDtypeStruct((1, ROUTER_LANES), F32),
        ],
        scratch_shapes=[pltpu.VMEM((1, ROUTER_LANES), F32)],
        compiler_params=_vmem(40 << 20),
        name="outproj_router",
    )(o_diff, o_four, o_mem, h, w_out, ln_g, ln_b, w_r, b_r)


def _dispatch_kernel(pos0_ref, pos1_ref, x_ref, xs_ref, sem):
    base = pl.program_id(0) * DISPATCH_TILE

    def issue(t, carry):
        src = _row_tile(x_ref, t)
        pltpu.make_async_copy(src, _row_tile(xs_ref, pos0_ref[base + t]), sem).start(priority=0)
        pltpu.make_async_copy(src, _row_tile(xs_ref, pos1_ref[base + t]), sem).start(priority=1)
        return carry

    lax.fori_loop(0, DISPATCH_TILE, issue, 0, unroll=8)
    for _ in range(2):
        pltpu.make_async_copy(x_ref, xs_ref.at[pl.ds(0, DISPATCH_TILE * ROW_SLABS), :], sem).wait()


def _dispatch(pos0, pos1, h1_rows):
    return pl.pallas_call(
        _dispatch_kernel,
        grid_spec=pltpu.PrefetchScalarGridSpec(
            num_scalar_prefetch=2,
            grid=(N_TOK // DISPATCH_TILE,),
            in_specs=[pl.BlockSpec((DISPATCH_TILE * ROW_SLABS, 128), lambda i, p0, p1: (i, 0))],
            out_specs=pl.BlockSpec(memory_space=pl.ANY),
            scratch_shapes=[pltpu.SemaphoreType.DMA(())],
        ),
        out_shape=jax.ShapeDtypeStruct((N_SLOTS * ROW_SLABS, 128), U32),
        name="moe_dispatch",
    )(pos0, pos1, h1_rows)


def _expert_kernel(tile_end_ref, xs_ref, w1_ref, w3_ref, w2_ref, ys_ref):
    @pl.when(pl.program_id(0) < tile_end_ref[N_EXPERTS - 1])
    def _():
        x = _load_row_tiles(xs_ref, MOE_TILE).astype(BF16)
        a = jnp.dot(x, w1_ref[0].astype(BF16), preferred_element_type=F32)
        b = jnp.dot(x, w3_ref[0].astype(BF16), preferred_element_type=F32)
        hid = (a * jax.nn.sigmoid(a) * b).astype(BF16)
        y = jnp.dot(hid, w2_ref[0].astype(BF16), preferred_element_type=F32)
        _store_row_tiles(ys_ref, y)


def _experts(tile_end, xs, w1, w3, w2):
    def tile(t, tile_end):
        return jnp.minimum(t, tile_end[N_EXPERTS - 1] - 1), 0

    def wsel(t, tile_end):
        e = jnp.int32(0)
        step = N_EXPERTS // 2
        while step:
            e = jnp.where(t >= tile_end[e + step - 1], e + step, e)
            step //= 2
        return e, 0, 0

    return pl.pallas_call(
        _expert_kernel,
        grid_spec=pltpu.PrefetchScalarGridSpec(
            num_scalar_prefetch=1,
            grid=(N_MOE_TILES,),
            in_specs=[
                pl.BlockSpec((MOE_TILE * ROW_SLABS, 128), tile),
                pl.BlockSpec((1, D_MODEL, D_EXPERT), wsel),
                pl.BlockSpec((1, D_MODEL, D_EXPERT), wsel),
                pl.BlockSpec((1, D_EXPERT, D_MODEL), wsel),
            ],
            out_specs=pl.BlockSpec((MOE_TILE * ROW_SLABS, 128), tile),
        ),
        out_shape=jax.ShapeDtypeStruct((N_SLOTS * ROW_SLABS, 128), U32),
        compiler_params=_vmem(40 << 20),
        name="moe_experts",
    )(tile_end, xs, w1, w3, w2)


def _combine_kernel(pos0_ref, pos1_ref, ys_ref, h1_ref, w0_ref, w1_ref, g_ref, b_ref, o_ref,
                    y0_buf, y1_buf, sems):
    i = pl.program_id(0)
    n = pl.num_programs(0)

    def gather(tile, slot):
        base = tile * COMBINE_TILE

        def issue(t, carry):
            pltpu.make_async_copy(_row_tile(ys_ref, pos0_ref[base + t]), _row_tile(y0_buf.at[slot], t),
                                  sems.at[slot]).start(priority=0)
            pltpu.make_async_copy(_row_tile(ys_ref, pos1_ref[base + t]), _row_tile(y1_buf.at[slot], t),
                                  sems.at[slot]).start(priority=1)
            return carry

        lax.fori_loop(0, COMBINE_TILE, issue, 0, unroll=8)

    @pl.when(i == 0)
    def _():
        gather(0, 0)

    @pl.when(i + 1 < n)
    def _():
        gather(i + 1, (i + 1) % 2)

    slot = i % 2
    whole = ys_ref.at[pl.ds(0, COMBINE_TILE * ROW_SLABS), :]
    pltpu.make_async_copy(whole, y0_buf.at[slot], sems.at[slot]).wait()
    pltpu.make_async_copy(whole, y1_buf.at[slot], sems.at[slot]).wait()
    w0 = jnp.concatenate([w0_ref[...]] * (D_MODEL // 128), axis=1)
    w1 = jnp.concatenate([w1_ref[...]] * (D_MODEL // 128), axis=1)
    f = (w0 * _load_row_tiles(y0_buf.at[slot], COMBINE_TILE)
         + w1 * _load_row_tiles(y1_buf.at[slot], COMBINE_TILE))
    o_ref[...] = _layer_norm(DEEPNORM_ALPHA * h1_ref[...] + f, g_ref[...], b_ref[...])


def _combine(pos0, pos1, ys, h1, w0, w1, ln_g, ln_b):
    row = lambda i, p0, p1: (i, 0)
    const = lambda i, p0, p1: (0, 0)
    return pl.pallas_call(
        _combine_kernel,
        grid_spec=pltpu.PrefetchScalarGridSpec(
            num_scalar_prefetch=2,
            grid=(N_TOK // COMBINE_TILE,),
            in_specs=[
                pl.BlockSpec(memory_space=pl.ANY),
                pl.BlockSpec((COMBINE_TILE, D_MODEL), row),
                pl.BlockSpec((COMBINE_TILE, 128), row),
                pl.BlockSpec((COMBINE_TILE, 128), row),
                pl.BlockSpec((1, D_MODEL), const),
                pl.BlockSpec((1, D_MODEL), const),
            ],
            out_specs=pl.BlockSpec((COMBINE_TILE, D_MODEL), row),
            scratch_shapes=[pltpu.VMEM((2, COMBINE_TILE * ROW_SLABS, 128), U32),
                            pltpu.VMEM((2, COMBINE_TILE * ROW_SLABS, 128), U32),
                            pltpu.SemaphoreType.DMA((2,))],
        ),
        out_shape=jax.ShapeDtypeStruct((N_TOK, D_MODEL), F32),
        compiler_params=pltpu.CompilerParams(dimension_semantics=("arbitrary",), vmem_limit_bytes=40 << 20),
        name="moe_combine",
    )(pos0, pos1, ys, h1, w0, w1, ln_g, ln_b)


def kernel(x, mem, ln0_g, ln0_b, rel_bias, w_in, w_mem_kv, w_fourier, lambda_q1, lambda_k1, lambda_q2,
           lambda_k2, subln_g, w_out, ln1_g, ln1_b, w_group, b_group, w_router, b_router, w1, w3, w2,
           ln2_g, ln2_b):
    l = 0
    x2 = x.reshape(N_TOK, D_MODEL)
    wi = w_in[l]
    w_qk = wi[:, :2 * DIFF_WIDTH].astype(BF16)
    w_vt = wi[:, 2 * DIFF_WIDTH:3 * DIFF_WIDTH].T.astype(BF16)
    w_fm = wi[:, 3 * DIFF_WIDTH:].astype(BF16)
    h, q, k, vt, f_in, mq = _ln0_inproj(x2, ln0_g.reshape(1, -1), ln0_b.reshape(1, -1), w_qk, w_vt, w_fm)

    T = ATT_TILE
    kk = jnp.arange(T, dtype=jnp.int32)[:, None]
    qq = jnp.arange(ATT_Q, dtype=jnp.int32)[None, :]
    buckets = jnp.stack([_t5_bucket(d * T + kk - qq) for d in range(-2, ATT_Q // T + 2)])
    lam_vecs = jnp.stack([lambda_q1[l], lambda_k1[l], lambda_q2[l], lambda_k2[l]]).astype(F32)
    o_diff = _diff_attention(rel_bias.astype(F32), q, k, vt, buckets, lam_vecs,
                             subln_g[l].astype(F32).reshape(DIFF_VDIM, 1))

    bdc, bds, m1, m2, twc, tws = (jnp.asarray(c) for c in _dft_constants())
    wf = w_fourier[l]
    bdw = jnp.zeros((FOURIER_WIDTH, FOURIER_WIDTH), F32)
    for g in range(FOURIER_GROUPS):
        sl = slice(g * FOURIER_GROUP_DIM, (g + 1) * FOURIER_GROUP_DIM)
        bdw = bdw.at[sl, sl].set(wf[g])
    o_four = _fourier(f_in, bdc, bds, bdw.astype(BF16), m1, m2, twc, tws)

    kcat, vcat = _mem_kv(mem.reshape(BATCH * MEM_LEN, D_MODEL), w_mem_kv[l].astype(BF16))
    o_mem = _mem_attention(mq, kcat, vcat)

    w_r = jnp.zeros((D_MODEL, ROUTER_LANES), F32)
    w_r = w_r.at[:, :N_EXPERTS].set(w_router[l].astype(F32))
    w_r = w_r.at[:, GROUP_LANE0:GROUP_LANE0 + N_GROUPS].set(w_group[l].astype(F32))
    wr_hi = w_r.astype(BF16)
    wr_lo = (w_r - wr_hi.astype(F32)).astype(BF16)
    b_r = jnp.zeros((1, ROUTER_LANES), F32)
    b_r = b_r.at[0, :N_EXPERTS].set(b_router[l].astype(F32))
    b_r = b_r.at[0, GROUP_LANE0:GROUP_LANE0 + N_GROUPS].set(b_group[l].astype(F32))
    h1, h1_rows, route, gate0, gate1, counts = _outproj_router(
        o_diff, o_four, o_mem, h, w_out[l].astype(BF16),
        ln1_g[l].reshape(1, -1), ln1_b[l].reshape(1, -1), jnp.concatenate([wr_hi, wr_lo], axis=1), b_r)

    cnt = counts[0, :N_EXPERTS].astype(jnp.int32)
    tiles_per_expert = (cnt + MOE_TILE - 1) // MOE_TILE
    tile_end = jnp.cumsum(tiles_per_expert).astype(jnp.int32)
    row_start = (tile_end - tiles_per_expert) * MOE_TILE
    expert_ids = jnp.arange(N_EXPERTS, dtype=jnp.int32)[None, :]

    def slot_of(expert, rank):
        return jnp.sum(jnp.where(expert[:, None] == expert_ids, row_start[None, :], 0), axis=1) + rank

    pos0 = slot_of(route[0], route[2])
    pos1 = slot_of(route[1], route[3])

    xs = _dispatch(pos0, pos1, h1_rows)
    ys = _experts(tile_end, xs, w1[l], w3[l], w2[l])
    out = _combine(pos0, pos1, ys, h1, gate0, gate1, ln2_g[l].reshape(1, -1), ln2_b[l].reshape(1, -1))
    return out.reshape(BATCH, SEQ, D_MODEL)
```

```python
import functools
import math

import numpy as np
import jax
import jax.numpy as jnp
from jax import lax
from jax.experimental import pallas as pl
from jax.experimental.pallas import tpu as pltpu

F32 = jnp.float32
BF16 = jnp.bfloat16

D_MODEL = 1024
BATCH = 4
SEQ = 4096
N_TOK = BATCH * SEQ
MEM_LEN = 256
MEM_HEADS = 4
MEM_HEAD_DIM = 64
MEM_WIDTH = 256
DIFF_HEADS = 4
DIFF_HEAD_DIM = 64
DIFF_VDIM = 2 * DIFF_HEAD_DIM
VT_ROWS = DIFF_VDIM + 16
DIFF_WIDTH = 512
FOURIER_GROUPS = 4
FOURIER_GROUP_DIM = 64
FOURIER_WIDTH = 256
N_BUCKETS = 32
MAX_DISTANCE = 128
N_GROUPS = 4
EXPERTS_PER_GROUP = 8
N_EXPERTS = 32
D_EXPERT = 256
LN_EPS = 1e-5
DEEPNORM_ALPHA = 2.0 ** 0.25
LAM_INIT = 0.8 - 0.6 * math.exp(-0.3 * 0)
LOG2E = 1.4426950408889634

ROW_TILE = 1024
ATT_TILE = 256
ATT_Q = 512
ATT_SUB = 256
MOE_TILE = 512
N_MOE_TILES = 2 * N_TOK // MOE_TILE + N_EXPERTS
N_SLOTS = N_MOE_TILES * MOE_TILE
ROW_SLABS = D_MODEL // 256
U32 = jnp.uint32
DISPATCH_TILE = 4096
COMBINE_TILE = 512
FFT_RADIX = 64
FFT_PITCH = 72
ROUTER_LANES = 128
GROUP_LANE0 = 32

_NT = (((1,), (1,)), ((), ()))


def _vmem(nbytes):
    return pltpu.CompilerParams(vmem_limit_bytes=int(nbytes))


def _layer_norm(x, g, b):
    mu = jnp.mean(x, axis=-1, keepdims=True)
    xc = x - mu
    var = jnp.mean(xc * xc, axis=-1, keepdims=True)
    return xc * lax.rsqrt(var + LN_EPS) * g + b


def _load_row_tiles(ref, rows):
    w = jnp.concatenate([ref[pl.ds(s, rows, stride=ROW_SLABS), :] for s in range(ROW_SLABS)], axis=1)
    lo = lax.bitcast_convert_type(lax.shift_left(w, jnp.uint32(16)), F32)
    hi = lax.bitcast_convert_type(w & jnp.uint32(0xFFFF0000), F32)
    return jnp.concatenate([lo, hi], axis=1)


def _store_row_tiles(ref, x, rounded=False):
    half = D_MODEL // 2
    if not rounded:
        x = x.astype(BF16).astype(F32)
    lo = lax.bitcast_convert_type(x[:, :half], jnp.uint32)
    hi = lax.bitcast_convert_type(x[:, half:], jnp.uint32)
    w = lax.shift_right_logical(lo, jnp.uint32(16)) | hi
    for s in range(ROW_SLABS):
        ref[pl.ds(s, x.shape[0], stride=ROW_SLABS), :] = w[:, s * 128:(s + 1) * 128]


def _row_tile(ref, t):
    return ref.at[pl.ds(pl.multiple_of(t * ROW_SLABS, ROW_SLABS), ROW_SLABS), :]


def _t5_bucket(rel):
    nb = N_BUCKETS // 2
    max_exact = nb // 2
    ret = (rel > 0).astype(jnp.int32) * nb
    n = jnp.abs(rel)
    nf = jnp.maximum(n, 1).astype(F32)
    large = max_exact + (jnp.log(nf / max_exact) / math.log(MAX_DISTANCE / max_exact)
                         * (nb - max_exact)).astype(jnp.int32)
    large = jnp.minimum(large, nb - 1)
    return ret + jnp.where(n < max_exact, n, large)


def _ln0_inproj_kernel(x_ref, g_ref, b_ref, wqk_ref, wvt_ref, wfm_ref,
                       h_ref, q_ref, k_ref, vt_ref, f_ref, mq_ref):
    h = _layer_norm(x_ref[...], g_ref[...], b_ref[...])
    h_ref[...] = h
    hb = h.astype(BF16)
    qk = jnp.dot(hb, wqk_ref[...], preferred_element_type=F32)
    q_ref[...] = (qk[:, :DIFF_WIDTH] * (DIFF_HEAD_DIM ** -0.5 * LOG2E)).astype(BF16)
    k_ref[...] = qk[:, DIFF_WIDTH:].astype(BF16)
    vt = lax.dot_general(wvt_ref[...], hb, _NT, preferred_element_type=F32)
    ones = jnp.ones((VT_ROWS - DIFF_VDIM, ROW_TILE), BF16)
    for hh in range(DIFF_HEADS):
        vt_ref[0, hh, :DIFF_VDIM, :] = vt[hh * DIFF_VDIM:(hh + 1) * DIFF_VDIM, :].astype(BF16)
        vt_ref[0, hh, DIFF_VDIM:, :] = ones
    fm = jnp.dot(hb, wfm_ref[...], preferred_element_type=F32)
    f_ref[...] = fm[:, :FOURIER_WIDTH].astype(BF16)
    mq_ref[...] = (fm[:, FOURIER_WIDTH:] * MEM_HEAD_DIM ** -0.5).astype(BF16)


def _ln0_inproj(x2, ln_g, ln_b, w_qk, w_vt, w_fm):
    rt = SEQ // ROW_TILE
    row = lambda r: (r, 0)
    const = lambda r: (0, 0)
    return pl.pallas_call(
        _ln0_inproj_kernel,
        grid=(N_TOK // ROW_TILE,),
        in_specs=[
            pl.BlockSpec((ROW_TILE, D_MODEL), row),
            pl.BlockSpec((1, D_MODEL), const),
            pl.BlockSpec((1, D_MODEL), const),
            pl.BlockSpec((D_MODEL, 2 * DIFF_WIDTH), const),
            pl.BlockSpec((DIFF_WIDTH, D_MODEL), const),
            pl.BlockSpec((D_MODEL, FOURIER_WIDTH + MEM_WIDTH), const),
        ],
        out_specs=[
            pl.BlockSpec((ROW_TILE, D_MODEL), row),
            pl.BlockSpec((ROW_TILE, DIFF_WIDTH), row),
            pl.BlockSpec((ROW_TILE, DIFF_WIDTH), row),
            pl.BlockSpec((1, DIFF_HEADS, VT_ROWS, ROW_TILE), lambda r: (r // rt, 0, 0, r % rt)),
            pl.BlockSpec((ROW_TILE, FOURIER_WIDTH), row),
            pl.BlockSpec((ROW_TILE, MEM_WIDTH), row),
        ],
        out_shape=[
            jax.ShapeDtypeStruct((N_TOK, D_MODEL), F32),
            jax.ShapeDtypeStruct((N_TOK, DIFF_WIDTH), BF16),
            jax.ShapeDtypeStruct((N_TOK, DIFF_WIDTH), BF16),
            jax.ShapeDtypeStruct((BATCH, DIFF_HEADS, VT_ROWS, SEQ), BF16),
            jax.ShapeDtypeStruct((N_TOK, FOURIER_WIDTH), BF16),
            jax.ShapeDtypeStruct((N_TOK, MEM_WIDTH), BF16),
        ],
        compiler_params=_vmem(48 << 20),
        name="ln0_inproj",
    )(x2, ln_g, ln_b, w_qk, w_vt, w_fm)


def _diff_attn_kernel(tbl_ref, q_ref, k_ref, vt_ref, bkt_ref, lam_ref, g_ref, o_ref,
                      bias_scr, p0_scr, p1_scr):
    T = ATT_TILE
    TQ = ATT_Q
    r = TQ // T
    h = pl.program_id(0)
    i = pl.program_id(2)
    p_scr = (p0_scr, p1_scr)

    @pl.when((pl.program_id(1) == 0) & (i == 0))
    def _build_bias():
        for d in range(r + 4):
            far = d in (0, r + 3)
            bk = bkt_ref[d, 0:8, :] if far else bkt_ref[d]
            bias = jnp.zeros(bk.shape, F32)
            for n in range(N_BUCKETS):
                bias = jnp.where(bk == n, tbl_ref[n, h], bias)
            bias_scr[d] = (jnp.broadcast_to(bias[0:1, :], (T, TQ)) if far else bias) * LOG2E

    q = q_ref[...]
    lane = lax.broadcasted_iota(jnp.int32, q.shape, 1)
    zero = jnp.zeros_like(q)
    q_comp = (jnp.where(lane < DIFF_HEAD_DIM, q, zero), jnp.where(lane >= DIFF_HEAD_DIM, q, zero))

    def bias_tile(j):
        return bias_scr[jnp.clip(j - r * i, -2, r + 1) + 2]

    SUB = ATT_SUB
    n_sub = SEQ // SUB
    m_tile = [[None] * n_sub for _ in range(2)]
    for c in range(2):
        s = lax.dot_general(k_ref[...], q_comp[c], _NT, preferred_element_type=F32)
        for u in range(n_sub):
            if SUB >= T:
                bias = jnp.concatenate([bias_tile(u * SUB // T + t) for t in range(SUB // T)], axis=0)
            else:
                bias = bias_tile(u * SUB // T)[(u * SUB) % T:(u * SUB) % T + SUB, :]
            sb = s[u * SUB:(u + 1) * SUB, :] + bias
            mu = jnp.max(sb, axis=0, keepdims=True)
            p_scr[c][u * SUB:(u + 1) * SUB, :] = jnp.exp2((sb - mu).astype(BF16))
            m_tile[c][u] = mu

    acc = []
    for c in range(2):
        m = functools.reduce(jnp.maximum, m_tile[c])
        a = None
        for u in range(n_sub):
            pv = jnp.dot(vt_ref[0, 0, :, u * SUB:(u + 1) * SUB], p_scr[c][u * SUB:(u + 1) * SUB, :],
                         preferred_element_type=F32)
            pv = pv * jnp.exp2(m_tile[c][u] - m)
            a = pv if a is None else a + pv
        acc.append(a)
    num = [a[:DIFF_VDIM] for a in acc]
    den = [a[DIFF_VDIM:DIFF_VDIM + 1] for a in acc]

    lam = (jnp.exp(jnp.sum(lam_ref[0:1, :] * lam_ref[1:2, :], axis=1, keepdims=True))
           - jnp.exp(jnp.sum(lam_ref[2:3, :] * lam_ref[3:4, :], axis=1, keepdims=True)) + LAM_INIT)
    o = num[0] * (1.0 / den[0]) - lam * (num[1] * (1.0 / den[1]))
    ms = jnp.mean(o * o, axis=0, keepdims=True)
    o = o * lax.rsqrt(ms + LN_EPS) * g_ref[...] * (1.0 - LAM_INIT)
    o_ref[...] = o.T.astype(BF16)


def _diff_attention(rel_bias, q, k, vt, buckets, lam_vecs, subln_col):
    T = ATT_TILE
    TQ = ATT_Q
    nq = SEQ // TQ
    n_bias = TQ // T + 4
    return pl.pallas_call(
        _diff_attn_kernel,
        grid=(DIFF_HEADS, BATCH, nq),
        in_specs=[
            pl.BlockSpec(memory_space=pltpu.SMEM),
            pl.BlockSpec((TQ, DIFF_VDIM), lambda h, b, i: (b * nq + i, h)),
            pl.BlockSpec((SEQ, DIFF_VDIM), lambda h, b, i: (b, h)),
            pl.BlockSpec((1, 1, VT_ROWS, SEQ), lambda h, b, i: (b, h, 0, 0)),
            pl.BlockSpec((n_bias, T, TQ), lambda h, b, i: (0, 0, 0)),
            pl.BlockSpec((4, DIFF_HEAD_DIM), lambda h, b, i: (0, 0)),
            pl.BlockSpec((DIFF_VDIM, 1), lambda h, b, i: (0, 0)),
        ],
        out_specs=pl.BlockSpec((TQ, DIFF_VDIM), lambda h, b, i: (b * nq + i, h)),
        out_shape=jax.ShapeDtypeStruct((N_TOK, DIFF_WIDTH), BF16),
        scratch_shapes=[pltpu.VMEM((n_bias, T, TQ), F32),
                        pltpu.VMEM((SEQ, TQ), BF16), pltpu.VMEM((SEQ, TQ), BF16)],
        compiler_params=_vmem(40 << 20),
        name="diff_attn",
    )(rel_bias, q, k, vt, buckets, lam_vecs, subln_col)


def _fourier_kernel(f_ref, bdc_ref, bds_ref, bdw_ref, m1_ref, m2_ref, twc_ref, tws_ref, o_ref,
                    zr_scr, zi_scr, d_scr, a_scr):
    R = FFT_RADIX
    W = FOURIER_WIDTH
    u = f_ref[...]
    w = bdw_ref[...]
    pc = jnp.dot(u, bdc_ref[...], preferred_element_type=F32).astype(BF16)
    ps = jnp.dot(u, bds_ref[...], preferred_element_type=F32).astype(BF16)
    zr = jnp.dot(pc, w, preferred_element_type=F32)
    zi = jnp.dot(ps, w, preferred_element_type=F32)
    P = FFT_PITCH
    halves = range(W // 128)
    for hh in halves:
        for n1 in range(R):
            zr_scr[hh, n1 * P:n1 * P + R, :] = zr[n1 * R:(n1 + 1) * R, hh * 128:(hh + 1) * 128]
            zi_scr[hh, n1 * P:n1 * P + R, :] = zi[n1 * R:(n1 + 1) * R, hh * 128:(hh + 1) * 128]

    for n2 in range(R):
        for hh in halves:
            cols = slice(n2 * W + hh * 128, n2 * W + (hh + 1) * 128)
            d_scr[0:R, cols] = zr_scr[hh, pl.ds(n2, R, stride=P), :].astype(BF16)
            d_scr[R:2 * R, cols] = zi_scr[hh, pl.ds(n2, R, stride=P), :].astype(BF16)
    a_scr[...] = jnp.dot(m1_ref[...], d_scr[...], preferred_element_type=F32)

    for n2 in range(R):
        tc = twc_ref[n2]
        ts = tws_ref[n2]
        for hh in halves:
            cols = slice(n2 * W + hh * 128, n2 * W + (hh + 1) * 128)
            ar = a_scr[0:R, cols]
            ai = a_scr[R:2 * R, cols]
            zr_scr[hh, pl.ds(n2, R, stride=P), :] = ar * tc + ai * ts
            zi_scr[hh, pl.ds(n2, R, stride=P), :] = ai * tc - ar * ts
    for k1 in range(R):
        for hh in halves:
            cols = slice(k1 * W + hh * 128, k1 * W + (hh + 1) * 128)
            d_scr[0:R, cols] = zr_scr[hh, k1 * P:k1 * P + R, :].astype(BF16)
            d_scr[R:2 * R, cols] = zi_scr[hh, k1 * P:k1 * P + R, :].astype(BF16)
    a_scr[0:R, :] = jnp.dot(m2_ref[...], d_scr[...], preferred_element_type=F32)
    for k1 in range(R):
        for hh in halves:
            zr_scr[hh, pl.ds(k1, R, stride=P), :] = a_scr[0:R, k1 * W + hh * 128:k1 * W + (hh + 1) * 128]
    for k2 in range(R):
        for hh in halves:
            o_ref[hh, k2 * R:(k2 + 1) * R, :] = zr_scr[hh, k2 * P:k2 * P + R, :]


def _fourier(f_in, bdc, bds, bdw, m1, m2, twc, tws):
    R = FFT_RADIX
    W = FOURIER_WIDTH
    const2 = lambda b: (0, 0)
    const3 = lambda b: (0, 0, 0)
    return pl.pallas_call(
        _fourier_kernel,
        grid=(BATCH,),
        in_specs=[
            pl.BlockSpec((SEQ, W), lambda b: (b, 0)),
            pl.BlockSpec((W, W), const2),
            pl.BlockSpec((W, W), const2),
            pl.BlockSpec((W, W), const2),
            pl.BlockSpec((2 * R, 2 * R), const2),
            pl.BlockSpec((R, 2 * R), const2),
            pl.BlockSpec((R, R, 128), const3),
            pl.BlockSpec((R, R, 128), const3),
        ],
        out_specs=pl.BlockSpec((W // 128, SEQ, 128), lambda b: (0, b, 0)),
        out_shape=jax.ShapeDtypeStruct((W // 128, N_TOK, 128), F32),
        scratch_shapes=[pltpu.VMEM((W // 128, R * FFT_PITCH, 128), F32),
                        pltpu.VMEM((W // 128, R * FFT_PITCH, 128), F32),
                        pltpu.VMEM((2 * R, R * W), BF16), pltpu.VMEM((2 * R, R * W), F32)],
        compiler_params=_vmem(52 << 20),
        name="fourier",
    )(f_in, bdc, bds, bdw, m1, m2, twc, tws)


@functools.lru_cache(maxsize=None)
def _dft_constants():
    R = FFT_RADIX
    a = np.arange(R, dtype=np.int64)
    ang = 2.0 * np.pi * ((a[:, None] * a[None, :]) % R).astype(np.float64) / R
    c64, s64 = np.cos(ang), np.sin(ang)
    scale = 1.0 / math.sqrt(SEQ * FOURIER_GROUP_DIM)
    m1 = np.block([[c64, s64], [-s64, c64]])
    m2 = np.concatenate([c64, s64], axis=1) * scale
    tang = 2.0 * np.pi * (a[:, None] * a[None, :]).astype(np.float64) / SEQ
    twc = np.repeat(np.cos(tang)[:, :, None], 128, axis=2).astype(np.float32)
    tws = np.repeat(np.sin(tang)[:, :, None], 128, axis=2).astype(np.float32)
    assert FOURIER_GROUP_DIM == R
    eye = np.eye(FOURIER_GROUPS)
    bdc = np.kron(eye, c64)
    bds = np.kron(eye, -s64)
    return bdc.astype(BF16), bds.astype(BF16), m1.astype(BF16), m2.astype(BF16), twc, tws


def _mem_kv_kernel(mem_ref, w_ref, kcat_ref, vcat_ref):
    kv = jnp.dot(mem_ref[...].astype(BF16), w_ref[...], preferred_element_type=F32)
    mk = kv[:, :MEM_WIDTH]
    mv = kv[:, MEM_WIDTH:]
    lane = lax.broadcasted_iota(jnp.int32, mk.shape, 1)
    for hh in range(MEM_HEADS):
        sel = (lane >= hh * MEM_HEAD_DIM) & (lane < (hh + 1) * MEM_HEAD_DIM)
        kcat_ref[0, hh * MEM_LEN:(hh + 1) * MEM_LEN, :] = jnp.where(sel, mk, 0.0).astype(BF16)
        vcat_ref[0, hh * MEM_LEN:(hh + 1) * MEM_LEN, :] = jnp.where(sel, mv, 0.0).astype(BF16)


def _mem_kv(mem2, w_kv):
    return pl.pallas_call(
        _mem_kv_kernel,
        grid=(BATCH,),
        in_specs=[
            pl.BlockSpec((MEM_LEN, D_MODEL), lambda b: (b, 0)),
            pl.BlockSpec((D_MODEL, 2 * MEM_WIDTH), lambda b: (0, 0)),
        ],
        out_specs=[
            pl.BlockSpec((1, MEM_HEADS * MEM_LEN, MEM_WIDTH), lambda b: (b, 0, 0)),
            pl.BlockSpec((1, MEM_HEADS * MEM_LEN, MEM_WIDTH), lambda b: (b, 0, 0)),
        ],
        out_shape=[
            jax.ShapeDtypeStruct((BATCH, MEM_HEADS * MEM_LEN, MEM_WIDTH), BF16),
            jax.ShapeDtypeStruct((BATCH, MEM_HEADS * MEM_LEN, MEM_WIDTH), BF16),
        ],
        name="mem_kv",
    )(mem2, w_kv)


def _mem_attn_kernel(mq_ref, kcat_ref, vcat_ref, o_ref):
    s = lax.dot_general(mq_ref[...], kcat_ref[0], _NT, preferred_element_type=F32)
    parts = []
    for hh in range(MEM_HEADS):
        sh = s[:, hh * MEM_LEN:(hh + 1) * MEM_LEN]
        p = jnp.exp(sh - jnp.max(sh, axis=1, keepdims=True))
        parts.append((p * (1.0 / jnp.sum(p, axis=1, keepdims=True))).astype(BF16))
    p_all = jnp.concatenate(parts, axis=1)
    o_ref[...] = jnp.dot(p_all, vcat_ref[0], preferred_element_type=F32).astype(BF16)


def _mem_attention(mq, kcat, vcat):
    rt = SEQ // ROW_TILE
    return pl.pallas_call(
        _mem_attn_kernel,
        grid=(N_TOK // ROW_TILE,),
        in_specs=[
            pl.BlockSpec((ROW_TILE, MEM_WIDTH), lambda r: (r, 0)),
            pl.BlockSpec((1, MEM_HEADS * MEM_LEN, MEM_WIDTH), lambda r: (r // rt, 0, 0)),
            pl.BlockSpec((1, MEM_HEADS * MEM_LEN, MEM_WIDTH), lambda r: (r // rt, 0, 0)),
        ],
        out_specs=pl.BlockSpec((ROW_TILE, MEM_WIDTH), lambda r: (r, 0)),
        out_shape=jax.ShapeDtypeStruct((N_TOK, MEM_WIDTH), BF16),
        name="mem_attn",
    )(mq, kcat, vcat)


def _route(logits):
    lane = lax.broadcasted_iota(jnp.int32, logits.shape, 1)
    big = jnp.int32(ROUTER_LANES)
    is_group = (lane >= GROUP_LANE0) & (lane < GROUP_LANE0 + N_GROUPS)
    gl = jnp.where(is_group, logits, -jnp.inf)
    gmax = jnp.max(gl, axis=1, keepdims=True)
    g_sel = jnp.min(jnp.where(gl == gmax, lane, big), axis=1, keepdims=True) - GROUP_LANE0
    g_gate = 1.0 / jnp.sum(jnp.where(is_group, jnp.exp(gl - gmax), 0.0), axis=1, keepdims=True)
    in_group = (lane >= g_sel * EXPERTS_PER_GROUP) & (lane < (g_sel + 1) * EXPERTS_PER_GROUP)
    el = jnp.where(in_group, logits, -jnp.inf)
    v1 = jnp.max(el, axis=1, keepdims=True)
    i1 = jnp.min(jnp.where(el == v1, lane, big), axis=1, keepdims=True)
    el2 = jnp.where(lane == i1, -jnp.inf, el)
    v2 = jnp.max(el2, axis=1, keepdims=True)
    i2 = jnp.min(jnp.where(el2 == v2, lane, big), axis=1, keepdims=True)
    e = jnp.exp(v2 - v1)
    w_first = g_gate / (1.0 + e)
    w_second = g_gate * e / (1.0 + e)
    return i1, i2, w_first, w_second


def _outproj_router_kernel(od_ref, of_ref, om_ref, h_ref, wo_ref, g_ref, b_ref,
                           wr_ref, br_ref,
                           h1_ref, h1_rows_ref, route_ref, w0_ref, w1_ref, cnt_ref, cnt_scr):
    @pl.when(pl.program_id(0) == 0)
    def _():
        cnt_scr[...] = jnp.zeros_like(cnt_scr)

    o_four = [of_ref[hh].astype(BF16) for hh in range(FOURIER_WIDTH // 128)]
    o = jnp.concatenate([od_ref[...]] + o_four + [om_ref[...]], axis=1)
    a = jnp.dot(o, wo_ref[...], preferred_element_type=F32)
    h1 = _layer_norm(DEEPNORM_ALPHA * h_ref[...] + a, g_ref[...], b_ref[...])
    h1_ref[...] = h1
    hi = h1.astype(BF16)
    hi_f = hi.astype(F32)
    _store_row_tiles(h1_rows_ref, hi_f, rounded=True)
    lo = (h1 - hi_f).astype(BF16)
    hw = jnp.dot(hi, wr_ref[...], preferred_element_type=F32)
    logits = (hw[:, :ROUTER_LANES] + hw[:, ROUTER_LANES:]
              + jnp.dot(lo, wr_ref[:, :ROUTER_LANES], preferred_element_type=F32) + br_ref[...])
    i1, i2, w_first, w_second = _route(logits)

    lane = lax.broadcasted_iota(jnp.int32, logits.shape, 1)
    onehot = jnp.where(lane == i1, 1.0, jnp.where(lane == i2, 1.0, 0.0))
    r_id = lax.broadcasted_iota(jnp.int32, (ROW_TILE, ROW_TILE), 0)
    c_id = lax.broadcasted_iota(jnp.int32, (ROW_TILE, ROW_TILE), 1)
    tri = jnp.where(r_id > c_id, 1.0, 0.0).astype(BF16)
    before = jnp.dot(tri, onehot.astype(BF16), preferred_element_type=F32) + cnt_scr[...]
    rank1 = jnp.sum(jnp.where(lane == i1, before, 0.0), axis=1, keepdims=True)
    rank2 = jnp.sum(jnp.where(lane == i2, before, 0.0), axis=1, keepdims=True)
    cnt_scr[...] += jnp.sum(onehot, axis=0, keepdims=True)
    cnt_ref[...] = cnt_scr[...]

    packed = jnp.where(lane == 0, i1.astype(F32),
                       jnp.where(lane == 1, i2.astype(F32),
                                 jnp.where(lane == 2, rank1, jnp.where(lane == 3, rank2, 0.0))))
    route_ref[...] = packed.T[:8, :].astype(jnp.int32)
    w0_ref[...] = jnp.broadcast_to(w_first, (ROW_TILE, 128))
    w1_ref[...] = jnp.broadcast_to(w_second, (ROW_TILE, 128))


def _outproj_router(o_diff, o_four, o_mem, h, w_out, ln_g, ln_b, w_r, b_r):
    row = lambda r: (r, 0)
    const = lambda r: (0, 0)
    return pl.pallas_call(
        _outproj_router_kernel,
        grid=(N_TOK // ROW_TILE,),
        in_specs=[
            pl.BlockSpec((ROW_TILE, DIFF_WIDTH), row),
            pl.BlockSpec((FOURIER_WIDTH // 128, ROW_TILE, 128), lambda r: (0, r, 0)),
            pl.BlockSpec((ROW_TILE, MEM_WIDTH), row),
            pl.BlockSpec((ROW_TILE, D_MODEL), row),
            pl.BlockSpec((D_MODEL, D_MODEL), const),
            pl.BlockSpec((1, D_MODEL), const),
            pl.BlockSpec((1, D_MODEL), const),
            pl.BlockSpec((D_MODEL, 2 * ROUTER_LANES), const),
            pl.BlockSpec((1, ROUTER_LANES), const),
        ],
        out_specs=[
            pl.BlockSpec((ROW_TILE, D_MODEL), row),
            pl.BlockSpec((ROW_TILE * ROW_SLABS, 128), row),
            pl.BlockSpec((8, ROW_TILE), lambda r: (0, r)),
            pl.BlockSpec((ROW_TILE, 128), row),
            pl.BlockSpec((ROW_TILE, 128), row),
            pl.BlockSpec((1, ROUTER_LANES), const),
        ],
        out_shape=[
            jax.ShapeDtypeStruct((N_TOK, D_MODEL), F32),
            jax.ShapeDtypeStruct((N_TOK * ROW_SLABS, 128), U32),
            jax.ShapeDtypeStruct((8, N_TOK), jnp.int32),
            jax.ShapeDtypeStruct((N_TOK, 128), F32),
            jax.ShapeDtypeStruct((N_TOK, 128), F32),
            jax.ShapeDtypeStruct((1, ROUTER_LANES), F32),
        ],
        scratch_shapes=[pltpu.VMEM((1, ROUTER_LANES), F32)],
        compiler_params=_vmem(40 << 20),
        name="outproj_router",
    )(o_diff, o_four, o_mem, h, w_out, ln_g, ln_b, w_r, b_r)


def _dispatch_kernel(pos0_ref, pos1_ref, x_ref, xs_ref, sem):
    base = pl.program_id(0) * DISPATCH_TILE

    def issue(t, carry):
        src = _row_tile(x_ref, t)
        pltpu.make_async_copy(src, _row_tile(xs_ref, pos0_ref[base + t]), sem).start(priority=0)
        pltpu.make_async_copy(src, _row_tile(xs_ref, pos1_ref[base + t]), sem).start(priority=1)
        return carry

    lax.fori_loop(0, DISPATCH_TILE, issue, 0, unroll=8)
    for _ in range(2):
        pltpu.make_async_copy(x_ref, xs_ref.at[pl.ds(0, DISPATCH_TILE * ROW_SLABS), :], sem).wait()


def _dispatch(pos0, pos1, h1_rows):
    return pl.pallas_call(
        _dispatch_kernel,
        grid_spec=pltpu.PrefetchScalarGridSpec(
            num_scalar_prefetch=2,
            grid=(N_TOK // DISPATCH_TILE,),
            in_specs=[pl.BlockSpec((DISPATCH_TILE * ROW_SLABS, 128), lambda i, p0, p1: (i, 0))],
            out_specs=pl.BlockSpec(memory_space=pl.ANY),
            scratch_shapes=[pltpu.SemaphoreType.DMA(())],
        ),
        out_shape=jax.ShapeDtypeStruct((N_SLOTS * ROW_SLABS, 128), U32),
        name="moe_dispatch",
    )(pos0, pos1, h1_rows)


def _expert_kernel(tile_end_ref, xs_ref, w1_ref, w3_ref, w2_ref, ys_ref):
    @pl.when(pl.program_id(0) < tile_end_ref[N_EXPERTS - 1])
    def _():
        x = _load_row_tiles(xs_ref, MOE_TILE).astype(BF16)
        a = jnp.dot(x, w1_ref[0].astype(BF16), preferred_element_type=F32)
        b = jnp.dot(x, w3_ref[0].astype(BF16), preferred_element_type=F32)
        hid = (a * jax.nn.sigmoid(a) * b).astype(BF16)
        y = jnp.dot(hid, w2_ref[0].astype(BF16), preferred_element_type=F32)
        _store_row_tiles(ys_ref, y)


def _experts(tile_end, xs, w1, w3, w2):
    def tile(t, tile_end):
        return jnp.minimum(t, tile_end[N_EXPERTS - 1] - 1), 0

    def wsel(t, tile_end):
        e = jnp.int32(0)
        step = N_EXPERTS // 2
        while step:
            e = jnp.where(t >= tile_end[e + step - 1], e + step, e)
            step //= 2
        return e, 0, 0

    return pl.pallas_call(
        _expert_kernel,
        grid_spec=pltpu.PrefetchScalarGridSpec(
            num_scalar_prefetch=1,
            grid=(N_MOE_TILES,),
            in_specs=[
                pl.BlockSpec((MOE_TILE * ROW_SLABS, 128), tile),
                pl.BlockSpec((1, D_MODEL, D_EXPERT), wsel),
                pl.BlockSpec((1, D_MODEL, D_EXPERT), wsel),
                pl.BlockSpec((1, D_EXPERT, D_MODEL), wsel),
            ],
            out_specs=pl.BlockSpec((MOE_TILE * ROW_SLABS, 128), tile),
        ),
        out_shape=jax.ShapeDtypeStruct((N_SLOTS * ROW_SLABS, 128), U32),
        compiler_params=_vmem(40 << 20),
        name="moe_experts",
    )(tile_end, xs, w1, w3, w2)


def _combine_kernel(pos0_ref, pos1_ref, ys_ref, h1_ref, w0_ref, w1_ref, g_ref, b_ref, o_ref,
                    y0_buf, y1_buf, sems):
    i = pl.program_id(0)
    n = pl.num_programs(0)

    def gather(tile, slot):
        base = tile * COMBINE_TILE

        def issue(t, carry):
            pltpu.make_async_copy(_row_tile(ys_ref, pos0_ref[base + t]), _row_tile(y0_buf.at[slot], t),
                                  sems.at[slot]).start(priority=0)
            pltpu.make_async_copy(_row_tile(ys_ref, pos1_ref[base + t]), _row_tile(y1_buf.at[slot], t),
                                  sems.at[slot]).start(priority=1)
            return carry

        lax.fori_loop(0, COMBINE_TILE, issue, 0, unroll=8)

    @pl.when(i == 0)
    def _():
        gather(0, 0)

    @pl.when(i + 1 < n)
    def _():
        gather(i + 1, (i + 1) % 2)

    slot = i % 2
    whole = ys_ref.at[pl.ds(0, COMBINE_TILE * ROW_SLABS), :]
    pltpu.make_async_copy(whole, y0_buf.at[slot], sems.at[slot]).wait()
    pltpu.make_async_copy(whole, y1_buf.at[slot], sems.at[slot]).wait()
    w0 = jnp.concatenate([w0_ref[...]] * (D_MODEL // 128), axis=1)
    w1 = jnp.concatenate([w1_ref[...]] * (D_MODEL // 128), axis=1)
    f = (w0 * _load_row_tiles(y0_buf.at[slot], COMBINE_TILE)
         + w1 * _load_row_tiles(y1_buf.at[slot], COMBINE_TILE))
    o_ref[...] = _layer_norm(DEEPNORM_ALPHA * h1_ref[...] + f, g_ref[...], b_ref[...])


def _combine(pos0, pos1, ys, h1, w0, w1, ln_g, ln_b):
    row = lambda i, p0, p1: (i, 0)
    const = lambda i, p0, p1: (0, 0)
    return pl.pallas_call(
        _combine_kernel,
        grid_spec=pltpu.PrefetchScalarGridSpec(
            num_scalar_prefetch=2,
            grid=(N_TOK // COMBINE_TILE,),
            in_specs=[
                pl.BlockSpec(memory_space=pl.ANY),
                pl.BlockSpec((COMBINE_TILE, D_MODEL), row),
                pl.BlockSpec((COMBINE_TILE, 128), row),
                pl.BlockSpec((COMBINE_TILE, 128), row),
                pl.BlockSpec((1, D_MODEL), const),
                pl.BlockSpec((1, D_MODEL), const),
            ],
            out_specs=pl.BlockSpec((COMBINE_TILE, D_MODEL), row),
            scratch_shapes=[pltpu.VMEM((2, COMBINE_TILE * ROW_SLABS, 128), U32),
                            pltpu.VMEM((2, COMBINE_TILE * ROW_SLABS, 128), U32),
                            pltpu.SemaphoreType.DMA((2,))],
        ),
        out_shape=jax.ShapeDtypeStruct((N_TOK, D_MODEL), F32),
        compiler_params=pltpu.CompilerParams(dimension_semantics=("arbitrary",), vmem_limit_bytes=40 << 20),
        name="moe_combine",
    )(pos0, pos1, ys, h1, w0, w1, ln_g, ln_b)


def kernel(x, mem, ln0_g, ln0_b, rel_bias, w_in, w_mem_kv, w_fourier, lambda_q1, lambda_k1, lambda_q2,
           lambda_k2, subln_g, w_out, ln1_g, ln1_b, w_group, b_group, w_router, b_router, w1, w3, w2,
           ln2_g, ln2_b):
    l = 0
    x2 = x.reshape(N_TOK, D_MODEL)
    wi = w_in[l]
    w_qk = wi[:, :2 * DIFF_WIDTH].astype(BF16)
    w_vt = wi[:, 2 * DIFF_WIDTH:3 * DIFF_WIDTH].T.astype(BF16)
    w_fm = wi[:, 3 * DIFF_WIDTH:].astype(BF16)
    h, q, k, vt, f_in, mq = _ln0_inproj(x2, ln0_g.reshape(1, -1), ln0_b.reshape(1, -1), w_qk, w_vt, w_fm)

    T = ATT_TILE
    kk = jnp.arange(T, dtype=jnp.int32)[:, None]
    qq = jnp.arange(ATT_Q, dtype=jnp.int32)[None, :]
    buckets = jnp.stack([_t5_bucket(d * T + kk - qq) for d in range(-2, ATT_Q // T + 2)])
    lam_vecs = jnp.stack([lambda_q1[l], lambda_k1[l], lambda_q2[l], lambda_k2[l]]).astype(F32)
    o_diff = _diff_attention(rel_bias.astype(F32), q, k, vt, buckets, lam_vecs,
                             subln_g[l].astype(F32).reshape(DIFF_VDIM, 1))

    bdc, bds, m1, m2, twc, tws = (jnp.asarray(c) for c in _dft_constants())
    wf = w_fourier[l]
    bdw = jnp.zeros((FOURIER_WIDTH, FOURIER_WIDTH), F32)
    for g in range(FOURIER_GROUPS):
        sl = slice(g * FOURIER_GROUP_DIM, (g + 1) * FOURIER_GROUP_DIM)
        bdw = bdw.at[sl, sl].set(wf[g])
    o_four = _fourier(f_in, bdc, bds, bdw.astype(BF16), m1, m2, twc, tws)

    kcat, vcat = _mem_kv(mem.reshape(BATCH * MEM_LEN, D_MODEL), w_mem_kv[l].astype(BF16))
    o_mem = _mem_attention(mq, kcat, vcat)

    w_r = jnp.zeros((D_MODEL, ROUTER_LANES), F32)
    w_r = w_r.at[:, :N_EXPERTS].set(w_router[l].astype(F32))
    w_r = w_r.at[:, GROUP_LANE0:GROUP_LANE0 + N_GROUPS].set(w_group[l].astype(F32))
    wr_hi = w_r.astype(BF16)
    wr_lo = (w_r - wr_hi.astype(F32)).astype(BF16)
    b_r = jnp.zeros((1, ROUTER_LANES), F32)
    b_r = b_r.at[0, :N_EXPERTS].set(b_router[l].astype(F32))
    b_r = b_r.at[0, GROUP_LANE0:GROUP_LANE0 + N_GROUPS].set(b_group[l].astype(F32))
    h1, h1_rows, route, gate0, gate1, counts = _outproj_router(
        o_diff, o_four, o_mem, h, w_out[l].astype(BF16),
        ln1_g[l].reshape(1, -1), ln1_b[l].reshape(1, -1), jnp.concatenate([wr_hi, wr_lo], axis=1), b_r)

    cnt = counts[0, :N_EXPERTS].astype(jnp.int32)
    tiles_per_expert = (cnt + MOE_TILE - 1) // MOE_TILE
    tile_end = jnp.cumsum(tiles_per_expert).astype(jnp.int32)
    row_start = (tile_end - tiles_per_expert) * MOE_TILE
    expert_ids = jnp.arange(N_EXPERTS, dtype=jnp.int32)[None, :]

    def slot_of(expert, rank):
        return jnp.sum(jnp.where(expert[:, None] == expert_ids, row_start[None, :], 0), axis=1) + rank

    pos0 = slot_of(route[0], route[2])
    pos1 = slot_of(route[1], route[3])

    xs = _dispatch(pos0, pos1, h1_rows)
    ys = _experts(tile_end, xs, w1[l], w3[l], w2[l])
    out = _combine(pos0, pos1, ys, h1, gate0, gate1, ln2_g[l].reshape(1, -1), ln2_b[l].reshape(1, -1))
    return out.reshape(BATCH, SEQ, D_MODEL)
```

```python
import functools
import math

import numpy as np
import jax
import jax.numpy as jnp
from jax import lax
from jax.experimental import pallas as pl
from jax.experimental.pallas import tpu as pltpu

F32 = jnp.float32
BF16 = jnp.bfloat16

D_MODEL = 1024
BATCH = 4
SEQ = 4096
N_TOK = BATCH * SEQ
MEM_LEN = 256
MEM_HEADS = 4
MEM_HEAD_DIM = 64
MEM_WIDTH = 256
DIFF_HEADS = 4
DIFF_HEAD_DIM = 64
DIFF_VDIM = 2 * DIFF_HEAD_DIM
VT_ROWS = DIFF_VDIM + 16
DIFF_WIDTH = 512
FOURIER_GROUPS = 4
FOURIER_GROUP_DIM = 64
FOURIER_WIDTH = 256
N_BUCKETS = 32
MAX_DISTANCE = 128
N_GROUPS = 4
EXPERTS_PER_GROUP = 8
N_EXPERTS = 32
D_EXPERT = 256
LN_EPS = 1e-5
DEEPNORM_ALPHA = 2.0 ** 0.25
LAM_INIT = 0.8 - 0.6 * math.exp(-0.3 * 0)
LOG2E = 1.4426950408889634

ROW_TILE = 1024
ATT_TILE = 256
ATT_Q = 512
ATT_SUB = 256
MOE_TILE = 1024
N_MOE_TILES = 2 * N_TOK // MOE_TILE + N_EXPERTS
N_SLOTS = N_MOE_TILES * MOE_TILE
ROW_SLABS = D_MODEL // 256
U32 = jnp.uint32
DISPATCH_TILE = 4096
COMBINE_TILE = 512
FFT_RADIX = 64
FFT_PITCH = 72
ROUTER_LANES = 128
GROUP_LANE0 = 32

_NT = (((1,), (1,)), ((), ()))


def _vmem(nbytes):
    return pltpu.CompilerParams(vmem_limit_bytes=int(nbytes))


def _layer_norm(x, g, b):
    mu = jnp.mean(x, axis=-1, keepdims=True)
    xc = x - mu
    var = jnp.mean(xc * xc, axis=-1, keepdims=True)
    return xc * lax.rsqrt(var + LN_EPS) * g + b


def _load_row_tiles(ref, rows):
    w = jnp.concatenate([ref[pl.ds(s, rows, stride=ROW_SLABS), :] for s in range(ROW_SLABS)], axis=1)
    lo = lax.bitcast_convert_type(lax.shift_left(w, jnp.uint32(16)), F32)
    hi = lax.bitcast_convert_type(w & jnp.uint32(0xFFFF0000), F32)
    return jnp.concatenate([lo, hi], axis=1)


def _store_row_tiles(ref, x, rounded=False):
    half = D_MODEL // 2
    if not rounded:
        x = x.astype(BF16).astype(F32)
    lo = lax.bitcast_convert_type(x[:, :half], jnp.uint32)
    hi = lax.bitcast_convert_type(x[:, half:], jnp.uint32)
    w = lax.shift_right_logical(lo, jnp.uint32(16)) | hi
    for s in range(ROW_SLABS):
        ref[pl.ds(s, x.shape[0], stride=ROW_SLABS), :] = w[:, s * 128:(s + 1) * 128]


def _row_tile(ref, t):
    return ref.at[pl.ds(pl.multiple_of(t * ROW_SLABS, ROW_SLABS), ROW_SLABS), :]


def _t5_bucket(rel):
    nb = N_BUCKETS // 2
    max_exact = nb // 2
    ret = (rel > 0).astype(jnp.int32) * nb
    n = jnp.abs(rel)
    nf = jnp.maximum(n, 1).astype(F32)
    large = max_exact + (jnp.log(nf / max_exact) / math.log(MAX_DISTANCE / max_exact)
                         * (nb - max_exact)).astype(jnp.int32)
    large = jnp.minimum(large, nb - 1)
    return ret + jnp.where(n < max_exact, n, large)


def _ln0_inproj_kernel(x_ref, g_ref, b_ref, wqk_ref, wvt_ref, wfm_ref,
                       h_ref, q_ref, k_ref, vt_ref, f_ref, mq_ref):
    h = _layer_norm(x_ref[...], g_ref[...], b_ref[...])
    h_ref[...] = h
    hb = h.astype(BF16)
    qk = jnp.dot(hb, wqk_ref[...], preferred_element_type=F32)
    q_ref[...] = (qk[:, :DIFF_WIDTH] * (DIFF_HEAD_DIM ** -0.5 * LOG2E)).astype(BF16)
    k_ref[...] = qk[:, DIFF_WIDTH:].astype(BF16)
    vt = lax.dot_general(wvt_ref[...], hb, _NT, preferred_element_type=F32)
    ones = jnp.ones((VT_ROWS - DIFF_VDIM, ROW_TILE), BF16)
    for hh in range(DIFF_HEADS):
        vt_ref[0, hh, :DIFF_VDIM, :] = vt[hh * DIFF_VDIM:(hh + 1) * DIFF_VDIM, :].astype(BF16)
        vt_ref[0, hh, DIFF_VDIM:, :] = ones
    fm = jnp.dot(hb, wfm_ref[...], preferred_element_type=F32)
    f_ref[...] = fm[:, :FOURIER_WIDTH].astype(BF16)
    mq_ref[...] = (fm[:, FOURIER_WIDTH:] * MEM_HEAD_DIM ** -0.5).astype(BF16)


def _ln0_inproj(x2, ln_g, ln_b, w_qk, w_vt, w_fm):
    rt = SEQ // ROW_TILE
    row = lambda r: (r, 0)
    const = lambda r: (0, 0)
    return pl.pallas_call(
        _ln0_inproj_kernel,
        grid=(N_TOK // ROW_TILE,),
        in_specs=[
            pl.BlockSpec((ROW_TILE, D_MODEL), row),
            pl.BlockSpec((1, D_MODEL), const),
            pl.BlockSpec((1, D_MODEL), const),
            pl.BlockSpec((D_MODEL, 2 * DIFF_WIDTH), const),
            pl.BlockSpec((DIFF_WIDTH, D_MODEL), const),
            pl.BlockSpec((D_MODEL, FOURIER_WIDTH + MEM_WIDTH), const),
        ],
        out_specs=[
            pl.BlockSpec((ROW_TILE, D_MODEL), row),
            pl.BlockSpec((ROW_TILE, DIFF_WIDTH), row),
            pl.BlockSpec((ROW_TILE, DIFF_WIDTH), row),
            pl.BlockSpec((1, DIFF_HEADS, VT_ROWS, ROW_TILE), lambda r: (r // rt, 0, 0, r % rt)),
            pl.BlockSpec((ROW_TILE, FOURIER_WIDTH), row),
            pl.BlockSpec((ROW_TILE, MEM_WIDTH), row),
        ],
        out_shape=[
            jax.ShapeDtypeStruct((N_TOK, D_MODEL), F32),
            jax.ShapeDtypeStruct((N_TOK, DIFF_WIDTH), BF16),
            jax.ShapeDtypeStruct((N_TOK, DIFF_WIDTH), BF16),
            jax.ShapeDtypeStruct((BATCH, DIFF_HEADS, VT_ROWS, SEQ), BF16),
            jax.ShapeDtypeStruct((N_TOK, FOURIER_WIDTH), BF16),
            jax.ShapeDtypeStruct((N_TOK, MEM_WIDTH), BF16),
        ],
        compiler_params=_vmem(48 << 20),
        name="ln0_inproj",
    )(x2, ln_g, ln_b, w_qk, w_vt, w_fm)


def _diff_attn_kernel(tbl_ref, q_ref, k_ref, vt_ref, bkt_ref, lam_ref, g_ref, o_ref,
                      bias_scr, p0_scr, p1_scr):
    T = ATT_TILE
    TQ = ATT_Q
    r = TQ // T
    h = pl.program_id(0)
    i = pl.program_id(2)
    p_scr = (p0_scr, p1_scr)

    @pl.when((pl.program_id(1) == 0) & (i == 0))
    def _build_bias():
        for d in range(r + 4):
            far = d in (0, r + 3)
            bk = bkt_ref[d, 0:8, :] if far else bkt_ref[d]
            bias = jnp.zeros(bk.shape, F32)
            for n in range(N_BUCKETS):
                bias = jnp.where(bk == n, tbl_ref[n, h], bias)
            bias_scr[d] = (jnp.broadcast_to(bias[0:1, :], (T, TQ)) if far else bias) * LOG2E

    q = q_ref[...]
    lane = lax.broadcasted_iota(jnp.int32, q.shape, 1)
    zero = jnp.zeros_like(q)
    q_comp = (jnp.where(lane < DIFF_HEAD_DIM, q, zero), jnp.where(lane >= DIFF_HEAD_DIM, q, zero))

    def bias_tile(j):
        return bias_scr[jnp.clip(j - r * i, -2, r + 1) + 2]

    SUB = ATT_SUB
    n_sub = SEQ // SUB
    m_tile = [[None] * n_sub for _ in range(2)]
    for c in range(2):
        s = lax.dot_general(k_ref[...], q_comp[c], _NT, preferred_element_type=F32)
        for u in range(n_sub):
            if SUB >= T:
                bias = jnp.concatenate([bias_tile(u * SUB // T + t) for t in range(SUB // T)], axis=0)
            else:
                bias = bias_tile(u * SUB // T)[(u * SUB) % T:(u * SUB) % T + SUB, :]
            sb = s[u * SUB:(u + 1) * SUB, :] + bias
            mu = jnp.max(sb, axis=0, keepdims=True)
            p_scr[c][u * SUB:(u + 1) * SUB, :] = jnp.exp2((sb - mu).astype(BF16))
            m_tile[c][u] = mu

    acc = []
    for c in range(2):
        m = functools.reduce(jnp.maximum, m_tile[c])
        a = None
        for u in range(n_sub):
            pv = jnp.dot(vt_ref[0, 0, :, u * SUB:(u + 1) * SUB], p_scr[c][u * SUB:(u + 1) * SUB, :],
                         preferred_element_type=F32)
            pv = pv * jnp.exp2(m_tile[c][u] - m)
            a = pv if a is None else a + pv
        acc.append(a)
    num = [a[:DIFF_VDIM] for a in acc]
    den = [a[DIFF_VDIM:DIFF_VDIM + 1] for a in acc]

    lam = (jnp.exp(jnp.sum(lam_ref[0:1, :] * lam_ref[1:2, :], axis=1, keepdims=True))
           - jnp.exp(jnp.sum(lam_ref[2:3, :] * lam_ref[3:4, :], axis=1, keepdims=True)) + LAM_INIT)
    o = num[0] * (1.0 / den[0]) - lam * (num[1] * (1.0 / den[1]))
    ms = jnp.mean(o * o, axis=0, keepdims=True)
    o = o * lax.rsqrt(ms + LN_EPS) * g_ref[...] * (1.0 - LAM_INIT)
    o_ref[...] = o.T.astype(BF16)


def _diff_attention(rel_bias, q, k, vt, buckets, lam_vecs, subln_col):
    T = ATT_TILE
    TQ = ATT_Q
    nq = SEQ // TQ
    n_bias = TQ // T + 4
    return pl.pallas_call(
        _diff_attn_kernel,
        grid=(DIFF_HEADS, BATCH, nq),
        in_specs=[
            pl.BlockSpec(memory_space=pltpu.SMEM),
            pl.BlockSpec((TQ, DIFF_VDIM), lambda h, b, i: (b * nq + i, h)),
            pl.BlockSpec((SEQ, DIFF_VDIM), lambda h, b, i: (b, h)),
            pl.BlockSpec((1, 1, VT_ROWS, SEQ), lambda h, b, i: (b, h, 0, 0)),
            pl.BlockSpec((n_bias, T, TQ), lambda h, b, i: (0, 0, 0)),
            pl.BlockSpec((4, DIFF_HEAD_DIM), lambda h, b, i: (0, 0)),
            pl.BlockSpec((DIFF_VDIM, 1), lambda h, b, i: (0, 0)),
        ],
        out_specs=pl.BlockSpec((TQ, DIFF_VDIM), lambda h, b, i: (b * nq + i, h)),
        out_shape=jax.ShapeDtypeStruct((N_TOK, DIFF_WIDTH), BF16),
        scratch_shapes=[pltpu.VMEM((n_bias, T, TQ), F32),
                        pltpu.VMEM((SEQ, TQ), BF16), pltpu.VMEM((SEQ, TQ), BF16)],
        compiler_params=_vmem(40 << 20),
        name="diff_attn",
    )(rel_bias, q, k, vt, buckets, lam_vecs, subln_col)


def _fourier_kernel(f_ref, bdc_ref, bds_ref, bdw_ref, m1_ref, m2_ref, twc_ref, tws_ref, o_ref,
                    zr_scr, zi_scr, d_scr, a_scr):
    R = FFT_RADIX
    W = FOURIER_WIDTH
    u = f_ref[...]
    w = bdw_ref[...]
    pc = jnp.dot(u, bdc_ref[...], preferred_element_type=F32).astype(BF16)
    ps = jnp.dot(u, bds_ref[...], preferred_element_type=F32).astype(BF16)
    zr = jnp.dot(pc, w, preferred_element_type=F32)
    zi = jnp.dot(ps, w, preferred_element_type=F32)
    P = FFT_PITCH
    halves = range(W // 128)
    for hh in halves:
        for n1 in range(R):
            zr_scr[hh, n1 * P:n1 * P + R, :] = zr[n1 * R:(n1 + 1) * R, hh * 128:(hh + 1) * 128]
            zi_scr[hh, n1 * P:n1 * P + R, :] = zi[n1 * R:(n1 + 1) * R, hh * 128:(hh + 1) * 128]

    for n2 in range(R):
        for hh in halves:
            cols = slice(n2 * W + hh * 128, n2 * W + (hh + 1) * 128)
            d_scr[0:R, cols] = zr_scr[hh, pl.ds(n2, R, stride=P), :].astype(BF16)
            d_scr[R:2 * R, cols] = zi_scr[hh, pl.ds(n2, R, stride=P), :].astype(BF16)
    a_scr[...] = jnp.dot(m1_ref[...], d_scr[...], preferred_element_type=F32)

    for n2 in range(R):
        tc = twc_ref[n2]
        ts = tws_ref[n2]
        for hh in halves:
            cols = slice(n2 * W + hh * 128, n2 * W + (hh + 1) * 128)
            ar = a_scr[0:R, cols]
            ai = a_scr[R:2 * R, cols]
            zr_scr[hh, pl.ds(n2, R, stride=P), :] = ar * tc + ai * ts
            zi_scr[hh, pl.ds(n2, R, stride=P), :] = ai * tc - ar * ts
    for k1 in range(R):
        for hh in halves:
            cols = slice(k1 * W + hh * 128, k1 * W + (hh + 1) * 128)
            d_scr[0:R, cols] = zr_scr[hh, k1 * P:k1 * P + R, :].astype(BF16)
            d_scr[R:2 * R, cols] = zi_scr[hh, k1 * P:k1 * P + R, :].astype(BF16)
    a_scr[0:R, :] = jnp.dot(m2_ref[...], d_scr[...], preferred_element_type=F32)
    for k1 in range(R):
        for hh in halves:
            zr_scr[hh, pl.ds(k1, R, stride=P), :] = a_scr[0:R, k1 * W + hh * 128:k1 * W + (hh + 1) * 128]
    for k2 in range(R):
        for hh in halves:
            o_ref[hh, k2 * R:(k2 + 1) * R, :] = zr_scr[hh, k2 * P:k2 * P + R, :]


def _fourier(f_in, bdc, bds, bdw, m1, m2, twc, tws):
    R = FFT_RADIX
    W = FOURIER_WIDTH
    const2 = lambda b: (0, 0)
    const3 = lambda b: (0, 0, 0)
    return pl.pallas_call(
        _fourier_kernel,
        grid=(BATCH,),
        in_specs=[
            pl.BlockSpec((SEQ, W), lambda b: (b, 0)),
            pl.BlockSpec((W, W), const2),
            pl.BlockSpec((W, W), const2),
            pl.BlockSpec((W, W), const2),
            pl.BlockSpec((2 * R, 2 * R), const2),
            pl.BlockSpec((R, 2 * R), const2),
            pl.BlockSpec((R, R, 128), const3),
            pl.BlockSpec((R, R, 128), const3),
        ],
        out_specs=pl.BlockSpec((W // 128, SEQ, 128), lambda b: (0, b, 0)),
        out_shape=jax.ShapeDtypeStruct((W // 128, N_TOK, 128), F32),
        scratch_shapes=[pltpu.VMEM((W // 128, R * FFT_PITCH, 128), F32),
                        pltpu.VMEM((W // 128, R * FFT_PITCH, 128), F32),
                        pltpu.VMEM((2 * R, R * W), BF16), pltpu.VMEM((2 * R, R * W), F32)],
        compiler_params=_vmem(52 << 20),
        name="fourier",
    )(f_in, bdc, bds, bdw, m1, m2, twc, tws)


@functools.lru_cache(maxsize=None)
def _dft_constants():
    R = FFT_RADIX
    a = np.arange(R, dtype=np.int64)
    ang = 2.0 * np.pi * ((a[:, None] * a[None, :]) % R).astype(np.float64) / R
    c64, s64 = np.cos(ang), np.sin(ang)
    scale = 1.0 / math.sqrt(SEQ * FOURIER_GROUP_DIM)
    m1 = np.block([[c64, s64], [-s64, c64]])
    m2 = np.concatenate([c64, s64], axis=1) * scale
    tang = 2.0 * np.pi * (a[:, None] * a[None, :]).astype(np.float64) / SEQ
    twc = np.repeat(np.cos(tang)[:, :, None], 128, axis=2).astype(np.float32)
    tws = np.repeat(np.sin(tang)[:, :, None], 128, axis=2).astype(np.float32)
    assert FOURIER_GROUP_DIM == R
    eye = np.eye(FOURIER_GROUPS)
    bdc = np.kron(eye, c64)
    bds = np.kron(eye, -s64)
    return bdc.astype(BF16), bds.astype(BF16), m1.astype(BF16), m2.astype(BF16), twc, tws


def _mem_kv_kernel(mem_ref, w_ref, kcat_ref, vcat_ref):
    kv = jnp.dot(mem_ref[...].astype(BF16), w_ref[...], preferred_element_type=F32)
    mk = kv[:, :MEM_WIDTH]
    mv = kv[:, MEM_WIDTH:]
    lane = lax.broadcasted_iota(jnp.int32, mk.shape, 1)
    for hh in range(MEM_HEADS):
        sel = (lane >= hh * MEM_HEAD_DIM) & (lane < (hh + 1) * MEM_HEAD_DIM)
        kcat_ref[0, hh * MEM_LEN:(hh + 1) * MEM_LEN, :] = jnp.where(sel, mk, 0.0).astype(BF16)
        vcat_ref[0, hh * MEM_LEN:(hh + 1) * MEM_LEN, :] = jnp.where(sel, mv, 0.0).astype(BF16)


def _mem_kv(mem2, w_kv):
    return pl.pallas_call(
        _mem_kv_kernel,
        grid=(BATCH,),
        in_specs=[
            pl.BlockSpec((MEM_LEN, D_MODEL), lambda b: (b, 0)),
            pl.BlockSpec((D_MODEL, 2 * MEM_WIDTH), lambda b: (0, 0)),
        ],
        out_specs=[
            pl.BlockSpec((1, MEM_HEADS * MEM_LEN, MEM_WIDTH), lambda b: (b, 0, 0)),
            pl.BlockSpec((1, MEM_HEADS * MEM_LEN, MEM_WIDTH), lambda b: (b, 0, 0)),
        ],
        out_shape=[
            jax.ShapeDtypeStruct((BATCH, MEM_HEADS * MEM_LEN, MEM_WIDTH), BF16),
            jax.ShapeDtypeStruct((BATCH, MEM_HEADS * MEM_LEN, MEM_WIDTH), BF16),
        ],
        name="mem_kv",
    )(mem2, w_kv)


def _mem_attn_kernel(mq_ref, kcat_ref, vcat_ref, o_ref):
    s = lax.dot_general(mq_ref[...], kcat_ref[0], _NT, preferred_element_type=F32)
    parts = []
    for hh in range(MEM_HEADS):
        sh = s[:, hh * MEM_LEN:(hh + 1) * MEM_LEN]
        p = jnp.exp(sh - jnp.max(sh, axis=1, keepdims=True))
        parts.append((p * (1.0 / jnp.sum(p, axis=1, keepdims=True))).astype(BF16))
    p_all = jnp.concatenate(parts, axis=1)
    o_ref[...] = jnp.dot(p_all, vcat_ref[0], preferred_element_type=F32).astype(BF16)


def _mem_attention(mq, kcat, vcat):
    rt = SEQ // ROW_TILE
    return pl.pallas_call(
        _mem_attn_kernel,
        grid=(N_TOK // ROW_TILE,),
        in_specs=[
            pl.BlockSpec((ROW_TILE, MEM_WIDTH), lambda r: (r, 0)),
            pl.BlockSpec((1, MEM_HEADS * MEM_LEN, MEM_WIDTH), lambda r: (r // rt, 0, 0)),
            pl.BlockSpec((1, MEM_HEADS * MEM_LEN, MEM_WIDTH), lambda r: (r // rt, 0, 0)),
        ],
        out_specs=pl.BlockSpec((ROW_TILE, MEM_WIDTH), lambda r: (r, 0)),
        out_shape=jax.ShapeDtypeStruct((N_TOK, MEM_WIDTH), BF16),
        name="mem_attn",
    )(mq, kcat, vcat)


def _route(logits):
    lane = lax.broadcasted_iota(jnp.int32, logits.shape, 1)
    big = jnp.int32(ROUTER_LANES)
    is_group = (lane >= GROUP_LANE0) & (lane < GROUP_LANE0 + N_GROUPS)
    gl = jnp.where(is_group, logits, -jnp.inf)
    gmax = jnp.max(gl, axis=1, keepdims=True)
    g_sel = jnp.min(jnp.where(gl == gmax, lane, big), axis=1, keepdims=True) - GROUP_LANE0
    g_gate = 1.0 / jnp.sum(jnp.where(is_group, jnp.exp(gl - gmax), 0.0), axis=1, keepdims=True)
    in_group = (lane >= g_sel * EXPERTS_PER_GROUP) & (lane < (g_sel + 1) * EXPERTS_PER_GROUP)
    el = jnp.where(in_group, logits, -jnp.inf)
    v1 = jnp.max(el, axis=1, keepdims=True)
    i1 = jnp.min(jnp.where(el == v1, lane, big), axis=1, keepdims=True)
    el2 = jnp.where(lane == i1, -jnp.inf, el)
    v2 = jnp.max(el2, axis=1, keepdims=True)
    i2 = jnp.min(jnp.where(el2 == v2, lane, big), axis=1, keepdims=True)
    e = jnp.exp(v2 - v1)
    w_first = g_gate / (1.0 + e)
    w_second = g_gate * e / (1.0 + e)
    return i1, i2, w_first, w_second


def _outproj_router_kernel(od_ref, of_ref, om_ref, h_ref, wo_ref, g_ref, b_ref,
                           wr_ref, br_ref,
                           h1_ref, h1_rows_ref, route_ref, w0_ref, w1_ref, cnt_ref, cnt_scr):
    @pl.when(pl.program_id(0) == 0)
    def _():
        cnt_scr[...] = jnp.zeros_like(cnt_scr)

    o_four = [of_ref[hh].astype(BF16) for hh in range(FOURIER_WIDTH // 128)]
    o = jnp.concatenate([od_ref[...]] + o_four + [om_ref[...]], axis=1)
    a = jnp.dot(o, wo_ref[...], preferred_element_type=F32)
    h1 = _layer_norm(DEEPNORM_ALPHA * h_ref[...] + a, g_ref[...], b_ref[...])
    h1_ref[...] = h1
    hi = h1.astype(BF16)
    hi_f = hi.astype(F32)
    _store_row_tiles(h1_rows_ref, hi_f, rounded=True)
    lo = (h1 - hi_f).astype(BF16)
    hw = jnp.dot(hi, wr_ref[...], preferred_element_type=F32)
    logits = (hw[:, :ROUTER_LANES] + hw[:, ROUTER_LANES:]
              + jnp.dot(lo, wr_ref[:, :ROUTER_LANES], preferred_element_type=F32) + br_ref[...])
    i1, i2, w_first, w_second = _route(logits)

    lane = lax.broadcasted_iota(jnp.int32, logits.shape, 1)
    onehot = jnp.where(lane == i1, 1.0, jnp.where(lane == i2, 1.0, 0.0))
    r_id = lax.broadcasted_iota(jnp.int32, (ROW_TILE, ROW_TILE), 0)
    c_id = lax.broadcasted_iota(jnp.int32, (ROW_TILE, ROW_TILE), 1)
    tri = jnp.where(r_id > c_id, 1.0, 0.0).astype(BF16)
    before = jnp.dot(tri, onehot.astype(BF16), preferred_element_type=F32) + cnt_scr[...]
    rank1 = jnp.sum(jnp.where(lane == i1, before, 0.0), axis=1, keepdims=True)
    rank2 = jnp.sum(jnp.where(lane == i2, before, 0.0), axis=1, keepdims=True)
    cnt_scr[...] += jnp.sum(onehot, axis=0, keepdims=True)
    cnt_ref[...] = cnt_scr[...]

    packed = jnp.where(lane == 0, i1.astype(F32),
                       jnp.where(lane == 1, i2.astype(F32),
                                 jnp.where(lane == 2, rank1, jnp.where(lane == 3, rank2, 0.0))))
    route_ref[...] = packed.T[:8, :].astype(jnp.int32)
    w0_ref[...] = jnp.broadcast_to(w_first, (ROW_TILE, 128))
    w1_ref[...] = jnp.broadcast_to(w_second, (ROW_TILE, 128))


def _outproj_router(o_diff, o_four, o_mem, h, w_out, ln_g, ln_b, w_r, b_r):
    row = lambda r: (r, 0)
    const = lambda r: (0, 0)
    return pl.pallas_call(
        _outproj_router_kernel,
        grid=(N_TOK // ROW_TILE,),
        in_specs=[
            pl.BlockSpec((ROW_TILE, DIFF_WIDTH), row),
            pl.BlockSpec((FOURIER_WIDTH // 128, ROW_TILE, 128), lambda r: (0, r, 0)),
            pl.BlockSpec((ROW_TILE, MEM_WIDTH), row),
            pl.BlockSpec((ROW_TILE, D_MODEL), row),
            pl.BlockSpec((D_MODEL, D_MODEL), const),
            pl.BlockSpec((1, D_MODEL), const),
            pl.BlockSpec((1, D_MODEL), const),
            pl.BlockSpec((D_MODEL, 2 * ROUTER_LANES), const),
            pl.BlockSpec((1, ROUTER_LANES), const),
        ],
        out_specs=[
            pl.BlockSpec((ROW_TILE, D_MODEL), row),
            pl.BlockSpec((ROW_TILE * ROW_SLABS, 128), row),
            pl.BlockSpec((8, ROW_TILE), lambda r: (0, r)),
            pl.BlockSpec((ROW_TILE, 128), row),
            pl.BlockSpec((ROW_TILE, 128), row),
            pl.BlockSpec((1, ROUTER_LANES), const),
        ],
        out_shape=[
            jax.ShapeDtypeStruct((N_TOK, D_MODEL), F32),
            jax.ShapeDtypeStruct((N_TOK * ROW_SLABS, 128), U32),
            jax.ShapeDtypeStruct((8, N_TOK), jnp.int32),
            jax.ShapeDtypeStruct((N_TOK, 128), F32),
            jax.ShapeDtypeStruct((N_TOK, 128), F32),
            jax.ShapeDtypeStruct((1, ROUTER_LANES), F32),
        ],
        scratch_shapes=[pltpu.VMEM((1, ROUTER_LANES), F32)],
        compiler_params=_vmem(40 << 20),
        name="outproj_router",
    )(o_diff, o_four, o_mem, h, w_out, ln_g, ln_b, w_r, b_r)


def _dispatch_kernel(pos0_ref, pos1_ref, x_ref, xs_ref, sem):
    base = pl.program_id(0) * DISPATCH_TILE

    def issue(t, carry):
        src = _row_tile(x_ref, t)
        pltpu.make_async_copy(src, _row_tile(xs_ref, pos0_ref[base + t]), sem).start(priority=0)
        pltpu.make_async_copy(src, _row_tile(xs_ref, pos1_ref[base + t]), sem).start(priority=1)
        return carry

    lax.fori_loop(0, DISPATCH_TILE, issue, 0, unroll=8)
    for _ in range(2):
        pltpu.make_async_copy(x_ref, xs_ref.at[pl.ds(0, DISPATCH_TILE * ROW_SLABS), :], sem).wait()


def _dispatch(pos0, pos1, h1_rows):
    return pl.pallas_call(
        _dispatch_kernel,
        grid_spec=pltpu.PrefetchScalarGridSpec(
            num_scalar_prefetch=2,
            grid=(N_TOK // DISPATCH_TILE,),
            in_specs=[pl.BlockSpec((DISPATCH_TILE * ROW_SLABS, 128), lambda i, p0, p1: (i, 0))],
            out_specs=pl.BlockSpec(memory_space=pl.ANY),
            scratch_shapes=[pltpu.SemaphoreType.DMA(())],
        ),
        out_shape=jax.ShapeDtypeStruct((N_SLOTS * ROW_SLABS, 128), U32),
        name="moe_dispatch",
    )(pos0, pos1, h1_rows)


def _expert_kernel(tile_end_ref, xs_ref, w1_ref, w3_ref, w2_ref, ys_ref):
    @pl.when(pl.program_id(0) < tile_end_ref[N_EXPERTS - 1])
    def _():
        x = _load_row_tiles(xs_ref, MOE_TILE).astype(BF16)
        a = jnp.dot(x, w1_ref[0].astype(BF16), preferred_element_type=F32)
        b = jnp.dot(x, w3_ref[0].astype(BF16), preferred_element_type=F32)
        hid = (a * jax.nn.sigmoid(a) * b).astype(BF16)
        y = jnp.dot(hid, w2_ref[0].astype(BF16), preferred_element_type=F32)
        _store_row_tiles(ys_ref, y)


def _experts(tile_end, xs, w1, w3, w2):
    def tile(t, tile_end):
        return jnp.minimum(t, tile_end[N_EXPERTS - 1] - 1), 0

    def wsel(t, tile_end):
        e = jnp.int32(0)
        step = N_EXPERTS // 2
        while step:
            e = jnp.where(t >= tile_end[e + step - 1], e + step, e)
            step //= 2
        return e, 0, 0

    return pl.pallas_call(
        _expert_kernel,
        grid_spec=pltpu.PrefetchScalarGridSpec(
            num_scalar_prefetch=1,
            grid=(N_MOE_TILES,),
            in_specs=[
                pl.BlockSpec((MOE_TILE * ROW_SLABS, 128), tile),
                pl.BlockSpec((1, D_MODEL, D_EXPERT), wsel),
                pl.BlockSpec((1, D_MODEL, D_EXPERT), wsel),
                pl.BlockSpec((1, D_EXPERT, D_MODEL), wsel),
            ],
            out_specs=pl.BlockSpec((MOE_TILE * ROW_SLABS, 128), tile),
        ),
        out_shape=jax.ShapeDtypeStruct((N_SLOTS * ROW_SLABS, 128), U32),
        compiler_params=_vmem(40 << 20),
        name="moe_experts",
    )(tile_end, xs, w1, w3, w2)


def _combine_kernel(pos0_ref, pos1_ref, ys_ref, h1_ref, w0_ref, w1_ref, g_ref, b_ref, o_ref,
                    y0_buf, y1_buf, sems):
    i = pl.program_id(0)
    n = pl.num_programs(0)

    def gather(tile, slot):
        base = tile * COMBINE_TILE

        def issue(t, carry):
            pltpu.make_async_copy(_row_tile(ys_ref, pos0_ref[base + t]), _row_tile(y0_buf.at[slot], t),
                                  sems.at[slot]).start(priority=0)
            pltpu.make_async_copy(_row_tile(ys_ref, pos1_ref[base + t]), _row_tile(y1_buf.at[slot], t),
                                  sems.at[slot]).start(priority=1)
            return carry

        lax.fori_loop(0, COMBINE_TILE, issue, 0, unroll=8)

    @pl.when(i == 0)
    def _():
        gather(0, 0)

    @pl.when(i + 1 < n)
    def _():
        gather(i + 1, (i + 1) % 2)

    slot = i % 2
    whole = ys_ref.at[pl.ds(0, COMBINE_TILE * ROW_SLABS), :]
    pltpu.make_async_copy(whole, y0_buf.at[slot], sems.at[slot]).wait()
    pltpu.make_async_copy(whole, y1_buf.at[slot], sems.at[slot]).wait()
    w0 = jnp.concatenate([w0_ref[...]] * (D_MODEL // 128), axis=1)
    w1 = jnp.concatenate([w1_ref[...]] * (D_MODEL // 128), axis=1)
    f = (w0 * _load_row_tiles(y0_buf.at[slot], COMBINE_TILE)
         + w1 * _load_row_tiles(y1_buf.at[slot], COMBINE_TILE))
    o_ref[...] = _layer_norm(DEEPNORM_ALPHA * h1_ref[...] + f, g_ref[...], b_ref[...])


def _combine(pos0, pos1, ys, h1, w0, w1, ln_g, ln_b):
    row = lambda i, p0, p1: (i, 0)
    const = lambda i, p0, p1: (0, 0)
    return pl.pallas_call(
        _combine_kernel,
        grid_spec=pltpu.PrefetchScalarGridSpec(
            num_scalar_prefetch=2,
            grid=(N_TOK // COMBINE_TILE,),
            in_specs=[
                pl.BlockSpec(memory_space=pl.ANY),
                pl.BlockSpec((COMBINE_TILE, D_MODEL), row),
                pl.BlockSpec((COMBINE_TILE, 128), row),
                pl.BlockSpec((COMBINE_TILE, 128), row),
                pl.BlockSpec((1, D_MODEL), const),
                pl.BlockSpec((1, D_MODEL), const),
            ],
            out_specs=pl.BlockSpec((COMBINE_TILE, D_MODEL), row),
            scratch_shapes=[pltpu.VMEM((2, COMBINE_TILE * ROW_SLABS, 128), U32),
                            pltpu.VMEM((2, COMBINE_TILE * ROW_SLABS, 128), U32),
                            pltpu.SemaphoreType.DMA((2,))],
        ),
        out_shape=jax.ShapeDtypeStruct((N_TOK, D_MODEL), F32),
        compiler_params=pltpu.CompilerParams(dimension_semantics=("arbitrary",), vmem_limit_bytes=40 << 20),
        name="moe_combine",
    )(pos0, pos1, ys, h1, w0, w1, ln_g, ln_b)


def kernel(x, mem, ln0_g, ln0_b, rel_bias, w_in, w_mem_kv, w_fourier, lambda_q1, lambda_k1, lambda_q2,
           lambda_k2, subln_g, w_out, ln1_g, ln1_b, w_group, b_group, w_router, b_router, w1, w3, w2,
           ln2_g, ln2_b):
    l = 0
    x2 = x.reshape(N_TOK, D_MODEL)
    wi = w_in[l]
    w_qk = wi[:, :2 * DIFF_WIDTH].astype(BF16)
    w_vt = wi[:, 2 * DIFF_WIDTH:3 * DIFF_WIDTH].T.astype(BF16)
    w_fm = wi[:, 3 * DIFF_WIDTH:].astype(BF16)
    h, q, k, vt, f_in, mq = _ln0_inproj(x2, ln0_g.reshape(1, -1), ln0_b.reshape(1, -1), w_qk, w_vt, w_fm)

    T = ATT_TILE
    kk = jnp.arange(T, dtype=jnp.int32)[:, None]
    qq = jnp.arange(ATT_Q, dtype=jnp.int32)[None, :]
    buckets = jnp.stack([_t5_bucket(d * T + kk - qq) for d in range(-2, ATT_Q // T + 2)])
    lam_vecs = jnp.stack([lambda_q1[l], lambda_k1[l], lambda_q2[l], lambda_k2[l]]).astype(F32)
    o_diff = _diff_attention(rel_bias.astype(F32), q, k, vt, buckets, lam_vecs,
                             subln_g[l].astype(F32).reshape(DIFF_VDIM, 1))

    bdc, bds, m1, m2, twc, tws = (jnp.asarray(c) for c in _dft_constants())
    wf = w_fourier[l]
    bdw = jnp.zeros((FOURIER_WIDTH, FOURIER_WIDTH), F32)
    for g in range(FOURIER_GROUPS):
        sl = slice(g * FOURIER_GROUP_DIM, (g + 1) * FOURIER_GROUP_DIM)
        bdw = bdw.at[sl, sl].set(wf[g])
    o_four = _fourier(f_in, bdc, bds, bdw.astype(BF16), m1, m2, twc, tws)

    kcat, vcat = _mem_kv(mem.reshape(BATCH * MEM_LEN, D_MODEL), w_mem_kv[l].astype(BF16))
    o_mem = _mem_attention(mq, kcat, vcat)

    w_r = jnp.zeros((D_MODEL, ROUTER_LANES), F32)
    w_r = w_r.at[:, :N_EXPERTS].set(w_router[l].astype(F32))
    w_r = w_r.at[:, GROUP_LANE0:GROUP_LANE0 + N_GROUPS].set(w_group[l].astype(F32))
    wr_hi = w_r.astype(BF16)
    wr_lo = (w_r - wr_hi.astype(F32)).astype(BF16)
    b_r = jnp.zeros((1, ROUTER_LANES), F32)
    b_r = b_r.at[0, :N_EXPERTS].set(b_router[l].astype(F32))
    b_r = b_r.at[0, GROUP_LANE0:GROUP_LANE0 + N_GROUPS].set(b_group[l].astype(F32))
    h1, h1_rows, route, gate0, gate1, counts = _outproj_router(
        o_diff, o_four, o_mem, h, w_out[l].astype(BF16),
        ln1_g[l].reshape(1, -1), ln1_b[l].reshape(1, -1), jnp.concatenate([wr_hi, wr_lo], axis=1), b_r)

    cnt = counts[0, :N_EXPERTS].astype(jnp.int32)
    tiles_per_expert = (cnt + MOE_TILE - 1) // MOE_TILE
    tile_end = jnp.cumsum(tiles_per_expert).astype(jnp.int32)
    row_start = (tile_end - tiles_per_expert) * MOE_TILE
    expert_ids = jnp.arange(N_EXPERTS, dtype=jnp.int32)[None, :]

    def slot_of(expert, rank):
        return jnp.sum(jnp.where(expert[:, None] == expert_ids, row_start[None, :], 0), axis=1) + rank

    pos0 = slot_of(route[0], route[2])
    pos1 = slot_of(route[1], route[3])

    xs = _dispatch(pos0, pos1, h1_rows)
    ys = _experts(tile_end, xs, w1[l], w3[l], w2[l])
    out = _combine(pos0, pos1, ys, h1, gate0, gate1, ln2_g[l].reshape(1, -1), ln2_b[l].reshape(1, -1))
    return out.reshape(BATCH, SEQ, D_MODEL)
```
